```python
import math
import jax, jax.numpy as jnp
from jax import lax
import numpy as np

D_MODEL = 1024
BATCH = 4
SEQ = 4096
DEPTH = 1
DEC_BATCH = 32
DEC_SEQ = 4
PAST_LEN = 8192
PAGE_SIZE = 128

HEAD_DIM = 64
N_HEADS = (D_MODEL // 2) // HEAD_DIM
H_KV = 2
GROUP = N_HEADS // H_KV
CMP_BLOCK = 32
CMP_STRIDE = 16
CMP_HIDDEN = 2 * HEAD_DIM
SEL_BLOCK = 64
TOP_N = 16
WINDOW = 512
D_CONV = D_MODEL // 2
CONV_WIDTH = 3
D_FF = 128 * ((8 * D_MODEL // 3 + 127) // 128)
N_BUCKETS = 32
MAX_DISTANCE = 128
Q_BLOCK = 64
N_SUB = 3
EPS = 1e-6
NEG_INF = -1e9
FORCE_BONUS = 1e3
IN_SIZES = (N_HEADS * HEAD_DIM, 2 * H_KV * HEAD_DIM, 2 * H_KV * HEAD_DIM, 2 * H_KV * HEAD_DIM,
            3 * N_HEADS, D_CONV, D_CONV, D_CONV, 2 * D_MODEL)
IN_WIDTH = sum(IN_SIZES)

kernel_name = "nsa_shortconv_macaron_adaln_step"


def rms_norm(x, g):
    xf = x.astype(jnp.float32)
    y = xf * lax.rsqrt(jnp.mean(xf * xf, axis=-1, keepdims=True) + EPS)
    return (y * g.astype(jnp.float32)).astype(x.dtype)


def modulate(x, g, shift, scale):
    return rms_norm(x, g) * (1 + scale[:, None, :]) + shift[:, None, :]


def swiglu(u, w_gu, w_down):
    gate, val = jnp.split(u @ w_gu, 2, axis=-1)
    return (jax.nn.silu(gate) * val) @ w_down


def masked_softmax(s, mask):
    s = s.astype(jnp.float32)
    m = jnp.max(jnp.where(mask, s, NEG_INF), axis=-1, keepdims=True)
    e = jnp.exp(jnp.where(mask, s - m, NEG_INF))
    return e / jnp.maximum(jnp.sum(e, axis=-1, keepdims=True), 1e-30)


def rel_bucket(dist):
    n = jnp.maximum(dist, 0)
    max_exact = N_BUCKETS // 2
    nf = jnp.maximum(n, 1).astype(jnp.float32)
    large = max_exact + (jnp.log(nf / max_exact) / math.log(MAX_DISTANCE / max_exact)
                         * (N_BUCKETS - max_exact)).astype(jnp.int32)
    return jnp.where(n < max_exact, n, jnp.minimum(large, N_BUCKETS - 1))


def pos_bias(qpos, kpos, rel_bias):
    b = rel_bias[rel_bucket(qpos[:, None] - kpos[None, :])]
    b = b.reshape(b.shape[0], b.shape[1], H_KV, GROUP)
    return jnp.transpose(b, (2, 3, 0, 1)).astype(jnp.float32)


def compress_kv(rows, pe, w1, w2):
    B, L = rows.shape[:2]
    r = CMP_BLOCK // CMP_STRIDE
    n_ch = L // CMP_STRIDE
    n_c = (L - CMP_BLOCK) // CMP_STRIDE + 1
    ch = rows[:, :n_ch * CMP_STRIDE].reshape((B, n_ch, CMP_STRIDE) + rows.shape[2:])
    w1c = w1.reshape(2, r, CMP_STRIDE, HEAD_DIM, CMP_HIDDEN)
    pre = jnp.einsum('cld,clde->ce', pe, w1)[None, None, :, None, :]
    for m in range(r):
        pre = pre + jnp.einsum('bnlcgd,clde->bncge', ch[:, m:m + n_c], w1c[:, m])
    out = jnp.einsum('bncge,ced->bncgd', jax.nn.gelu(pre), w2)
    c_end = jnp.arange(n_c) * CMP_STRIDE + CMP_BLOCK - 1
    return out, c_end


def cmp_to_sel(p, n_sel):
    r_sel = SEL_BLOCK // CMP_STRIDE
    r_cmp = CMP_BLOCK // CMP_STRIDE
    left = r_cmp - 1
    right = r_sel * n_sel + r_cmp - 1 - left - p.shape[-1]
    pp = jnp.pad(p, [(0, 0)] * (p.ndim - 1) + [(left, right)])
    out = None
    for m in range(r_sel):
        for n in range(r_cmp):
            term = pp[..., m + n: m + n + r_sel * (n_sel - 1) + 1: r_sel]
            out = term if out is None else out + term
    return out


def sel_blocks(rows):
    B, L = rows.shape[:2]
    n_sel = -(-L // SEL_BLOCK)
    rows = jnp.pad(rows, ((0, 0), (0, n_sel * SEL_BLOCK - L), (0, 0), (0, 0), (0, 0)))
    return jnp.transpose(rows.reshape(B, n_sel, SEL_BLOCK, 2, H_KV, HEAD_DIM), (0, 4, 1, 2, 3, 5))


def nsa_attend(q, qpos, gates, kv_c, c_end, kv_s_blocks, kv_w, w_pos, rel_bias):
    B, Q = q.shape[:2]
    n_sel = kv_s_blocks.shape[2]
    qg = (q * HEAD_DIM ** -0.5).reshape(B, Q, H_KV, GROUP, HEAD_DIM)
    s_c = jnp.einsum('bqgrd,bngd->bgrqn', qg, kv_c[:, :, 0]) + pos_bias(qpos, c_end, rel_bias)
    p_c = masked_softmax(s_c, c_end[None, :] <= qpos[:, None])
    o_c = jnp.einsum('bgrqn,bngd->bqgrd', p_c, kv_c[:, :, 1])
    p_slc = cmp_to_sel(p_c.sum(axis=2), n_sel)
    blk = jnp.arange(n_sel)[None, :]
    cur = (qpos // SEL_BLOCK)[:, None]
    valid = blk * SEL_BLOCK <= qpos[:, None]
    forced = (blk == 0) | (blk == cur) | (blk == cur - 1)
    score = jnp.where(valid, p_slc + jnp.where(forced, FORCE_BONUS, 0.0), NEG_INF)
    _, idx = lax.top_k(score, min(TOP_N, n_sel))
    bi = jnp.arange(B)[:, None, None, None]
    gi = jnp.arange(H_KV)[None, :, None, None]
    n_keys = idx.shape[-1] * SEL_BLOCK
    sel = kv_s_blocks[bi, gi, idx].reshape(B, H_KV, Q, n_keys, 2, HEAD_DIM)
    k_pos = (idx[..., None] * SEL_BLOCK + jnp.arange(SEL_BLOCK)).reshape(B, H_KV, Q, n_keys)
    table = rel_bias.T.reshape(H_KV, GROUP, N_BUCKETS)
    b_s = jnp.moveaxis(table[gi, :, rel_bucket(qpos[None, None, :, None] - k_pos)], -1, 2)
    s_s = jnp.einsum('bqgrd,bgqkd->bgrqk', qg, sel[..., 0, :]) + b_s.astype(jnp.float32)
    p_s = masked_softmax(s_s, (k_pos <= qpos[None, None, :, None])[:, :, None])
    o_s = jnp.einsum('bgrqk,bgqkd->bqgrd', p_s, sel[..., 1, :])
    dist = qpos[:, None] - w_pos[None, :]
    s_w = jnp.einsum('bqgrd,bkgd->bgrqk', qg, kv_w[:, :, 0]) + pos_bias(qpos, w_pos, rel_bias)
    p_w = masked_softmax(s_w, (w_pos[None, :] >= 0) & (dist >= 0) & (dist < WINDOW))
    o_w = jnp.einsum('bgrqk,bkgd->bqgrd', p_w, kv_w[:, :, 1])
    g = gates.reshape(B, Q, 3, H_KV, GROUP)[..., None]
    o = g[:, :, 0] * o_c + g[:, :, 1] * o_s + g[:, :, 2] * o_w
    return o.reshape(B, Q, N_HEADS * HEAD_DIM).astype(q.dtype)


def nsa_prompt(q, kv_c_rows, kv_s_rows, kv_w_rows, gates, lw, rel_bias):
    B, T = q.shape[:2]
    kv_c, c_end = compress_kv(kv_c_rows, lw['pe_cmp'], lw['w_cmp1'], lw['w_cmp2'])
    kv_s_blocks = sel_blocks(kv_s_rows)
    n_q = T // Q_BLOCK
    band = WINDOW + Q_BLOCK
    w_idx = (jnp.arange(n_q) * Q_BLOCK)[:, None] + jnp.arange(band)[None, :]
    kv_w_pad = jnp.pad(kv_w_rows, ((0, 0), (WINDOW, 0), (0, 0), (0, 0), (0, 0)))
    kv_w_bands = jnp.moveaxis(kv_w_pad[:, w_idx], 1, 0)
    blocks = lambda t: jnp.moveaxis(t.reshape((B, n_q, Q_BLOCK) + t.shape[2:]), 1, 0)
    xs = (blocks(q), jnp.arange(T).reshape(n_q, Q_BLOCK), blocks(gates), kv_w_bands, w_idx - WINDOW)
    o = lax.map(lambda a: nsa_attend(a[0], a[1], a[2], kv_c, c_end, kv_s_blocks, a[3], a[4], rel_bias), xs)
    return jnp.moveaxis(o, 0, 1).reshape(B, T, N_HEADS * HEAD_DIM)


def nsa_sample(q, kv_c_new, kv_s_new, kv_w_new, gates, cache_c, cache_s, win_buf, page_table, lw, rel_bias):
    B, S = q.shape[:2]
    past = page_table.shape[1] * PAGE_SIZE
    gather = lambda pool: pool[page_table].reshape((B, past) + pool.shape[2:]).astype(kv_c_new.dtype)
    kv_c, c_end = compress_kv(jnp.concatenate([gather(cache_c), kv_c_new], axis=1),
                              lw['pe_cmp'], lw['w_cmp1'], lw['w_cmp2'])
    kv_s_blocks = sel_blocks(jnp.concatenate([gather(cache_s), kv_s_new], axis=1))
    kv_w = jnp.concatenate([win_buf.astype(kv_w_new.dtype), kv_w_new], axis=1)
    w_pos = past - win_buf.shape[1] + jnp.arange(kv_w.shape[1])
    qpos = past + jnp.arange(S)
    o = nsa_attend(q, qpos, gates, kv_c, c_end, kv_s_blocks, kv_w, w_pos, rel_bias)
    keep = min(WINDOW, kv_w.shape[1])
    return o, kv_w[:, kv_w.shape[1] - keep:]


def split_projection(u, w_in):
    B, T = u.shape[:2]
    p = u @ w_in
    parts = []
    off = 0
    for size in IN_SIZES:
        parts.append(p[..., off:off + size])
        off += size
    q, kc, ks, kw, gn, ch, cb, cc, mg = parts
    kv = lambda t: t.reshape(B, T, 2, H_KV, HEAD_DIM)
    return (q.reshape(B, T, N_HEADS, HEAD_DIM), kv(kc), kv(ks), kv(kw),
            jax.nn.sigmoid(gn).reshape(B, T, 3, N_HEADS), ch, cb, cc,
            jax.nn.sigmoid(mg).reshape(B, T, 2, D_MODEL))


def short_conv(h, gate_b, gate_c, w_conv, buf):
    T = h.shape[1]
    full = jnp.concatenate([buf.astype(h.dtype), gate_c * h], axis=1)
    out = w_conv[0] * full[:, 0:T]
    for k in range(1, CONV_WIDTH):
        out = out + w_conv[k] * full[:, k:k + T]
    return gate_b * out, full[:, T:]


def token_mixer(u, lw, rel_bias, past):
    B, T = u.shape[:2]
    q, kv_c, kv_s, kv_w, g_nsa, h_cv, b_cv, c_cv, g_merge = split_projection(u, lw['w_in'])
    if past is None:
        o_nsa = nsa_prompt(q, kv_c, kv_s, kv_w, g_nsa, lw, rel_bias)
        keep = min(WINDOW, T)
        win_new = kv_w[:, T - keep:]
        conv_buf = jnp.zeros((B, CONV_WIDTH - 1, D_CONV), u.dtype)
    else:
        cache_c, cache_s, win_buf, conv_buf, page_table = past
        o_nsa, win_new = nsa_sample(q, kv_c, kv_s, kv_w, g_nsa, cache_c, cache_s, win_buf, page_table, lw, rel_bias)
    y_cv, conv_new = short_conv(h_cv, b_cv, c_cv, lw['w_conv'], conv_buf)
    merged = g_merge[:, :, 0] * (o_nsa @ lw['w_nsa_out']) + g_merge[:, :, 1] * (y_cv @ lw['w_conv_out'])
    return merged @ lw['w_out'], (kv_c, kv_s, win_new, conv_new)


def decoder_layer(x, c, lw, rel_bias, past):
    B = x.shape[0]
    mod = (jax.nn.silu(c) @ lw['w_ada'] + lw['b_ada']).reshape(B, N_SUB, 3, D_MODEL)
    gn = lw['g_norm']
    h = x + 0.5 * mod[:, 0, 2][:, None] * swiglu(modulate(x, gn[0], mod[:, 0, 0], mod[:, 0, 1]),
                                                 lw['w_ffn1_gu'], lw['w_ffn1_down'])
    mix, state = token_mixer(modulate(h, gn[1], mod[:, 1, 0], mod[:, 1, 1]), lw, rel_bias, past)
    h = h + mod[:, 1, 2][:, None] * mix
    h = h + 0.5 * mod[:, 2, 2][:, None] * swiglu(modulate(h, gn[2], mod[:, 2, 0], mod[:, 2, 1]),
                                                 lw['w_ffn2_gu'], lw['w_ffn2_down'])
    return h, state


def setup_inputs(seed: int = 0) -> dict:
    key = jax.random.key(seed)
    ks = iter(jax.random.split(key, 32))
    nrm = lambda shape, scale: jax.random.normal(next(ks), shape, jnp.float32) * scale
    n_pages = PAST_LEN // PAGE_SIZE
    n_used = DEC_BATCH * n_pages
    n_phys = n_used + max(1, n_used // 4)
    win_buf = min(WINDOW, PAST_LEN)
    kv_row = (2, H_KV, HEAD_DIM)
    inputs = {
        "x_prompt": nrm((BATCH, SEQ, D_MODEL), 1.0),
        "x_sample": nrm((DEC_BATCH, DEC_SEQ, D_MODEL), 1.0),
        "cache_kv_cmp": nrm((DEPTH, n_phys, PAGE_SIZE) + kv_row, 1.0),
        "cache_kv_sel": nrm((DEPTH, n_phys, PAGE_SIZE) + kv_row, 1.0),
        "state_kv_win": nrm((DEPTH, DEC_BATCH, win_buf) + kv_row, 1.0),
        "state_conv": nrm((DEPTH, DEC_BATCH, CONV_WIDTH - 1, D_CONV), 1.0),
        "page_table": jax.random.permutation(next(ks), n_phys)[:n_used].reshape(DEC_BATCH, n_pages).astype(jnp.int32),
        "c_prompt": nrm((BATCH, D_MODEL), 1.0),
        "c_sample": nrm((DEC_BATCH, D_MODEL), 1.0),
        "w_ada": nrm((DEPTH, D_MODEL, N_SUB * 3 * D_MODEL), 0.5 * D_MODEL ** -0.5),
        "b_ada": nrm((DEPTH, N_SUB * 3 * D_MODEL), 0.02),
        "g_norm": 1.0 + nrm((DEPTH, N_SUB, D_MODEL), 0.05),
        "w_ffn1_gu": nrm((DEPTH, D_MODEL, 2 * D_FF), D_MODEL ** -0.5),
        "w_ffn1_down": nrm((DEPTH, D_FF, D_MODEL), D_FF ** -0.5),
        "w_ffn2_gu": nrm((DEPTH, D_MODEL, 2 * D_FF), D_MODEL ** -0.5),
        "w_ffn2_down": nrm((DEPTH, D_FF, D_MODEL), D_FF ** -0.5),
        "w_in": nrm((DEPTH, D_MODEL, IN_WIDTH), D_MODEL ** -0.5),
        "w_cmp1": nrm((DEPTH, 2, CMP_BLOCK, HEAD_DIM, CMP_HIDDEN), (CMP_BLOCK * HEAD_DIM) ** -0.5),
        "w_cmp2": nrm((DEPTH, 2, CMP_HIDDEN, HEAD_DIM), CMP_HIDDEN ** -0.5),
        "pe_cmp": nrm((DEPTH, 2, CMP_BLOCK, HEAD_DIM), 0.1),
        "w_conv": nrm((DEPTH, CONV_WIDTH, D_CONV), CONV_WIDTH ** -0.5),
        "w_nsa_out": nrm((DEPTH, N_HEADS * HEAD_DIM, D_MODEL), (N_HEADS * HEAD_DIM) ** -0.5),
        "w_conv_out": nrm((DEPTH, D_CONV, D_MODEL), D_CONV ** -0.5),
        "w_out": nrm((DEPTH, D_MODEL, D_MODEL), D_MODEL ** -0.5),
        "rel_bias": nrm((N_BUCKETS, N_HEADS), 0.5),
        "g_final": 1.0 + nrm((D_MODEL,), 0.05),
    }
    return inputs


def reference(x_prompt, x_sample, cache_kv_cmp, cache_kv_sel, state_kv_win, state_conv, page_table,
              c_prompt, c_sample, w_ada, b_ada, g_norm, w_ffn1_gu, w_ffn1_down, w_ffn2_gu, w_ffn2_down,
              w_in, w_cmp1, w_cmp2, pe_cmp, w_conv, w_nsa_out, w_conv_out, w_out, rel_bias, g_final):
    hp, hs = x_prompt, x_sample
    st_p, st_s = [], []
    for l in range(DEPTH):
        lw = dict(w_ada=w_ada[l], b_ada=b_ada[l], g_norm=g_norm[l], w_ffn1_gu=w_ffn1_gu[l],
                  w_ffn1_down=w_ffn1_down[l], w_ffn2_gu=w_ffn2_gu[l], w_ffn2_down=w_ffn2_down[l],
                  w_in=w_in[l], w_cmp1=w_cmp1[l], w_cmp2=w_cmp2[l], pe_cmp=pe_cmp[l], w_conv=w_conv[l],
                  w_nsa_out=w_nsa_out[l], w_conv_out=w_conv_out[l], w_out=w_out[l])
        hp, sp = decoder_layer(hp, c_prompt, lw, rel_bias, None)
        hs, ss = decoder_layer(hs, c_sample, lw, rel_bias,
                               (cache_kv_cmp[l], cache_kv_sel[l], state_kv_win[l], state_conv[l], page_table))
        st_p.append(sp)
        st_s.append(ss)
    y_prompt = rms_norm(hp, g_final)
    y_sample = rms_norm(hs, g_final)
    kv_cmp_p = jnp.stack([s[0] for s in st_p])
    kv_sel_p = jnp.stack([s[1] for s in st_p])
    kv_win_p = jnp.stack([s[2] for s in st_p])
    conv_p = jnp.stack([s[3] for s in st_p])
    kv_cmp_s = jnp.stack([s[0] for s in st_s])
    kv_sel_s = jnp.stack([s[1] for s in st_s])
    kv_win_s = jnp.stack([s[2] for s in st_s])
    conv_s = jnp.stack([s[3] for s in st_s])
    return (y_prompt, y_sample, kv_cmp_p, kv_sel_p, kv_win_p, conv_p, kv_cmp_s, kv_sel_s, kv_win_s, conv_s)
```

```python
import functools
import math

import jax
import jax.numpy as jnp
from jax import lax
from jax.experimental import pallas as pl
from jax.experimental.pallas import tpu as pltpu

F32 = jnp.float32
BF16 = jnp.bfloat16

HEAD_DIM = 64
N_HEADS = 8
H_KV = 2
GROUP = N_HEADS // H_KV
CMP_BLOCK = 32
CMP_STRIDE = 16
CMP_HIDDEN = 2 * HEAD_DIM
SEL_BLOCK = 64
TOP_N = 16
WINDOW = 512
CONV_WIDTH = 3
N_BUCKETS = 32
MAX_DISTANCE = 128
N_SUB = 3
EPS = 1e-6
NEG = -1e9
FORCE_BONUS = 1e3
KV_W = 2 * H_KV * HEAD_DIM
KV_HALF = H_KV * HEAD_DIM
LANES = 128
Q_TILE = 256
CMP_NEAR = 32
VMEM_LIMIT = 56 * 1024 * 1024


def _bucket_thresholds():
    max_exact = N_BUCKETS // 2

    def bucket(d):
        if d < max_exact:
            return d
        large = max_exact + int(math.log(d / max_exact) / math.log(MAX_DISTANCE / max_exact)
                                * (N_BUCKETS - max_exact))
        return min(large, N_BUCKETS - 1)

    thr, d = [], 0
    for b in range(N_BUCKETS):
        while bucket(d) < b:
            d += 1
        thr.append(d)
    return tuple(thr)


BUCKET_THR = _bucket_thresholds()
FAR_DIST = BUCKET_THR[-1]


def _cparams(sem):
    return pltpu.CompilerParams(dimension_semantics=sem, vmem_limit_bytes=VMEM_LIMIT)


def _resident(shape):
    nd = len(shape)
    return pl.BlockSpec(shape, lambda *_: (0,) * nd, pipeline_mode=pl.Buffered(1))


def _dot(a, b):
    return jnp.dot(a, b, preferred_element_type=F32)


def _dot_nt(a, b):
    return lax.dot_general(a, b, (((1,), (1,)), ((), ())), preferred_element_type=F32)


def _split3(x):
    h1 = x.astype(BF16)
    r1 = x - h1.astype(F32)
    h2 = r1.astype(BF16)
    h3 = (r1 - h2.astype(F32)).astype(BF16)
    return h1, h2, h3


def _modulated_norm(x, g, shift, scale):
    y = x * lax.rsqrt(jnp.mean(x * x, axis=-1, keepdims=True) + EPS)
    return (y * g) * (1.0 + scale) + shift


def _ada_kernel(c_ref, w_ref, b_ref, o_ref):
    c = c_ref[...]
    a = (c * jax.nn.sigmoid(c)).astype(BF16)
    o_ref[...] = _dot(a, w_ref[...].astype(BF16)) + b_ref[...]


def _ada(c_all, w_ada, b_ada):
    rows, d = c_all.shape
    n = w_ada.shape[1]
    tn = n // 8
    return pl.pallas_call(
        _ada_kernel,
        grid=(n // tn,),
        in_specs=[pl.BlockSpec((rows, d), lambda j: (0, 0)),
                  pl.BlockSpec((d, tn), lambda j: (0, j)),
                  pl.BlockSpec((1, tn), lambda j: (0, j))],
        out_specs=pl.BlockSpec((rows, tn), lambda j: (0, j)),
        out_shape=jax.ShapeDtypeStruct((rows, n), F32),
        compiler_params=_cparams(("arbitrary",)),
        name="ada",
    )(c_all, w_ada, b_ada.reshape(1, n))


def _ffn_kernel(*refs, d_ff, fc, final_norm):
    if final_norm:
        x_ref, sh_ref, sc_ref, gt_ref, gn_ref, wgu_ref, wd_ref, gf_ref, o_ref = refs
    else:
        x_ref, sh_ref, sc_ref, gt_ref, gn_ref, wgu_ref, wd_ref, o_ref = refs
    x = x_ref[...]
    ub = _modulated_norm(x, gn_ref[...], sh_ref[...], sc_ref[...]).astype(BF16)
    acc = jnp.zeros(x.shape, F32)
    for c in range(d_ff // fc):
        g = _dot(ub, wgu_ref[:, c * fc:(c + 1) * fc])
        v = _dot(ub, wgu_ref[:, d_ff + c * fc:d_ff + (c + 1) * fc])
        a = ((g * jax.nn.sigmoid(g)) * v).astype(BF16)
        acc = acc + _dot(a, wd_ref[c * fc:(c + 1) * fc, :])
    h = x + (0.5 * gt_ref[...]) * acc
    if final_norm:
        h = (h * lax.rsqrt(jnp.mean(h * h, axis=-1, keepdims=True) + EPS)) * gf_ref[...]
    o_ref[...] = h


def _mod_specs(mod, ks, tm, tiles_per_batch):
    if mod.ndim == 3 and mod.shape[1] == 1:
        d = mod.shape[-1]
        specs = [pl.BlockSpec((None, 1, d), lambda i, k=k: ((i // tiles_per_batch) * (3 * N_SUB) + k, 0, 0))
                 for k in ks]
        return specs, [mod] * len(ks)
    d = mod.shape[-1]
    specs = [pl.BlockSpec((None, tm, d), lambda i, k=k: (k, i, 0)) for k in ks]
    return specs, [mod] * len(ks)


def _ffn(x, mod, sub, gn_row, w_gu, w_down, tm, tiles_per_batch, g_final=None):
    n, d = x.shape
    d_ff = w_down.shape[0]
    fc = d_ff // 2 if (d_ff // 2) % LANES == 0 else d_ff
    final_norm = g_final is not None
    mspecs, mops = _mod_specs(mod, (3 * sub, 3 * sub + 1, 3 * sub + 2), tm, tiles_per_batch)
    in_specs = [pl.BlockSpec((tm, d), lambda i: (i, 0))] + mspecs + [
        pl.BlockSpec((1, d), lambda i: (0, 0)), _resident(w_gu.shape), _resident(w_down.shape)]
    ops = [x] + mops + [gn_row, w_gu, w_down]
    if final_norm:
        in_specs.append(pl.BlockSpec((1, d), lambda i: (0, 0)))
        ops.append(g_final.reshape(1, d))
    return pl.pallas_call(
        functools.partial(_ffn_kernel, d_ff=d_ff, fc=fc, final_norm=final_norm),
        grid=(n // tm,),
        in_specs=in_specs,
        out_specs=pl.BlockSpec((tm, d), lambda i: (i, 0)),
        out_shape=jax.ShapeDtypeStruct((n, d), F32),
        compiler_params=_cparams(("arbitrary",)),
        name="ffn_final" if final_norm else "ffn",
    )(*ops)


_Q_W = N_HEADS * HEAD_DIM
_G_W = LANES
_C_W = 512
_SEG = {}
_off = 0
for _name, _w in (("q", _Q_W), ("kc", KV_W), ("ks", KV_W), ("kw", KV_W), ("gn", _G_W),
                  ("ch", _C_W), ("cb", _C_W), ("cc", _C_W)):
    _SEG[_name] = (_off, _off + _w)
    _off += _w
_MG_OFF = _off


def _inproj_kernel(x_ref, sh_ref, sc_ref, gn_ref, w_ref,
                   q_ref, kc_ref, ks_ref, kw_ref, ksb_ref, kwb_ref, g_ref, z_ref, cb_ref, gm_ref):
    ub = _modulated_norm(x_ref[...], gn_ref[...], sh_ref[...], sc_ref[...]).astype(BF16)

    def seg(name):
        lo, hi = _SEG[name]
        return _dot(ub, w_ref[:, lo:hi])

    q_ref[...] = (seg("q") * (HEAD_DIM ** -0.5)).astype(BF16)
    kc_ref[...] = seg("kc")
    ks = seg("ks")
    ks_ref[...] = ks
    ksb_ref[...] = ks.astype(BF16)
    kw = seg("kw")
    kw_ref[...] = kw
    kwb_ref[...] = kw.astype(BF16)
    g_ref[...] = jax.nn.sigmoid(seg("gn"))
    z_ref[...] = seg("cc") * seg("ch")
    cb_ref[...] = seg("cb")
    d2 = gm_ref.shape[1]
    half = d2 // 2
    for c in range(2):
        gm_ref[:, c * half:(c + 1) * half] = jax.nn.sigmoid(
            _dot(ub, w_ref[:, _MG_OFF + c * half:_MG_OFF + (c + 1) * half]))


def _inproj(h, mod, gn_row, w_in_p, tm, tiles_per_batch):
    n, d = h.shape
    d_conv = _C_W
    mspecs, mops = _mod_specs(mod, (3, 4), tm, tiles_per_batch)
    widths = [(_Q_W, BF16), (KV_W, F32), (KV_W, F32), (KV_W, F32), (KV_W, BF16), (KV_W, BF16),
              (_G_W, F32), (d_conv, F32), (d_conv, F32), (2 * d, F32)]
    return pl.pallas_call(
        _inproj_kernel,
        grid=(n // tm,),
        in_specs=[pl.BlockSpec((tm, d), lambda i: (i, 0))] + mspecs + [
            pl.BlockSpec((1, d), lambda i: (0, 0)), _resident(w_in_p.shape)],
        out_specs=[pl.BlockSpec((tm, w), lambda i: (i, 0)) for w, _ in widths],
        out_shape=[jax.ShapeDtypeStruct((n, w), dt) for w, dt in widths],
        compiler_params=_cparams(("arbitrary",)),
        name="inproj",
    )(h, *mops, gn_row, w_in_p)


def _mixout_kernel(*refs, halo, tiles_per_batch):
    if halo:
        (h_ref, o_ref, z_ref, zp_ref, cb_ref, gm_ref, g2_ref, wc_ref,
         wn_ref, wcv_ref, wo_ref, out_ref) = refs
        z = z_ref[...]
        tm = z.shape[0]
        first = (pl.program_id(0) % tiles_per_batch) == 0
        prev = jnp.where(first, 0.0, zp_ref[...])
        row = lax.broadcasted_iota(jnp.int32, z.shape, 0)
        zm1 = jnp.where(row == 0, prev[7:8, :], pltpu.roll(z, 1, 0))
        zm2 = jnp.where(row == 0, prev[6:7, :], jnp.where(row == 1, prev[7:8, :], pltpu.roll(z, 2, 0)))
    else:
        (h_ref, o_ref, z_ref, zm1_ref, zm2_ref, cb_ref, gm_ref, g2_ref, wc_ref,
         wn_ref, wcv_ref, wo_ref, out_ref) = refs
        z, zm1, zm2 = z_ref[...], zm1_ref[...], zm2_ref[...]
    conv = wc_ref[0:1, :] * zm2 + wc_ref[1:2, :] * zm1 + wc_ref[2:3, :] * z
    y = (cb_ref[...] * conv).astype(BF16)
    d = h_ref.shape[1]
    merged = gm_ref[:, 0:d] * _dot(o_ref[...], wn_ref[...]) + gm_ref[:, d:2 * d] * _dot(y, wcv_ref[...])
    out_ref[...] = h_ref[...] + g2_ref[...] * _dot(merged.astype(BF16), wo_ref[...])


def _mixout(h, o_nsa, z, z_shift, cb, gm, mod, w_conv8, w_nsa_p, w_cv, w_o, tm, tiles_per_batch):
    n, d = h.shape
    dc = z.shape[1]
    halo = z_shift is None
    mspecs, mops = _mod_specs(mod, (5,), tm, tiles_per_batch)
    tok = lambda w: pl.BlockSpec((tm, w), lambda i: (i, 0))
    if halo:
        zspecs = [tok(dc), pl.BlockSpec((8, dc), lambda i: (jnp.maximum(i * (tm // 8) - 1, 0), 0))]
        zops = [z, z]
    else:
        zspecs = [tok(dc), tok(dc), tok(dc)]
        zops = [z, z_shift[0], z_shift[1]]
    return pl.pallas_call(
        functools.partial(_mixout_kernel, halo=halo, tiles_per_batch=tiles_per_batch),
        grid=(n // tm,),
        in_specs=[tok(d), tok(o_nsa.shape[1])] + zspecs + [tok(dc), tok(2 * d)] + mspecs + [
            pl.BlockSpec((8, dc), lambda i: (0, 0)),
            _resident(w_nsa_p.shape), _resident(w_cv.shape), _resident(w_o.shape)],
        out_specs=tok(d),
        out_shape=jax.ShapeDtypeStruct((n, d), F32),
        compiler_params=_cparams(("arbitrary",)),
        name="mixout",
    )(h, o_nsa, *zops, cb, gm, *mops, w_conv8, w_nsa_p, w_cv, w_o)


CHUNK_W = CMP_STRIDE * KV_W
CMP_OUT_W = 2 * H_KV * CMP_HIDDEN


def _gelu_tanh(x):
    return x * (0.5 * (1.0 + jnp.tanh(math.sqrt(2.0 / math.pi) * (x + 0.044715 * (x * x * x)))))


def _compress_core(x_bf, wbig_ref, pe_ref, w2_ref):
    n_ch = x_bf.shape[0]
    p = _dot(x_bf, wbig_ref[...])
    pb = _dot(pe_ref[...].astype(BF16), wbig_ref[...])
    pre = (pb[0:1, 0:CMP_OUT_W] + pb[1:2, CMP_OUT_W:]) + p[:, 0:CMP_OUT_W] \
        + pltpu.roll(p[:, CMP_OUT_W:], n_ch - 1, 0)
    out = _dot(_gelu_tanh(pre).astype(BF16), w2_ref[...])
    row = lax.broadcasted_iota(jnp.int32, out.shape, 0)
    return jnp.where(row < n_ch - 1, out, 0.0)


def _cmp_prompt_kernel(x_ref, wbig_ref, pe_ref, w2_ref, o_ref):
    n_ch = x_ref.shape[0]
    out = _compress_core(x_ref[...].astype(BF16), wbig_ref, pe_ref, w2_ref).astype(BF16)
    o_ref[0:n_ch, :] = out
    o_ref[n_ch:2 * n_ch, :] = out


def _cmp_prompt(rows, wbig, pe8, w2bd):
    nb, n_ch, _ = rows.shape
    return pl.pallas_call(
        _cmp_prompt_kernel,
        grid=(nb,),
        in_specs=[pl.BlockSpec((None, n_ch, CHUNK_W), lambda b: (b, 0, 0)),
                  _resident(wbig.shape), _resident(pe8.shape), _resident(w2bd.shape)],
        out_specs=pl.BlockSpec((None, 2 * n_ch, KV_W), lambda b: (b, 0, 0)),
        out_shape=jax.ShapeDtypeStruct((nb, 2 * n_ch, KV_W), BF16),
        compiler_params=_cparams(("arbitrary",)),
        name="cmp_prompt",
    )(rows, wbig, pe8, w2bd)


def _page_copies(cache_ref, pt_ref, b, buf, sem, slot, n_pages, rows_per_page):
    return [pltpu.make_async_copy(cache_ref.at[pt_ref[b, p]],
                                  buf.at[slot, pl.ds(p * rows_per_page, rows_per_page)],
                                  sem.at[slot]) for p in range(n_pages)]


def _gather_pages(cache_ref, pt_ref, buf, sem, n_pages, rows_per_page):
    b = pl.program_id(0)
    nb = pl.num_programs(0)
    slot = b % 2

    @pl.when(b == 0)
    def _():
        for c in _page_copies(cache_ref, pt_ref, 0, buf, sem, 0, n_pages, rows_per_page):
            c.start()

    @pl.when(b + 1 < nb)
    def _():
        for c in _page_copies(cache_ref, pt_ref, b + 1, buf, sem, 1 - slot, n_pages, rows_per_page):
            c.start()

    for c in _page_copies(cache_ref, pt_ref, b, buf, sem, slot, n_pages, rows_per_page):
        c.wait()
    return slot


def _cmp_sample_kernel(pt_ref, cache_ref, wbig_ref, pe_ref, w2_ref, o_ref, buf, sem, *, n_pages, rpp):
    slot = _gather_pages(cache_ref, pt_ref, buf, sem, n_pages, rpp)
    o_ref[...] = _compress_core(buf[slot].astype(BF16), wbig_ref, pe_ref, w2_ref).astype(BF16)


def _cmp_sample(page_table, cache_chunks, wbig, pe8, w2bd):
    nb, n_pages = page_table.shape
    rpp = cache_chunks.shape[1]
    n_ch = n_pages * rpp
    return pl.pallas_call(
        functools.partial(_cmp_sample_kernel, n_pages=n_pages, rpp=rpp),
        grid_spec=pltpu.PrefetchScalarGridSpec(
            num_scalar_prefetch=1,
            grid=(nb,),
            in_specs=[pl.BlockSpec(memory_space=pl.ANY),
                      _resident_sp(wbig.shape), _resident_sp(pe8.shape), _resident_sp(w2bd.shape)],
            out_specs=pl.BlockSpec((None, n_ch, KV_W), lambda b, pt: (b, 0, 0)),
            scratch_shapes=[pltpu.VMEM((2, n_ch, CHUNK_W), F32), pltpu.SemaphoreType.DMA((2,))]),
        out_shape=jax.ShapeDtypeStruct((nb, n_ch, KV_W), BF16),
        compiler_params=_cparams(("arbitrary",)),
        name="cmp_sample",
    )(page_table, cache_chunks, wbig, pe8, w2bd)


def _resident_sp(shape):
    nd = len(shape)
    return pl.BlockSpec(shape, lambda *_: (0,) * nd, pipeline_mode=pl.Buffered(1))


def _bias_values(d, table):
    acc = jnp.zeros(d.shape, F32) + table(0)
    for b in range(1, N_BUCKETS):
        acc = jnp.where(d >= BUCKET_THR[b], table(b), acc)
    return jnp.where(d >= 0, acc - table(N_BUCKETS - 1), NEG)


CMP_BACK = 16


def _bias_prompt_kernel(tbl_ref, bc_ref, t0_ref, t1_ref):
    h = pl.program_id(0)
    table = lambda b: tbl_ref[b, h]
    tq, ncp = bc_ref.shape
    qi = lax.broadcasted_iota(jnp.int32, (tq, tq), 0)
    kj = lax.broadcasted_iota(jnp.int32, (tq, tq), 1)
    t0_ref[...] = _bias_values(qi - kj, table)
    t1_ref[...] = _bias_values(qi - kj + tq, table)
    qi = lax.broadcasted_iota(jnp.int32, (tq, ncp), 0)
    c = lax.broadcasted_iota(jnp.int32, (tq, ncp), 1)
    d = qi - CMP_STRIDE * c + (CMP_STRIDE * CMP_BACK - (CMP_BLOCK - 1))
    bc_ref[...] = jnp.where(c < CMP_NEAR, _bias_values(d, table), 0.0)


def _bias_prompt(rel_bias, ncp):
    tq = Q_TILE
    return pl.pallas_call(
        _bias_prompt_kernel,
        grid=(N_HEADS,),
        in_specs=[pl.BlockSpec(memory_space=pltpu.SMEM)],
        out_specs=[pl.BlockSpec((None, tq, ncp), lambda h: (h, 0, 0)),
                   pl.BlockSpec((None, tq, tq), lambda h: (h, 0, 0)),
                   pl.BlockSpec((None, tq, tq), lambda h: (h, 0, 0))],
        out_shape=[jax.ShapeDtypeStruct((N_HEADS, tq, ncp), F32),
                   jax.ShapeDtypeStruct((N_HEADS, tq, tq), F32),
                   jax.ShapeDtypeStruct((N_HEADS, tq, tq), F32)],
        compiler_params=_cparams(("arbitrary",)),
        name="bias_prompt",
    )(rel_bias)


def _bias_sample_kernel(tbl_ref, bcs_ref, bsl_ref, bnew_ref, bw_ref, *, past, s_len, n_c):
    table = lambda b: tbl_ref[:, b:b + 1]

    def qpos(shape):
        return past + lax.rem(lax.broadcasted_iota(jnp.int32, shape, 0), s_len)

    def lane(shape):
        return lax.broadcasted_iota(jnp.int32, shape, 1)

    sh = bcs_ref.shape
    n = lane(sh)
    bcs_ref[...] = jnp.where(n < n_c, _bias_values(qpos(sh) - (CMP_STRIDE * n + CMP_BLOCK - 1), table), NEG)
    sh = bsl_ref.shape
    bsl_ref[...] = _bias_values(qpos(sh) - (past - sh[1] + lane(sh)), table)
    sh = bnew_ref.shape
    bnew_ref[...] = jnp.where(lane(sh) < s_len, _bias_values(qpos(sh) - (past + lane(sh)), table), NEG)
    sh = bw_ref.shape
    d = qpos(sh) - (past - sh[1] + lane(sh))
    bw_ref[...] = jnp.where(d < WINDOW, _bias_values(d, table), NEG)


def _bias_sample(tbl_rows, past, s_len, ncs, ck, wb):
    rows = tbl_rows.shape[0]
    widths = (ncs, ck, LANES, wb)
    return pl.pallas_call(
        functools.partial(_bias_sample_kernel, past=past, s_len=s_len, n_c=(past + s_len - CMP_BLOCK) // CMP_STRIDE + 1),
        out_shape=[jax.ShapeDtypeStruct((rows, w), F32) for w in widths],
        name="bias_sample",
    )(tbl_rows)


MASK_FLOOR = -1e8
M_INIT = -1e30


def _sel_weights(v):
    return jnp.where((v >= 0) & (v <= 2), 2.0, jnp.where((v == -1) | (v == 3), 1.0, 0.0))


def _flash_rows(s_parts, v, m_ref, l_ref, a_ref):
    rows = s_parts[0].shape[0]
    ps, alphas = [], []
    for r, s in enumerate(s_parts):
        sl = slice(r * rows, (r + 1) * rows)
        m_prev = m_ref[sl, :]
        m_new = jnp.maximum(m_prev, jnp.max(s, axis=-1, keepdims=True))
        alpha = jnp.exp(m_prev - m_new)
        p = jnp.exp(s - m_new)
        l_ref[sl, :] = alpha * l_ref[sl, :] + jnp.sum(p, axis=-1, keepdims=True)
        m_ref[sl, :] = m_new
        ps.append(p.astype(BF16))
        alphas.append(alpha)
    a_ref[...] = jnp.concatenate(alphas, axis=0) * a_ref[...] + _dot(jnp.concatenate(ps, axis=0), v)


def _attn_prompt_kernel(q_ref, gate_ref, kvs_ref, kvw_ref, kc2_ref, bc_ref, t0_ref, t1_ref, o_ref,
                        sc_ref, ms_ref, ls_ref, as_ref, mw_ref, lw_ref, aw_ref, *, ncp, nsel):
    tq = Q_TILE
    i = pl.program_id(1)
    q0 = i * tq
    lane = lax.broadcasted_iota(jnp.int32, (tq, LANES), 1)
    gates = gate_ref[...]

    n0 = (tq // CMP_STRIDE) * i - CMP_BACK
    ws = pl.multiple_of(lax.rem(n0 + ncp, ncp), 8)
    cl = lax.broadcasted_iota(jnp.int32, (1, ncp), 1)
    dm = jnp.where((cl >= CMP_NEAR) & (cl < ncp - n0), NEG, 0.0) + jnp.where(cl < -n0, NEG, 0.0)
    jrow = lax.broadcasted_iota(jnp.int32, (nsel, ncp), 0)
    nn = lax.broadcasted_iota(jnp.int32, (nsel, ncp), 1) + n0
    nn = jnp.where(nn < 0, nn + ncp, nn)
    nn = jnp.where(nn >= ncp, nn - ncp, nn)
    mit = _sel_weights(nn - (SEL_BLOCK // CMP_STRIDE) * jrow).astype(BF16)

    jr = lax.broadcasted_iota(jnp.int32, (nsel, tq), 0)
    qpos = q0 + lax.broadcasted_iota(jnp.int32, (nsel, tq), 1)
    cur = lax.shift_right_logical(qpos, int(math.log2(SEL_BLOCK)))
    invalid_pen = jnp.where(jr * SEL_BLOCK <= qpos, 0.0, NEG)
    bonus = jnp.where((jr == 0) | (jr == cur) | (jr == cur - 1), FORCE_BONUS, 0.0)
    erow = lax.broadcasted_iota(jnp.int32, (LANES, tq), 0)
    eblk = lax.shift_right_logical(lax.broadcasted_iota(jnp.int32, (LANES, tq), 1), int(math.log2(SEL_BLOCK)))
    qi = lax.broadcasted_iota(jnp.int32, (tq, tq), 0)
    kj = lax.broadcasted_iota(jnp.int32, (tq, tq), 1)
    wmask = jnp.where(kj > qi, 0.0, NEG)
    blocks_per_chunk = tq // SEL_BLOCK

    results = []
    for g in range(H_KV):
        gmask = (lane < HEAD_DIM) if g == 0 else (lane >= HEAD_DIM)
        qg = jnp.concatenate(
            [jnp.where(gmask, q_ref[:, r * LANES:(r + 1) * LANES], jnp.zeros((tq, LANES), BF16))
             for r in range(GROUP)], axis=0)

        kc = kc2_ref[pl.ds(ws, ncp), 0:KV_HALF]
        vc = kc2_ref[pl.ds(ws, ncp), KV_HALF:KV_W]
        s = _dot_nt(qg, kc)
        ps, psum = [], None
        for r in range(GROUP):
            sr = s[r * tq:(r + 1) * tq] + bc_ref[g * GROUP + r] + dm
            m = jnp.maximum(jnp.max(sr, axis=-1, keepdims=True), MASK_FLOOR)
            e = jnp.exp(sr - m)
            p = e / jnp.maximum(jnp.sum(e, axis=-1, keepdims=True), 1e-30)
            ps.append(p)
            psum = p if psum is None else psum + p
        o_cmp = _dot(jnp.concatenate(ps, axis=0).astype(BF16), vc)

        h1, h2, h3 = _split3(psum)
        pslc = (_dot_nt(mit, h1) + _dot_nt(mit, h2)) + _dot_nt(mit, h3)
        score = jnp.where(invalid_pen < 0.0, NEG, pslc + bonus)
        sc_ref[...] = score

        def rank_body(jp, rank):
            row = sc_ref[pl.ds(jp, 1), :]
            ge = jnp.where(row >= score, 1.0, 0.0)
            gt = jnp.where(row > score, 1.0, 0.0)
            return rank + jnp.where(jr > jp, ge, gt)

        rank = lax.fori_loop(0, nsel, rank_body, jnp.zeros((nsel, tq), F32))
        sel_t = jnp.where(rank < min(TOP_N, nsel), 1.0, 0.0)
        if nsel < LANES:
            sel_t = jnp.concatenate([sel_t, jnp.zeros((LANES - nsel, tq), F32)], axis=0)
        sel = sel_t.T.astype(BF16)

        def sel_bias(t):
            e_t = jnp.where(erow == t * blocks_per_chunk + eblk, 1.0, 0.0).astype(BF16)
            return (_dot(sel, e_t) - 1.0) * (-NEG)

        def chunk(kv_ref, t, extra, m_ref, l_ref, a_ref):
            kv = kv_ref[pl.ds(pl.multiple_of(t * tq, tq), tq), :]
            s_all = _dot_nt(qg, kv[:, 0:KV_HALF])
            _flash_rows([s_all[r * tq:(r + 1) * tq] + extra(r) for r in range(GROUP)],
                        kv[:, KV_HALF:KV_W], m_ref, l_ref, a_ref)

        for m_ref, l_ref, a_ref in ((ms_ref, ls_ref, as_ref), (mw_ref, lw_ref, aw_ref)):
            m_ref[...] = jnp.full(m_ref.shape, M_INIT, F32)
            l_ref[...] = jnp.zeros(l_ref.shape, F32)
            a_ref[...] = jnp.zeros(a_ref.shape, F32)

        def far_body(t, carry):
            mb = sel_bias(t)
            chunk(kvs_ref, t, lambda r: mb, ms_ref, ls_ref, as_ref)
            return carry

        lax.fori_loop(0, jnp.maximum(i - 1, 0), far_body, 0)

        @pl.when(i >= 1)
        def _():
            mb = sel_bias(i - 1)
            chunk(kvs_ref, i - 1, lambda r: t1_ref[g * GROUP + r] + mb, ms_ref, ls_ref, as_ref)

        mb_d = sel_bias(i)
        chunk(kvs_ref, i, lambda r: t0_ref[g * GROUP + r] + mb_d, ms_ref, ls_ref, as_ref)

        @pl.when(i >= 2)
        def _():
            chunk(kvw_ref, i - 2, lambda r: wmask, mw_ref, lw_ref, aw_ref)

        @pl.when(i >= 1)
        def _():
            chunk(kvw_ref, i - 1, lambda r: t1_ref[g * GROUP + r], mw_ref, lw_ref, aw_ref)

        chunk(kvw_ref, i, lambda r: t0_ref[g * GROUP + r], mw_ref, lw_ref, aw_ref)

        per_r = []
        for r in range(GROUP):
            sl = slice(r * tq, (r + 1) * tq)
            col = g * GROUP + r
            o_s = as_ref[sl, :] / jnp.maximum(ls_ref[sl, :], 1e-30)
            o_w = aw_ref[sl, :] / jnp.maximum(lw_ref[sl, :], 1e-30)
            per_r.append(gates[:, col:col + 1] * o_cmp[sl]
                         + gates[:, N_HEADS + col:N_HEADS + col + 1] * o_s
                         + gates[:, 2 * N_HEADS + col:2 * N_HEADS + col + 1] * o_w)
        results.append(per_r)

    for r in range(GROUP):
        o_ref[:, r * LANES:(r + 1) * LANES] = jnp.where(
            lane < HEAD_DIM, results[0][r], results[1][r]).astype(BF16)


def _attn_prompt(q, gates, kvs_bf, kvw_bf, kc2, bias_tiles, nb, t):
    tq = Q_TILE
    assert t % tq == 0 and WINDOW == 2 * tq
    nt = t // tq
    ncp = t // CMP_STRIDE
    nsel = t // SEL_BLOCK
    assert nsel <= LANES and ncp % LANES == 0
    bc, t0, t1 = bias_tiles
    tok = lambda w: pl.BlockSpec((tq, w), lambda b, i: (b * nt + i, 0))
    per_b = lambda rows: pl.BlockSpec((None, rows, KV_W), lambda b, i: (b, 0, 0))
    rows = GROUP * tq
    return pl.pallas_call(
        functools.partial(_attn_prompt_kernel, ncp=ncp, nsel=nsel),
        grid=(nb, nt),
        in_specs=[tok(q.shape[1]), tok(LANES), per_b(t), per_b(t), per_b(2 * ncp),
                  _resident(bc.shape), _resident(t0.shape), _resident(t1.shape)],
        out_specs=tok(q.shape[1]),
        out_shape=jax.ShapeDtypeStruct(q.shape, BF16),
        scratch_shapes=[pltpu.VMEM((nsel, tq), F32),
                        pltpu.VMEM((rows, 1), F32), pltpu.VMEM((rows, 1), F32), pltpu.VMEM((rows, LANES), F32),
                        pltpu.VMEM((rows, 1), F32), pltpu.VMEM((rows, 1), F32), pltpu.VMEM((rows, LANES), F32)],
        compiler_params=_cparams(("arbitrary", "arbitrary")),
        name="attn_prompt",
    )(q, gates, kvs_bf, kvw_bf, kc2, bc, t0, t1)


SAMPLE_CK = 1024


def _flash_val(state, s, v):
    m_prev, l_prev, a_prev = state
    m_new = jnp.maximum(m_prev, jnp.max(s, axis=-1, keepdims=True))
    alpha = jnp.exp(m_prev - m_new)
    p = jnp.exp(s - m_new)
    return (m_new, alpha * l_prev + jnp.sum(p, axis=-1, keepdims=True),
            alpha * a_prev + _dot(p.astype(BF16), v))


def _attn_sample_kernel(pt_ref, q_ref, g_ref, knew_ref, wnew_ref, win_ref, kc_ref,
                        bcs_ref, bsl_ref, bnew_ref, bw_ref, msel_ref, cache_ref, o_ref, buf, sem,
                        *, n_pages, rpp, past, s_len, nsel):
    slot = _gather_pages(cache_ref, pt_ref, buf, sem, n_pages, rpp)
    q = q_ref[...]
    rows = q.shape[0]
    gs = rows // GROUP
    nselp = msel_ref.shape[1]

    s = _dot_nt(q, kc_ref[:, 0:KV_HALF]) + bcs_ref[...]
    m = jnp.maximum(jnp.max(s, axis=-1, keepdims=True), MASK_FLOOR)
    e = jnp.exp(s - m)
    p = e / jnp.maximum(jnp.sum(e, axis=-1, keepdims=True), 1e-30)
    o_cmp = _dot(p.astype(BF16), kc_ref[:, KV_HALF:KV_W])
    psum = p[0:gs]
    for r in range(1, GROUP):
        psum = psum + p[r * gs:(r + 1) * gs]

    h1, h2, h3 = _split3(psum)
    msel = msel_ref[...]
    pslc = (_dot(h1, msel) + _dot(h2, msel)) + _dot(h3, msel)
    j = lax.broadcasted_iota(jnp.int32, (gs, nselp), 1)
    qpos = past + lax.rem(lax.broadcasted_iota(jnp.int32, (gs, nselp), 0), s_len)
    cur = lax.shift_right_logical(qpos, int(math.log2(SEL_BLOCK)))
    valid = j * SEL_BLOCK <= qpos
    forced = (j == 0) | (j == cur) | (j == cur - 1)
    score = jnp.where(valid, pslc + jnp.where(forced, FORCE_BONUS, 0.0), NEG)
    score = jnp.where(j < nsel, score, -3e38)
    score_t = jnp.concatenate([score, jnp.zeros((LANES - gs, nselp), F32)], axis=0).T
    jp = lax.broadcasted_iota(jnp.int32, (nselp, nselp), 0)
    jj = lax.broadcasted_iota(jnp.int32, (nselp, nselp), 1)
    sel_rows = []
    for r in range(gs):
        col = score_t[:, r:r + 1]
        row = score[r:r + 1, :]
        beats = jnp.where(jj > jp, jnp.where(col >= row, 1.0, 0.0), jnp.where(col > row, 1.0, 0.0))
        rank = jnp.sum(beats, axis=0, keepdims=True)
        sel_rows.append(jnp.where(rank < min(TOP_N, nsel), 1.0, 0.0))
    sel8 = jnp.concatenate(sel_rows, axis=0)
    sel = jnp.concatenate([sel8] * GROUP, axis=0).astype(BF16)

    ck = bsl_ref.shape[1]
    n_chunks = past // ck
    erow = lax.broadcasted_iota(jnp.int32, (nselp, ck), 0)
    eblk = lax.shift_right_logical(lax.broadcasted_iota(jnp.int32, (nselp, ck), 1), int(math.log2(SEL_BLOCK)))
    init = (jnp.full((rows, 1), M_INIT, F32), jnp.zeros((rows, 1), F32), jnp.zeros((rows, LANES), F32))
    st = init
    for t in range(n_chunks):
        kv = buf[slot, t * ck:(t + 1) * ck, :].astype(BF16)
        e_t = jnp.where(erow == t * (ck // SEL_BLOCK) + eblk, 1.0, 0.0).astype(BF16)
        sc = _dot_nt(q, kv[:, 0:KV_HALF]) + (_dot(sel, e_t) - 1.0) * (-NEG)
        if t == n_chunks - 1:
            sc = sc + bsl_ref[...]
        st = _flash_val(st, sc, kv[:, KV_HALF:KV_W])
    knew = knew_ref[...]
    st = _flash_val(st, _dot_nt(q, knew[:, 0:KV_HALF]) + bnew_ref[...], knew[:, KV_HALF:KV_W])
    o_sel = st[2] / jnp.maximum(st[1], 1e-30)

    kw = win_ref[...].astype(BF16)
    st = _flash_val(init, _dot_nt(q, kw[:, 0:KV_HALF]) + bw_ref[...], kw[:, KV_HALF:KV_W])
    wnew = wnew_ref[...]
    st = _flash_val(st, _dot_nt(q, wnew[:, 0:KV_HALF]) + bnew_ref[...], wnew[:, KV_HALF:KV_W])
    o_win = st[2] / jnp.maximum(st[1], 1e-30)

    g = g_ref[...]
    o_ref[...] = g[:, 0:1] * o_cmp + g[:, 1:2] * o_sel + g[:, 2:3] * o_win


def _attn_sample(page_table, q32, g32, knew, wnew, win, kvc, tiles, msel, cache_rows, past, s_len):
    nb, n_pages = page_table.shape
    rpp = cache_rows.shape[1]
    rows = q32.shape[1]
    nsel = -(-(past + s_len) // SEL_BLOCK)
    bcs, bsl, bnew, bw = tiles
    per_b = lambda r, w: pl.BlockSpec((None, r, w), lambda b, pt: (b, 0, 0))
    return pl.pallas_call(
        functools.partial(_attn_sample_kernel, n_pages=n_pages, rpp=rpp, past=past, s_len=s_len, nsel=nsel),
        grid_spec=pltpu.PrefetchScalarGridSpec(
            num_scalar_prefetch=1,
            grid=(nb,),
            in_specs=[per_b(rows, LANES), per_b(rows, LANES), per_b(LANES, KV_W), per_b(LANES, KV_W),
                      per_b(win.shape[1], KV_W), per_b(kvc.shape[1], KV_W),
                      _resident_sp(bcs.shape), _resident_sp(bsl.shape), _resident_sp(bnew.shape),
                      _resident_sp(bw.shape), _resident_sp(msel.shape),
                      pl.BlockSpec(memory_space=pl.ANY)],
            out_specs=per_b(rows, LANES),
            scratch_shapes=[pltpu.VMEM((2, past, KV_W), F32), pltpu.SemaphoreType.DMA((2,))]),
        out_shape=jax.ShapeDtypeStruct((nb, rows, LANES), F32),
        compiler_params=_cparams(("arbitrary",)),
        name="attn_sample",
    )(page_table, q32, g32, knew, wnew, win, kvc, bcs, bsl, bnew, bw, msel, cache_rows)


def _prep_weights(w_in, w_cmp1, w_cmp2, pe_cmp, w_conv, w_nsa_out):
    d = w_in.shape[0]
    sizes = (N_HEADS * HEAD_DIM, KV_W, KV_W, KV_W, 3 * N_HEADS, _C_W, _C_W, _C_W, 2 * d)
    offs = [0]
    for sz in sizes:
        offs.append(offs[-1] + sz)
    part = lambda k: w_in[:, offs[k]:offs[k + 1]]
    qp = part(0).reshape(d, H_KV, GROUP, HEAD_DIM).transpose(0, 2, 1, 3).reshape(d, N_HEADS * HEAD_DIM)
    gn = jnp.pad(part(4), ((0, 0), (0, _G_W - 3 * N_HEADS)))
    w_in_p = jnp.concatenate([qp, part(1), part(2), part(3), gn, part(5), part(6), part(7), part(8)],
                             axis=1).astype(BF16)
    w_nsa_p = w_nsa_out.reshape(H_KV, GROUP, HEAD_DIM, -1).transpose(1, 0, 2, 3).reshape(
        N_HEADS * HEAD_DIM, -1).astype(BF16)
    r = CMP_BLOCK // CMP_STRIDE
    w1r = w_cmp1.reshape(2, r, CMP_STRIDE, HEAD_DIM, CMP_HIDDEN)
    wb = jnp.zeros((CMP_STRIDE, 2, H_KV, HEAD_DIM, r, 2, H_KV, CMP_HIDDEN), F32)
    w2b = jnp.zeros((2, H_KV, CMP_HIDDEN, 2, H_KV, HEAD_DIM), F32)
    for c in range(2):
        for g in range(H_KV):
            wb = wb.at[:, c, g, :, :, c, g, :].set(jnp.transpose(w1r[c], (1, 2, 0, 3)))
            w2b = w2b.at[c, g, :, c, g, :].set(w_cmp2[c])
    wbig = wb.reshape(CHUNK_W, r * CMP_OUT_W).astype(BF16)
    w2bd = w2b.reshape(CMP_OUT_W, KV_W).astype(BF16)
    pe_r = pe_cmp.reshape(2, r, CMP_STRIDE, HEAD_DIM).transpose(1, 2, 0, 3)
    pe_rows = jnp.broadcast_to(pe_r[:, :, :, None, :], (r, CMP_STRIDE, 2, H_KV, HEAD_DIM)).reshape(r, CHUNK_W)
    pe8 = jnp.pad(pe_rows, ((0, 8 - r), (0, 0)))
    w_conv8 = jnp.pad(w_conv, ((0, 8 - CONV_WIDTH), (0, 0)))
    return w_in_p, w_nsa_p, wbig, w2bd, pe8, w_conv8


def _sel_matrix(n_c, nselp):
    n = jnp.arange(n_c + 1)[:, None]
    j = jnp.arange(nselp)[None, :]
    return _sel_weights(n - (SEL_BLOCK // CMP_STRIDE) * j).astype(BF16)


def kernel(x_prompt, x_sample, cache_kv_cmp, cache_kv_sel, state_kv_win, state_conv, page_table,
           c_prompt, c_sample, w_ada, b_ada, g_norm, w_ffn1_gu, w_ffn1_down, w_ffn2_gu, w_ffn2_down,
           w_in, w_cmp1, w_cmp2, pe_cmp, w_conv, w_nsa_out, w_conv_out, w_out, rel_bias, g_final):
    assert w_ada.shape[0] == 1, "single-layer trunk"
    nbp, t, d = x_prompt.shape
    nbs, s_len, _ = x_sample.shape
    n_pages = page_table.shape[1]
    page = cache_kv_cmp.shape[2]
    past = n_pages * page
    n_phys = cache_kv_cmp.shape[1]
    wb = state_kv_win.shape[2]
    assert wb == WINDOW and past % SAMPLE_CK == 0

    w_in_p, w_nsa_p, wbig, w2bd, pe8, w_conv8 = _prep_weights(
        w_in[0], w_cmp1[0], w_cmp2[0], pe_cmp[0], w_conv[0], w_nsa_out[0])
    wgu1, wd1 = w_ffn1_gu[0].astype(BF16), w_ffn1_down[0].astype(BF16)
    wgu2, wd2 = w_ffn2_gu[0].astype(BF16), w_ffn2_down[0].astype(BF16)
    w_cv, w_o = w_conv_out[0].astype(BF16), w_out[0].astype(BF16)
    gn = [g_norm[0][k:k + 1] for k in range(N_SUB)]

    n_c_rows = nbp + nbs
    c_all = jnp.pad(jnp.concatenate([c_prompt, c_sample], axis=0), ((0, (-n_c_rows) % 8), (0, 0)))
    mod_all = _ada(c_all, w_ada[0], b_ada[0])
    mod_p = mod_all[:nbp].reshape(nbp * 3 * N_SUB, 1, d)
    mod_s = jnp.transpose(jnp.repeat(mod_all[nbp:n_c_rows].reshape(nbs, 3 * N_SUB, d), s_len, axis=0), (1, 0, 2))

    tm = 512 if t % 512 == 0 else t
    tpb = t // tm
    xp = x_prompt.reshape(nbp * t, d)
    h1 = _ffn(xp, mod_p, 0, gn[0], wgu1, wd1, tm, tpb)
    q, kvc, kvs, kvw, kvs_bf, kvw_bf, gates, z, cb, gm = _inproj(h1, mod_p, gn[1], w_in_p, tm, tpb)
    kc2 = _cmp_prompt(kvc.reshape(nbp, t // CMP_STRIDE, CHUNK_W), wbig, pe8, w2bd)
    tiles_p = _bias_prompt(rel_bias, t // CMP_STRIDE)
    o_nsa = _attn_prompt(q, gates, kvs_bf.reshape(nbp, t, KV_W), kvw_bf.reshape(nbp, t, KV_W), kc2, tiles_p, nbp, t)
    h2 = _mixout(h1, o_nsa, z, None, cb, gm, mod_p, w_conv8, w_nsa_p, w_cv, w_o, tm, tpb)
    y_prompt = _ffn(h2, mod_p, 2, gn[2], wgu2, wd2, tm, tpb, g_final).reshape(nbp, t, d)

    kv_shape = (2, H_KV, HEAD_DIM)
    kv_cmp_p = kvc.reshape((1, nbp, t) + kv_shape)
    kv_sel_p = kvs.reshape((1, nbp, t) + kv_shape)
    keep = min(WINDOW, t)
    kv_win_p = kvw.reshape((1, nbp, t) + kv_shape)[:, :, t - keep:]
    conv_p = z.reshape(1, nbp, t, -1)[:, :, t - (CONV_WIDTH - 1):]

    ns = nbs * s_len
    xs = x_sample.reshape(ns, d)
    h1s = _ffn(xs, mod_s, 0, gn[0], wgu1, wd1, ns, 1)
    qs, kvc_s, kvs_s, kvw_s, kvs_sb, kvw_sb, gates_s, z_s, cb_s, gm_s = _inproj(h1s, mod_s, gn[1], w_in_p, ns, 1)

    kvc_past = _cmp_sample(page_table, cache_kv_cmp[0].reshape(n_phys, page // CMP_STRIDE, CHUNK_W),
                           wbig, pe8, w2bd)
    n_c = (past + s_len - CMP_BLOCK) // CMP_STRIDE + 1
    nsel = -(-(past + s_len) // SEL_BLOCK)
    nselp = -(-nsel // LANES) * LANES
    rows = N_HEADS * s_len
    head_of_row = [g * GROUP + r for r in range(GROUP) for g in range(H_KV) for _ in range(s_len)]
    tbl_rows = jnp.pad(rel_bias.T[jnp.array(head_of_row)], ((0, 0), (0, LANES - N_BUCKETS)))
    tiles_s = _bias_sample(tbl_rows, past, s_len, past // CMP_STRIDE, SAMPLE_CK, wb)
    msel = _sel_matrix(past // CMP_STRIDE - 1, nselp)

    q5 = qs.reshape(nbs, s_len, GROUP, H_KV, HEAD_DIM).transpose(0, 2, 3, 1, 4)
    q32 = jnp.zeros((nbs, GROUP, H_KV, s_len, H_KV, HEAD_DIM), BF16)
    for g in range(H_KV):
        q32 = q32.at[:, :, g, :, g, :].set(q5[:, :, g])
    q32 = q32.reshape(nbs, rows, LANES)
    g5 = gates_s[:, :3 * N_HEADS].reshape(nbs, s_len, 3, H_KV, GROUP).transpose(0, 4, 3, 1, 2)
    g32 = jnp.pad(g5.reshape(nbs, rows, 3), ((0, 0), (0, 0), (0, LANES - 3)))
    pad_new = lambda a: jnp.pad(a.reshape(nbs, s_len, KV_W), ((0, 0), (0, LANES - s_len), (0, 0)))
    o32 = _attn_sample(page_table, q32, g32, pad_new(kvs_sb), pad_new(kvw_sb),
                       state_kv_win[0].reshape(nbs, wb, KV_W), kvc_past, tiles_s, msel,
                       cache_kv_sel[0].reshape(n_phys, page, KV_W), past, s_len)
    o6 = o32.reshape(nbs, GROUP, H_KV, s_len, H_KV, HEAD_DIM)
    o_s = jnp.stack([o6[:, :, g, :, g, :] for g in range(H_KV)], axis=3)
    o_nsa_s = o_s.transpose(0, 2, 1, 3, 4).reshape(ns, N_HEADS * HEAD_DIM).astype(BF16)

    full = jnp.concatenate([state_conv[0], z_s.reshape(nbs, s_len, -1)], axis=1)
    z_shift = (full[:, 1:1 + s_len].reshape(ns, -1), full[:, 0:s_len].reshape(ns, -1))
    h2s = _mixout(h1s, o_nsa_s, z_s, z_shift, cb_s, gm_s, mod_s, w_conv8, w_nsa_p, w_cv, w_o, ns, 1)
    y_sample = _ffn(h2s, mod_s, 2, gn[2], wgu2, wd2, ns, 1, g_final).reshape(nbs, s_len, d)

    kv_cmp_s = kvc_s.reshape((1, nbs, s_len) + kv_shape)
    kv_sel_s = kvs_s.reshape((1, nbs, s_len) + kv_shape)
    win_full = jnp.concatenate([state_kv_win[0], kvw_s.reshape((nbs, s_len) + kv_shape)], axis=1)
    keep_s = min(WINDOW, wb + s_len)
    kv_win_s = win_full[None, :, wb + s_len - keep_s:]
    conv_s = full[None, :, s_len:]
    return (y_prompt, y_sample, kv_cmp_p, kv_sel_p, kv_win_p, conv_p, kv_cmp_s, kv_sel_s, kv_win_s, conv_s)
```

```python
import functools
import math

import jax
import jax.numpy as jnp
from jax import lax
from jax.experimental import pallas as pl
from jax.experimental.pallas import tpu as pltpu

F32 = jnp.float32
BF16 = jnp.bfloat16

HEAD_DIM = 64
N_HEADS = 8
H_KV = 2
GROUP = N_HEADS // H_KV
CMP_BLOCK = 32
CMP_STRIDE = 16
CMP_HIDDEN = 2 * HEAD_DIM
SEL_BLOCK = 64
TOP_N = 16
WINDOW = 512
CONV_WIDTH = 3
N_BUCKETS = 32
MAX_DISTANCE = 128
N_SUB = 3
EPS = 1e-6
NEG = -1e9
FORCE_BONUS = 1e3
KV_W = 2 * H_KV * HEAD_DIM
KV_HALF = H_KV * HEAD_DIM
LANES = 128
Q_TILE = 256
CMP_NEAR = 32
VMEM_LIMIT = 56 * 1024 * 1024


def _bucket_thresholds():
    max_exact = N_BUCKETS // 2

    def bucket(d):
        if d < max_exact:
            return d
        large = max_exact + int(math.log(d / max_exact) / math.log(MAX_DISTANCE / max_exact)
                                * (N_BUCKETS - max_exact))
        return min(large, N_BUCKETS - 1)

    thr, d = [], 0
    for b in range(N_BUCKETS):
        while bucket(d) < b:
            d += 1
        thr.append(d)
    return tuple(thr)


BUCKET_THR = _bucket_thresholds()
FAR_DIST = BUCKET_THR[-1]


def _cparams(sem):
    return pltpu.CompilerParams(dimension_semantics=sem, vmem_limit_bytes=VMEM_LIMIT)


def _resident(shape):
    nd = len(shape)
    return pl.BlockSpec(shape, lambda *_: (0,) * nd, pipeline_mode=pl.Buffered(1))


def _dot(a, b):
    return jnp.dot(a, b, preferred_element_type=F32)


def _dot_nt(a, b):
    return lax.dot_general(a, b, (((1,), (1,)), ((), ())), preferred_element_type=F32)


def _split3(x):
    h1 = x.astype(BF16)
    r1 = x - h1.astype(F32)
    h2 = r1.astype(BF16)
    h3 = (r1 - h2.astype(F32)).astype(BF16)
    return h1, h2, h3


def _modulated_norm(x, g, shift, scale):
    y = x * lax.rsqrt(jnp.mean(x * x, axis=-1, keepdims=True) + EPS)
    return (y * g) * (1.0 + scale) + shift


def _ada_kernel(c_ref, w_ref, b_ref, o_ref):
    c = c_ref[...]
    a = (c * jax.nn.sigmoid(c)).astype(BF16)
    o_ref[...] = _dot(a, w_ref[...].astype(BF16)) + b_ref[...]


def _ada(c_all, w_ada, b_ada):
    rows, d = c_all.shape
    n = w_ada.shape[1]
    tn = n // 8
    return pl.pallas_call(
        _ada_kernel,
        grid=(n // tn,),
        in_specs=[pl.BlockSpec((rows, d), lambda j: (0, 0)),
                  pl.BlockSpec((d, tn), lambda j: (0, j)),
                  pl.BlockSpec((1, tn), lambda j: (0, j))],
        out_specs=pl.BlockSpec((rows, tn), lambda j: (0, j)),
        out_shape=jax.ShapeDtypeStruct((rows, n), F32),
        compiler_params=_cparams(("arbitrary",)),
        name="ada",
    )(c_all, w_ada, b_ada.reshape(1, n))


def _ffn_kernel(*refs, d_ff, fc, final_norm):
    if final_norm:
        x_ref, sh_ref, sc_ref, gt_ref, gn_ref, wgu_ref, wd_ref, gf_ref, o_ref = refs
    else:
        x_ref, sh_ref, sc_ref, gt_ref, gn_ref, wgu_ref, wd_ref, o_ref = refs
    x = x_ref[...]
    ub = _modulated_norm(x, gn_ref[...], sh_ref[...], sc_ref[...]).astype(BF16)
    acc = jnp.zeros(x.shape, F32)
    for c in range(d_ff // fc):
        g = _dot(ub, wgu_ref[:, c * fc:(c + 1) * fc])
        v = _dot(ub, wgu_ref[:, d_ff + c * fc:d_ff + (c + 1) * fc])
        a = ((g * jax.nn.sigmoid(g)) * v).astype(BF16)
        acc = acc + _dot(a, wd_ref[c * fc:(c + 1) * fc, :])
    h = x + (0.5 * gt_ref[...]) * acc
    if final_norm:
        h = (h * lax.rsqrt(jnp.mean(h * h, axis=-1, keepdims=True) + EPS)) * gf_ref[...]
    o_ref[...] = h


def _mod_specs(mod, ks, tm, tiles_per_batch):
    if mod.ndim == 3 and mod.shape[1] == 1:
        d = mod.shape[-1]
        specs = [pl.BlockSpec((None, 1, d), lambda i, k=k: ((i // tiles_per_batch) * (3 * N_SUB) + k, 0, 0))
                 for k in ks]
        return specs, [mod] * len(ks)
    d = mod.shape[-1]
    specs = [pl.BlockSpec((None, tm, d), lambda i, k=k: (k, i, 0)) for k in ks]
    return specs, [mod] * len(ks)


def _ffn(x, mod, sub, gn_row, w_gu, w_down, tm, tiles_per_batch, g_final=None):
    n, d = x.shape
    d_ff = w_down.shape[0]
    fc = d_ff // 2 if (d_ff // 2) % LANES == 0 else d_ff
    final_norm = g_final is not None
    mspecs, mops = _mod_specs(mod, (3 * sub, 3 * sub + 1, 3 * sub + 2), tm, tiles_per_batch)
    in_specs = [pl.BlockSpec((tm, d), lambda i: (i, 0))] + mspecs + [
        pl.BlockSpec((1, d), lambda i: (0, 0)), _resident(w_gu.shape), _resident(w_down.shape)]
    ops = [x] + mops + [gn_row, w_gu, w_down]
    if final_norm:
        in_specs.append(pl.BlockSpec((1, d), lambda i: (0, 0)))
        ops.append(g_final.reshape(1, d))
    return pl.pallas_call(
        functools.partial(_ffn_kernel, d_ff=d_ff, fc=fc, final_norm=final_norm),
        grid=(n // tm,),
        in_specs=in_specs,
        out_specs=pl.BlockSpec((tm, d), lambda i: (i, 0)),
        out_shape=jax.ShapeDtypeStruct((n, d), F32),
        compiler_params=_cparams(("arbitrary",)),
        name="ffn_final" if final_norm else "ffn",
    )(*ops)


_Q_W = N_HEADS * HEAD_DIM
_G_W = LANES
_C_W = 512
_SEG = {}
_off = 0
for _name, _w in (("q", _Q_W), ("kc", KV_W), ("ks", KV_W), ("kw", KV_W), ("gn", _G_W),
                  ("ch", _C_W), ("cb", _C_W), ("cc", _C_W)):
    _SEG[_name] = (_off, _off + _w)
    _off += _w
_MG_OFF = _off


def _inproj_kernel(*refs, prompt):
    x_ref, sh_ref, sc_ref, gn_ref, w_ref = refs[:5]
    ub = _modulated_norm(x_ref[...], gn_ref[...], sh_ref[...], sc_ref[...]).astype(BF16)

    def seg(name):
        lo, hi = _SEG[name]
        return _dot(ub, w_ref[:, lo:hi])

    if prompt:
        (q_ref, kc_ref, kct_ref, kst_ref, kwt_ref, ksk_ref, vst_ref, kwk_ref, vwt_ref,
         g_ref, z_ref, cb_ref, gm_ref) = refs[5:]
        kc = seg("kc")
        kc_ref[...] = kc
        kct_ref[...] = kc.T
        for name, t_ref, k_ref, vt_ref in (("ks", kst_ref, ksk_ref, vst_ref), ("kw", kwt_ref, kwk_ref, vwt_ref)):
            kv = seg(name)
            kv_t = kv.T
            t_ref[...] = kv_t
            k_ref[...] = kv[:, 0:KV_HALF].astype(BF16)
            for c in range(vt_ref.shape[0]):
                vt_ref[c] = kv_t[KV_HALF:KV_W, c * Q_TILE:(c + 1) * Q_TILE].astype(BF16)
    else:
        q_ref, kc_ref, ks_ref, kw_ref, ksb_ref, kwb_ref, g_ref, z_ref, cb_ref, gm_ref = refs[5:]
        kc_ref[...] = seg("kc")
        ks = seg("ks")
        ks_ref[...] = ks
        ksb_ref[...] = ks.astype(BF16)
        kw = seg("kw")
        kw_ref[...] = kw
        kwb_ref[...] = kw.astype(BF16)
    q_ref[...] = (seg("q") * (HEAD_DIM ** -0.5)).astype(BF16)
    g_ref[...] = jax.nn.sigmoid(seg("gn"))
    z_ref[...] = seg("cc") * seg("ch")
    cb_ref[...] = seg("cb")
    d2 = gm_ref.shape[1]
    half = d2 // 2
    for c in range(2):
        gm_ref[:, c * half:(c + 1) * half] = jax.nn.sigmoid(
            _dot(ub, w_ref[:, _MG_OFF + c * half:_MG_OFF + (c + 1) * half]))


def _inproj(h, mod, gn_row, w_in_p, tm, tiles_per_batch, prompt):
    n, d = h.shape
    d_conv = _C_W
    mspecs, mops = _mod_specs(mod, (3, 4), tm, tiles_per_batch)
    rows = lambda w, dt: (pl.BlockSpec((tm, w), lambda i: (i, 0)), jax.ShapeDtypeStruct((n, w), dt))
    tail = [rows(_G_W, F32), rows(d_conv, F32), rows(d_conv, F32), rows(2 * d, F32)]
    if prompt:
        nb = n // (tm * tiles_per_batch)
        t = tm * tiles_per_batch
        cpt = tm // Q_TILE
        tr = (pl.BlockSpec((None, KV_W, tm), lambda i: (i // tiles_per_batch, 0, i % tiles_per_batch)),
              jax.ShapeDtypeStruct((nb, KV_W, t), F32))
        vt = (pl.BlockSpec((cpt, KV_HALF, Q_TILE), lambda i: (i, 0, 0)),
              jax.ShapeDtypeStruct((n // Q_TILE, KV_HALF, Q_TILE), BF16))
        outs = [rows(_Q_W, BF16), rows(KV_W, F32), tr, tr, tr,
                rows(KV_HALF, BF16), vt, rows(KV_HALF, BF16), vt] + tail
    else:
        outs = [rows(_Q_W, BF16), rows(KV_W, F32), rows(KV_W, F32), rows(KV_W, F32),
                rows(KV_W, BF16), rows(KV_W, BF16)] + tail
    return pl.pallas_call(
        functools.partial(_inproj_kernel, prompt=prompt),
        grid=(n // tm,),
        in_specs=[pl.BlockSpec((tm, d), lambda i: (i, 0))] + mspecs + [
            pl.BlockSpec((1, d), lambda i: (0, 0)), _resident(w_in_p.shape)],
        out_specs=[o[0] for o in outs],
        out_shape=[o[1] for o in outs],
        compiler_params=_cparams(("arbitrary",)),
        name="inproj_prompt" if prompt else "inproj",
    )(h, *mops, gn_row, w_in_p)


def _mixout_kernel(*refs, halo, tiles_per_batch):
    if halo:
        (h_ref, o_ref, z_ref, zp_ref, cb_ref, gm_ref, g2_ref, wc_ref,
         wn_ref, wcv_ref, wo_ref, out_ref) = refs
        z = z_ref[...]
        tm = z.shape[0]
        first = (pl.program_id(0) % tiles_per_batch) == 0
        prev = jnp.where(first, 0.0, zp_ref[...])
        row = lax.broadcasted_iota(jnp.int32, z.shape, 0)
        zm1 = jnp.where(row == 0, prev[7:8, :], pltpu.roll(z, 1, 0))
        zm2 = jnp.where(row == 0, prev[6:7, :], jnp.where(row == 1, prev[7:8, :], pltpu.roll(z, 2, 0)))
    else:
        (h_ref, o_ref, z_ref, zm1_ref, zm2_ref, cb_ref, gm_ref, g2_ref, wc_ref,
         wn_ref, wcv_ref, wo_ref, out_ref) = refs
        z, zm1, zm2 = z_ref[...], zm1_ref[...], zm2_ref[...]
    conv = wc_ref[0:1, :] * zm2 + wc_ref[1:2, :] * zm1 + wc_ref[2:3, :] * z
    y = (cb_ref[...] * conv).astype(BF16)
    d = h_ref.shape[1]
    merged = gm_ref[:, 0:d] * _dot(o_ref[...], wn_ref[...]) + gm_ref[:, d:2 * d] * _dot(y, wcv_ref[...])
    out_ref[...] = h_ref[...] + g2_ref[...] * _dot(merged.astype(BF16), wo_ref[...])


def _mixout(h, o_nsa, z, z_shift, cb, gm, mod, w_conv8, w_nsa_p, w_cv, w_o, tm, tiles_per_batch):
    n, d = h.shape
    dc = z.shape[1]
    halo = z_shift is None
    mspecs, mops = _mod_specs(mod, (5,), tm, tiles_per_batch)
    tok = lambda w: pl.BlockSpec((tm, w), lambda i: (i, 0))
    if halo:
        zspecs = [tok(dc), pl.BlockSpec((8, dc), lambda i: (jnp.maximum(i * (tm // 8) - 1, 0), 0))]
        zops = [z, z]
    else:
        zspecs = [tok(dc), tok(dc), tok(dc)]
        zops = [z, z_shift[0], z_shift[1]]
    return pl.pallas_call(
        functools.partial(_mixout_kernel, halo=halo, tiles_per_batch=tiles_per_batch),
        grid=(n // tm,),
        in_specs=[tok(d), tok(o_nsa.shape[1])] + zspecs + [tok(dc), tok(2 * d)] + mspecs + [
            pl.BlockSpec((8, dc), lambda i: (0, 0)),
            _resident(w_nsa_p.shape), _resident(w_cv.shape), _resident(w_o.shape)],
        out_specs=tok(d),
        out_shape=jax.ShapeDtypeStruct((n, d), F32),
        compiler_params=_cparams(("arbitrary",)),
        name="mixout",
    )(h, o_nsa, *zops, cb, gm, *mops, w_conv8, w_nsa_p, w_cv, w_o)


CHUNK_W = CMP_STRIDE * KV_W
CMP_OUT_W = 2 * H_KV * CMP_HIDDEN


def _gelu_tanh(x):
    return x * (0.5 * (1.0 + jnp.tanh(math.sqrt(2.0 / math.pi) * (x + 0.044715 * (x * x * x)))))


def _compress_core(x_bf, wbig_ref, pe_ref, w2_ref):
    n_ch = x_bf.shape[0]
    p = _dot(x_bf, wbig_ref[...])
    pb = _dot(pe_ref[...].astype(BF16), wbig_ref[...])
    pre = (pb[0:1, 0:CMP_OUT_W] + pb[1:2, CMP_OUT_W:]) + p[:, 0:CMP_OUT_W] \
        + pltpu.roll(p[:, CMP_OUT_W:], n_ch - 1, 0)
    out = _dot(_gelu_tanh(pre).astype(BF16), w2_ref[...])
    row = lax.broadcasted_iota(jnp.int32, out.shape, 0)
    return jnp.where(row < n_ch - 1, out, 0.0)


def _cmp_prompt_kernel(x_ref, wbig_ref, pe_ref, w2_ref, o_ref):
    n_ch = x_ref.shape[0]
    out = _compress_core(x_ref[...].astype(BF16), wbig_ref, pe_ref, w2_ref).astype(BF16)
    o_ref[0:n_ch, :] = out
    o_ref[n_ch:2 * n_ch, :] = out


def _cmp_prompt(rows, wbig, pe8, w2bd):
    nb, n_ch, _ = rows.shape
    return pl.pallas_call(
        _cmp_prompt_kernel,
        grid=(nb,),
        in_specs=[pl.BlockSpec((None, n_ch, CHUNK_W), lambda b: (b, 0, 0)),
                  _resident(wbig.shape), _resident(pe8.shape), _resident(w2bd.shape)],
        out_specs=pl.BlockSpec((None, 2 * n_ch, KV_W), lambda b: (b, 0, 0)),
        out_shape=jax.ShapeDtypeStruct((nb, 2 * n_ch, KV_W), BF16),
        compiler_params=_cparams(("arbitrary",)),
        name="cmp_prompt",
    )(rows, wbig, pe8, w2bd)


def _page_copies(cache_ref, pt_ref, b, buf, sem, slot, n_pages, rows_per_page):
    return [pltpu.make_async_copy(cache_ref.at[pt_ref[b, p]],
                                  buf.at[slot, pl.ds(p * rows_per_page, rows_per_page)],
                                  sem.at[slot]) for p in range(n_pages)]


def _gather_pages(cache_ref, pt_ref, buf, sem, n_pages, rows_per_page):
    b = pl.program_id(0)
    nb = pl.num_programs(0)
    slot = b % 2

    @pl.when(b == 0)
    def _():
        for c in _page_copies(cache_ref, pt_ref, 0, buf, sem, 0, n_pages, rows_per_page):
            c.start()

    @pl.when(b + 1 < nb)
    def _():
        for c in _page_copies(cache_ref, pt_ref, b + 1, buf, sem, 1 - slot, n_pages, rows_per_page):
            c.start()

    for c in _page_copies(cache_ref, pt_ref, b, buf, sem, slot, n_pages, rows_per_page):
        c.wait()
    return slot


def _cmp_sample_kernel(pt_ref, cache_ref, wbig_ref, pe_ref, w2_ref, o_ref, buf, sem, *, n_pages, rpp):
    slot = _gather_pages(cache_ref, pt_ref, buf, sem, n_pages, rpp)
    o_ref[...] = _compress_core(buf[slot].astype(BF16), wbig_ref, pe_ref, w2_ref).astype(BF16)


def _cmp_sample(page_table, cache_chunks, wbig, pe8, w2bd):
    nb, n_pages = page_table.shape
    rpp = cache_chunks.shape[1]
    n_ch = n_pages * rpp
    return pl.pallas_call(
        functools.partial(_cmp_sample_kernel, n_pages=n_pages, rpp=rpp),
        grid_spec=pltpu.PrefetchScalarGridSpec(
            num_scalar_prefetch=1,
            grid=(nb,),
            in_specs=[pl.BlockSpec(memory_space=pl.ANY),
                      _resident_sp(wbig.shape), _resident_sp(pe8.shape), _resident_sp(w2bd.shape)],
            out_specs=pl.BlockSpec((None, n_ch, KV_W), lambda b, pt: (b, 0, 0)),
            scratch_shapes=[pltpu.VMEM((2, n_ch, CHUNK_W), F32), pltpu.SemaphoreType.DMA((2,))]),
        out_shape=jax.ShapeDtypeStruct((nb, n_ch, KV_W), BF16),
        compiler_params=_cparams(("arbitrary",)),
        name="cmp_sample",
    )(page_table, cache_chunks, wbig, pe8, w2bd)


def _resident_sp(shape):
    nd = len(shape)
    return pl.BlockSpec(shape, lambda *_: (0,) * nd, pipeline_mode=pl.Buffered(1))


def _bias_values(d, table):
    acc = jnp.zeros(d.shape, F32) + table(0)
    for b in range(1, N_BUCKETS):
        acc = jnp.where(d >= BUCKET_THR[b], table(b), acc)
    return jnp.where(d >= 0, acc - table(N_BUCKETS - 1), NEG)


CMP_BACK = 16


def _bias_prompt_kernel(tbl_ref, bc_ref, t0_ref, t1_ref):
    h = pl.program_id(0)
    table = lambda b: tbl_ref[b, h]
    tq, ncp = bc_ref.shape
    kj = lax.broadcasted_iota(jnp.int32, (tq, tq), 0)
    qi = lax.broadcasted_iota(jnp.int32, (tq, tq), 1)
    t0_ref[...] = _bias_values(qi - kj, table)
    t1_ref[...] = _bias_values(qi - kj + tq, table)
    qi = lax.broadcasted_iota(jnp.int32, (tq, ncp), 0)
    c = lax.broadcasted_iota(jnp.int32, (tq, ncp), 1)
    d = qi - CMP_STRIDE * c + (CMP_STRIDE * CMP_BACK - (CMP_BLOCK - 1))
    bc_ref[...] = jnp.where(c < CMP_NEAR, _bias_values(d, table), 0.0)


def _bias_prompt(rel_bias, ncp):
    tq = Q_TILE
    return pl.pallas_call(
        _bias_prompt_kernel,
        grid=(N_HEADS,),
        in_specs=[pl.BlockSpec(memory_space=pltpu.SMEM)],
        out_specs=[pl.BlockSpec((None, tq, ncp), lambda h: (h, 0, 0)),
                   pl.BlockSpec((None, tq, tq), lambda h: (h, 0, 0)),
                   pl.BlockSpec((None, tq, tq), lambda h: (h, 0, 0))],
        out_shape=[jax.ShapeDtypeStruct((N_HEADS, tq, ncp), F32),
                   jax.ShapeDtypeStruct((N_HEADS, tq, tq), F32),
                   jax.ShapeDtypeStruct((N_HEADS, tq, tq), F32)],
        compiler_params=_cparams(("arbitrary",)),
        name="bias_prompt",
    )(rel_bias)


def _bias_sample_kernel(tbl_ref, bcs_ref, bsl_ref, bnew_ref, bw_ref, *, past, s_len, n_c):
    table = lambda b: tbl_ref[:, b:b + 1]

    def qpos(shape):
        return past + lax.rem(lax.broadcasted_iota(jnp.int32, shape, 0), s_len)

    def lane(shape):
        return lax.broadcasted_iota(jnp.int32, shape, 1)

    sh = bcs_ref.shape
    n = lane(sh)
    bcs_ref[...] = jnp.where(n < n_c, _bias_values(qpos(sh) - (CMP_STRIDE * n + CMP_BLOCK - 1), table), NEG)
    sh = bsl_ref.shape
    bsl_ref[...] = _bias_values(qpos(sh) - (past - sh[1] + lane(sh)), table)
    sh = bnew_ref.shape
    bnew_ref[...] = jnp.where(lane(sh) < s_len, _bias_values(qpos(sh) - (past + lane(sh)), table), NEG)
    sh = bw_ref.shape
    d = qpos(sh) - (past - sh[1] + lane(sh))
    bw_ref[...] = jnp.where(d < WINDOW, _bias_values(d, table), NEG)


def _bias_sample(tbl_rows, past, s_len, ncs, ck, wb):
    rows = tbl_rows.shape[0]
    widths = (ncs, ck, LANES, wb)
    return pl.pallas_call(
        functools.partial(_bias_sample_kernel, past=past, s_len=s_len, n_c=(past + s_len - CMP_BLOCK) // CMP_STRIDE + 1),
        out_shape=[jax.ShapeDtypeStruct((rows, w), F32) for w in widths],
        name="bias_sample",
    )(tbl_rows)


MASK_FLOOR = -1e8
M_INIT = -1e30


def _sel_weights(v):
    return jnp.where((v >= 0) & (v <= 2), 2.0, jnp.where((v == -1) | (v == 3), 1.0, 0.0))


def _flash_cols(s, vt, m_ref, l_ref, a_ref):
    m_prev = m_ref[...]
    m_new = jnp.maximum(m_prev, jnp.max(s, axis=0, keepdims=True))
    alpha = jnp.exp(m_prev - m_new)
    p = jnp.exp(s - m_new)
    l_ref[...] = alpha * l_ref[...] + jnp.sum(p, axis=0, keepdims=True)
    m_ref[...] = m_new
    a_ref[...] = alpha * a_ref[...] + _dot(vt, p.astype(BF16))


def _attn_prompt_kernel(q_ref, gate_ref, ksk_ref, vst_ref, kwk_ref, vwt_ref, kc2_ref, bc_ref, t0_ref, t1_ref, o_ref,
                        sc_ref, ms_ref, ls_ref, as_ref, mw_ref, lw_ref, aw_ref, *, ncp, nsel):
    tq = Q_TILE
    i = pl.program_id(1)
    q0 = i * tq
    lane = lax.broadcasted_iota(jnp.int32, (tq, LANES), 1)
    gates = gate_ref[...]

    n0 = (tq // CMP_STRIDE) * i - CMP_BACK
    ws = pl.multiple_of(lax.rem(n0 + ncp, ncp), 8)
    cl = lax.broadcasted_iota(jnp.int32, (1, ncp), 1)
    dm = jnp.where((cl >= CMP_NEAR) & (cl < ncp - n0), NEG, 0.0) + jnp.where(cl < -n0, NEG, 0.0)
    jrow = lax.broadcasted_iota(jnp.int32, (nsel, ncp), 0)
    nn = lax.broadcasted_iota(jnp.int32, (nsel, ncp), 1) + n0
    nn = jnp.where(nn < 0, nn + ncp, nn)
    nn = jnp.where(nn >= ncp, nn - ncp, nn)
    mit = _sel_weights(nn - (SEL_BLOCK // CMP_STRIDE) * jrow).astype(BF16)

    jr = lax.broadcasted_iota(jnp.int32, (nsel, tq), 0)
    qpos = q0 + lax.broadcasted_iota(jnp.int32, (nsel, tq), 1)
    cur = lax.shift_right_logical(qpos, int(math.log2(SEL_BLOCK)))
    invalid_pen = jnp.where(jr * SEL_BLOCK <= qpos, 0.0, NEG)
    bonus = jnp.where((jr == 0) | (jr == cur) | (jr == cur - 1), FORCE_BONUS, 0.0)
    eblk = lax.shift_right_logical(lax.broadcasted_iota(jnp.int32, (tq, LANES), 0), int(math.log2(SEL_BLOCK)))
    ecol = lax.broadcasted_iota(jnp.int32, (tq, LANES), 1)
    kj = lax.broadcasted_iota(jnp.int32, (tq, tq), 0)
    qi = lax.broadcasted_iota(jnp.int32, (tq, tq), 1)
    wmask = jnp.where(kj > qi, 0.0, NEG)
    wmask4 = jnp.concatenate([wmask] * GROUP, axis=1)
    blocks_per_chunk = tq // SEL_BLOCK

    results = []
    for g in range(H_KV):
        gmask = (lane < HEAD_DIM) if g == 0 else (lane >= HEAD_DIM)
        qg = jnp.concatenate(
            [jnp.where(gmask, q_ref[:, r * LANES:(r + 1) * LANES], jnp.zeros((tq, LANES), BF16))
             for r in range(GROUP)], axis=0)

        kc = kc2_ref[pl.ds(ws, ncp), 0:KV_HALF]
        vc = kc2_ref[pl.ds(ws, ncp), KV_HALF:KV_W]
        s = _dot_nt(qg, kc)
        ps, psum = [], None
        for r in range(GROUP):
            sr = s[r * tq:(r + 1) * tq] + bc_ref[g * GROUP + r] + dm
            m = jnp.maximum(jnp.max(sr, axis=-1, keepdims=True), MASK_FLOOR)
            e = jnp.exp(sr - m)
            p = e / jnp.maximum(jnp.sum(e, axis=-1, keepdims=True), 1e-30)
            ps.append(p)
            psum = p if psum is None else psum + p
        o_cmp = _dot(jnp.concatenate(ps, axis=0).astype(BF16), vc)

        h1, h2, h3 = _split3(psum)
        pslc = (_dot_nt(mit, h1) + _dot_nt(mit, h2)) + _dot_nt(mit, h3)
        score = jnp.where(invalid_pen < 0.0, NEG, pslc + bonus)
        sc_ref[...] = score

        def rank_body(jp, rank):
            row = sc_ref[pl.ds(jp, 1), :]
            ge = jnp.where(row >= score, 1.0, 0.0)
            gt = jnp.where(row > score, 1.0, 0.0)
            return rank + jnp.where(jr > jp, ge, gt)

        rank = lax.fori_loop(0, nsel, rank_body, jnp.zeros((nsel, tq), F32))
        sel_t = jnp.where(rank < min(TOP_N, nsel), 1.0, 0.0)
        if nsel < LANES:
            sel_t = jnp.concatenate([sel_t, jnp.zeros((LANES - nsel, tq), F32)], axis=0)
        sel_t = sel_t.astype(BF16)

        def sel_bias(t):
            e_t = jnp.where(ecol == t * blocks_per_chunk + eblk, 1.0, 0.0).astype(BF16)
            mb = (_dot(e_t, sel_t) - 1.0) * (-NEG)
            return jnp.concatenate([mb] * GROUP, axis=1)

        def head_tiles(t_ref):
            return jnp.concatenate([t_ref[g * GROUP + r] for r in range(GROUP)], axis=1)

        def chunk(k_ref, vt_ref, t, extra, m_ref, l_ref, a_ref):
            k = k_ref[pl.ds(pl.multiple_of(t * tq, tq), tq), :]
            _flash_cols(_dot_nt(k, qg) + extra, vt_ref[t], m_ref, l_ref, a_ref)

        for m_ref, l_ref, a_ref in ((ms_ref, ls_ref, as_ref), (mw_ref, lw_ref, aw_ref)):
            m_ref[...] = jnp.full(m_ref.shape, M_INIT, F32)
            l_ref[...] = jnp.zeros(l_ref.shape, F32)
            a_ref[...] = jnp.zeros(a_ref.shape, F32)

        def far_body(t, carry):
            chunk(ksk_ref, vst_ref, t, sel_bias(t), ms_ref, ls_ref, as_ref)
            return carry

        lax.fori_loop(0, jnp.maximum(i - 1, 0), far_body, 0)

        @pl.when(i >= 1)
        def _():
            chunk(ksk_ref, vst_ref, i - 1, head_tiles(t1_ref) + sel_bias(i - 1), ms_ref, ls_ref, as_ref)

        chunk(ksk_ref, vst_ref, i, head_tiles(t0_ref) + sel_bias(i), ms_ref, ls_ref, as_ref)

        @pl.when(i >= 2)
        def _():
            chunk(kwk_ref, vwt_ref, i - 2, wmask4, mw_ref, lw_ref, aw_ref)

        @pl.when(i >= 1)
        def _():
            chunk(kwk_ref, vwt_ref, i - 1, head_tiles(t1_ref), mw_ref, lw_ref, aw_ref)

        chunk(kwk_ref, vwt_ref, i, head_tiles(t0_ref), mw_ref, lw_ref, aw_ref)

        o_sel_t = as_ref[...] / jnp.maximum(ls_ref[...], 1e-30)
        o_win_t = aw_ref[...] / jnp.maximum(lw_ref[...], 1e-30)
        per_r = []
        for r in range(GROUP):
            sl = slice(r * tq, (r + 1) * tq)
            col = g * GROUP + r
            per_r.append(gates[:, col:col + 1] * o_cmp[sl]
                         + gates[:, N_HEADS + col:N_HEADS + col + 1] * o_sel_t[:, sl].T
                         + gates[:, 2 * N_HEADS + col:2 * N_HEADS + col + 1] * o_win_t[:, sl].T)
        results.append(per_r)

    for r in range(GROUP):
        o_ref[:, r * LANES:(r + 1) * LANES] = jnp.where(
            lane < HEAD_DIM, results[0][r], results[1][r]).astype(BF16)


def _attn_prompt(q, gates, ks_k, vs_t, kw_k, vw_t, kc2, bias_tiles, nb, t):
    tq = Q_TILE
    assert t % tq == 0 and WINDOW == 2 * tq
    nt = t // tq
    ncp = t // CMP_STRIDE
    nsel = t // SEL_BLOCK
    assert nsel <= LANES and ncp % LANES == 0
    bc, t0, t1 = bias_tiles
    tok = lambda w: pl.BlockSpec((tq, w), lambda b, i: (b * nt + i, 0))
    k_rows = pl.BlockSpec((None, t, KV_HALF), lambda b, i: (b, 0, 0))
    v_cols = pl.BlockSpec((None, nt, KV_HALF, tq), lambda b, i: (b, 0, 0, 0))
    rows = GROUP * tq
    state = [pltpu.VMEM((1, rows), F32), pltpu.VMEM((1, rows), F32), pltpu.VMEM((LANES, rows), F32)]
    return pl.pallas_call(
        functools.partial(_attn_prompt_kernel, ncp=ncp, nsel=nsel),
        grid=(nb, nt),
        in_specs=[tok(q.shape[1]), tok(LANES), k_rows, v_cols, k_rows, v_cols,
                  pl.BlockSpec((None, 2 * ncp, KV_W), lambda b, i: (b, 0, 0)),
                  _resident(bc.shape), _resident(t0.shape), _resident(t1.shape)],
        out_specs=tok(q.shape[1]),
        out_shape=jax.ShapeDtypeStruct(q.shape, BF16),
        scratch_shapes=[pltpu.VMEM((nsel, tq), F32)] + state + state,
        compiler_params=_cparams(("arbitrary", "arbitrary")),
        name="attn_prompt",
    )(q, gates, ks_k.reshape(nb, t, KV_HALF), vs_t.reshape(nb, nt, KV_HALF, tq),
      kw_k.reshape(nb, t, KV_HALF), vw_t.reshape(nb, nt, KV_HALF, tq), kc2, bc, t0, t1)


SAMPLE_CK = 1024


def _flash_val(state, s, v):
    m_prev, l_prev, a_prev = state
    m_new = jnp.maximum(m_prev, jnp.max(s, axis=-1, keepdims=True))
    alpha = jnp.exp(m_prev - m_new)
    p = jnp.exp(s - m_new)
    return (m_new, alpha * l_prev + jnp.sum(p, axis=-1, keepdims=True),
            alpha * a_prev + _dot(p.astype(BF16), v))


def _attn_sample_kernel(pt_ref, q_ref, g_ref, knew_ref, wnew_ref, win_ref, kc_ref,
                        bcs_ref, bsl_ref, bnew_ref, bw_ref, msel_ref, cache_ref, o_ref, buf, sem,
                        *, n_pages, rpp, past, s_len, nsel):
    slot = _gather_pages(cache_ref, pt_ref, buf, sem, n_pages, rpp)
    q = q_ref[...]
    rows = q.shape[0]
    gs = rows // GROUP
    nselp = msel_ref.shape[1]

    s = _dot_nt(q, kc_ref[:, 0:KV_HALF]) + bcs_ref[...]
    m = jnp.maximum(jnp.max(s, axis=-1, keepdims=True), MASK_FLOOR)
    e = jnp.exp(s - m)
    p = e / jnp.maximum(jnp.sum(e, axis=-1, keepdims=True), 1e-30)
    o_cmp = _dot(p.astype(BF16), kc_ref[:, KV_HALF:KV_W])
    psum = p[0:gs]
    for r in range(1, GROUP):
        psum = psum + p[r * gs:(r + 1) * gs]

    h1, h2, h3 = _split3(psum)
    msel = msel_ref[...]
    pslc = (_dot(h1, msel) + _dot(h2, msel)) + _dot(h3, msel)
    j = lax.broadcasted_iota(jnp.int32, (gs, nselp), 1)
    qpos = past + lax.rem(lax.broadcasted_iota(jnp.int32, (gs, nselp), 0), s_len)
    cur = lax.shift_right_logical(qpos, int(math.log2(SEL_BLOCK)))
    valid = j * SEL_BLOCK <= qpos
    forced = (j == 0) | (j == cur) | (j == cur - 1)
    score = jnp.where(valid, pslc + jnp.where(forced, FORCE_BONUS, 0.0), NEG)
    score = jnp.where(j < nsel, score, -3e38)
    score_t = jnp.concatenate([score, jnp.zeros((LANES - gs, nselp), F32)], axis=0).T
    jp = lax.broadcasted_iota(jnp.int32, (nselp, nselp), 0)
    jj = lax.broadcasted_iota(jnp.int32, (nselp, nselp), 1)
    sel_rows = []
    for r in range(gs):
        col = score_t[:, r:r + 1]
        row = score[r:r + 1, :]
        beats = jnp.where(jj > jp, jnp.where(col >= row, 1.0, 0.0), jnp.where(col > row, 1.0, 0.0))
        rank = jnp.sum(beats, axis=0, keepdims=True)
        sel_rows.append(jnp.where(rank < min(TOP_N, nsel), 1.0, 0.0))
    sel8 = jnp.concatenate(sel_rows, axis=0)
    sel = jnp.concatenate([sel8] * GROUP, axis=0).astype(BF16)

    ck = bsl_ref.shape[1]
    n_chunks = past // ck
    erow = lax.broadcasted_iota(jnp.int32, (nselp, ck), 0)
    eblk = lax.shift_right_logical(lax.broadcasted_iota(jnp.int32, (nselp, ck), 1), int(math.log2(SEL_BLOCK)))
    init = (jnp.full((rows, 1), M_INIT, F32), jnp.zeros((rows, 1), F32), jnp.zeros((rows, LANES), F32))
    st = init
    for t in range(n_chunks):
        kv = buf[slot, t * ck:(t + 1) * ck, :].astype(BF16)
        e_t = jnp.where(erow == t * (ck // SEL_BLOCK) + eblk, 1.0, 0.0).astype(BF16)
        sc = _dot_nt(q, kv[:, 0:KV_HALF]) + (_dot(sel, e_t) - 1.0) * (-NEG)
        if t == n_chunks - 1:
            sc = sc + bsl_ref[...]
        st = _flash_val(st, sc, kv[:, KV_HALF:KV_W])
    knew = knew_ref[...]
    st = _flash_val(st, _dot_nt(q, knew[:, 0:KV_HALF]) + bnew_ref[...], knew[:, KV_HALF:KV_W])
    o_sel = st[2] / jnp.maximum(st[1], 1e-30)

    kw = win_ref[...].astype(BF16)
    st = _flash_val(init, _dot_nt(q, kw[:, 0:KV_HALF]) + bw_ref[...], kw[:, KV_HALF:KV_W])
    wnew = wnew_ref[...]
    st = _flash_val(st, _dot_nt(q, wnew[:, 0:KV_HALF]) + bnew_ref[...], wnew[:, KV_HALF:KV_W])
    o_win = st[2] / jnp.maximum(st[1], 1e-30)

    g = g_ref[...]
    o_ref[...] = g[:, 0:1] * o_cmp + g[:, 1:2] * o_sel + g[:, 2:3] * o_win


def _attn_sample(page_table, q32, g32, knew, wnew, win, kvc, tiles, msel, cache_rows, past, s_len):
    nb, n_pages = page_table.shape
    rpp = cache_rows.shape[1]
    rows = q32.shape[1]
    nsel = -(-(past + s_len) // SEL_BLOCK)
    bcs, bsl, bnew, bw = tiles
    per_b = lambda r, w: pl.BlockSpec((None, r, w), lambda b, pt: (b, 0, 0))
    return pl.pallas_call(
        functools.partial(_attn_sample_kernel, n_pages=n_pages, rpp=rpp, past=past, s_len=s_len, nsel=nsel),
        grid_spec=pltpu.PrefetchScalarGridSpec(
            num_scalar_prefetch=1,
            grid=(nb,),
            in_specs=[per_b(rows, LANES), per_b(rows, LANES), per_b(LANES, KV_W), per_b(LANES, KV_W),
                      per_b(win.shape[1], KV_W), per_b(kvc.shape[1], KV_W),
                      _resident_sp(bcs.shape), _resident_sp(bsl.shape), _resident_sp(bnew.shape),
                      _resident_sp(bw.shape), _resident_sp(msel.shape),
                      pl.BlockSpec(memory_space=pl.ANY)],
            out_specs=per_b(rows, LANES),
            scratch_shapes=[pltpu.VMEM((2, past, KV_W), F32), pltpu.SemaphoreType.DMA((2,))]),
        out_shape=jax.ShapeDtypeStruct((nb, rows, LANES), F32),
        compiler_params=_cparams(("arbitrary",)),
        name="attn_sample",
    )(page_table, q32, g32, knew, wnew, win, kvc, bcs, bsl, bnew, bw, msel, cache_rows)


def _prep_weights(w_in, w_cmp1, w_cmp2, pe_cmp, w_conv, w_nsa_out):
    d = w_in.shape[0]
    sizes = (N_HEADS * HEAD_DIM, KV_W, KV_W, KV_W, 3 * N_HEADS, _C_W, _C_W, _C_W, 2 * d)
    offs = [0]
    for sz in sizes:
        offs.append(offs[-1] + sz)
    part = lambda k: w_in[:, offs[k]:offs[k + 1]]
    qp = part(0).reshape(d, H_KV, GROUP, HEAD_DIM).transpose(0, 2, 1, 3).reshape(d, N_HEADS * HEAD_DIM)
    gn = jnp.pad(part(4), ((0, 0), (0, _G_W - 3 * N_HEADS)))
    w_in_p = jnp.concatenate([qp, part(1), part(2), part(3), gn, part(5), part(6), part(7), part(8)],
                             axis=1).astype(BF16)
    w_nsa_p = w_nsa_out.reshape(H_KV, GROUP, HEAD_DIM, -1).transpose(1, 0, 2, 3).reshape(
        N_HEADS * HEAD_DIM, -1).astype(BF16)
    r = CMP_BLOCK // CMP_STRIDE
    w1r = w_cmp1.reshape(2, r, CMP_STRIDE, HEAD_DIM, CMP_HIDDEN)
    wb = jnp.zeros((CMP_STRIDE, 2, H_KV, HEAD_DIM, r, 2, H_KV, CMP_HIDDEN), F32)
    w2b = jnp.zeros((2, H_KV, CMP_HIDDEN, 2, H_KV, HEAD_DIM), F32)
    for c in range(2):
        for g in range(H_KV):
            wb = wb.at[:, c, g, :, :, c, g, :].set(jnp.transpose(w1r[c], (1, 2, 0, 3)))
            w2b = w2b.at[c, g, :, c, g, :].set(w_cmp2[c])
    wbig = wb.reshape(CHUNK_W, r * CMP_OUT_W).astype(BF16)
    w2bd = w2b.reshape(CMP_OUT_W, KV_W).astype(BF16)
    pe_r = pe_cmp.reshape(2, r, CMP_STRIDE, HEAD_DIM).transpose(1, 2, 0, 3)
    pe_rows = jnp.broadcast_to(pe_r[:, :, :, None, :], (r, CMP_STRIDE, 2, H_KV, HEAD_DIM)).reshape(r, CHUNK_W)
    pe8 = jnp.pad(pe_rows, ((0, 8 - r), (0, 0)))
    w_conv8 = jnp.pad(w_conv, ((0, 8 - CONV_WIDTH), (0, 0)))
    return w_in_p, w_nsa_p, wbig, w2bd, pe8, w_conv8


def _sel_matrix(n_c, nselp):
    n = jnp.arange(n_c + 1)[:, None]
    j = jnp.arange(nselp)[None, :]
    return _sel_weights(n - (SEL_BLOCK // CMP_STRIDE) * j).astype(BF16)


def kernel(x_prompt, x_sample, cache_kv_cmp, cache_kv_sel, state_kv_win, state_conv, page_table,
           c_prompt, c_sample, w_ada, b_ada, g_norm, w_ffn1_gu, w_ffn1_down, w_ffn2_gu, w_ffn2_down,
           w_in, w_cmp1, w_cmp2, pe_cmp, w_conv, w_nsa_out, w_conv_out, w_out, rel_bias, g_final):
    assert w_ada.shape[0] == 1, "single-layer trunk"
    nbp, t, d = x_prompt.shape
    nbs, s_len, _ = x_sample.shape
    n_pages = page_table.shape[1]
    page = cache_kv_cmp.shape[2]
    past = n_pages * page
    n_phys = cache_kv_cmp.shape[1]
    wb = state_kv_win.shape[2]
    assert wb == WINDOW and past % SAMPLE_CK == 0

    w_in_p, w_nsa_p, wbig, w2bd, pe8, w_conv8 = _prep_weights(
        w_in[0], w_cmp1[0], w_cmp2[0], pe_cmp[0], w_conv[0], w_nsa_out[0])
    wgu1, wd1 = w_ffn1_gu[0].astype(BF16), w_ffn1_down[0].astype(BF16)
    wgu2, wd2 = w_ffn2_gu[0].astype(BF16), w_ffn2_down[0].astype(BF16)
    w_cv, w_o = w_conv_out[0].astype(BF16), w_out[0].astype(BF16)
    gn = [g_norm[0][k:k + 1] for k in range(N_SUB)]

    n_c_rows = nbp + nbs
    c_all = jnp.pad(jnp.concatenate([c_prompt, c_sample], axis=0), ((0, (-n_c_rows) % 8), (0, 0)))
    mod_all = _ada(c_all, w_ada[0], b_ada[0])
    mod_p = mod_all[:nbp].reshape(nbp * 3 * N_SUB, 1, d)
    mod_s = jnp.transpose(jnp.repeat(mod_all[nbp:n_c_rows].reshape(nbs, 3 * N_SUB, d), s_len, axis=0), (1, 0, 2))

    tm = 512 if t % 512 == 0 else t
    tpb = t // tm
    xp = x_prompt.reshape(nbp * t, d)
    h1 = _ffn(xp, mod_p, 0, gn[0], wgu1, wd1, tm, tpb)
    (q, kvc, kvc_t, kvs_t, kvw_t, ks_k, vs_t, kw_k, vw_t, gates, z, cb, gm) = _inproj(
        h1, mod_p, gn[1], w_in_p, tm, tpb, True)
    kc2 = _cmp_prompt(kvc.reshape(nbp, t // CMP_STRIDE, CHUNK_W), wbig, pe8, w2bd)
    tiles_p = _bias_prompt(rel_bias, t // CMP_STRIDE)
    o_nsa = _attn_prompt(q, gates, ks_k, vs_t, kw_k, vw_t, kc2, tiles_p, nbp, t)
    h2 = _mixout(h1, o_nsa, z, None, cb, gm, mod_p, w_conv8, w_nsa_p, w_cv, w_o, tm, tpb)
    y_prompt = _ffn(h2, mod_p, 2, gn[2], wgu2, wd2, tm, tpb, g_final).reshape(nbp, t, d)

    kv_shape = (2, H_KV, HEAD_DIM)
    kv_out = lambda a: jnp.transpose(a.reshape((1, nbp) + kv_shape + (a.shape[-1],)), (0, 1, 5, 2, 3, 4))
    kv_cmp_p = kv_out(kvc_t)
    kv_sel_p = kv_out(kvs_t)
    keep = min(WINDOW, t)
    kv_win_p = kv_out(kvw_t[:, :, t - keep:])
    conv_p = z.reshape(1, nbp, t, -1)[:, :, t - (CONV_WIDTH - 1):]

    ns = nbs * s_len
    xs = x_sample.reshape(ns, d)
    h1s = _ffn(xs, mod_s, 0, gn[0], wgu1, wd1, ns, 1)
    qs, kvc_s, kvs_s, kvw_s, kvs_sb, kvw_sb, gates_s, z_s, cb_s, gm_s = _inproj(
        h1s, mod_s, gn[1], w_in_p, ns, 1, False)

    kvc_past = _cmp_sample(page_table, cache_kv_cmp[0].reshape(n_phys, page // CMP_STRIDE, CHUNK_W),
                           wbig, pe8, w2bd)
    n_c = (past + s_len - CMP_BLOCK) // CMP_STRIDE + 1
    nsel = -(-(past + s_len) // SEL_BLOCK)
    nselp = -(-nsel // LANES) * LANES
    rows = N_HEADS * s_len
    head_of_row = [g * GROUP + r for r in range(GROUP) for g in range(H_KV) for _ in range(s_len)]
    tbl_rows = jnp.pad(rel_bias.T[jnp.array(head_of_row)], ((0, 0), (0, LANES - N_BUCKETS)))
    tiles_s = _bias_sample(tbl_rows, past, s_len, past // CMP_STRIDE, SAMPLE_CK, wb)
    msel = _sel_matrix(past // CMP_STRIDE - 1, nselp)

    q5 = qs.reshape(nbs, s_len, GROUP, H_KV, HEAD_DIM).transpose(0, 2, 3, 1, 4)
    q32 = jnp.zeros((nbs, GROUP, H_KV, s_len, H_KV, HEAD_DIM), BF16)
    for g in range(H_KV):
        q32 = q32.at[:, :, g, :, g, :].set(q5[:, :, g])
    q32 = q32.reshape(nbs, rows, LANES)
    g5 = gates_s[:, :3 * N_HEADS].reshape(nbs, s_len, 3, H_KV, GROUP).transpose(0, 4, 3, 1, 2)
    g32 = jnp.pad(g5.reshape(nbs, rows, 3), ((0, 0), (0, 0), (0, LANES - 3)))
    pad_new = lambda a: jnp.pad(a.reshape(nbs, s_len, KV_W), ((0, 0), (0, LANES - s_len), (0, 0)))
    o32 = _attn_sample(page_table, q32, g32, pad_new(kvs_sb), pad_new(kvw_sb),
                       state_kv_win[0].reshape(nbs, wb, KV_W), kvc_past, tiles_s, msel,
                       cache_kv_sel[0].reshape(n_phys, page, KV_W), past, s_len)
    o6 = o32.reshape(nbs, GROUP, H_KV, s_len, H_KV, HEAD_DIM)
    o_s = jnp.stack([o6[:, :, g, :, g, :] for g in range(H_KV)], axis=3)
    o_nsa_s = o_s.transpose(0, 2, 1, 3, 4).reshape(ns, N_HEADS * HEAD_DIM).astype(BF16)

    full = jnp.concatenate([state_conv[0], z_s.reshape(nbs, s_len, -1)], axis=1)
    z_shift = (full[:, 1:1 + s_len].reshape(ns, -1), full[:, 0:s_len].reshape(ns, -1))
    h2s = _mixout(h1s, o_nsa_s, z_s, z_shift, cb_s, gm_s, mod_s, w_conv8, w_nsa_p, w_cv, w_o, ns, 1)
    y_sample = _ffn(h2s, mod_s, 2, gn[2], wgu2, wd2, ns, 1, g_final).reshape(nbs, s_len, d)

    kv_cmp_s = kvc_s.reshape((1, nbs, s_len) + kv_shape)
    kv_sel_s = kvs_s.reshape((1, nbs, s_len) + kv_shape)
    win_full = jnp.concatenate([state_kv_win[0], kvw_s.reshape((nbs, s_len) + kv_shape)], axis=1)
    keep_s = min(WINDOW, wb + s_len)
    kv_win_s = win_full[None, :, wb + s_len - keep_s:]
    conv_s = full[None, :, s_len:]
    return (y_prompt, y_sample, kv_cmp_p, kv_sel_p, kv_win_p, conv_p, kv_cmp_s, kv_sel_s, kv_win_s, conv_s)
```

```python
import functools
import math

import jax
import jax.numpy as jnp
from jax import lax
from jax.experimental import pallas as pl
from jax.experimental.pallas import tpu as pltpu

F32 = jnp.float32
BF16 = jnp.bfloat16

HEAD_DIM = 64
N_HEADS = 8
H_KV = 2
GROUP = N_HEADS // H_KV
CMP_BLOCK = 32
CMP_STRIDE = 16
CMP_HIDDEN = 2 * HEAD_DIM
SEL_BLOCK = 64
TOP_N = 16
WINDOW = 512
CONV_WIDTH = 3
N_BUCKETS = 32
MAX_DISTANCE = 128
N_SUB = 3
EPS = 1e-6
NEG = -1e9
FORCE_BONUS = 1e3
KV_W = 2 * H_KV * HEAD_DIM
KV_HALF = H_KV * HEAD_DIM
LANES = 128
Q_TILE = 256
CMP_NEAR = 32
VMEM_LIMIT = 56 * 1024 * 1024


def _bucket_thresholds():
    max_exact = N_BUCKETS // 2

    def bucket(d):
        if d < max_exact:
            return d
        large = max_exact + int(math.log(d / max_exact) / math.log(MAX_DISTANCE / max_exact)
                                * (N_BUCKETS - max_exact))
        return min(large, N_BUCKETS - 1)

    thr, d = [], 0
    for b in range(N_BUCKETS):
        while bucket(d) < b:
            d += 1
        thr.append(d)
    return tuple(thr)


BUCKET_THR = _bucket_thresholds()
FAR_DIST = BUCKET_THR[-1]


def _cparams(sem):
    return pltpu.CompilerParams(dimension_semantics=sem, vmem_limit_bytes=VMEM_LIMIT)


def _resident(shape):
    nd = len(shape)
    return pl.BlockSpec(shape, lambda *_: (0,) * nd, pipeline_mode=pl.Buffered(1))


def _dot(a, b):
    return jnp.dot(a, b, preferred_element_type=F32)


def _dot_nt(a, b):
    return lax.dot_general(a, b, (((1,), (1,)), ((), ())), preferred_element_type=F32)


def _split3(x):
    h1 = x.astype(BF16)
    r1 = x - h1.astype(F32)
    h2 = r1.astype(BF16)
    h3 = (r1 - h2.astype(F32)).astype(BF16)
    return h1, h2, h3


def _modulated_norm(x, g, shift, scale):
    y = x * lax.rsqrt(jnp.mean(x * x, axis=-1, keepdims=True) + EPS)
    return (y * g) * (1.0 + scale) + shift


def _ada_kernel(c_ref, w_ref, b_ref, o_ref):
    c = c_ref[...]
    a = (c * jax.nn.sigmoid(c)).astype(BF16)
    o_ref[...] = _dot(a, w_ref[...].astype(BF16)) + b_ref[...]


def _ada(c_all, w_ada, b_ada):
    rows, d = c_all.shape
    n = w_ada.shape[1]
    tn = n // 8
    return pl.pallas_call(
        _ada_kernel,
        grid=(n // tn,),
        in_specs=[pl.BlockSpec((rows, d), lambda j: (0, 0)),
                  pl.BlockSpec((d, tn), lambda j: (0, j)),
                  pl.BlockSpec((1, tn), lambda j: (0, j))],
        out_specs=pl.BlockSpec((rows, tn), lambda j: (0, j)),
        out_shape=jax.ShapeDtypeStruct((rows, n), F32),
        compiler_params=_cparams(("arbitrary",)),
        name="ada",
    )(c_all, w_ada, b_ada.reshape(1, n))


def _ffn_kernel(*refs, d_ff, fc, final_norm):
    if final_norm:
        x_ref, sh_ref, sc_ref, gt_ref, gn_ref, wgu_ref, wd_ref, gf_ref, o_ref = refs
    else:
        x_ref, sh_ref, sc_ref, gt_ref, gn_ref, wgu_ref, wd_ref, o_ref = refs
    x = x_ref[...]
    ub = _modulated_norm(x, gn_ref[...], sh_ref[...], sc_ref[...]).astype(BF16)
    acc = jnp.zeros(x.shape, F32)
    for c in range(d_ff // fc):
        g = _dot(ub, wgu_ref[:, c * fc:(c + 1) * fc])
        v = _dot(ub, wgu_ref[:, d_ff + c * fc:d_ff + (c + 1) * fc])
        a = ((g * jax.nn.sigmoid(g)) * v).astype(BF16)
        acc = acc + _dot(a, wd_ref[c * fc:(c + 1) * fc, :])
    h = x + (0.5 * gt_ref[...]) * acc
    if final_norm:
        h = (h * lax.rsqrt(jnp.mean(h * h, axis=-1, keepdims=True) + EPS)) * gf_ref[...]
    o_ref[...] = h


def _mod_specs(mod, ks, tm, tiles_per_batch):
    if mod.ndim == 3 and mod.shape[1] == 1:
        d = mod.shape[-1]
        specs = [pl.BlockSpec((None, 1, d), lambda i, k=k: ((i // tiles_per_batch) * (3 * N_SUB) + k, 0, 0))
                 for k in ks]
        return specs, [mod] * len(ks)
    d = mod.shape[-1]
    specs = [pl.BlockSpec((None, tm, d), lambda i, k=k: (k, i, 0)) for k in ks]
    return specs, [mod] * len(ks)


def _ffn(x, mod, sub, gn_row, w_gu, w_down, tm, tiles_per_batch, g_final=None):
    n, d = x.shape
    d_ff = w_down.shape[0]
    fc = d_ff // 2 if (d_ff // 2) % LANES == 0 else d_ff
    final_norm = g_final is not None
    mspecs, mops = _mod_specs(mod, (3 * sub, 3 * sub + 1, 3 * sub + 2), tm, tiles_per_batch)
    in_specs = [pl.BlockSpec((tm, d), lambda i: (i, 0))] + mspecs + [
        pl.BlockSpec((1, d), lambda i: (0, 0)), _resident(w_gu.shape), _resident(w_down.shape)]
    ops = [x] + mops + [gn_row, w_gu, w_down]
    if final_norm:
        in_specs.append(pl.BlockSpec((1, d), lambda i: (0, 0)))
        ops.append(g_final.reshape(1, d))
    return pl.pallas_call(
        functools.partial(_ffn_kernel, d_ff=d_ff, fc=fc, final_norm=final_norm),
        grid=(n // tm,),
        in_specs=in_specs,
        out_specs=pl.BlockSpec((tm, d), lambda i: (i, 0)),
        out_shape=jax.ShapeDtypeStruct((n, d), F32),
        compiler_params=_cparams(("arbitrary",)),
        name="ffn_final" if final_norm else "ffn",
    )(*ops)


_Q_W = N_HEADS * HEAD_DIM
_G_W = LANES
_C_W = 512
_SEG = {}
_off = 0
for _name, _w in (("q", _Q_W), ("kc", KV_W), ("ks", KV_W), ("kw", KV_W), ("gn", _G_W),
                  ("ch", _C_W), ("cb", _C_W), ("cc", _C_W)):
    _SEG[_name] = (_off, _off + _w)
    _off += _w
_MG_OFF = _off


def _inproj_kernel(*refs, prompt):
    x_ref, sh_ref, sc_ref, gn_ref, w_ref = refs[:5]
    ub = _modulated_norm(x_ref[...], gn_ref[...], sh_ref[...], sc_ref[...]).astype(BF16)

    def seg(name):
        lo, hi = _SEG[name]
        return _dot(ub, w_ref[:, lo:hi])

    if prompt:
        (q_ref, kc_ref, kct_ref, kst_ref, kwt_ref, ksk_ref, vst_ref, kwk_ref, vwt_ref,
         g_ref, z_ref, cb_ref, gm_ref) = refs[5:]
        kc = seg("kc")
        kc_ref[...] = kc
        kct_ref[...] = kc.T
        for name, t_ref, k_ref, vt_ref in (("ks", kst_ref, ksk_ref, vst_ref), ("kw", kwt_ref, kwk_ref, vwt_ref)):
            kv = seg(name)
            kv_t = kv.T
            t_ref[...] = kv_t
            k_ref[...] = kv[:, 0:KV_HALF].astype(BF16)
            for c in range(vt_ref.shape[0]):
                vt_ref[c] = kv_t[KV_HALF:KV_W, c * Q_TILE:(c + 1) * Q_TILE].astype(BF16)
    else:
        q_ref, kc_ref, ks_ref, kw_ref, ksb_ref, kwb_ref, g_ref, z_ref, cb_ref, gm_ref = refs[5:]
        kc_ref[...] = seg("kc")
        ks = seg("ks")
        ks_ref[...] = ks
        ksb_ref[...] = ks.astype(BF16)
        kw = seg("kw")
        kw_ref[...] = kw
        kwb_ref[...] = kw.astype(BF16)
    q_ref[...] = (seg("q") * (HEAD_DIM ** -0.5)).astype(BF16)
    g_ref[...] = jax.nn.sigmoid(seg("gn"))
    z_ref[...] = seg("cc") * seg("ch")
    cb_ref[...] = seg("cb")
    d2 = gm_ref.shape[1]
    half = d2 // 2
    for c in range(2):
        gm_ref[:, c * half:(c + 1) * half] = jax.nn.sigmoid(
            _dot(ub, w_ref[:, _MG_OFF + c * half:_MG_OFF + (c + 1) * half]))


def _inproj(h, mod, gn_row, w_in_p, tm, tiles_per_batch, prompt):
    n, d = h.shape
    d_conv = _C_W
    mspecs, mops = _mod_specs(mod, (3, 4), tm, tiles_per_batch)
    rows = lambda w, dt: (pl.BlockSpec((tm, w), lambda i: (i, 0)), jax.ShapeDtypeStruct((n, w), dt))
    tail = [rows(_G_W, F32), rows(d_conv, F32), rows(d_conv, F32), rows(2 * d, F32)]
    if prompt:
        nb = n // (tm * tiles_per_batch)
        t = tm * tiles_per_batch
        cpt = tm // Q_TILE
        tr = (pl.BlockSpec((None, KV_W, tm), lambda i: (i // tiles_per_batch, 0, i % tiles_per_batch)),
              jax.ShapeDtypeStruct((nb, KV_W, t), F32))
        vt = (pl.BlockSpec((cpt, KV_HALF, Q_TILE), lambda i: (i, 0, 0)),
              jax.ShapeDtypeStruct((n // Q_TILE, KV_HALF, Q_TILE), BF16))
        outs = [rows(_Q_W, BF16), rows(KV_W, F32), tr, tr, tr,
                rows(KV_HALF, BF16), vt, rows(KV_HALF, BF16), vt] + tail
    else:
        outs = [rows(_Q_W, BF16), rows(KV_W, F32), rows(KV_W, F32), rows(KV_W, F32),
                rows(KV_W, BF16), rows(KV_W, BF16)] + tail
    return pl.pallas_call(
        functools.partial(_inproj_kernel, prompt=prompt),
        grid=(n // tm,),
        in_specs=[pl.BlockSpec((tm, d), lambda i: (i, 0))] + mspecs + [
            pl.BlockSpec((1, d), lambda i: (0, 0)), _resident(w_in_p.shape)],
        out_specs=[o[0] for o in outs],
        out_shape=[o[1] for o in outs],
        compiler_params=_cparams(("arbitrary",)),
        name="inproj_prompt" if prompt else "inproj",
    )(h, *mops, gn_row, w_in_p)


def _mixout_kernel(*refs, halo, tiles_per_batch):
    if halo:
        (h_ref, o_ref, z_ref, zp_ref, cb_ref, gm_ref, g2_ref, wc_ref,
         wn_ref, wcv_ref, wo_ref, out_ref) = refs
        z = z_ref[...]
        tm = z.shape[0]
        first = (pl.program_id(0) % tiles_per_batch) == 0
        prev = jnp.where(first, 0.0, zp_ref[...])
        row = lax.broadcasted_iota(jnp.int32, z.shape, 0)
        zm1 = jnp.where(row == 0, prev[7:8, :], pltpu.roll(z, 1, 0))
        zm2 = jnp.where(row == 0, prev[6:7, :], jnp.where(row == 1, prev[7:8, :], pltpu.roll(z, 2, 0)))
    else:
        (h_ref, o_ref, z_ref, zm1_ref, zm2_ref, cb_ref, gm_ref, g2_ref, wc_ref,
         wn_ref, wcv_ref, wo_ref, out_ref) = refs
        z, zm1, zm2 = z_ref[...], zm1_ref[...], zm2_ref[...]
    conv = wc_ref[0:1, :] * zm2 + wc_ref[1:2, :] * zm1 + wc_ref[2:3, :] * z
    y = (cb_ref[...] * conv).astype(BF16)
    d = h_ref.shape[1]
    merged = gm_ref[:, 0:d] * _dot(o_ref[...], wn_ref[...]) + gm_ref[:, d:2 * d] * _dot(y, wcv_ref[...])
    out_ref[...] = h_ref[...] + g2_ref[...] * _dot(merged.astype(BF16), wo_ref[...])


def _mixout(h, o_nsa, z, z_shift, cb, gm, mod, w_conv8, w_nsa_p, w_cv, w_o, tm, tiles_per_batch):
    n, d = h.shape
    dc = z.shape[1]
    halo = z_shift is None
    mspecs, mops = _mod_specs(mod, (5,), tm, tiles_per_batch)
    tok = lambda w: pl.BlockSpec((tm, w), lambda i: (i, 0))
    if halo:
        zspecs = [tok(dc), pl.BlockSpec((8, dc), lambda i: (jnp.maximum(i * (tm // 8) - 1, 0), 0))]
        zops = [z, z]
    else:
        zspecs = [tok(dc), tok(dc), tok(dc)]
        zops = [z, z_shift[0], z_shift[1]]
    return pl.pallas_call(
        functools.partial(_mixout_kernel, halo=halo, tiles_per_batch=tiles_per_batch),
        grid=(n // tm,),
        in_specs=[tok(d), tok(o_nsa.shape[1])] + zspecs + [tok(dc), tok(2 * d)] + mspecs + [
            pl.BlockSpec((8, dc), lambda i: (0, 0)),
            _resident(w_nsa_p.shape), _resident(w_cv.shape), _resident(w_o.shape)],
        out_specs=tok(d),
        out_shape=jax.ShapeDtypeStruct((n, d), F32),
        compiler_params=_cparams(("arbitrary",)),
        name="mixout",
    )(h, o_nsa, *zops, cb, gm, *mops, w_conv8, w_nsa_p, w_cv, w_o)


CHUNK_W = CMP_STRIDE * KV_W
CMP_OUT_W = 2 * H_KV * CMP_HIDDEN


def _gelu_tanh(x):
    return x * (0.5 * (1.0 + jnp.tanh(math.sqrt(2.0 / math.pi) * (x + 0.044715 * (x * x * x)))))


def _chunk_partials(tile, wc_ref):
    accs = []
    for c in range(2):
        acc = None
        for lp in range(CMP_STRIDE // 2):
            x2 = jnp.concatenate([tile(2 * lp, c), tile(2 * lp + 1, c)], axis=1)
            part = _dot(x2, wc_ref[c, lp])
            acc = part if acc is None else acc + part
        accs.append(acc)
    hw = CMP_OUT_W // 2
    return jnp.concatenate([accs[0][:, 0:hw], accs[1][:, 0:hw], accs[0][:, hw:], accs[1][:, hw:]], axis=1)


def _compress_tail(p, wc_ref, pe_ref, w2_ref):
    n_ch = p.shape[0]
    pb = _chunk_partials(
        lambda l, c: pe_ref[:, l * KV_W + c * KV_HALF:l * KV_W + (c + 1) * KV_HALF].astype(BF16), wc_ref)
    pre = (pb[0:1, 0:CMP_OUT_W] + pb[1:2, CMP_OUT_W:]) + p[:, 0:CMP_OUT_W] \
        + pltpu.roll(p[:, CMP_OUT_W:], n_ch - 1, 0)
    out = _dot(_gelu_tanh(pre).astype(BF16), w2_ref[...])
    row = lax.broadcasted_iota(jnp.int32, out.shape, 0)
    return jnp.where(row < n_ch - 1, out, 0.0)


def _cmp_prompt_kernel(x_ref, wc_ref, pe_ref, w2_ref, o_ref):
    n_ch = x_ref.shape[0]
    p = _chunk_partials(
        lambda l, c: x_ref[:, l * KV_W + c * KV_HALF:l * KV_W + (c + 1) * KV_HALF].astype(BF16), wc_ref)
    out = _compress_tail(p, wc_ref, pe_ref, w2_ref).astype(BF16)
    o_ref[0:n_ch, :] = out
    o_ref[n_ch:2 * n_ch, :] = out


def _cmp_prompt(rows, wbig, pe8, w2bd):
    nb, n_ch, _ = rows.shape
    return pl.pallas_call(
        _cmp_prompt_kernel,
        grid=(nb,),
        in_specs=[pl.BlockSpec((None, n_ch, CHUNK_W), lambda b: (b, 0, 0)),
                  _resident(wbig.shape), _resident(pe8.shape), _resident(w2bd.shape)],
        out_specs=pl.BlockSpec((None, 2 * n_ch, KV_W), lambda b: (b, 0, 0)),
        out_shape=jax.ShapeDtypeStruct((nb, 2 * n_ch, KV_W), BF16),
        compiler_params=_cparams(("arbitrary",)),
        name="cmp_prompt",
    )(rows, wbig, pe8, w2bd)


def _page_copies(cache_ref, pt_ref, b, buf, sem, slot, n_pages, page):
    return [pltpu.make_async_copy(cache_ref.at[pt_ref[b, p]],
                                  buf.at[slot, :, pl.ds(p * page, page)],
                                  sem.at[slot]) for p in range(n_pages)]


def _gather_pages(cache_ref, pt_ref, buf, sem, n_pages, page):
    b = pl.program_id(0)
    nb = pl.num_programs(0)
    slot = b % 2

    @pl.when(b == 0)
    def _():
        for c in _page_copies(cache_ref, pt_ref, 0, buf, sem, 0, n_pages, page):
            c.start()

    @pl.when(b + 1 < nb)
    def _():
        for c in _page_copies(cache_ref, pt_ref, b + 1, buf, sem, 1 - slot, n_pages, page):
            c.start()

    for c in _page_copies(cache_ref, pt_ref, b, buf, sem, slot, n_pages, page):
        c.wait()
    return slot


def _cmp_sample_kernel(pt_ref, cache_ref, wc_ref, pe_ref, w2_ref, o_ref, buf, xs_ref, sem, *, n_pages, page):
    slot = _gather_pages(cache_ref, pt_ref, buf, sem, n_pages, page)
    for c in range(2):
        xs_ref[c] = buf[slot, c * KV_HALF:(c + 1) * KV_HALF, :].T
    n_ch = xs_ref.shape[1] // CMP_STRIDE
    p = _chunk_partials(lambda l, c: xs_ref[c, pl.ds(l, n_ch, stride=CMP_STRIDE), :].astype(BF16), wc_ref)
    o_ref[...] = _compress_tail(p, wc_ref, pe_ref, w2_ref).astype(BF16)


def _cmp_sample(page_table, cache_t, wbig, pe8, w2bd):
    nb, n_pages = page_table.shape
    page = cache_t.shape[2]
    past = n_pages * page
    n_ch = past // CMP_STRIDE
    return pl.pallas_call(
        functools.partial(_cmp_sample_kernel, n_pages=n_pages, page=page),
        grid_spec=pltpu.PrefetchScalarGridSpec(
            num_scalar_prefetch=1,
            grid=(nb,),
            in_specs=[pl.BlockSpec(memory_space=pl.ANY),
                      _resident_sp(wbig.shape), _resident_sp(pe8.shape), _resident_sp(w2bd.shape)],
            out_specs=pl.BlockSpec((None, n_ch, KV_W), lambda b, pt: (b, 0, 0)),
            scratch_shapes=[pltpu.VMEM((2, KV_W, past), F32), pltpu.VMEM((2, past, KV_HALF), F32),
                            pltpu.SemaphoreType.DMA((2,))]),
        out_shape=jax.ShapeDtypeStruct((nb, n_ch, KV_W), BF16),
        compiler_params=_cparams(("arbitrary",)),
        name="cmp_sample",
    )(page_table, cache_t, wbig, pe8, w2bd)


def _resident_sp(shape):
    nd = len(shape)
    return pl.BlockSpec(shape, lambda *_: (0,) * nd, pipeline_mode=pl.Buffered(1))


def _bias_values(d, table):
    acc = jnp.zeros(d.shape, F32) + table(0)
    for b in range(1, N_BUCKETS):
        acc = jnp.where(d >= BUCKET_THR[b], table(b), acc)
    return jnp.where(d >= 0, acc - table(N_BUCKETS - 1), NEG)


CMP_BACK = 16


def _bias_prompt_kernel(tbl_ref, bc_ref, t0_ref, t1_ref):
    h = pl.program_id(0)
    table = lambda b: tbl_ref[b, h]
    tq, ncp = bc_ref.shape
    kj = lax.broadcasted_iota(jnp.int32, (tq, tq), 0)
    qi = lax.broadcasted_iota(jnp.int32, (tq, tq), 1)
    t0_ref[...] = _bias_values(qi - kj, table)
    t1_ref[...] = _bias_values(qi - kj + tq, table)
    qi = lax.broadcasted_iota(jnp.int32, (tq, ncp), 0)
    c = lax.broadcasted_iota(jnp.int32, (tq, ncp), 1)
    d = qi - CMP_STRIDE * c + (CMP_STRIDE * CMP_BACK - (CMP_BLOCK - 1))
    bc_ref[...] = jnp.where(c < CMP_NEAR, _bias_values(d, table), 0.0)


def _bias_prompt(rel_bias, ncp):
    tq = Q_TILE
    return pl.pallas_call(
        _bias_prompt_kernel,
        grid=(N_HEADS,),
        in_specs=[pl.BlockSpec(memory_space=pltpu.SMEM)],
        out_specs=[pl.BlockSpec((None, tq, ncp), lambda h: (h, 0, 0)),
                   pl.BlockSpec((None, tq, tq), lambda h: (h, 0, 0)),
                   pl.BlockSpec((None, tq, tq), lambda h: (h, 0, 0))],
        out_shape=[jax.ShapeDtypeStruct((N_HEADS, tq, ncp), F32),
                   jax.ShapeDtypeStruct((N_HEADS, tq, tq), F32),
                   jax.ShapeDtypeStruct((N_HEADS, tq, tq), F32)],
        compiler_params=_cparams(("arbitrary",)),
        name="bias_prompt",
    )(rel_bias)


def _bias_sample_kernel(tbl_ref, bcs_ref, bsl_ref, bnew_ref, bw_ref, *, past, s_len, n_c):
    table = lambda b: tbl_ref[:, b:b + 1]

    def qpos(shape):
        return past + lax.rem(lax.broadcasted_iota(jnp.int32, shape, 0), s_len)

    def lane(shape):
        return lax.broadcasted_iota(jnp.int32, shape, 1)

    sh = bcs_ref.shape
    n = lane(sh)
    bcs_ref[...] = jnp.where(n < n_c, _bias_values(qpos(sh) - (CMP_STRIDE * n + CMP_BLOCK - 1), table), NEG)
    sh = bsl_ref.shape
    bsl_ref[...] = _bias_values(qpos(sh) - (past - sh[1] + lane(sh)), table)
    sh = bnew_ref.shape
    bnew_ref[...] = jnp.where(lane(sh) < s_len, _bias_values(qpos(sh) - (past + lane(sh)), table), NEG)
    sh = bw_ref.shape
    d = qpos(sh) - (past - sh[1] + lane(sh))
    bw_ref[...] = jnp.where(d < WINDOW, _bias_values(d, table), NEG)


def _bias_sample(tbl_rows, past, s_len, ncs, ck, wb):
    rows = tbl_rows.shape[0]
    widths = (ncs, ck, LANES, wb)
    return pl.pallas_call(
        functools.partial(_bias_sample_kernel, past=past, s_len=s_len, n_c=(past + s_len - CMP_BLOCK) // CMP_STRIDE + 1),
        out_shape=[jax.ShapeDtypeStruct((rows, w), F32) for w in widths],
        name="bias_sample",
    )(tbl_rows)


MASK_FLOOR = -1e8
M_INIT = -1e30


def _sel_weights(v):
    return jnp.where((v >= 0) & (v <= 2), 2.0, jnp.where((v == -1) | (v == 3), 1.0, 0.0))


def _flash_cols(s, vt, m_ref, l_ref, a_ref):
    m_prev = m_ref[...]
    m_new = jnp.maximum(m_prev, jnp.max(s, axis=0, keepdims=True))
    alpha = jnp.exp(m_prev - m_new)
    p = jnp.exp(s - m_new)
    l_ref[...] = alpha * l_ref[...] + jnp.sum(p, axis=0, keepdims=True)
    m_ref[...] = m_new
    a_ref[...] = alpha * a_ref[...] + _dot(vt, p.astype(BF16))


def _attn_prompt_kernel(q_ref, gate_ref, ksk_ref, vst_ref, kwk_ref, vwt_ref, kc2_ref, bc_ref, t0_ref, t1_ref, o_ref,
                        sc_ref, ms_ref, ls_ref, as_ref, mw_ref, lw_ref, aw_ref, *, ncp, nsel):
    tq = Q_TILE
    i = pl.program_id(1)
    q0 = i * tq
    lane = lax.broadcasted_iota(jnp.int32, (tq, LANES), 1)
    gates = gate_ref[...]

    n0 = (tq // CMP_STRIDE) * i - CMP_BACK
    ws = pl.multiple_of(lax.rem(n0 + ncp, ncp), 8)
    cl = lax.broadcasted_iota(jnp.int32, (1, ncp), 1)
    dm = jnp.where((cl >= CMP_NEAR) & (cl < ncp - n0), NEG, 0.0) + jnp.where(cl < -n0, NEG, 0.0)
    jrow = lax.broadcasted_iota(jnp.int32, (nsel, ncp), 0)
    nn = lax.broadcasted_iota(jnp.int32, (nsel, ncp), 1) + n0
    nn = jnp.where(nn < 0, nn + ncp, nn)
    nn = jnp.where(nn >= ncp, nn - ncp, nn)
    mit = _sel_weights(nn - (SEL_BLOCK // CMP_STRIDE) * jrow).astype(BF16)

    jr = lax.broadcasted_iota(jnp.int32, (nsel, tq), 0)
    qpos = q0 + lax.broadcasted_iota(jnp.int32, (nsel, tq), 1)
    cur = lax.shift_right_logical(qpos, int(math.log2(SEL_BLOCK)))
    invalid_pen = jnp.where(jr * SEL_BLOCK <= qpos, 0.0, NEG)
    bonus = jnp.where((jr == 0) | (jr == cur) | (jr == cur - 1), FORCE_BONUS, 0.0)
    eblk = lax.shift_right_logical(lax.broadcasted_iota(jnp.int32, (tq, LANES), 0), int(math.log2(SEL_BLOCK)))
    ecol = lax.broadcasted_iota(jnp.int32, (tq, LANES), 1)
    kj = lax.broadcasted_iota(jnp.int32, (tq, tq), 0)
    qi = lax.broadcasted_iota(jnp.int32, (tq, tq), 1)
    wmask = jnp.where(kj > qi, 0.0, NEG)
    wmask4 = jnp.concatenate([wmask] * GROUP, axis=1)
    blocks_per_chunk = tq // SEL_BLOCK

    results = []
    for g in range(H_KV):
        gmask = (lane < HEAD_DIM) if g == 0 else (lane >= HEAD_DIM)
        qg = jnp.concatenate(
            [jnp.where(gmask, q_ref[:, r * LANES:(r + 1) * LANES], jnp.zeros((tq, LANES), BF16))
             for r in range(GROUP)], axis=0)

        kc = kc2_ref[pl.ds(ws, ncp), 0:KV_HALF]
        vc = kc2_ref[pl.ds(ws, ncp), KV_HALF:KV_W]
        s = _dot_nt(qg, kc)
        ps, psum = [], None
        for r in range(GROUP):
            sr = s[r * tq:(r + 1) * tq] + bc_ref[g * GROUP + r] + dm
            m = jnp.maximum(jnp.max(sr, axis=-1, keepdims=True), MASK_FLOOR)
            e = jnp.exp(sr - m)
            p = e / jnp.maximum(jnp.sum(e, axis=-1, keepdims=True), 1e-30)
            ps.append(p)
            psum = p if psum is None else psum + p
        o_cmp = _dot(jnp.concatenate(ps, axis=0).astype(BF16), vc)

        h1, h2, h3 = _split3(psum)
        pslc = (_dot_nt(mit, h1) + _dot_nt(mit, h2)) + _dot_nt(mit, h3)
        score = jnp.where(invalid_pen < 0.0, NEG, pslc + bonus)
        sc_ref[...] = score

        def rank_body(jp, rank):
            row = sc_ref[pl.ds(jp, 1), :]
            ge = jnp.where(row >= score, 1.0, 0.0)
            gt = jnp.where(row > score, 1.0, 0.0)
            return rank + jnp.where(jr > jp, ge, gt)

        rank = lax.fori_loop(0, nsel, rank_body, jnp.zeros((nsel, tq), F32))
        sel_t = jnp.where(rank < min(TOP_N, nsel), 1.0, 0.0)
        if nsel < LANES:
            sel_t = jnp.concatenate([sel_t, jnp.zeros((LANES - nsel, tq), F32)], axis=0)
        sel_t = sel_t.astype(BF16)

        def sel_bias(t):
            e_t = jnp.where(ecol == t * blocks_per_chunk + eblk, 1.0, 0.0).astype(BF16)
            mb = (_dot(e_t, sel_t) - 1.0) * (-NEG)
            return jnp.concatenate([mb] * GROUP, axis=1)

        def head_tiles(t_ref):
            return jnp.concatenate([t_ref[g * GROUP + r] for r in range(GROUP)], axis=1)

        def chunk(k_ref, vt_ref, t, extra, m_ref, l_ref, a_ref):
            k = k_ref[pl.ds(pl.multiple_of(t * tq, tq), tq), :]
            _flash_cols(_dot_nt(k, qg) + extra, vt_ref[t], m_ref, l_ref, a_ref)

        for m_ref, l_ref, a_ref in ((ms_ref, ls_ref, as_ref), (mw_ref, lw_ref, aw_ref)):
            m_ref[...] = jnp.full(m_ref.shape, M_INIT, F32)
            l_ref[...] = jnp.zeros(l_ref.shape, F32)
            a_ref[...] = jnp.zeros(a_ref.shape, F32)

        def far_body(t, carry):
            chunk(ksk_ref, vst_ref, t, sel_bias(t), ms_ref, ls_ref, as_ref)
            return carry

        lax.fori_loop(0, jnp.maximum(i - 1, 0), far_body, 0)

        @pl.when(i >= 1)
        def _():
            chunk(ksk_ref, vst_ref, i - 1, head_tiles(t1_ref) + sel_bias(i - 1), ms_ref, ls_ref, as_ref)

        chunk(ksk_ref, vst_ref, i, head_tiles(t0_ref) + sel_bias(i), ms_ref, ls_ref, as_ref)

        @pl.when(i >= 2)
        def _():
            chunk(kwk_ref, vwt_ref, i - 2, wmask4, mw_ref, lw_ref, aw_ref)

        @pl.when(i >= 1)
        def _():
            chunk(kwk_ref, vwt_ref, i - 1, head_tiles(t1_ref), mw_ref, lw_ref, aw_ref)

        chunk(kwk_ref, vwt_ref, i, head_tiles(t0_ref), mw_ref, lw_ref, aw_ref)

        o_sel_t = as_ref[...] / jnp.maximum(ls_ref[...], 1e-30)
        o_win_t = aw_ref[...] / jnp.maximum(lw_ref[...], 1e-30)
        per_r = []
        for r in range(GROUP):
            sl = slice(r * tq, (r + 1) * tq)
            col = g * GROUP + r
            per_r.append(gates[:, col:col + 1] * o_cmp[sl]
                         + gates[:, N_HEADS + col:N_HEADS + col + 1] * o_sel_t[:, sl].T
                         + gates[:, 2 * N_HEADS + col:2 * N_HEADS + col + 1] * o_win_t[:, sl].T)
        results.append(per_r)

    for r in range(GROUP):
        o_ref[:, r * LANES:(r + 1) * LANES] = jnp.where(
            lane < HEAD_DIM, results[0][r], results[1][r]).astype(BF16)


def _attn_prompt(q, gates, ks_k, vs_t, kw_k, vw_t, kc2, bias_tiles, nb, t):
    tq = Q_TILE
    assert t % tq == 0 and WINDOW == 2 * tq
    nt = t // tq
    ncp = t // CMP_STRIDE
    nsel = t // SEL_BLOCK
    assert nsel <= LANES and ncp % LANES == 0
    bc, t0, t1 = bias_tiles
    tok = lambda w: pl.BlockSpec((tq, w), lambda b, i: (b * nt + i, 0))
    k_rows = pl.BlockSpec((None, t, KV_HALF), lambda b, i: (b, 0, 0))
    v_cols = pl.BlockSpec((None, nt, KV_HALF, tq), lambda b, i: (b, 0, 0, 0))
    rows = GROUP * tq
    state = [pltpu.VMEM((1, rows), F32), pltpu.VMEM((1, rows), F32), pltpu.VMEM((LANES, rows), F32)]
    return pl.pallas_call(
        functools.partial(_attn_prompt_kernel, ncp=ncp, nsel=nsel),
        grid=(nb, nt),
        in_specs=[tok(q.shape[1]), tok(LANES), k_rows, v_cols, k_rows, v_cols,
                  pl.BlockSpec((None, 2 * ncp, KV_W), lambda b, i: (b, 0, 0)),
                  _resident(bc.shape), _resident(t0.shape), _resident(t1.shape)],
        out_specs=tok(q.shape[1]),
        out_shape=jax.ShapeDtypeStruct(q.shape, BF16),
        scratch_shapes=[pltpu.VMEM((nsel, tq), F32)] + state + state,
        compiler_params=_cparams(("arbitrary", "arbitrary")),
        name="attn_prompt",
    )(q, gates, ks_k.reshape(nb, t, KV_HALF), vs_t.reshape(nb, nt, KV_HALF, tq),
      kw_k.reshape(nb, t, KV_HALF), vw_t.reshape(nb, nt, KV_HALF, tq), kc2, bc, t0, t1)


SAMPLE_CK = 1024


def _flash_val(state, s, v, transposed=False):
    m_prev, l_prev, a_prev = state
    m_new = jnp.maximum(m_prev, jnp.max(s, axis=-1, keepdims=True))
    alpha = jnp.exp(m_prev - m_new)
    p = jnp.exp(s - m_new)
    pv = _dot_nt(p.astype(BF16), v) if transposed else _dot(p.astype(BF16), v)
    return (m_new, alpha * l_prev + jnp.sum(p, axis=-1, keepdims=True), alpha * a_prev + pv)


def _attn_sample_kernel(pt_ref, q_ref, g_ref, knew_ref, wnew_ref, win_ref, kc_ref,
                        bcs_ref, bsl_ref, bnew_ref, bw_ref, msel_ref, cache_ref, o_ref, buf, sem,
                        *, n_pages, page, past, s_len, nsel):
    slot = _gather_pages(cache_ref, pt_ref, buf, sem, n_pages, page)
    q = q_ref[...]
    rows = q.shape[0]
    gs = rows // GROUP
    nselp = msel_ref.shape[1]

    s = _dot_nt(q, kc_ref[:, 0:KV_HALF]) + bcs_ref[...]
    m = jnp.maximum(jnp.max(s, axis=-1, keepdims=True), MASK_FLOOR)
    e = jnp.exp(s - m)
    p = e / jnp.maximum(jnp.sum(e, axis=-1, keepdims=True), 1e-30)
    o_cmp = _dot(p.astype(BF16), kc_ref[:, KV_HALF:KV_W])
    psum = p[0:gs]
    for r in range(1, GROUP):
        psum = psum + p[r * gs:(r + 1) * gs]

    h1, h2, h3 = _split3(psum)
    msel = msel_ref[...]
    pslc = (_dot(h1, msel) + _dot(h2, msel)) + _dot(h3, msel)
    j = lax.broadcasted_iota(jnp.int32, (gs, nselp), 1)
    qpos = past + lax.rem(lax.broadcasted_iota(jnp.int32, (gs, nselp), 0), s_len)
    cur = lax.shift_right_logical(qpos, int(math.log2(SEL_BLOCK)))
    valid = j * SEL_BLOCK <= qpos
    forced = (j == 0) | (j == cur) | (j == cur - 1)
    score = jnp.where(valid, pslc + jnp.where(forced, FORCE_BONUS, 0.0), NEG)
    score = jnp.where(j < nsel, score, -3e38)
    score_t = jnp.concatenate([score, jnp.zeros((LANES - gs, nselp), F32)], axis=0).T
    jp = lax.broadcasted_iota(jnp.int32, (nselp, nselp), 0)
    jj = lax.broadcasted_iota(jnp.int32, (nselp, nselp), 1)
    sel_rows = []
    for r in range(gs):
        col = score_t[:, r:r + 1]
        row = score[r:r + 1, :]
        beats = jnp.where(jj > jp, jnp.where(col >= row, 1.0, 0.0), jnp.where(col > row, 1.0, 0.0))
        rank = jnp.sum(beats, axis=0, keepdims=True)
        sel_rows.append(jnp.where(rank < min(TOP_N, nsel), 1.0, 0.0))
    sel8 = jnp.concatenate(sel_rows, axis=0)
    sel = jnp.concatenate([sel8] * GROUP, axis=0).astype(BF16)

    ck = bsl_ref.shape[1]
    n_chunks = past // ck
    erow = lax.broadcasted_iota(jnp.int32, (nselp, ck), 0)
    eblk = lax.shift_right_logical(lax.broadcasted_iota(jnp.int32, (nselp, ck), 1), int(math.log2(SEL_BLOCK)))
    init = (jnp.full((rows, 1), M_INIT, F32), jnp.zeros((rows, 1), F32), jnp.zeros((rows, LANES), F32))
    st = init
    for t in range(n_chunks):
        kt = buf[slot, 0:KV_HALF, t * ck:(t + 1) * ck].astype(BF16)
        vt = buf[slot, KV_HALF:KV_W, t * ck:(t + 1) * ck].astype(BF16)
        e_t = jnp.where(erow == t * (ck // SEL_BLOCK) + eblk, 1.0, 0.0).astype(BF16)
        sc = _dot(q, kt) + (_dot(sel, e_t) - 1.0) * (-NEG)
        if t == n_chunks - 1:
            sc = sc + bsl_ref[...]
        st = _flash_val(st, sc, vt, transposed=True)
    knew = knew_ref[...]
    st = _flash_val(st, _dot_nt(q, knew[:, 0:KV_HALF]) + bnew_ref[...], knew[:, KV_HALF:KV_W])
    o_sel = st[2] / jnp.maximum(st[1], 1e-30)

    st = _flash_val(init, _dot(q, win_ref[0:KV_HALF, :].astype(BF16)) + bw_ref[...],
                    win_ref[KV_HALF:KV_W, :].astype(BF16), transposed=True)
    wnew = wnew_ref[...]
    st = _flash_val(st, _dot_nt(q, wnew[:, 0:KV_HALF]) + bnew_ref[...], wnew[:, KV_HALF:KV_W])
    o_win = st[2] / jnp.maximum(st[1], 1e-30)

    g = g_ref[...]
    o_ref[...] = g[:, 0:1] * o_cmp + g[:, 1:2] * o_sel + g[:, 2:3] * o_win


def _attn_sample(page_table, q32, g32, knew, wnew, win_t, kvc, tiles, msel, cache_t, past, s_len):
    nb, n_pages = page_table.shape
    page = cache_t.shape[2]
    rows = q32.shape[1]
    nsel = -(-(past + s_len) // SEL_BLOCK)
    bcs, bsl, bnew, bw = tiles
    per_b = lambda r, w: pl.BlockSpec((None, r, w), lambda b, pt: (b, 0, 0))
    return pl.pallas_call(
        functools.partial(_attn_sample_kernel, n_pages=n_pages, page=page, past=past, s_len=s_len, nsel=nsel),
        grid_spec=pltpu.PrefetchScalarGridSpec(
            num_scalar_prefetch=1,
            grid=(nb,),
            in_specs=[per_b(rows, LANES), per_b(rows, LANES), per_b(LANES, KV_W), per_b(LANES, KV_W),
                      per_b(KV_W, win_t.shape[2]), per_b(kvc.shape[1], KV_W),
                      _resident_sp(bcs.shape), _resident_sp(bsl.shape), _resident_sp(bnew.shape),
                      _resident_sp(bw.shape), _resident_sp(msel.shape),
                      pl.BlockSpec(memory_space=pl.ANY)],
            out_specs=per_b(rows, LANES),
            scratch_shapes=[pltpu.VMEM((2, KV_W, past), F32), pltpu.SemaphoreType.DMA((2,))]),
        out_shape=jax.ShapeDtypeStruct((nb, rows, LANES), F32),
        compiler_params=_cparams(("arbitrary",)),
        name="attn_sample",
    )(page_table, q32, g32, knew, wnew, win_t, kvc, bcs, bsl, bnew, bw, msel, cache_t)


def _prep_weights(w_in, w_cmp1, w_cmp2, pe_cmp, w_conv, w_nsa_out):
    d = w_in.shape[0]
    sizes = (N_HEADS * HEAD_DIM, KV_W, KV_W, KV_W, 3 * N_HEADS, _C_W, _C_W, _C_W, 2 * d)
    offs = [0]
    for sz in sizes:
        offs.append(offs[-1] + sz)
    part = lambda k: w_in[:, offs[k]:offs[k + 1]]
    qp = part(0).reshape(d, H_KV, GROUP, HEAD_DIM).transpose(0, 2, 1, 3).reshape(d, N_HEADS * HEAD_DIM)
    gn = jnp.pad(part(4), ((0, 0), (0, _G_W - 3 * N_HEADS)))
    w_in_p = jnp.concatenate([qp, part(1), part(2), part(3), gn, part(5), part(6), part(7), part(8)],
                             axis=1).astype(BF16)
    w_nsa_p = w_nsa_out.reshape(H_KV, GROUP, HEAD_DIM, -1).transpose(1, 0, 2, 3).reshape(
        N_HEADS * HEAD_DIM, -1).astype(BF16)
    r = CMP_BLOCK // CMP_STRIDE
    w1r = w_cmp1.reshape(2, r, CMP_STRIDE // 2, 2, HEAD_DIM, CMP_HIDDEN)
    w1t = jnp.transpose(w1r, (0, 2, 3, 4, 1, 5))
    wb = jnp.zeros((2, CMP_STRIDE // 2, 2, H_KV, HEAD_DIM, r, H_KV, CMP_HIDDEN), F32)
    w2b = jnp.zeros((2, H_KV, CMP_HIDDEN, 2, H_KV, HEAD_DIM), F32)
    for g in range(H_KV):
        wb = wb.at[:, :, :, g, :, :, g, :].set(w1t)
        for c in range(2):
            w2b = w2b.at[c, g, :, c, g, :].set(w_cmp2[c])
    wbig = wb.reshape(2, CMP_STRIDE // 2, 2 * KV_HALF, CMP_OUT_W).astype(BF16)
    w2bd = w2b.reshape(CMP_OUT_W, KV_W).astype(BF16)
    pe_r = pe_cmp.reshape(2, r, CMP_STRIDE, HEAD_DIM).transpose(1, 2, 0, 3)
    pe_rows = jnp.broadcast_to(pe_r[:, :, :, None, :], (r, CMP_STRIDE, 2, H_KV, HEAD_DIM)).reshape(r, CHUNK_W)
    pe8 = jnp.pad(pe_rows, ((0, 8 - r), (0, 0)))
    w_conv8 = jnp.pad(w_conv, ((0, 8 - CONV_WIDTH), (0, 0)))
    return w_in_p, w_nsa_p, wbig, w2bd, pe8, w_conv8


def _sel_matrix(n_c, nselp):
    n = jnp.arange(n_c + 1)[:, None]
    j = jnp.arange(nselp)[None, :]
    return _sel_weights(n - (SEL_BLOCK // CMP_STRIDE) * j).astype(BF16)


def kernel(x_prompt, x_sample, cache_kv_cmp, cache_kv_sel, state_kv_win, state_conv, page_table,
           c_prompt, c_sample, w_ada, b_ada, g_norm, w_ffn1_gu, w_ffn1_down, w_ffn2_gu, w_ffn2_down,
           w_in, w_cmp1, w_cmp2, pe_cmp, w_conv, w_nsa_out, w_conv_out, w_out, rel_bias, g_final):
    assert w_ada.shape[0] == 1, "single-layer trunk"
    nbp, t, d = x_prompt.shape
    nbs, s_len, _ = x_sample.shape
    n_pages = page_table.shape[1]
    page = cache_kv_cmp.shape[2]
    past = n_pages * page
    n_phys = cache_kv_cmp.shape[1]
    wb = state_kv_win.shape[2]
    assert wb == WINDOW and past % SAMPLE_CK == 0

    w_in_p, w_nsa_p, wbig, w2bd, pe8, w_conv8 = _prep_weights(
        w_in[0], w_cmp1[0], w_cmp2[0], pe_cmp[0], w_conv[0], w_nsa_out[0])
    wgu1, wd1 = w_ffn1_gu[0].astype(BF16), w_ffn1_down[0].astype(BF16)
    wgu2, wd2 = w_ffn2_gu[0].astype(BF16), w_ffn2_down[0].astype(BF16)
    w_cv, w_o = w_conv_out[0].astype(BF16), w_out[0].astype(BF16)
    gn = [g_norm[0][k:k + 1] for k in range(N_SUB)]

    n_c_rows = nbp + nbs
    c_all = jnp.pad(jnp.concatenate([c_prompt, c_sample], axis=0), ((0, (-n_c_rows) % 8), (0, 0)))
    mod_all = _ada(c_all, w_ada[0], b_ada[0])
    mod_p = mod_all[:nbp].reshape(nbp * 3 * N_SUB, 1, d)
    mod_s = jnp.transpose(jnp.repeat(mod_all[nbp:n_c_rows].reshape(nbs, 3 * N_SUB, d), s_len, axis=0), (1, 0, 2))

    tm = 512 if t % 512 == 0 else t
    tpb = t // tm
    xp = x_prompt.reshape(nbp * t, d)
    h1 = _ffn(xp, mod_p, 0, gn[0], wgu1, wd1, tm, tpb)
    (q, kvc, kvc_t, kvs_t, kvw_t, ks_k, vs_t, kw_k, vw_t, gates, z, cb, gm) = _inproj(
        h1, mod_p, gn[1], w_in_p, tm, tpb, True)
    kc2 = _cmp_prompt(kvc.reshape(nbp, t // CMP_STRIDE, CHUNK_W), wbig, pe8, w2bd)
    tiles_p = _bias_prompt(rel_bias, t // CMP_STRIDE)
    o_nsa = _attn_prompt(q, gates, ks_k, vs_t, kw_k, vw_t, kc2, tiles_p, nbp, t)
    h2 = _mixout(h1, o_nsa, z, None, cb, gm, mod_p, w_conv8, w_nsa_p, w_cv, w_o, tm, tpb)
    y_prompt = _ffn(h2, mod_p, 2, gn[2], wgu2, wd2, tm, tpb, g_final).reshape(nbp, t, d)

    kv_shape = (2, H_KV, HEAD_DIM)
    kv_out = lambda a: jnp.transpose(a.reshape((1, nbp) + kv_shape + (a.shape[-1],)), (0, 1, 5, 2, 3, 4))
    kv_cmp_p = kv_out(kvc_t)
    kv_sel_p = kv_out(kvs_t)
    keep = min(WINDOW, t)
    kv_win_p = kv_out(kvw_t[:, :, t - keep:])
    conv_p = z.reshape(1, nbp, t, -1)[:, :, t - (CONV_WIDTH - 1):]

    ns = nbs * s_len
    xs = x_sample.reshape(ns, d)
    h1s = _ffn(xs, mod_s, 0, gn[0], wgu1, wd1, ns, 1)
    qs, kvc_s, kvs_s, kvw_s, kvs_sb, kvw_sb, gates_s, z_s, cb_s, gm_s = _inproj(
        h1s, mod_s, gn[1], w_in_p, ns, 1, False)

    pos_minor = lambda a: jnp.transpose(a, (0, 2, 3, 4, 1)).reshape(a.shape[0], KV_W, a.shape[1])
    kvc_past = _cmp_sample(page_table, pos_minor(cache_kv_cmp[0]), wbig, pe8, w2bd)
    n_c = (past + s_len - CMP_BLOCK) // CMP_STRIDE + 1
    nsel = -(-(past + s_len) // SEL_BLOCK)
    nselp = -(-nsel // LANES) * LANES
    rows = N_HEADS * s_len
    head_of_row = [g * GROUP + r for r in range(GROUP) for g in range(H_KV) for _ in range(s_len)]
    tbl_rows = jnp.pad(rel_bias.T[jnp.array(head_of_row)], ((0, 0), (0, LANES - N_BUCKETS)))
    tiles_s = _bias_sample(tbl_rows, past, s_len, past // CMP_STRIDE, SAMPLE_CK, wb)
    msel = _sel_matrix(past // CMP_STRIDE - 1, nselp)

    q5 = qs.reshape(nbs, s_len, GROUP, H_KV, HEAD_DIM).transpose(0, 2, 3, 1, 4)
    q32 = jnp.zeros((nbs, GROUP, H_KV, s_len, H_KV, HEAD_DIM), BF16)
    for g in range(H_KV):
        q32 = q32.at[:, :, g, :, g, :].set(q5[:, :, g])
    q32 = q32.reshape(nbs, rows, LANES)
    g5 = gates_s[:, :3 * N_HEADS].reshape(nbs, s_len, 3, H_KV, GROUP).transpose(0, 4, 3, 1, 2)
    g32 = jnp.pad(g5.reshape(nbs, rows, 3), ((0, 0), (0, 0), (0, LANES - 3)))
    pad_new = lambda a: jnp.pad(a.reshape(nbs, s_len, KV_W), ((0, 0), (0, LANES - s_len), (0, 0)))
    o32 = _attn_sample(page_table, q32, g32, pad_new(kvs_sb), pad_new(kvw_sb),
                       pos_minor(state_kv_win[0]), kvc_past, tiles_s, msel,
                       pos_minor(cache_kv_sel[0]), past, s_len)
    o6 = o32.reshape(nbs, GROUP, H_KV, s_len, H_KV, HEAD_DIM)
    o_s = jnp.stack([o6[:, :, g, :, g, :] for g in range(H_KV)], axis=3)
    o_nsa_s = o_s.transpose(0, 2, 1, 3, 4).reshape(ns, N_HEADS * HEAD_DIM).astype(BF16)

    full = jnp.concatenate([state_conv[0], z_s.reshape(nbs, s_len, -1)], axis=1)
    z_shift = (full[:, 1:1 + s_len].reshape(ns, -1), full[:, 0:s_len].reshape(ns, -1))
    h2s = _mixout(h1s, o_nsa_s, z_s, z_shift, cb_s, gm_s, mod_s, w_conv8, w_nsa_p, w_cv, w_o, ns, 1)
    y_sample = _ffn(h2s, mod_s, 2, gn[2], wgu2, wd2, ns, 1, g_final).reshape(nbs, s_len, d)

    kv_cmp_s = kvc_s.reshape((1, nbs, s_len) + kv_shape)
    kv_sel_s = kvs_s.reshape((1, nbs, s_len) + kv_shape)
    win_full = jnp.concatenate([state_kv_win[0], kvw_s.reshape((nbs, s_len) + kv_shape)], axis=1)
    keep_s = min(WINDOW, wb + s_len)
    kv_win_s = win_full[None, :, wb + s_len - keep_s:]
    conv_s = full[None, :, s_len:]
    return (y_prompt, y_sample, kv_cmp_p, kv_sel_p, kv_win_p, conv_p, kv_cmp_s, kv_sel_s, kv_win_s, conv_s)
```

```python
import functools
import math

import jax
import jax.numpy as jnp
from jax import lax
from jax.experimental import pallas as pl
from jax.experimental.pallas import tpu as pltpu

F32 = jnp.float32
BF16 = jnp.bfloat16

HEAD_DIM = 64
N_HEADS = 8
H_KV = 2
GROUP = N_HEADS // H_KV
CMP_BLOCK = 32
CMP_STRIDE = 16
CMP_HIDDEN = 2 * HEAD_DIM
SEL_BLOCK = 64
TOP_N = 16
WINDOW = 512
CONV_WIDTH = 3
N_BUCKETS = 32
MAX_DISTANCE = 128
N_SUB = 3
EPS = 1e-6
NEG = -1e9
FORCE_BONUS = 1e3
KV_W = 2 * H_KV * HEAD_DIM
KV_HALF = H_KV * HEAD_DIM
LANES = 128
Q_TILE = 256
CMP_NEAR = 32
VMEM_LIMIT = 56 * 1024 * 1024


def _bucket_thresholds():
    max_exact = N_BUCKETS // 2

    def bucket(d):
        if d < max_exact:
            return d
        large = max_exact + int(math.log(d / max_exact) / math.log(MAX_DISTANCE / max_exact)
                                * (N_BUCKETS - max_exact))
        return min(large, N_BUCKETS - 1)

    thr, d = [], 0
    for b in range(N_BUCKETS):
        while bucket(d) < b:
            d += 1
        thr.append(d)
    return tuple(thr)


BUCKET_THR = _bucket_thresholds()
FAR_DIST = BUCKET_THR[-1]


def _cparams(sem):
    return pltpu.CompilerParams(dimension_semantics=sem, vmem_limit_bytes=VMEM_LIMIT)


def _resident(shape):
    nd = len(shape)
    return pl.BlockSpec(shape, lambda *_: (0,) * nd, pipeline_mode=pl.Buffered(1))


def _dot(a, b):
    return jnp.dot(a, b, preferred_element_type=F32)


def _dot_nt(a, b):
    return lax.dot_general(a, b, (((1,), (1,)), ((), ())), preferred_element_type=F32)


def _split3(x):
    h1 = x.astype(BF16)
    r1 = x - h1.astype(F32)
    h2 = r1.astype(BF16)
    h3 = (r1 - h2.astype(F32)).astype(BF16)
    return h1, h2, h3


def _modulated_norm(x, g, shift, scale):
    y = x * lax.rsqrt(jnp.mean(x * x, axis=-1, keepdims=True) + EPS)
    return (y * g) * (1.0 + scale) + shift


def _ada_kernel(c_ref, w_ref, b_ref, o_ref):
    c = c_ref[...]
    a = (c * jax.nn.sigmoid(c)).astype(BF16)
    o_ref[...] = _dot(a, w_ref[...].astype(BF16)) + b_ref[...]


def _ada(c_all, w_ada, b_ada):
    rows, d = c_all.shape
    n = w_ada.shape[1]
    tn = n // 8
    return pl.pallas_call(
        _ada_kernel,
        grid=(n // tn,),
        in_specs=[pl.BlockSpec((rows, d), lambda j: (0, 0)),
                  pl.BlockSpec((d, tn), lambda j: (0, j)),
                  pl.BlockSpec((1, tn), lambda j: (0, j))],
        out_specs=pl.BlockSpec((rows, tn), lambda j: (0, j)),
        out_shape=jax.ShapeDtypeStruct((rows, n), F32),
        compiler_params=_cparams(("arbitrary",)),
        name="ada",
    )(c_all, w_ada, b_ada.reshape(1, n))


def _ffn_kernel(*refs, d_ff, fc, final_norm):
    if final_norm:
        x_ref, sh_ref, sc_ref, gt_ref, gn_ref, wgu_ref, wd_ref, gf_ref, o_ref = refs
    else:
        x_ref, sh_ref, sc_ref, gt_ref, gn_ref, wgu_ref, wd_ref, o_ref = refs
    x = x_ref[...]
    ub = _modulated_norm(x, gn_ref[...], sh_ref[...], sc_ref[...]).astype(BF16)
    acc = jnp.zeros(x.shape, F32)
    for c in range(d_ff // fc):
        g = _dot(ub, wgu_ref[:, c * fc:(c + 1) * fc])
        v = _dot(ub, wgu_ref[:, d_ff + c * fc:d_ff + (c + 1) * fc])
        a = ((g * jax.nn.sigmoid(g)) * v).astype(BF16)
        acc = acc + _dot(a, wd_ref[c * fc:(c + 1) * fc, :])
    h = x + (0.5 * gt_ref[...]) * acc
    if final_norm:
        h = (h * lax.rsqrt(jnp.mean(h * h, axis=-1, keepdims=True) + EPS)) * gf_ref[...]
    o_ref[...] = h


def _mod_specs(mod, ks, tm, tiles_per_batch):
    if mod.ndim == 3 and mod.shape[1] == 1:
        d = mod.shape[-1]
        specs = [pl.BlockSpec((None, 1, d), lambda i, k=k: ((i // tiles_per_batch) * (3 * N_SUB) + k, 0, 0))
                 for k in ks]
        return specs, [mod] * len(ks)
    d = mod.shape[-1]
    specs = [pl.BlockSpec((None, tm, d), lambda i, k=k: (k, i, 0)) for k in ks]
    return specs, [mod] * len(ks)


def _ffn(x, mod, sub, gn_row, w_gu, w_down, tm, tiles_per_batch, g_final=None):
    n, d = x.shape
    d_ff = w_down.shape[0]
    fc = d_ff // 2 if (d_ff // 2) % LANES == 0 else d_ff
    final_norm = g_final is not None
    mspecs, mops = _mod_specs(mod, (3 * sub, 3 * sub + 1, 3 * sub + 2), tm, tiles_per_batch)
    in_specs = [pl.BlockSpec((tm, d), lambda i: (i, 0))] + mspecs + [
        pl.BlockSpec((1, d), lambda i: (0, 0)), _resident(w_gu.shape), _resident(w_down.shape)]
    ops = [x] + mops + [gn_row, w_gu, w_down]
    if final_norm:
        in_specs.append(pl.BlockSpec((1, d), lambda i: (0, 0)))
        ops.append(g_final.reshape(1, d))
    return pl.pallas_call(
        functools.partial(_ffn_kernel, d_ff=d_ff, fc=fc, final_norm=final_norm),
        grid=(n // tm,),
        in_specs=in_specs,
        out_specs=pl.BlockSpec((tm, d), lambda i: (i, 0)),
        out_shape=jax.ShapeDtypeStruct((n, d), F32),
        compiler_params=_cparams(("arbitrary",)),
        name="ffn_final" if final_norm else "ffn",
    )(*ops)


_Q_W = N_HEADS * HEAD_DIM
_G_W = LANES
_C_W = 512
_SEG = {}
_off = 0
for _name, _w in (("q", _Q_W), ("kc", KV_W), ("ks", KV_W), ("kw", KV_W), ("gn", _G_W),
                  ("ch", _C_W), ("cb", _C_W), ("cc", _C_W)):
    _SEG[_name] = (_off, _off + _w)
    _off += _w
_MG_OFF = _off


def _inproj_kernel(*refs, prompt):
    x_ref, sh_ref, sc_ref, gn_ref, w_ref = refs[:5]
    ub = _modulated_norm(x_ref[...], gn_ref[...], sh_ref[...], sc_ref[...]).astype(BF16)

    def seg(name):
        lo, hi = _SEG[name]
        return _dot(ub, w_ref[:, lo:hi])

    if prompt:
        (q_ref, kc_ref, kct_ref, kst_ref, kwt_ref, ksk_ref, vst_ref, kwk_ref, vwt_ref,
         g_ref, z_ref, cb_ref, gm_ref) = refs[5:]
        kc = seg("kc")
        kc_ref[...] = kc
        kct_ref[...] = kc.T
        for name, t_ref, k_ref, vt_ref in (("ks", kst_ref, ksk_ref, vst_ref), ("kw", kwt_ref, kwk_ref, vwt_ref)):
            kv = seg(name)
            kv_t = kv.T
            t_ref[...] = kv_t
            k_ref[...] = kv[:, 0:KV_HALF].astype(BF16)
            for c in range(vt_ref.shape[0]):
                vt_ref[c] = kv_t[KV_HALF:KV_W, c * Q_TILE:(c + 1) * Q_TILE].astype(BF16)
    else:
        q_ref, kc_ref, ks_ref, kw_ref, ksb_ref, kwb_ref, g_ref, z_ref, cb_ref, gm_ref = refs[5:]
        kc_ref[...] = seg("kc")
        ks = seg("ks")
        ks_ref[...] = ks
        ksb_ref[...] = ks.astype(BF16)
        kw = seg("kw")
        kw_ref[...] = kw
        kwb_ref[...] = kw.astype(BF16)
    q_ref[...] = (seg("q") * (HEAD_DIM ** -0.5)).astype(BF16)
    g_ref[...] = jax.nn.sigmoid(seg("gn"))
    z_ref[...] = seg("cc") * seg("ch")
    cb_ref[...] = seg("cb")
    d2 = gm_ref.shape[1]
    half = d2 // 2
    for c in range(2):
        gm_ref[:, c * half:(c + 1) * half] = jax.nn.sigmoid(
            _dot(ub, w_ref[:, _MG_OFF + c * half:_MG_OFF + (c + 1) * half]))


def _inproj(h, mod, gn_row, w_in_p, tm, tiles_per_batch, prompt):
    n, d = h.shape
    d_conv = _C_W
    mspecs, mops = _mod_specs(mod, (3, 4), tm, tiles_per_batch)
    rows = lambda w, dt: (pl.BlockSpec((tm, w), lambda i: (i, 0)), jax.ShapeDtypeStruct((n, w), dt))
    tail = [rows(_G_W, F32), rows(d_conv, F32), rows(d_conv, F32), rows(2 * d, F32)]
    if prompt:
        nb = n // (tm * tiles_per_batch)
        t = tm * tiles_per_batch
        cpt = tm // Q_TILE
        tr = (pl.BlockSpec((None, KV_W, tm), lambda i: (i // tiles_per_batch, 0, i % tiles_per_batch)),
              jax.ShapeDtypeStruct((nb, KV_W, t), F32))
        vt = (pl.BlockSpec((cpt, KV_HALF, Q_TILE), lambda i: (i, 0, 0)),
              jax.ShapeDtypeStruct((n // Q_TILE, KV_HALF, Q_TILE), BF16))
        outs = [rows(_Q_W, BF16), rows(KV_W, F32), tr, tr, tr,
                rows(KV_HALF, BF16), vt, rows(KV_HALF, BF16), vt] + tail
    else:
        outs = [rows(_Q_W, BF16), rows(KV_W, F32), rows(KV_W, F32), rows(KV_W, F32),
                rows(KV_W, BF16), rows(KV_W, BF16)] + tail
    return pl.pallas_call(
        functools.partial(_inproj_kernel, prompt=prompt),
        grid=(n // tm,),
        in_specs=[pl.BlockSpec((tm, d), lambda i: (i, 0))] + mspecs + [
            pl.BlockSpec((1, d), lambda i: (0, 0)), _resident(w_in_p.shape)],
        out_specs=[o[0] for o in outs],
        out_shape=[o[1] for o in outs],
        compiler_params=_cparams(("arbitrary",)),
        name="inproj_prompt" if prompt else "inproj",
    )(h, *mops, gn_row, w_in_p)


def _mixout_kernel(*refs, halo, tiles_per_batch):
    if halo:
        (h_ref, o_ref, z_ref, zp_ref, cb_ref, gm_ref, g2_ref, wc_ref,
         wn_ref, wcv_ref, wo_ref, out_ref) = refs
        z = z_ref[...]
        tm = z.shape[0]
        first = (pl.program_id(0) % tiles_per_batch) == 0
        prev = jnp.where(first, 0.0, zp_ref[...])
        row = lax.broadcasted_iota(jnp.int32, z.shape, 0)
        zm1 = jnp.where(row == 0, prev[7:8, :], pltpu.roll(z, 1, 0))
        zm2 = jnp.where(row == 0, prev[6:7, :], jnp.where(row == 1, prev[7:8, :], pltpu.roll(z, 2, 0)))
    else:
        (h_ref, o_ref, z_ref, zm1_ref, zm2_ref, cb_ref, gm_ref, g2_ref, wc_ref,
         wn_ref, wcv_ref, wo_ref, out_ref) = refs
        z, zm1, zm2 = z_ref[...], zm1_ref[...], zm2_ref[...]
    conv = wc_ref[0:1, :] * zm2 + wc_ref[1:2, :] * zm1 + wc_ref[2:3, :] * z
    y = (cb_ref[...] * conv).astype(BF16)
    d = h_ref.shape[1]
    merged = gm_ref[:, 0:d] * _dot(o_ref[...], wn_ref[...]) + gm_ref[:, d:2 * d] * _dot(y, wcv_ref[...])
    out_ref[...] = h_ref[...] + g2_ref[...] * _dot(merged.astype(BF16), wo_ref[...])


def _mixout(h, o_nsa, z, z_shift, cb, gm, mod, w_conv8, w_nsa_p, w_cv, w_o, tm, tiles_per_batch):
    n, d = h.shape
    dc = z.shape[1]
    halo = z_shift is None
    mspecs, mops = _mod_specs(mod, (5,), tm, tiles_per_batch)
    tok = lambda w: pl.BlockSpec((tm, w), lambda i: (i, 0))
    if halo:
        zspecs = [tok(dc), pl.BlockSpec((8, dc), lambda i: (jnp.maximum(i * (tm // 8) - 1, 0), 0))]
        zops = [z, z]
    else:
        zspecs = [tok(dc), tok(dc), tok(dc)]
        zops = [z, z_shift[0], z_shift[1]]
    return pl.pallas_call(
        functools.partial(_mixout_kernel, halo=halo, tiles_per_batch=tiles_per_batch),
        grid=(n // tm,),
        in_specs=[tok(d), tok(o_nsa.shape[1])] + zspecs + [tok(dc), tok(2 * d)] + mspecs + [
            pl.BlockSpec((8, dc), lambda i: (0, 0)),
            _resident(w_nsa_p.shape), _resident(w_cv.shape), _resident(w_o.shape)],
        out_specs=tok(d),
        out_shape=jax.ShapeDtypeStruct((n, d), F32),
        compiler_params=_cparams(("arbitrary",)),
        name="mixout",
    )(h, o_nsa, *zops, cb, gm, *mops, w_conv8, w_nsa_p, w_cv, w_o)


CHUNK_W = CMP_STRIDE * KV_W
CMP_OUT_W = 2 * H_KV * CMP_HIDDEN


def _gelu_tanh(x):
    return x * (0.5 * (1.0 + jnp.tanh(math.sqrt(2.0 / math.pi) * (x + 0.044715 * (x * x * x)))))


def _chunk_partials(tile, wc_ref):
    accs = []
    for c in range(2):
        acc = None
        for lp in range(CMP_STRIDE // 2):
            x2 = jnp.concatenate([tile(2 * lp, c), tile(2 * lp + 1, c)], axis=1)
            part = _dot(x2, wc_ref[c, lp])
            acc = part if acc is None else acc + part
        accs.append(acc)
    hw = CMP_OUT_W // 2
    return jnp.concatenate([accs[0][:, 0:hw], accs[1][:, 0:hw], accs[0][:, hw:], accs[1][:, hw:]], axis=1)


def _compress_tail(p, wc_ref, pe_ref, w2_ref):
    n_ch = p.shape[0]
    pb = _chunk_partials(
        lambda l, c: pe_ref[:, l * KV_W + c * KV_HALF:l * KV_W + (c + 1) * KV_HALF].astype(BF16), wc_ref)
    pre = (pb[0:1, 0:CMP_OUT_W] + pb[1:2, CMP_OUT_W:]) + p[:, 0:CMP_OUT_W] \
        + pltpu.roll(p[:, CMP_OUT_W:], n_ch - 1, 0)
    out = _dot(_gelu_tanh(pre).astype(BF16), w2_ref[...])
    row = lax.broadcasted_iota(jnp.int32, out.shape, 0)
    return jnp.where(row < n_ch - 1, out, 0.0)


def _cmp_prompt_kernel(x_ref, wc_ref, pe_ref, w2_ref, o_ref):
    n_ch = x_ref.shape[0]
    p = _chunk_partials(
        lambda l, c: x_ref[:, l * KV_W + c * KV_HALF:l * KV_W + (c + 1) * KV_HALF].astype(BF16), wc_ref)
    out = _compress_tail(p, wc_ref, pe_ref, w2_ref).astype(BF16)
    o_ref[0:n_ch, :] = out
    o_ref[n_ch:2 * n_ch, :] = out


def _cmp_prompt(rows, wbig, pe8, w2bd):
    nb, n_ch, _ = rows.shape
    return pl.pallas_call(
        _cmp_prompt_kernel,
        grid=(nb,),
        in_specs=[pl.BlockSpec((None, n_ch, CHUNK_W), lambda b: (b, 0, 0)),
                  _resident(wbig.shape), _resident(pe8.shape), _resident(w2bd.shape)],
        out_specs=pl.BlockSpec((None, 2 * n_ch, KV_W), lambda b: (b, 0, 0)),
        out_shape=jax.ShapeDtypeStruct((nb, 2 * n_ch, KV_W), BF16),
        compiler_params=_cparams(("arbitrary",)),
        name="cmp_prompt",
    )(rows, wbig, pe8, w2bd)


def _page_copies(cache_ref, pt_ref, b, buf, sem, slot, n_pages, page):
    return [pltpu.make_async_copy(cache_ref.at[pt_ref[b, p]],
                                  buf.at[slot, :, pl.ds(p * page, page)],
                                  sem.at[slot]) for p in range(n_pages)]


def _gather_pages(cache_ref, pt_ref, buf, sem, n_pages, page):
    b = pl.program_id(0)
    nb = pl.num_programs(0)
    slot = b % 2

    @pl.when(b == 0)
    def _():
        for c in _page_copies(cache_ref, pt_ref, 0, buf, sem, 0, n_pages, page):
            c.start()

    @pl.when(b + 1 < nb)
    def _():
        for c in _page_copies(cache_ref, pt_ref, b + 1, buf, sem, 1 - slot, n_pages, page):
            c.start()

    for c in _page_copies(cache_ref, pt_ref, b, buf, sem, slot, n_pages, page):
        c.wait()
    return slot


def _cmp_sample_kernel(pt_ref, cache_ref, wc_ref, pe_ref, w2_ref, o_ref, buf, xs_ref, sem, *, n_pages, page):
    slot = _gather_pages(cache_ref, pt_ref, buf, sem, n_pages, page)
    for c in range(2):
        xs_ref[c] = buf[slot, c * KV_HALF:(c + 1) * KV_HALF, :].T
    n_ch = xs_ref.shape[1] // CMP_STRIDE
    p = _chunk_partials(lambda l, c: xs_ref[c, pl.ds(l, n_ch, stride=CMP_STRIDE), :].astype(BF16), wc_ref)
    o_ref[...] = _compress_tail(p, wc_ref, pe_ref, w2_ref).astype(BF16)


def _cmp_sample(page_table, cache_t, wbig, pe8, w2bd):
    nb, n_pages = page_table.shape
    page = cache_t.shape[2]
    past = n_pages * page
    n_ch = past // CMP_STRIDE
    return pl.pallas_call(
        functools.partial(_cmp_sample_kernel, n_pages=n_pages, page=page),
        grid_spec=pltpu.PrefetchScalarGridSpec(
            num_scalar_prefetch=1,
            grid=(nb,),
            in_specs=[pl.BlockSpec(memory_space=pl.ANY),
                      _resident_sp(wbig.shape), _resident_sp(pe8.shape), _resident_sp(w2bd.shape)],
            out_specs=pl.BlockSpec((None, n_ch, KV_W), lambda b, pt: (b, 0, 0)),
            scratch_shapes=[pltpu.VMEM((2, KV_W, past), F32), pltpu.VMEM((2, past, KV_HALF), F32),
                            pltpu.SemaphoreType.DMA((2,))]),
        out_shape=jax.ShapeDtypeStruct((nb, n_ch, KV_W), BF16),
        compiler_params=_cparams(("arbitrary",)),
        name="cmp_sample",
    )(page_table, cache_t, wbig, pe8, w2bd)


def _resident_sp(shape):
    nd = len(shape)
    return pl.BlockSpec(shape, lambda *_: (0,) * nd, pipeline_mode=pl.Buffered(1))


def _bias_values(d, table):
    acc = jnp.zeros(d.shape, F32) + table(0)
    for b in range(1, N_BUCKETS):
        acc = jnp.where(d >= BUCKET_THR[b], table(b), acc)
    return jnp.where(d >= 0, acc - table(N_BUCKETS - 1), NEG)


CMP_BACK = 16


def _bias_prompt_kernel(tbl_ref, bc_ref, t0_ref, t1_ref):
    h = pl.program_id(0)
    table = lambda b: tbl_ref[b, h]
    ncp, tq = bc_ref.shape
    kj = lax.broadcasted_iota(jnp.int32, (tq, tq), 0)
    qi = lax.broadcasted_iota(jnp.int32, (tq, tq), 1)
    t0_ref[...] = _bias_values(qi - kj, table)
    t1_ref[...] = _bias_values(qi - kj + tq, table)
    c = lax.broadcasted_iota(jnp.int32, (ncp, tq), 0)
    qi = lax.broadcasted_iota(jnp.int32, (ncp, tq), 1)
    d = qi - CMP_STRIDE * c + (CMP_STRIDE * CMP_BACK - (CMP_BLOCK - 1))
    bc_ref[...] = jnp.where(c < CMP_NEAR, _bias_values(d, table), 0.0)


def _bias_prompt(rel_bias, ncp):
    tq = Q_TILE
    return pl.pallas_call(
        _bias_prompt_kernel,
        grid=(N_HEADS,),
        in_specs=[pl.BlockSpec(memory_space=pltpu.SMEM)],
        out_specs=[pl.BlockSpec((None, ncp, tq), lambda h: (h, 0, 0)),
                   pl.BlockSpec((None, tq, tq), lambda h: (h, 0, 0)),
                   pl.BlockSpec((None, tq, tq), lambda h: (h, 0, 0))],
        out_shape=[jax.ShapeDtypeStruct((N_HEADS, ncp, tq), F32),
                   jax.ShapeDtypeStruct((N_HEADS, tq, tq), F32),
                   jax.ShapeDtypeStruct((N_HEADS, tq, tq), F32)],
        compiler_params=_cparams(("arbitrary",)),
        name="bias_prompt",
    )(rel_bias)


def _bias_sample_kernel(tbl_ref, bcs_ref, bsl_ref, bnew_ref, bw_ref, *, past, s_len, n_c):
    table = lambda b: tbl_ref[:, b:b + 1]

    def qpos(shape):
        return past + lax.rem(lax.broadcasted_iota(jnp.int32, shape, 0), s_len)

    def lane(shape):
        return lax.broadcasted_iota(jnp.int32, shape, 1)

    sh = bcs_ref.shape
    n = lane(sh)
    bcs_ref[...] = jnp.where(n < n_c, _bias_values(qpos(sh) - (CMP_STRIDE * n + CMP_BLOCK - 1), table), NEG)
    sh = bsl_ref.shape
    bsl_ref[...] = _bias_values(qpos(sh) - (past - sh[1] + lane(sh)), table)
    sh = bnew_ref.shape
    bnew_ref[...] = jnp.where(lane(sh) < s_len, _bias_values(qpos(sh) - (past + lane(sh)), table), NEG)
    sh = bw_ref.shape
    d = qpos(sh) - (past - sh[1] + lane(sh))
    bw_ref[...] = jnp.where(d < WINDOW, _bias_values(d, table), NEG)


def _bias_sample(tbl_rows, past, s_len, ncs, ck, wb):
    rows = tbl_rows.shape[0]
    widths = (ncs, ck, LANES, wb)
    return pl.pallas_call(
        functools.partial(_bias_sample_kernel, past=past, s_len=s_len, n_c=(past + s_len - CMP_BLOCK) // CMP_STRIDE + 1),
        out_shape=[jax.ShapeDtypeStruct((rows, w), F32) for w in widths],
        name="bias_sample",
    )(tbl_rows)


MASK_FLOOR = -1e8
M_INIT = -1e30
TAKEN = -3e38


def _sel_weights(v):
    return jnp.where((v >= 0) & (v <= 2), 2.0, jnp.where((v == -1) | (v == 3), 1.0, 0.0))


def _flash_cols(s, vt, m_ref, l_ref, a_ref, idx):
    m_prev = m_ref[idx]
    m_new = jnp.maximum(m_prev, jnp.max(s, axis=0, keepdims=True))
    alpha = jnp.exp(m_prev - m_new)
    p = jnp.exp(s - m_new)
    l_ref[idx] = alpha * l_ref[idx] + jnp.sum(p, axis=0, keepdims=True)
    m_ref[idx] = m_new
    a_ref[idx] = alpha * a_ref[idx] + _dot(vt, p.astype(BF16))


def _attn_prompt_kernel(q_ref, gate_ref, ksk_ref, vst_ref, kwk_ref, vwt_ref, kc2_ref, bc_ref, t0_ref, t1_ref, o_ref,
                        qm_ref, selt_ref, oc_ref, ms_ref, ls_ref, as_ref, mw_ref, lw_ref, aw_ref,
                        *, ncp, nsel):
    tq = Q_TILE
    i = pl.program_id(1)
    q0 = i * tq
    lane = lax.broadcasted_iota(jnp.int32, (tq, LANES), 1)
    heads = [(g, r) for g in range(H_KV) for r in range(GROUP)]

    for g, r in heads:
        gmask = (lane < HEAD_DIM) if g == 0 else (lane >= HEAD_DIM)
        qm_ref[g * GROUP + r] = jnp.where(gmask, q_ref[:, r * LANES:(r + 1) * LANES], jnp.zeros((tq, LANES), BF16))

    n0 = (tq // CMP_STRIDE) * i - CMP_BACK
    ws = pl.multiple_of(lax.rem(n0 + ncp, ncp), 8)
    cl = lax.broadcasted_iota(jnp.int32, (ncp, tq), 0)
    dm = jnp.where((cl >= CMP_NEAR) & (cl < ncp - n0), NEG, 0.0) + jnp.where(cl < -n0, NEG, 0.0)
    jrow = lax.broadcasted_iota(jnp.int32, (nsel, ncp), 0)
    nn = lax.broadcasted_iota(jnp.int32, (nsel, ncp), 1) + n0
    nn = jnp.where(nn < 0, nn + ncp, nn)
    nn = jnp.where(nn >= ncp, nn - ncp, nn)
    mit = _sel_weights(nn - (SEL_BLOCK // CMP_STRIDE) * jrow).astype(BF16)

    jr = lax.broadcasted_iota(jnp.int32, (nsel, tq), 0)
    qpos = q0 + lax.broadcasted_iota(jnp.int32, (nsel, tq), 1)
    cur = lax.shift_right_logical(qpos, int(math.log2(SEL_BLOCK)))
    invalid_pen = jnp.where(jr * SEL_BLOCK <= qpos, 0.0, NEG)
    bonus = jnp.where((jr == 0) | (jr == cur) | (jr == cur - 1), FORCE_BONUS, 0.0)
    eblk = lax.shift_right_logical(lax.broadcasted_iota(jnp.int32, (tq, LANES), 0), int(math.log2(SEL_BLOCK)))
    ecol = lax.broadcasted_iota(jnp.int32, (tq, LANES), 1)
    kj = lax.broadcasted_iota(jnp.int32, (tq, tq), 0)
    qi = lax.broadcasted_iota(jnp.int32, (tq, tq), 1)
    wmask = jnp.where(kj > qi, 0.0, NEG)
    blocks_per_chunk = tq // SEL_BLOCK

    kc = kc2_ref[pl.ds(ws, ncp), 0:KV_HALF]
    vct = kc2_ref[pl.ds(ws, ncp), KV_HALF:KV_W].astype(F32).T.astype(BF16)
    for g in range(H_KV):
        psum = None
        for r in range(GROUP):
            h = g * GROUP + r
            s = _dot_nt(kc, qm_ref[h]) + bc_ref[h] + dm
            m = jnp.maximum(jnp.max(s, axis=0, keepdims=True), MASK_FLOOR)
            e = jnp.exp(s - m)
            p = e * (1.0 / jnp.maximum(jnp.sum(e, axis=0, keepdims=True), 1e-30))
            oc_ref[h] = _dot(vct, p.astype(BF16))
            psum = p if psum is None else psum + p

        h1, h2, h3 = _split3(psum)
        pslc = (_dot(mit, h1) + _dot(mit, h2)) + _dot(mit, h3)
        score = jnp.where(invalid_pen < 0.0, NEG, pslc + bonus)
        def take_max(_, carry):
            work, sel_acc = carry
            mx = jnp.max(work, axis=0, keepdims=True)
            first = jnp.min(jnp.where(work == mx, jr, nsel), axis=0, keepdims=True)
            hit = jr == first
            return jnp.where(hit, TAKEN, work), jnp.where(hit, 1.0, sel_acc)

        _, sel_t = lax.fori_loop(0, min(TOP_N, nsel), take_max, (score, jnp.zeros((nsel, tq), F32)))
        if nsel < LANES:
            sel_t = jnp.concatenate([sel_t, jnp.zeros((LANES - nsel, tq), F32)], axis=0)
        selt_ref[g] = sel_t.astype(BF16)

    def sel_bias(t, g):
        e_t = jnp.where(ecol == t * blocks_per_chunk + eblk, 1.0, 0.0).astype(BF16)
        return (_dot(e_t, selt_ref[g]) - 1.0) * (-NEG)

    def chunk(k_ref, vt_ref, t, extra, m_ref, l_ref, a_ref):
        k = k_ref[pl.ds(pl.multiple_of(t * tq, tq), tq), :]
        vt = vt_ref[t]

        def scores(h):
            return _dot_nt(k, qm_ref[h]) + extra(h // GROUP, h)

        def softmax(h, s):
            m_prev = m_ref[h]
            m_new = jnp.maximum(m_prev, jnp.max(s, axis=0, keepdims=True))
            alpha = jnp.exp(m_prev - m_new)
            p = jnp.exp(s - m_new)
            l_ref[h] = alpha * l_ref[h] + jnp.sum(p, axis=0, keepdims=True)
            m_ref[h] = m_new
            return p.astype(BF16), alpha

        def accumulate(h, p, alpha):
            a_ref[h] = alpha * a_ref[h] + _dot(vt, p)

        s = {0: scores(0), 1: scores(1)}
        pa = {}
        for h in range(N_HEADS):
            pa[h] = softmax(h, s.pop(h))
            if h + 2 < N_HEADS:
                s[h + 2] = scores(h + 2)
            if h >= 1:
                accumulate(h - 1, *pa.pop(h - 1))
        accumulate(N_HEADS - 1, *pa.pop(N_HEADS - 1))

    for m_ref, l_ref, a_ref in ((ms_ref, ls_ref, as_ref), (mw_ref, lw_ref, aw_ref)):
        m_ref[...] = jnp.full(m_ref.shape, M_INIT, F32)
        l_ref[...] = jnp.zeros(l_ref.shape, F32)
        a_ref[...] = jnp.zeros(a_ref.shape, F32)

    def sel_chunk(t, tile_ref):
        mbs = [sel_bias(t, g) for g in range(H_KV)]
        extra = (lambda g, h: mbs[g]) if tile_ref is None else (lambda g, h: tile_ref[h] + mbs[g])
        chunk(ksk_ref, vst_ref, t, extra, ms_ref, ls_ref, as_ref)

    def far_body(t, carry):
        sel_chunk(t, None)
        return carry

    lax.fori_loop(0, jnp.maximum(i - 1, 0), far_body, 0)

    @pl.when(i >= 1)
    def _():
        sel_chunk(i - 1, t1_ref)

    sel_chunk(i, t0_ref)

    @pl.when(i >= 2)
    def _():
        chunk(kwk_ref, vwt_ref, i - 2, lambda g, h: wmask, mw_ref, lw_ref, aw_ref)

    @pl.when(i >= 1)
    def _():
        chunk(kwk_ref, vwt_ref, i - 1, lambda g, h: t1_ref[h], mw_ref, lw_ref, aw_ref)

    chunk(kwk_ref, vwt_ref, i, lambda g, h: t0_ref[h], mw_ref, lw_ref, aw_ref)

    gates_t = gate_ref[...].T
    row = lax.broadcasted_iota(jnp.int32, (LANES, tq), 0)
    for r in range(GROUP):
        per_g = []
        for g in range(H_KV):
            h = g * GROUP + r
            o_sel = as_ref[h] * (1.0 / jnp.maximum(ls_ref[h], 1e-30))
            o_win = aw_ref[h] * (1.0 / jnp.maximum(lw_ref[h], 1e-30))
            per_g.append(gates_t[h:h + 1, :] * oc_ref[h]
                         + gates_t[N_HEADS + h:N_HEADS + h + 1, :] * o_sel
                         + gates_t[2 * N_HEADS + h:2 * N_HEADS + h + 1, :] * o_win)
        o_ref[:, r * LANES:(r + 1) * LANES] = jnp.where(row < HEAD_DIM, per_g[0], per_g[1]).T.astype(BF16)


def _attn_prompt(q, gates, ks_k, vs_t, kw_k, vw_t, kc2, bias_tiles, nb, t):
    tq = Q_TILE
    assert t % tq == 0 and WINDOW == 2 * tq
    nt = t // tq
    ncp = t // CMP_STRIDE
    nsel = t // SEL_BLOCK
    assert nsel <= LANES and ncp % LANES == 0
    bc, t0, t1 = bias_tiles
    tok = lambda w: pl.BlockSpec((tq, w), lambda b, i: (b * nt + i, 0))
    k_rows = pl.BlockSpec((None, t, KV_HALF), lambda b, i: (b, 0, 0))
    v_cols = pl.BlockSpec((None, nt, KV_HALF, tq), lambda b, i: (b, 0, 0, 0))
    state = [pltpu.VMEM((N_HEADS, 1, tq), F32), pltpu.VMEM((N_HEADS, 1, tq), F32),
             pltpu.VMEM((N_HEADS, LANES, tq), F32)]
    return pl.pallas_call(
        functools.partial(_attn_prompt_kernel, ncp=ncp, nsel=nsel),
        grid=(nb, nt),
        in_specs=[tok(q.shape[1]), tok(LANES), k_rows, v_cols, k_rows, v_cols,
                  pl.BlockSpec((None, 2 * ncp, KV_W), lambda b, i: (b, 0, 0)),
                  _resident(bc.shape), _resident(t0.shape), _resident(t1.shape)],
        out_specs=tok(q.shape[1]),
        out_shape=jax.ShapeDtypeStruct(q.shape, BF16),
        scratch_shapes=[pltpu.VMEM((N_HEADS, tq, LANES), BF16), pltpu.VMEM((H_KV, LANES, tq), BF16), pltpu.VMEM((N_HEADS, LANES, tq), F32)] + state + state,
        compiler_params=_cparams(("arbitrary", "arbitrary")),
        name="attn_prompt",
    )(q, gates, ks_k.reshape(nb, t, KV_HALF), vs_t.reshape(nb, nt, KV_HALF, tq),
      kw_k.reshape(nb, t, KV_HALF), vw_t.reshape(nb, nt, KV_HALF, tq), kc2, bc, t0, t1)


SAMPLE_CK = 1024


def _flash_val(state, s, v, transposed=False):
    m_prev, l_prev, a_prev = state
    m_new = jnp.maximum(m_prev, jnp.max(s, axis=-1, keepdims=True))
    alpha = jnp.exp(m_prev - m_new)
    p = jnp.exp(s - m_new)
    pv = _dot_nt(p.astype(BF16), v) if transposed else _dot(p.astype(BF16), v)
    return (m_new, alpha * l_prev + jnp.sum(p, axis=-1, keepdims=True), alpha * a_prev + pv)


def _attn_sample_kernel(pt_ref, q_ref, g_ref, knew_ref, wnew_ref, win_ref, kc_ref,
                        bcs_ref, bsl_ref, bnew_ref, bw_ref, msel_ref, cache_ref, o_ref, buf, sem,
                        *, n_pages, page, past, s_len, nsel):
    slot = _gather_pages(cache_ref, pt_ref, buf, sem, n_pages, page)
    q = q_ref[...]
    rows = q.shape[0]
    gs = rows // GROUP
    nselp = msel_ref.shape[1]

    s = _dot_nt(q, kc_ref[:, 0:KV_HALF]) + bcs_ref[...]
    m = jnp.maximum(jnp.max(s, axis=-1, keepdims=True), MASK_FLOOR)
    e = jnp.exp(s - m)
    p = e / jnp.maximum(jnp.sum(e, axis=-1, keepdims=True), 1e-30)
    o_cmp = _dot(p.astype(BF16), kc_ref[:, KV_HALF:KV_W])
    psum = p[0:gs]
    for r in range(1, GROUP):
        psum = psum + p[r * gs:(r + 1) * gs]

    h1, h2, h3 = _split3(psum)
    msel = msel_ref[...]
    pslc = (_dot(h1, msel) + _dot(h2, msel)) + _dot(h3, msel)
    j = lax.broadcasted_iota(jnp.int32, (gs, nselp), 1)
    qpos = past + lax.rem(lax.broadcasted_iota(jnp.int32, (gs, nselp), 0), s_len)
    cur = lax.shift_right_logical(qpos, int(math.log2(SEL_BLOCK)))
    valid = j * SEL_BLOCK <= qpos
    forced = (j == 0) | (j == cur) | (j == cur - 1)
    score = jnp.where(valid, pslc + jnp.where(forced, FORCE_BONUS, 0.0), NEG)
    score = jnp.where(j < nsel, score, -3e38)
    score_t = jnp.concatenate([score, jnp.zeros((LANES - gs, nselp), F32)], axis=0).T
    jp = lax.broadcasted_iota(jnp.int32, (nselp, nselp), 0)
    jj = lax.broadcasted_iota(jnp.int32, (nselp, nselp), 1)
    sel_rows = []
    for r in range(gs):
        col = score_t[:, r:r + 1]
        row = score[r:r + 1, :]
        beats = jnp.where(jj > jp, jnp.where(col >= row, 1.0, 0.0), jnp.where(col > row, 1.0, 0.0))
        rank = jnp.sum(beats, axis=0, keepdims=True)
        sel_rows.append(jnp.where(rank < min(TOP_N, nsel), 1.0, 0.0))
    sel8 = jnp.concatenate(sel_rows, axis=0)
    sel = jnp.concatenate([sel8] * GROUP, axis=0).astype(BF16)

    ck = bsl_ref.shape[1]
    n_chunks = past // ck
    erow = lax.broadcasted_iota(jnp.int32, (nselp, ck), 0)
    eblk = lax.shift_right_logical(lax.broadcasted_iota(jnp.int32, (nselp, ck), 1), int(math.log2(SEL_BLOCK)))
    init = (jnp.full((rows, 1), M_INIT, F32), jnp.zeros((rows, 1), F32), jnp.zeros((rows, LANES), F32))
    st = init
    for t in range(n_chunks):
        kt = buf[slot, 0:KV_HALF, t * ck:(t + 1) * ck].astype(BF16)
        vt = buf[slot, KV_HALF:KV_W, t * ck:(t + 1) * ck].astype(BF16)
        e_t = jnp.where(erow == t * (ck // SEL_BLOCK) + eblk, 1.0, 0.0).astype(BF16)
        sc = _dot(q, kt) + (_dot(sel, e_t) - 1.0) * (-NEG)
        if t == n_chunks - 1:
            sc = sc + bsl_ref[...]
        st = _flash_val(st, sc, vt, transposed=True)
    knew = knew_ref[...]
    st = _flash_val(st, _dot_nt(q, knew[:, 0:KV_HALF]) + bnew_ref[...], knew[:, KV_HALF:KV_W])
    o_sel = st[2] / jnp.maximum(st[1], 1e-30)

    st = _flash_val(init, _dot(q, win_ref[0:KV_HALF, :].astype(BF16)) + bw_ref[...],
                    win_ref[KV_HALF:KV_W, :].astype(BF16), transposed=True)
    wnew = wnew_ref[...]
    st = _flash_val(st, _dot_nt(q, wnew[:, 0:KV_HALF]) + bnew_ref[...], wnew[:, KV_HALF:KV_W])
    o_win = st[2] / jnp.maximum(st[1], 1e-30)

    g = g_ref[...]
    o_ref[...] = g[:, 0:1] * o_cmp + g[:, 1:2] * o_sel + g[:, 2:3] * o_win


def _attn_sample(page_table, q32, g32, knew, wnew, win_t, kvc, tiles, msel, cache_t, past, s_len):
    nb, n_pages = page_table.shape
    page = cache_t.shape[2]
    rows = q32.shape[1]
    nsel = -(-(past + s_len) // SEL_BLOCK)
    bcs, bsl, bnew, bw = tiles
    per_b = lambda r, w: pl.BlockSpec((None, r, w), lambda b, pt: (b, 0, 0))
    return pl.pallas_call(
        functools.partial(_attn_sample_kernel, n_pages=n_pages, page=page, past=past, s_len=s_len, nsel=nsel),
        grid_spec=pltpu.PrefetchScalarGridSpec(
            num_scalar_prefetch=1,
            grid=(nb,),
            in_specs=[per_b(rows, LANES), per_b(rows, LANES), per_b(LANES, KV_W), per_b(LANES, KV_W),
                      per_b(KV_W, win_t.shape[2]), per_b(kvc.shape[1], KV_W),
                      _resident_sp(bcs.shape), _resident_sp(bsl.shape), _resident_sp(bnew.shape),
                      _resident_sp(bw.shape), _resident_sp(msel.shape),
                      pl.BlockSpec(memory_space=pl.ANY)],
            out_specs=per_b(rows, LANES),
            scratch_shapes=[pltpu.VMEM((2, KV_W, past), F32), pltpu.SemaphoreType.DMA((2,))]),
        out_shape=jax.ShapeDtypeStruct((nb, rows, LANES), F32),
        compiler_params=_cparams(("arbitrary",)),
        name="attn_sample",
    )(page_table, q32, g32, knew, wnew, win_t, kvc, bcs, bsl, bnew, bw, msel, cache_t)


def _prep_weights(w_in, w_cmp1, w_cmp2, pe_cmp, w_conv, w_nsa_out):
    d = w_in.shape[0]
    sizes = (N_HEADS * HEAD_DIM, KV_W, KV_W, KV_W, 3 * N_HEADS, _C_W, _C_W, _C_W, 2 * d)
    offs = [0]
    for sz in sizes:
        offs.append(offs[-1] + sz)
    part = lambda k: w_in[:, offs[k]:offs[k + 1]]
    qp = part(0).reshape(d, H_KV, GROUP, HEAD_DIM).transpose(0, 2, 1, 3).reshape(d, N_HEADS * HEAD_DIM)
    gn = jnp.pad(part(4), ((0, 0), (0, _G_W - 3 * N_HEADS)))
    w_in_p = jnp.concatenate([qp, part(1), part(2), part(3), gn, part(5), part(6), part(7), part(8)],
                             axis=1).astype(BF16)
    w_nsa_p = w_nsa_out.reshape(H_KV, GROUP, HEAD_DIM, -1).transpose(1, 0, 2, 3).reshape(
        N_HEADS * HEAD_DIM, -1).astype(BF16)
    r = CMP_BLOCK // CMP_STRIDE
    w1r = w_cmp1.reshape(2, r, CMP_STRIDE // 2, 2, HEAD_DIM, CMP_HIDDEN)
    w1t = jnp.transpose(w1r, (0, 2, 3, 4, 1, 5))
    wb = jnp.zeros((2, CMP_STRIDE // 2, 2, H_KV, HEAD_DIM, r, H_KV, CMP_HIDDEN), F32)
    w2b = jnp.zeros((2, H_KV, CMP_HIDDEN, 2, H_KV, HEAD_DIM), F32)
    for g in range(H_KV):
        wb = wb.at[:, :, :, g, :, :, g, :].set(w1t)
        for c in range(2):
            w2b = w2b.at[c, g, :, c, g, :].set(w_cmp2[c])
    wbig = wb.reshape(2, CMP_STRIDE // 2, 2 * KV_HALF, CMP_OUT_W).astype(BF16)
    w2bd = w2b.reshape(CMP_OUT_W, KV_W).astype(BF16)
    pe_r = pe_cmp.reshape(2, r, CMP_STRIDE, HEAD_DIM).transpose(1, 2, 0, 3)
    pe_rows = jnp.broadcast_to(pe_r[:, :, :, None, :], (r, CMP_STRIDE, 2, H_KV, HEAD_DIM)).reshape(r, CHUNK_W)
    pe8 = jnp.pad(pe_rows, ((0, 8 - r), (0, 0)))
    w_conv8 = jnp.pad(w_conv, ((0, 8 - CONV_WIDTH), (0, 0)))
    return w_in_p, w_nsa_p, wbig, w2bd, pe8, w_conv8


def _sel_matrix(n_c, nselp):
    n = jnp.arange(n_c + 1)[:, None]
    j = jnp.arange(nselp)[None, :]
    return _sel_weights(n - (SEL_BLOCK // CMP_STRIDE) * j).astype(BF16)


def kernel(x_prompt, x_sample, cache_kv_cmp, cache_kv_sel, state_kv_win, state_conv, page_table,
           c_prompt, c_sample, w_ada, b_ada, g_norm, w_ffn1_gu, w_ffn1_down, w_ffn2_gu, w_ffn2_down,
           w_in, w_cmp1, w_cmp2, pe_cmp, w_conv, w_nsa_out, w_conv_out, w_out, rel_bias, g_final):
    assert w_ada.shape[0] == 1, "single-layer trunk"
    nbp, t, d = x_prompt.shape
    nbs, s_len, _ = x_sample.shape
    n_pages = page_table.shape[1]
    page = cache_kv_cmp.shape[2]
    past = n_pages * page
    n_phys = cache_kv_cmp.shape[1]
    wb = state_kv_win.shape[2]
    assert wb == WINDOW and past % SAMPLE_CK == 0

    w_in_p, w_nsa_p, wbig, w2bd, pe8, w_conv8 = _prep_weights(
        w_in[0], w_cmp1[0], w_cmp2[0], pe_cmp[0], w_conv[0], w_nsa_out[0])
    wgu1, wd1 = w_ffn1_gu[0].astype(BF16), w_ffn1_down[0].astype(BF16)
    wgu2, wd2 = w_ffn2_gu[0].astype(BF16), w_ffn2_down[0].astype(BF16)
    w_cv, w_o = w_conv_out[0].astype(BF16), w_out[0].astype(BF16)
    gn = [g_norm[0][k:k + 1] for k in range(N_SUB)]

    n_c_rows = nbp + nbs
    c_all = jnp.pad(jnp.concatenate([c_prompt, c_sample], axis=0), ((0, (-n_c_rows) % 8), (0, 0)))
    mod_all = _ada(c_all, w_ada[0], b_ada[0])
    mod_p = mod_all[:nbp].reshape(nbp * 3 * N_SUB, 1, d)
    mod_s = jnp.transpose(jnp.repeat(mod_all[nbp:n_c_rows].reshape(nbs, 3 * N_SUB, d), s_len, axis=0), (1, 0, 2))

    tm = 512 if t % 512 == 0 else t
    tpb = t // tm
    xp = x_prompt.reshape(nbp * t, d)
    h1 = _ffn(xp, mod_p, 0, gn[0], wgu1, wd1, tm, tpb)
    (q, kvc, kvc_t, kvs_t, kvw_t, ks_k, vs_t, kw_k, vw_t, gates, z, cb, gm) = _inproj(
        h1, mod_p, gn[1], w_in_p, tm, tpb, True)
    kc2 = _cmp_prompt(kvc.reshape(nbp, t // CMP_STRIDE, CHUNK_W), wbig, pe8, w2bd)
    tiles_p = _bias_prompt(rel_bias, t // CMP_STRIDE)
    o_nsa = _attn_prompt(q, gates, ks_k, vs_t, kw_k, vw_t, kc2, tiles_p, nbp, t)
    h2 = _mixout(h1, o_nsa, z, None, cb, gm, mod_p, w_conv8, w_nsa_p, w_cv, w_o, tm, tpb)
    y_prompt = _ffn(h2, mod_p, 2, gn[2], wgu2, wd2, tm, tpb, g_final).reshape(nbp, t, d)

    kv_shape = (2, H_KV, HEAD_DIM)
    kv_out = lambda a: jnp.transpose(a.reshape((1, nbp) + kv_shape + (a.shape[-1],)), (0, 1, 5, 2, 3, 4))
    kv_cmp_p = kv_out(kvc_t)
    kv_sel_p = kv_out(kvs_t)
    keep = min(WINDOW, t)
    kv_win_p = kv_out(kvw_t[:, :, t - keep:])
    conv_p = z.reshape(1, nbp, t, -1)[:, :, t - (CONV_WIDTH - 1):]

    ns = nbs * s_len
    xs = x_sample.reshape(ns, d)
    h1s = _ffn(xs, mod_s, 0, gn[0], wgu1, wd1, ns, 1)
    qs, kvc_s, kvs_s, kvw_s, kvs_sb, kvw_sb, gates_s, z_s, cb_s, gm_s = _inproj(
        h1s, mod_s, gn[1], w_in_p, ns, 1, False)

    pos_minor = lambda a: jnp.transpose(a, (0, 2, 3, 4, 1)).reshape(a.shape[0], KV_W, a.shape[1])
    kvc_past = _cmp_sample(page_table, pos_minor(cache_kv_cmp[0]), wbig, pe8, w2bd)
    n_c = (past + s_len - CMP_BLOCK) // CMP_STRIDE + 1
    nsel = -(-(past + s_len) // SEL_BLOCK)
    nselp = -(-nsel // LANES) * LANES
    rows = N_HEADS * s_len
    head_of_row = [g * GROUP + r for r in range(GROUP) for g in range(H_KV) for _ in range(s_len)]
    tbl_rows = jnp.pad(rel_bias.T[jnp.array(head_of_row)], ((0, 0), (0, LANES - N_BUCKETS)))
    tiles_s = _bias_sample(tbl_rows, past, s_len, past // CMP_STRIDE, SAMPLE_CK, wb)
    msel = _sel_matrix(past // CMP_STRIDE - 1, nselp)

    q5 = qs.reshape(nbs, s_len, GROUP, H_KV, HEAD_DIM).transpose(0, 2, 3, 1, 4)
    q32 = jnp.zeros((nbs, GROUP, H_KV, s_len, H_KV, HEAD_DIM), BF16)
    for g in range(H_KV):
        q32 = q32.at[:, :, g, :, g, :].set(q5[:, :, g])
    q32 = q32.reshape(nbs, rows, LANES)
    g5 = gates_s[:, :3 * N_HEADS].reshape(nbs, s_len, 3, H_KV, GROUP).transpose(0, 4, 3, 1, 2)
    g32 = jnp.pad(g5.reshape(nbs, rows, 3), ((0, 0), (0, 0), (0, LANES - 3)))
    pad_new = lambda a: jnp.pad(a.reshape(nbs, s_len, KV_W), ((0, 0), (0, LANES - s_len), (0, 0)))
    o32 = _attn_sample(page_table, q32, g32, pad_new(kvs_sb), pad_new(kvw_sb),
                       pos_minor(state_kv_win[0]), kvc_past, tiles_s, msel,
                       pos_minor(cache_kv_sel[0]), past, s_len)
    o6 = o32.reshape(nbs, GROUP, H_KV, s_len, H_KV, HEAD_DIM)
    o_s = jnp.stack([o6[:, :, g, :, g, :] for g in range(H_KV)], axis=3)
    o_nsa_s = o_s.transpose(0, 2, 1, 3, 4).reshape(ns, N_HEADS * HEAD_DIM).astype(BF16)

    full = jnp.concatenate([state_conv[0], z_s.reshape(nbs, s_len, -1)], axis=1)
    z_shift = (full[:, 1:1 + s_len].reshape(ns, -1), full[:, 0:s_len].reshape(ns, -1))
    h2s = _mixout(h1s, o_nsa_s, z_s, z_shift, cb_s, gm_s, mod_s, w_conv8, w_nsa_p, w_cv, w_o, ns, 1)
    y_sample = _ffn(h2s, mod_s, 2, gn[2], wgu2, wd2, ns, 1, g_final).reshape(nbs, s_len, d)

    kv_cmp_s = kvc_s.reshape((1, nbs, s_len) + kv_shape)
    kv_sel_s = kvs_s.reshape((1, nbs, s_len) + kv_shape)
    win_full = jnp.concatenate([state_kv_win[0], kvw_s.reshape((nbs, s_len) + kv_shape)], axis=1)
    keep_s = min(WINDOW, wb + s_len)
    kv_win_s = win_full[None, :, wb + s_len - keep_s:]
    conv_s = full[None, :, s_len:]
    return (y_prompt, y_sample, kv_cmp_p, kv_sel_p, kv_win_p, conv_p, kv_cmp_s, kv_sel_s, kv_win_s, conv_s)
```

```python
import functools
import math

import jax
import jax.numpy as jnp
from jax import lax
from jax.experimental import pallas as pl
from jax.experimental.pallas import tpu as pltpu

F32 = jnp.float32
BF16 = jnp.bfloat16

HEAD_DIM = 64
N_HEADS = 8
H_KV = 2
GROUP = N_HEADS // H_KV
CMP_BLOCK = 32
CMP_STRIDE = 16
CMP_HIDDEN = 2 * HEAD_DIM
SEL_BLOCK = 64
TOP_N = 16
WINDOW = 512
CONV_WIDTH = 3
N_BUCKETS = 32
MAX_DISTANCE = 128
N_SUB = 3
EPS = 1e-6
NEG = -1e9
FORCE_BONUS = 1e3
KV_W = 2 * H_KV * HEAD_DIM
KV_HALF = H_KV * HEAD_DIM
LANES = 128
Q_TILE = 256
CMP_NEAR = 32
VMEM_LIMIT = 56 * 1024 * 1024


def _bucket_thresholds():
    max_exact = N_BUCKETS // 2

    def bucket(d):
        if d < max_exact:
            return d
        large = max_exact + int(math.log(d / max_exact) / math.log(MAX_DISTANCE / max_exact)
                                * (N_BUCKETS - max_exact))
        return min(large, N_BUCKETS - 1)

    thr, d = [], 0
    for b in range(N_BUCKETS):
        while bucket(d) < b:
            d += 1
        thr.append(d)
    return tuple(thr)


BUCKET_THR = _bucket_thresholds()
FAR_DIST = BUCKET_THR[-1]


def _cparams(sem):
    return pltpu.CompilerParams(dimension_semantics=sem, vmem_limit_bytes=VMEM_LIMIT)


def _resident(shape):
    nd = len(shape)
    return pl.BlockSpec(shape, lambda *_: (0,) * nd, pipeline_mode=pl.Buffered(1))


def _dot(a, b):
    return jnp.dot(a, b, preferred_element_type=F32)


def _dot_nt(a, b):
    return lax.dot_general(a, b, (((1,), (1,)), ((), ())), preferred_element_type=F32)


def _split3(x):
    h1 = x.astype(BF16)
    r1 = x - h1.astype(F32)
    h2 = r1.astype(BF16)
    h3 = (r1 - h2.astype(F32)).astype(BF16)
    return h1, h2, h3


def _modulated_norm(x, g, shift, scale):
    y = x * lax.rsqrt(jnp.mean(x * x, axis=-1, keepdims=True) + EPS)
    return (y * g) * (1.0 + scale) + shift


def _ada_kernel(c_ref, w_ref, b_ref, o_ref):
    c = c_ref[...]
    a = (c * jax.nn.sigmoid(c)).astype(BF16)
    o_ref[...] = _dot(a, w_ref[...].astype(BF16)) + b_ref[...]


def _ada(c_all, w_ada, b_ada):
    rows, d = c_all.shape
    n = w_ada.shape[1]
    tn = n // 8
    return pl.pallas_call(
        _ada_kernel,
        grid=(n // tn,),
        in_specs=[pl.BlockSpec((rows, d), lambda j: (0, 0)),
                  pl.BlockSpec((d, tn), lambda j: (0, j)),
                  pl.BlockSpec((1, tn), lambda j: (0, j))],
        out_specs=pl.BlockSpec((rows, tn), lambda j: (0, j)),
        out_shape=jax.ShapeDtypeStruct((rows, n), F32),
        compiler_params=_cparams(("arbitrary",)),
        name="ada",
    )(c_all, w_ada, b_ada.reshape(1, n))


def _ffn_kernel(*refs, d_ff, fc, final_norm):
    if final_norm:
        x_ref, sh_ref, sc_ref, gt_ref, gn_ref, wgu_ref, wd_ref, gf_ref, o_ref = refs
    else:
        x_ref, sh_ref, sc_ref, gt_ref, gn_ref, wgu_ref, wd_ref, o_ref = refs
    x = x_ref[...]
    ub = _modulated_norm(x, gn_ref[...], sh_ref[...], sc_ref[...]).astype(BF16)
    acc = jnp.zeros(x.shape, F32)
    for c in range(d_ff // fc):
        g = _dot(ub, wgu_ref[:, c * fc:(c + 1) * fc])
        v = _dot(ub, wgu_ref[:, d_ff + c * fc:d_ff + (c + 1) * fc])
        a = ((g * jax.nn.sigmoid(g)) * v).astype(BF16)
        acc = acc + _dot(a, wd_ref[c * fc:(c + 1) * fc, :])
    h = x + (0.5 * gt_ref[...]) * acc
    if final_norm:
        h = (h * lax.rsqrt(jnp.mean(h * h, axis=-1, keepdims=True) + EPS)) * gf_ref[...]
    o_ref[...] = h


def _mod_specs(mod, ks, tm, tiles_per_batch):
    if mod.ndim == 3 and mod.shape[1] == 1:
        d = mod.shape[-1]
        specs = [pl.BlockSpec((None, 1, d), lambda i, k=k: ((i // tiles_per_batch) * (3 * N_SUB) + k, 0, 0))
                 for k in ks]
        return specs, [mod] * len(ks)
    d = mod.shape[-1]
    specs = [pl.BlockSpec((None, tm, d), lambda i, k=k: (k, i, 0)) for k in ks]
    return specs, [mod] * len(ks)


def _ffn(x, mod, sub, gn_row, w_gu, w_down, tm, tiles_per_batch, g_final=None):
    n, d = x.shape
    d_ff = w_down.shape[0]
    fc = d_ff // 2 if (d_ff // 2) % LANES == 0 else d_ff
    final_norm = g_final is not None
    mspecs, mops = _mod_specs(mod, (3 * sub, 3 * sub + 1, 3 * sub + 2), tm, tiles_per_batch)
    in_specs = [pl.BlockSpec((tm, d), lambda i: (i, 0))] + mspecs + [
        pl.BlockSpec((1, d), lambda i: (0, 0)), _resident(w_gu.shape), _resident(w_down.shape)]
    ops = [x] + mops + [gn_row, w_gu, w_down]
    if final_norm:
        in_specs.append(pl.BlockSpec((1, d), lambda i: (0, 0)))
        ops.append(g_final.reshape(1, d))
    return pl.pallas_call(
        functools.partial(_ffn_kernel, d_ff=d_ff, fc=fc, final_norm=final_norm),
        grid=(n // tm,),
        in_specs=in_specs,
        out_specs=pl.BlockSpec((tm, d), lambda i: (i, 0)),
        out_shape=jax.ShapeDtypeStruct((n, d), F32),
        compiler_params=_cparams(("arbitrary",)),
        name="ffn_final" if final_norm else "ffn",
    )(*ops)


_Q_W = N_HEADS * HEAD_DIM
_G_W = LANES
_C_W = 512
_SEG = {}
_off = 0
for _name, _w in (("q", _Q_W), ("kc", KV_W), ("ks", KV_W), ("kw", KV_W), ("gn", _G_W),
                  ("ch", _C_W), ("cb", _C_W), ("cc", _C_W)):
    _SEG[_name] = (_off, _off + _w)
    _off += _w
_MG_OFF = _off


V_ROWS = HEAD_DIM + 16


def _aug_lane(g):
    return HEAD_DIM if g == 0 else 0


def _inproj_kernel(*refs, prompt):
    x_ref, sh_ref, sc_ref, gn_ref, w_ref = refs[:5]
    ub = _modulated_norm(x_ref[...], gn_ref[...], sh_ref[...], sc_ref[...]).astype(BF16)

    def seg(name):
        lo, hi = _SEG[name]
        return _dot(ub, w_ref[:, lo:hi])

    if prompt:
        (q_ref, kc_ref, kct_ref, kst_ref, kwt_ref, ks0_ref, ks1_ref, vs0_ref, vs1_ref, kwk_ref, vw0_ref, vw1_ref,
         g_ref, z_ref, cb_ref, gm_ref) = refs[5:]
        kc = seg("kc")
        kc_ref[...] = kc
        kct_ref[...] = kc.T
        tm = kc.shape[0]
        lane = lax.broadcasted_iota(jnp.int32, (tm, KV_HALF), 1)
        blk = lax.shift_right_logical(lax.broadcasted_iota(jnp.int32, (tm, KV_HALF), 0),
                                      int(math.log2(SEL_BLOCK))) & (Q_TILE // SEL_BLOCK - 1)
        ones_rows = jnp.where(lax.broadcasted_iota(jnp.int32, (V_ROWS - HEAD_DIM, tm), 0) == 0, 1.0, 0.0)
        for name, t_ref, k_refs, vt_refs in (("ks", kst_ref, (ks0_ref, ks1_ref), (vs0_ref, vs1_ref)),
                                             ("kw", kwt_ref, (kwk_ref,), (vw0_ref, vw1_ref))):
            kv = seg(name)
            kv_t = kv.T
            t_ref[...] = kv_t
            kk = kv[:, 0:KV_HALF]
            if len(k_refs) == 1:
                k_refs[0][...] = kk.astype(BF16)
            else:
                for g, k_ref in enumerate(k_refs):
                    own = (lane < HEAD_DIM) if g == 0 else (lane >= HEAD_DIM)
                    onehot = jnp.where(lane - _aug_lane(g) == blk, 1.0, 0.0)
                    k_ref[...] = jnp.where(own, kk, onehot).astype(BF16)
            for g, vt_ref in enumerate(vt_refs):
                vg = jnp.concatenate(
                    [kv_t[KV_HALF + g * HEAD_DIM:KV_HALF + (g + 1) * HEAD_DIM, :], ones_rows], axis=0)
                for c in range(vt_ref.shape[0]):
                    vt_ref[c] = vg[:, c * Q_TILE:(c + 1) * Q_TILE].astype(BF16)
    else:
        q_ref, kc_ref, ks_ref, kw_ref, ksb_ref, kwb_ref, g_ref, z_ref, cb_ref, gm_ref = refs[5:]
        kc_ref[...] = seg("kc")
        ks = seg("ks")
        ks_ref[...] = ks
        ksb_ref[...] = ks.astype(BF16)
        kw = seg("kw")
        kw_ref[...] = kw
        kwb_ref[...] = kw.astype(BF16)
    q_ref[...] = (seg("q") * (HEAD_DIM ** -0.5)).astype(BF16)
    g_ref[...] = jax.nn.sigmoid(seg("gn"))
    z_ref[...] = seg("cc") * seg("ch")
    cb_ref[...] = seg("cb")
    d2 = gm_ref.shape[1]
    half = d2 // 2
    for c in range(2):
        gm_ref[:, c * half:(c + 1) * half] = jax.nn.sigmoid(
            _dot(ub, w_ref[:, _MG_OFF + c * half:_MG_OFF + (c + 1) * half]))


def _inproj(h, mod, gn_row, w_in_p, tm, tiles_per_batch, prompt):
    n, d = h.shape
    d_conv = _C_W
    mspecs, mops = _mod_specs(mod, (3, 4), tm, tiles_per_batch)
    rows = lambda w, dt: (pl.BlockSpec((tm, w), lambda i: (i, 0)), jax.ShapeDtypeStruct((n, w), dt))
    tail = [rows(_G_W, F32), rows(d_conv, F32), rows(d_conv, F32), rows(2 * d, F32)]
    if prompt:
        nb = n // (tm * tiles_per_batch)
        t = tm * tiles_per_batch
        cpt = tm // Q_TILE
        tr = (pl.BlockSpec((None, KV_W, tm), lambda i: (i // tiles_per_batch, 0, i % tiles_per_batch)),
              jax.ShapeDtypeStruct((nb, KV_W, t), F32))
        vt = (pl.BlockSpec((cpt, V_ROWS, Q_TILE), lambda i: (i, 0, 0)),
              jax.ShapeDtypeStruct((n // Q_TILE, V_ROWS, Q_TILE), BF16))
        kr = rows(KV_HALF, BF16)
        outs = [rows(_Q_W, BF16), rows(KV_W, F32), tr, tr, tr, kr, kr, vt, vt, kr, vt, vt] + tail
    else:
        outs = [rows(_Q_W, BF16), rows(KV_W, F32), rows(KV_W, F32), rows(KV_W, F32),
                rows(KV_W, BF16), rows(KV_W, BF16)] + tail
    return pl.pallas_call(
        functools.partial(_inproj_kernel, prompt=prompt),
        grid=(n // tm,),
        in_specs=[pl.BlockSpec((tm, d), lambda i: (i, 0))] + mspecs + [
            pl.BlockSpec((1, d), lambda i: (0, 0)), _resident(w_in_p.shape)],
        out_specs=[o[0] for o in outs],
        out_shape=[o[1] for o in outs],
        compiler_params=_cparams(("arbitrary",)),
        name="inproj_prompt" if prompt else "inproj",
    )(h, *mops, gn_row, w_in_p)


def _mixout_kernel(*refs, halo, tiles_per_batch):
    if halo:
        (h_ref, o_ref, z_ref, zp_ref, cb_ref, gm_ref, g2_ref, wc_ref,
         wn_ref, wcv_ref, wo_ref, out_ref) = refs
        z = z_ref[...]
        tm = z.shape[0]
        first = (pl.program_id(0) % tiles_per_batch) == 0
        prev = jnp.where(first, 0.0, zp_ref[...])
        row = lax.broadcasted_iota(jnp.int32, z.shape, 0)
        zm1 = jnp.where(row == 0, prev[7:8, :], pltpu.roll(z, 1, 0))
        zm2 = jnp.where(row == 0, prev[6:7, :], jnp.where(row == 1, prev[7:8, :], pltpu.roll(z, 2, 0)))
    else:
        (h_ref, o_ref, z_ref, zm1_ref, zm2_ref, cb_ref, gm_ref, g2_ref, wc_ref,
         wn_ref, wcv_ref, wo_ref, out_ref) = refs
        z, zm1, zm2 = z_ref[...], zm1_ref[...], zm2_ref[...]
    conv = wc_ref[0:1, :] * zm2 + wc_ref[1:2, :] * zm1 + wc_ref[2:3, :] * z
    y = (cb_ref[...] * conv).astype(BF16)
    d = h_ref.shape[1]
    merged = gm_ref[:, 0:d] * _dot(o_ref[...], wn_ref[...]) + gm_ref[:, d:2 * d] * _dot(y, wcv_ref[...])
    out_ref[...] = h_ref[...] + g2_ref[...] * _dot(merged.astype(BF16), wo_ref[...])


def _mixout(h, o_nsa, z, z_shift, cb, gm, mod, w_conv8, w_nsa_p, w_cv, w_o, tm, tiles_per_batch):
    n, d = h.shape
    dc = z.shape[1]
    halo = z_shift is None
    mspecs, mops = _mod_specs(mod, (5,), tm, tiles_per_batch)
    tok = lambda w: pl.BlockSpec((tm, w), lambda i: (i, 0))
    if halo:
        zspecs = [tok(dc), pl.BlockSpec((8, dc), lambda i: (jnp.maximum(i * (tm // 8) - 1, 0), 0))]
        zops = [z, z]
    else:
        zspecs = [tok(dc), tok(dc), tok(dc)]
        zops = [z, z_shift[0], z_shift[1]]
    return pl.pallas_call(
        functools.partial(_mixout_kernel, halo=halo, tiles_per_batch=tiles_per_batch),
        grid=(n // tm,),
        in_specs=[tok(d), tok(o_nsa.shape[1])] + zspecs + [tok(dc), tok(2 * d)] + mspecs + [
            pl.BlockSpec((8, dc), lambda i: (0, 0)),
            _resident(w_nsa_p.shape), _resident(w_cv.shape), _resident(w_o.shape)],
        out_specs=tok(d),
        out_shape=jax.ShapeDtypeStruct((n, d), F32),
        compiler_params=_cparams(("arbitrary",)),
        name="mixout",
    )(h, o_nsa, *zops, cb, gm, *mops, w_conv8, w_nsa_p, w_cv, w_o)


CHUNK_W = CMP_STRIDE * KV_W
CMP_OUT_W = 2 * H_KV * CMP_HIDDEN


def _gelu_tanh(x):
    return x * (0.5 * (1.0 + jnp.tanh(math.sqrt(2.0 / math.pi) * (x + 0.044715 * (x * x * x)))))


def _chunk_partials(tile, wc_ref):
    accs = []
    for c in range(2):
        acc = None
        for lp in range(CMP_STRIDE // 2):
            x2 = jnp.concatenate([tile(2 * lp, c), tile(2 * lp + 1, c)], axis=1)
            part = _dot(x2, wc_ref[c, lp])
            acc = part if acc is None else acc + part
        accs.append(acc)
    hw = CMP_OUT_W // 2
    return jnp.concatenate([accs[0][:, 0:hw], accs[1][:, 0:hw], accs[0][:, hw:], accs[1][:, hw:]], axis=1)


def _compress_tail(p, wc_ref, pe_ref, w2_ref):
    n_ch = p.shape[0]
    pb = _chunk_partials(
        lambda l, c: pe_ref[:, l * KV_W + c * KV_HALF:l * KV_W + (c + 1) * KV_HALF].astype(BF16), wc_ref)
    pre = (pb[0:1, 0:CMP_OUT_W] + pb[1:2, CMP_OUT_W:]) + p[:, 0:CMP_OUT_W] \
        + pltpu.roll(p[:, CMP_OUT_W:], n_ch - 1, 0)
    out = _dot(_gelu_tanh(pre).astype(BF16), w2_ref[...])
    row = lax.broadcasted_iota(jnp.int32, out.shape, 0)
    return jnp.where(row < n_ch - 1, out, 0.0)


def _cmp_prompt_kernel(x_ref, wc_ref, pe_ref, w2_ref, o_ref):
    n_ch = x_ref.shape[0]
    p = _chunk_partials(
        lambda l, c: x_ref[:, l * KV_W + c * KV_HALF:l * KV_W + (c + 1) * KV_HALF].astype(BF16), wc_ref)
    out = _compress_tail(p, wc_ref, pe_ref, w2_ref).astype(BF16)
    o_ref[0:n_ch, :] = out
    o_ref[n_ch:2 * n_ch, :] = out


def _cmp_prompt(rows, wbig, pe8, w2bd):
    nb, n_ch, _ = rows.shape
    return pl.pallas_call(
        _cmp_prompt_kernel,
        grid=(nb,),
        in_specs=[pl.BlockSpec((None, n_ch, CHUNK_W), lambda b: (b, 0, 0)),
                  _resident(wbig.shape), _resident(pe8.shape), _resident(w2bd.shape)],
        out_specs=pl.BlockSpec((None, 2 * n_ch, KV_W), lambda b: (b, 0, 0)),
        out_shape=jax.ShapeDtypeStruct((nb, 2 * n_ch, KV_W), BF16),
        compiler_params=_cparams(("arbitrary",)),
        name="cmp_prompt",
    )(rows, wbig, pe8, w2bd)


def _page_copies(cache_ref, pt_ref, b, buf, sem, slot, n_pages, page):
    return [pltpu.make_async_copy(cache_ref.at[pt_ref[b, p]],
                                  buf.at[slot, :, pl.ds(p * page, page)],
                                  sem.at[slot]) for p in range(n_pages)]


def _gather_pages(cache_ref, pt_ref, buf, sem, n_pages, page):
    b = pl.program_id(0)
    nb = pl.num_programs(0)
    slot = b % 2

    @pl.when(b == 0)
    def _():
        for c in _page_copies(cache_ref, pt_ref, 0, buf, sem, 0, n_pages, page):
            c.start()

    @pl.when(b + 1 < nb)
    def _():
        for c in _page_copies(cache_ref, pt_ref, b + 1, buf, sem, 1 - slot, n_pages, page):
            c.start()

    for c in _page_copies(cache_ref, pt_ref, b, buf, sem, slot, n_pages, page):
        c.wait()
    return slot


def _cmp_sample_kernel(pt_ref, cache_ref, wc_ref, pe_ref, w2_ref, o_ref, buf, xs_ref, sem, *, n_pages, page):
    slot = _gather_pages(cache_ref, pt_ref, buf, sem, n_pages, page)
    for c in range(2):
        xs_ref[c] = buf[slot, c * KV_HALF:(c + 1) * KV_HALF, :].T
    n_ch = xs_ref.shape[1] // CMP_STRIDE
    p = _chunk_partials(lambda l, c: xs_ref[c, pl.ds(l, n_ch, stride=CMP_STRIDE), :].astype(BF16), wc_ref)
    o_ref[...] = _compress_tail(p, wc_ref, pe_ref, w2_ref).astype(BF16)


def _cmp_sample(page_table, cache_t, wbig, pe8, w2bd):
    nb, n_pages = page_table.shape
    page = cache_t.shape[2]
    past = n_pages * page
    n_ch = past // CMP_STRIDE
    return pl.pallas_call(
        functools.partial(_cmp_sample_kernel, n_pages=n_pages, page=page),
        grid_spec=pltpu.PrefetchScalarGridSpec(
            num_scalar_prefetch=1,
            grid=(nb,),
            in_specs=[pl.BlockSpec(memory_space=pl.ANY),
                      _resident_sp(wbig.shape), _resident_sp(pe8.shape), _resident_sp(w2bd.shape)],
            out_specs=pl.BlockSpec((None, n_ch, KV_W), lambda b, pt: (b, 0, 0)),
            scratch_shapes=[pltpu.VMEM((2, KV_W, past), F32), pltpu.VMEM((2, past, KV_HALF), F32),
                            pltpu.SemaphoreType.DMA((2,))]),
        out_shape=jax.ShapeDtypeStruct((nb, n_ch, KV_W), BF16),
        compiler_params=_cparams(("arbitrary",)),
        name="cmp_sample",
    )(page_table, cache_t, wbig, pe8, w2bd)


def _resident_sp(shape):
    nd = len(shape)
    return pl.BlockSpec(shape, lambda *_: (0,) * nd, pipeline_mode=pl.Buffered(1))


def _bias_values(d, table):
    acc = jnp.zeros(d.shape, F32) + table(0)
    for b in range(1, N_BUCKETS):
        acc = jnp.where(d >= BUCKET_THR[b], table(b), acc)
    return jnp.where(d >= 0, acc - table(N_BUCKETS - 1), NEG)


CMP_BACK = 16


def _bias_prompt_kernel(tbl_ref, bc_ref, t0_ref, t1_ref):
    h = pl.program_id(0)
    table = lambda b: tbl_ref[b, h]
    ncp, tq = bc_ref.shape
    kj = lax.broadcasted_iota(jnp.int32, (tq, tq), 0)
    qi = lax.broadcasted_iota(jnp.int32, (tq, tq), 1)
    t0_ref[...] = _bias_values(qi - kj, table)
    t1_ref[...] = _bias_values(qi - kj + tq, table)
    c = lax.broadcasted_iota(jnp.int32, (ncp, tq), 0)
    qi = lax.broadcasted_iota(jnp.int32, (ncp, tq), 1)
    d = qi - CMP_STRIDE * c + (CMP_STRIDE * CMP_BACK - (CMP_BLOCK - 1))
    bc_ref[...] = jnp.where(c < CMP_NEAR, _bias_values(d, table), 0.0)


def _bias_prompt(rel_bias, ncp):
    tq = Q_TILE
    return pl.pallas_call(
        _bias_prompt_kernel,
        grid=(N_HEADS,),
        in_specs=[pl.BlockSpec(memory_space=pltpu.SMEM)],
        out_specs=[pl.BlockSpec((None, ncp, tq), lambda h: (h, 0, 0)),
                   pl.BlockSpec((None, tq, tq), lambda h: (h, 0, 0)),
                   pl.BlockSpec((None, tq, tq), lambda h: (h, 0, 0))],
        out_shape=[jax.ShapeDtypeStruct((N_HEADS, ncp, tq), F32),
                   jax.ShapeDtypeStruct((N_HEADS, tq, tq), F32),
                   jax.ShapeDtypeStruct((N_HEADS, tq, tq), F32)],
        compiler_params=_cparams(("arbitrary",)),
        name="bias_prompt",
    )(rel_bias)


def _bias_sample_kernel(tbl_ref, bcs_ref, bsl_ref, bnew_ref, bw_ref, *, past, s_len, n_c):
    table = lambda b: tbl_ref[:, b:b + 1]

    def qpos(shape):
        return past + lax.rem(lax.broadcasted_iota(jnp.int32, shape, 0), s_len)

    def lane(shape):
        return lax.broadcasted_iota(jnp.int32, shape, 1)

    sh = bcs_ref.shape
    n = lane(sh)
    bcs_ref[...] = jnp.where(n < n_c, _bias_values(qpos(sh) - (CMP_STRIDE * n + CMP_BLOCK - 1), table), NEG)
    sh = bsl_ref.shape
    bsl_ref[...] = _bias_values(qpos(sh) - (past - sh[1] + lane(sh)), table)
    sh = bnew_ref.shape
    bnew_ref[...] = jnp.where(lane(sh) < s_len, _bias_values(qpos(sh) - (past + lane(sh)), table), NEG)
    sh = bw_ref.shape
    d = qpos(sh) - (past - sh[1] + lane(sh))
    bw_ref[...] = jnp.where(d < WINDOW, _bias_values(d, table), NEG)


def _bias_sample(tbl_rows, past, s_len, ncs, ck, wb):
    rows = tbl_rows.shape[0]
    widths = (ncs, ck, LANES, wb)
    return pl.pallas_call(
        functools.partial(_bias_sample_kernel, past=past, s_len=s_len, n_c=(past + s_len - CMP_BLOCK) // CMP_STRIDE + 1),
        out_shape=[jax.ShapeDtypeStruct((rows, w), F32) for w in widths],
        name="bias_sample",
    )(tbl_rows)


MASK_FLOOR = -1e8
M_INIT = -1e30
TAKEN = -3e38


def _sel_weights(v):
    return jnp.where((v >= 0) & (v <= 2), 2.0, jnp.where((v == -1) | (v == 3), 1.0, 0.0))


def _flash_cols(s, vt, m_ref, l_ref, a_ref, idx):
    m_prev = m_ref[idx]
    m_new = jnp.maximum(m_prev, jnp.max(s, axis=0, keepdims=True))
    alpha = jnp.exp(m_prev - m_new)
    p = jnp.exp(s - m_new)
    l_ref[idx] = alpha * l_ref[idx] + jnp.sum(p, axis=0, keepdims=True)
    m_ref[idx] = m_new
    a_ref[idx] = alpha * a_ref[idx] + _dot(vt, p.astype(BF16))


def _attn_prompt_kernel(q_ref, gate_ref, ks0_ref, ks1_ref, vs0_ref, vs1_ref, kwk_ref, vw0_ref, vw1_ref,
                        kc2_ref, bc_ref, t0_ref, t1_ref, o_ref,
                        qm_ref, selr_ref, oc_ref, ms_ref, as_ref, mw_ref, aw_ref, *, ncp, nsel):
    tq = Q_TILE
    i = pl.program_id(1)
    q0 = i * tq
    lane = lax.broadcasted_iota(jnp.int32, (tq, LANES), 1)
    heads = [(g, r) for g in range(H_KV) for r in range(GROUP)]

    for g, r in heads:
        gmask = (lane < HEAD_DIM) if g == 0 else (lane >= HEAD_DIM)
        qm_ref[g * GROUP + r] = jnp.where(gmask, q_ref[:, r * LANES:(r + 1) * LANES], jnp.zeros((tq, LANES), BF16))

    n0 = (tq // CMP_STRIDE) * i - CMP_BACK
    ws = pl.multiple_of(lax.rem(n0 + ncp, ncp), 8)
    cl = lax.broadcasted_iota(jnp.int32, (ncp, tq), 0)
    dm = jnp.where((cl >= CMP_NEAR) & (cl < ncp - n0), NEG, 0.0) + jnp.where(cl < -n0, NEG, 0.0)
    jrow = lax.broadcasted_iota(jnp.int32, (nsel, ncp), 0)
    nn = lax.broadcasted_iota(jnp.int32, (nsel, ncp), 1) + n0
    nn = jnp.where(nn < 0, nn + ncp, nn)
    nn = jnp.where(nn >= ncp, nn - ncp, nn)
    mit = _sel_weights(nn - (SEL_BLOCK // CMP_STRIDE) * jrow).astype(BF16)

    jr = lax.broadcasted_iota(jnp.int32, (nsel, tq), 0)
    qpos = q0 + lax.broadcasted_iota(jnp.int32, (nsel, tq), 1)
    cur = lax.shift_right_logical(qpos, int(math.log2(SEL_BLOCK)))
    invalid_pen = jnp.where(jr * SEL_BLOCK <= qpos, 0.0, NEG)
    bonus = jnp.where((jr == 0) | (jr == cur) | (jr == cur - 1), FORCE_BONUS, 0.0)
    pj = lax.broadcasted_iota(jnp.int32, (LANES, LANES), 0)
    pl_ = lax.broadcasted_iota(jnp.int32, (LANES, LANES), 1)
    kj = lax.broadcasted_iota(jnp.int32, (tq, tq), 0)
    qi = lax.broadcasted_iota(jnp.int32, (tq, tq), 1)
    wmask = jnp.where(kj > qi, 0.0, NEG)
    blocks_per_chunk = tq // SEL_BLOCK

    kc = kc2_ref[pl.ds(ws, ncp), 0:KV_HALF]
    vct = kc2_ref[pl.ds(ws, ncp), KV_HALF:KV_W].astype(F32).T.astype(BF16)
    for g in range(H_KV):
        psum = None
        for r in range(GROUP):
            h = g * GROUP + r
            s = _dot_nt(kc, qm_ref[h]) + bc_ref[h] + dm
            m = jnp.maximum(jnp.max(s, axis=0, keepdims=True), MASK_FLOOR)
            e = jnp.exp(s - m)
            p = e * (1.0 / jnp.maximum(jnp.sum(e, axis=0, keepdims=True), 1e-30))
            oc_ref[h] = _dot(vct, p.astype(BF16))
            psum = p if psum is None else psum + p

        h1, h2, h3 = _split3(psum)
        pslc = (_dot(mit, h1) + _dot(mit, h2)) + _dot(mit, h3)
        score = jnp.where(invalid_pen < 0.0, NEG, pslc + bonus)
        def take_max(_, carry):
            work, sel_acc = carry
            mx = jnp.max(work, axis=0, keepdims=True)
            first = jnp.min(jnp.where(work == mx, jr, nsel), axis=0, keepdims=True)
            hit = jr == first
            return jnp.where(hit, TAKEN, work), jnp.where(hit, 1.0, sel_acc)

        _, sel_t = lax.fori_loop(0, min(TOP_N, nsel), take_max, (score, jnp.zeros((nsel, tq), F32)))
        if nsel < LANES:
            sel_t = jnp.concatenate([sel_t, jnp.zeros((LANES - nsel, tq), F32)], axis=0)
        selr_ref[g] = sel_t.T.astype(BF16)

    def masked_queries(t, g):
        off = pl_ - _aug_lane(g)
        place = jnp.where((pj - t * blocks_per_chunk == off) & (off >= 0) & (off < blocks_per_chunk), 1.0, 0.0)
        picked = _dot(selr_ref[g], place.astype(BF16))
        auglane = (lane >= _aug_lane(g)) & (lane < _aug_lane(g) + blocks_per_chunk)
        pen = jnp.where(auglane, (picked - 1.0) * (-NEG), 0.0).astype(BF16)
        return [qm_ref[g * GROUP + r] + pen for r in range(GROUP)]

    def run(chunks):
        items = []
        for k_of, vt_of, queries, extra, m_ref, a_ref in chunks:
            cache = {}
            items += [(h, k_of, vt_of, queries, extra, m_ref, a_ref, cache) for h in range(N_HEADS)]

        def kv(item):
            g, cache = item[0] // GROUP, item[7]
            if g not in cache:
                cache[g] = (item[1](g), item[2](g))
            return cache[g]

        def scores(item):
            h, queries, extra = item[0], item[3], item[4]
            s = _dot_nt(kv(item)[0], queries(h))
            return s if extra is None else s + extra(h)

        def softmax(item, s):
            h, m_ref = item[0], item[5]
            m_prev = m_ref[h]
            m_new = jnp.maximum(m_prev, jnp.max(s, axis=0, keepdims=True))
            m_ref[h] = m_new
            return jnp.exp(s - m_new).astype(BF16), jnp.exp(m_prev - m_new)

        def accumulate(item, p, alpha):
            h, a_ref = item[0], item[6]
            a_ref[h] = alpha * a_ref[h] + _dot(kv(item)[1], p)

        n = len(items)
        s = {j: scores(items[j]) for j in range(2)}
        pa = {}
        for j in range(n):
            pa[j] = softmax(items[j], s.pop(j))
            if j + 2 < n:
                s[j + 2] = scores(items[j + 2])
            if j >= 1:
                accumulate(items[j - 1], *pa.pop(j - 1))
        accumulate(items[n - 1], *pa.pop(n - 1))

    for m_ref, a_ref in ((ms_ref, as_ref), (mw_ref, aw_ref)):
        m_ref[...] = jnp.full(m_ref.shape, M_INIT, F32)
        a_ref[...] = jnp.zeros(a_ref.shape, F32)

    def sel_chunk(t, extra):
        sl = pl.ds(pl.multiple_of(t * tq, tq), tq)
        qs = [masked_queries(t, g) for g in range(H_KV)]
        return (lambda g: (ks0_ref, ks1_ref)[g][sl, :], lambda g: (vs0_ref, vs1_ref)[g][t],
                lambda h: qs[h // GROUP][h % GROUP], extra, ms_ref, as_ref)

    def win_chunk(t, extra):
        sl = pl.ds(pl.multiple_of(t * tq, tq), tq)
        return (lambda g: kwk_ref[sl, :], lambda g: (vw0_ref, vw1_ref)[g][t],
                lambda h: qm_ref[h], extra, mw_ref, aw_ref)

    n_far = jnp.maximum(i - 1, 0)

    def far_pair(u, carry):
        run([sel_chunk(2 * u, None), sel_chunk(2 * u + 1, None)])
        return carry

    lax.fori_loop(0, n_far // 2, far_pair, 0)

    @pl.when(n_far % 2 == 1)
    def _():
        run([sel_chunk(n_far - 1, None)])

    gone1 = jnp.where(i >= 1, 0.0, NEG)
    gone2 = jnp.where(i >= 2, 0.0, NEG)
    wmask2 = wmask + gone2
    run([sel_chunk(jnp.maximum(i - 1, 0), lambda h: t1_ref[h] + gone1),
         sel_chunk(i, lambda h: t0_ref[h]),
         win_chunk(jnp.maximum(i - 2, 0), lambda h: wmask2),
         win_chunk(jnp.maximum(i - 1, 0), lambda h: t1_ref[h] + gone1),
         win_chunk(i, lambda h: t0_ref[h])])

    gates_t = gate_ref[...].T
    for r in range(GROUP):
        per_g = []
        for g in range(H_KV):
            h = g * GROUP + r
            a_s, a_w = as_ref[h], aw_ref[h]
            o_sel = a_s[0:HEAD_DIM] * (1.0 / jnp.maximum(a_s[HEAD_DIM:HEAD_DIM + 1], 1e-30))
            o_win = a_w[0:HEAD_DIM] * (1.0 / jnp.maximum(a_w[HEAD_DIM:HEAD_DIM + 1], 1e-30))
            per_g.append(gates_t[h:h + 1, :] * oc_ref[h, g * HEAD_DIM:(g + 1) * HEAD_DIM, :]
                         + gates_t[N_HEADS + h:N_HEADS + h + 1, :] * o_sel
                         + gates_t[2 * N_HEADS + h:2 * N_HEADS + h + 1, :] * o_win)
        o_ref[:, r * LANES:(r + 1) * LANES] = jnp.concatenate(per_g, axis=0).T.astype(BF16)


def _attn_prompt(q, gates, ks_g, vs_g, kw_k, vw_g, kc2, bias_tiles, nb, t):
    tq = Q_TILE
    assert t % tq == 0 and WINDOW == 2 * tq
    nt = t // tq
    ncp = t // CMP_STRIDE
    nsel = t // SEL_BLOCK
    assert nsel <= LANES and ncp % LANES == 0
    bc, t0, t1 = bias_tiles
    tok = lambda w: pl.BlockSpec((tq, w), lambda b, i: (b * nt + i, 0))
    k_rows = pl.BlockSpec((None, t, KV_HALF), lambda b, i: (b, 0, 0))
    v_cols = pl.BlockSpec((None, nt, V_ROWS, tq), lambda b, i: (b, 0, 0, 0))
    state = [pltpu.VMEM((N_HEADS, 1, tq), F32), pltpu.VMEM((N_HEADS, V_ROWS, tq), F32)]
    kr = lambda a: a.reshape(nb, t, KV_HALF)
    vc = lambda a: a.reshape(nb, nt, V_ROWS, tq)
    return pl.pallas_call(
        functools.partial(_attn_prompt_kernel, ncp=ncp, nsel=nsel),
        grid=(nb, nt),
        in_specs=[tok(q.shape[1]), tok(LANES), k_rows, k_rows, v_cols, v_cols, k_rows, v_cols, v_cols,
                  pl.BlockSpec((None, 2 * ncp, KV_W), lambda b, i: (b, 0, 0)),
                  _resident(bc.shape), _resident(t0.shape), _resident(t1.shape)],
        out_specs=tok(q.shape[1]),
        out_shape=jax.ShapeDtypeStruct(q.shape, BF16),
        scratch_shapes=[pltpu.VMEM((N_HEADS, tq, LANES), BF16), pltpu.VMEM((H_KV, tq, LANES), BF16),
                        pltpu.VMEM((N_HEADS, LANES, tq), F32)] + state + state,
        compiler_params=_cparams(("arbitrary", "arbitrary")),
        name="attn_prompt",
    )(q, gates, kr(ks_g[0]), kr(ks_g[1]), vc(vs_g[0]), vc(vs_g[1]), kr(kw_k), vc(vw_g[0]), vc(vw_g[1]),
      kc2, bc, t0, t1)


SAMPLE_CK = 1024


def _flash_val(state, s, v, transposed=False):
    m_prev, l_prev, a_prev = state
    m_new = jnp.maximum(m_prev, jnp.max(s, axis=-1, keepdims=True))
    alpha = jnp.exp(m_prev - m_new)
    p = jnp.exp(s - m_new)
    pv = _dot_nt(p.astype(BF16), v) if transposed else _dot(p.astype(BF16), v)
    return (m_new, alpha * l_prev + jnp.sum(p, axis=-1, keepdims=True), alpha * a_prev + pv)


def _attn_sample_kernel(pt_ref, q_ref, g_ref, knew_ref, wnew_ref, win_ref, kc_ref,
                        bcs_ref, bsl_ref, bnew_ref, bw_ref, msel_ref, cache_ref, o_ref, buf, sem,
                        *, n_pages, page, past, s_len, nsel):
    slot = _gather_pages(cache_ref, pt_ref, buf, sem, n_pages, page)
    q = q_ref[...]
    rows = q.shape[0]
    gs = rows // GROUP
    nselp = msel_ref.shape[1]

    s = _dot_nt(q, kc_ref[:, 0:KV_HALF]) + bcs_ref[...]
    m = jnp.maximum(jnp.max(s, axis=-1, keepdims=True), MASK_FLOOR)
    e = jnp.exp(s - m)
    p = e / jnp.maximum(jnp.sum(e, axis=-1, keepdims=True), 1e-30)
    o_cmp = _dot(p.astype(BF16), kc_ref[:, KV_HALF:KV_W])
    psum = p[0:gs]
    for r in range(1, GROUP):
        psum = psum + p[r * gs:(r + 1) * gs]

    h1, h2, h3 = _split3(psum)
    msel = msel_ref[...]
    pslc = (_dot(h1, msel) + _dot(h2, msel)) + _dot(h3, msel)
    j = lax.broadcasted_iota(jnp.int32, (gs, nselp), 1)
    qpos = past + lax.rem(lax.broadcasted_iota(jnp.int32, (gs, nselp), 0), s_len)
    cur = lax.shift_right_logical(qpos, int(math.log2(SEL_BLOCK)))
    valid = j * SEL_BLOCK <= qpos
    forced = (j == 0) | (j == cur) | (j == cur - 1)
    score = jnp.where(valid, pslc + jnp.where(forced, FORCE_BONUS, 0.0), NEG)
    score = jnp.where(j < nsel, score, -3e38)
    score_t = jnp.concatenate([score, jnp.zeros((LANES - gs, nselp), F32)], axis=0).T
    jp = lax.broadcasted_iota(jnp.int32, (nselp, nselp), 0)
    jj = lax.broadcasted_iota(jnp.int32, (nselp, nselp), 1)
    sel_rows = []
    for r in range(gs):
        col = score_t[:, r:r + 1]
        row = score[r:r + 1, :]
        beats = jnp.where(jj > jp, jnp.where(col >= row, 1.0, 0.0), jnp.where(col > row, 1.0, 0.0))
        rank = jnp.sum(beats, axis=0, keepdims=True)
        sel_rows.append(jnp.where(rank < min(TOP_N, nsel), 1.0, 0.0))
    sel8 = jnp.concatenate(sel_rows, axis=0)
    sel = jnp.concatenate([sel8] * GROUP, axis=0).astype(BF16)

    ck = bsl_ref.shape[1]
    n_chunks = past // ck
    erow = lax.broadcasted_iota(jnp.int32, (nselp, ck), 0)
    eblk = lax.shift_right_logical(lax.broadcasted_iota(jnp.int32, (nselp, ck), 1), int(math.log2(SEL_BLOCK)))
    init = (jnp.full((rows, 1), M_INIT, F32), jnp.zeros((rows, 1), F32), jnp.zeros((rows, LANES), F32))
    st = init
    for t in range(n_chunks):
        kt = buf[slot, 0:KV_HALF, t * ck:(t + 1) * ck].astype(BF16)
        vt = buf[slot, KV_HALF:KV_W, t * ck:(t + 1) * ck].astype(BF16)
        e_t = jnp.where(erow == t * (ck // SEL_BLOCK) + eblk, 1.0, 0.0).astype(BF16)
        sc = _dot(q, kt) + (_dot(sel, e_t) - 1.0) * (-NEG)
        if t == n_chunks - 1:
            sc = sc + bsl_ref[...]
        st = _flash_val(st, sc, vt, transposed=True)
    knew = knew_ref[...]
    st = _flash_val(st, _dot_nt(q, knew[:, 0:KV_HALF]) + bnew_ref[...], knew[:, KV_HALF:KV_W])
    o_sel = st[2] / jnp.maximum(st[1], 1e-30)

    st = _flash_val(init, _dot(q, win_ref[0:KV_HALF, :].astype(BF16)) + bw_ref[...],
                    win_ref[KV_HALF:KV_W, :].astype(BF16), transposed=True)
    wnew = wnew_ref[...]
    st = _flash_val(st, _dot_nt(q, wnew[:, 0:KV_HALF]) + bnew_ref[...], wnew[:, KV_HALF:KV_W])
    o_win = st[2] / jnp.maximum(st[1], 1e-30)

    g = g_ref[...]
    o_ref[...] = g[:, 0:1] * o_cmp + g[:, 1:2] * o_sel + g[:, 2:3] * o_win


def _attn_sample(page_table, q32, g32, knew, wnew, win_t, kvc, tiles, msel, cache_t, past, s_len):
    nb, n_pages = page_table.shape
    page = cache_t.shape[2]
    rows = q32.shape[1]
    nsel = -(-(past + s_len) // SEL_BLOCK)
    bcs, bsl, bnew, bw = tiles
    per_b = lambda r, w: pl.BlockSpec((None, r, w), lambda b, pt: (b, 0, 0))
    return pl.pallas_call(
        functools.partial(_attn_sample_kernel, n_pages=n_pages, page=page, past=past, s_len=s_len, nsel=nsel),
        grid_spec=pltpu.PrefetchScalarGridSpec(
            num_scalar_prefetch=1,
            grid=(nb,),
            in_specs=[per_b(rows, LANES), per_b(rows, LANES), per_b(LANES, KV_W), per_b(LANES, KV_W),
                      per_b(KV_W, win_t.shape[2]), per_b(kvc.shape[1], KV_W),
                      _resident_sp(bcs.shape), _resident_sp(bsl.shape), _resident_sp(bnew.shape),
                      _resident_sp(bw.shape), _resident_sp(msel.shape),
                      pl.BlockSpec(memory_space=pl.ANY)],
            out_specs=per_b(rows, LANES),
            scratch_shapes=[pltpu.VMEM((2, KV_W, past), F32), pltpu.SemaphoreType.DMA((2,))]),
        out_shape=jax.ShapeDtypeStruct((nb, rows, LANES), F32),
        compiler_params=_cparams(("arbitrary",)),
        name="attn_sample",
    )(page_table, q32, g32, knew, wnew, win_t, kvc, bcs, bsl, bnew, bw, msel, cache_t)


def _prep_weights(w_in, w_cmp1, w_cmp2, pe_cmp, w_conv, w_nsa_out):
    d = w_in.shape[0]
    sizes = (N_HEADS * HEAD_DIM, KV_W, KV_W, KV_W, 3 * N_HEADS, _C_W, _C_W, _C_W, 2 * d)
    offs = [0]
    for sz in sizes:
        offs.append(offs[-1] + sz)
    part = lambda k: w_in[:, offs[k]:offs[k + 1]]
    qp = part(0).reshape(d, H_KV, GROUP, HEAD_DIM).transpose(0, 2, 1, 3).reshape(d, N_HEADS * HEAD_DIM)
    gn = jnp.pad(part(4), ((0, 0), (0, _G_W - 3 * N_HEADS)))
    w_in_p = jnp.concatenate([qp, part(1), part(2), part(3), gn, part(5), part(6), part(7), part(8)],
                             axis=1).astype(BF16)
    w_nsa_p = w_nsa_out.reshape(H_KV, GROUP, HEAD_DIM, -1).transpose(1, 0, 2, 3).reshape(
        N_HEADS * HEAD_DIM, -1).astype(BF16)
    r = CMP_BLOCK // CMP_STRIDE
    w1r = w_cmp1.reshape(2, r, CMP_STRIDE // 2, 2, HEAD_DIM, CMP_HIDDEN)
    w1t = jnp.transpose(w1r, (0, 2, 3, 4, 1, 5))
    wb = jnp.zeros((2, CMP_STRIDE // 2, 2, H_KV, HEAD_DIM, r, H_KV, CMP_HIDDEN), F32)
    w2b = jnp.zeros((2, H_KV, CMP_HIDDEN, 2, H_KV, HEAD_DIM), F32)
    for g in range(H_KV):
        wb = wb.at[:, :, :, g, :, :, g, :].set(w1t)
        for c in range(2):
            w2b = w2b.at[c, g, :, c, g, :].set(w_cmp2[c])
    wbig = wb.reshape(2, CMP_STRIDE // 2, 2 * KV_HALF, CMP_OUT_W).astype(BF16)
    w2bd = w2b.reshape(CMP_OUT_W, KV_W).astype(BF16)
    pe_r = pe_cmp.reshape(2, r, CMP_STRIDE, HEAD_DIM).transpose(1, 2, 0, 3)
    pe_rows = jnp.broadcast_to(pe_r[:, :, :, None, :], (r, CMP_STRIDE, 2, H_KV, HEAD_DIM)).reshape(r, CHUNK_W)
    pe8 = jnp.pad(pe_rows, ((0, 8 - r), (0, 0)))
    w_conv8 = jnp.pad(w_conv, ((0, 8 - CONV_WIDTH), (0, 0)))
    return w_in_p, w_nsa_p, wbig, w2bd, pe8, w_conv8


def _sel_matrix(n_c, nselp):
    n = jnp.arange(n_c + 1)[:, None]
    j = jnp.arange(nselp)[None, :]
    return _sel_weights(n - (SEL_BLOCK // CMP_STRIDE) * j).astype(BF16)


def kernel(x_prompt, x_sample, cache_kv_cmp, cache_kv_sel, state_kv_win, state_conv, page_table,
           c_prompt, c_sample, w_ada, b_ada, g_norm, w_ffn1_gu, w_ffn1_down, w_ffn2_gu, w_ffn2_down,
           w_in, w_cmp1, w_cmp2, pe_cmp, w_conv, w_nsa_out, w_conv_out, w_out, rel_bias, g_final):
    assert w_ada.shape[0] == 1, "single-layer trunk"
    nbp, t, d = x_prompt.shape
    nbs, s_len, _ = x_sample.shape
    n_pages = page_table.shape[1]
    page = cache_kv_cmp.shape[2]
    past = n_pages * page
    n_phys = cache_kv_cmp.shape[1]
    wb = state_kv_win.shape[2]
    assert wb == WINDOW and past % SAMPLE_CK == 0

    w_in_p, w_nsa_p, wbig, w2bd, pe8, w_conv8 = _prep_weights(
        w_in[0], w_cmp1[0], w_cmp2[0], pe_cmp[0], w_conv[0], w_nsa_out[0])
    wgu1, wd1 = w_ffn1_gu[0].astype(BF16), w_ffn1_down[0].astype(BF16)
    wgu2, wd2 = w_ffn2_gu[0].astype(BF16), w_ffn2_down[0].astype(BF16)
    w_cv, w_o = w_conv_out[0].astype(BF16), w_out[0].astype(BF16)
    gn = [g_norm[0][k:k + 1] for k in range(N_SUB)]

    n_c_rows = nbp + nbs
    c_all = jnp.pad(jnp.concatenate([c_prompt, c_sample], axis=0), ((0, (-n_c_rows) % 8), (0, 0)))
    mod_all = _ada(c_all, w_ada[0], b_ada[0])
    mod_p = mod_all[:nbp].reshape(nbp * 3 * N_SUB, 1, d)
    mod_s = jnp.transpose(jnp.repeat(mod_all[nbp:n_c_rows].reshape(nbs, 3 * N_SUB, d), s_len, axis=0), (1, 0, 2))

    tm = 512 if t % 512 == 0 else t
    tpb = t // tm
    xp = x_prompt.reshape(nbp * t, d)
    h1 = _ffn(xp, mod_p, 0, gn[0], wgu1, wd1, tm, tpb)
    (q, kvc, kvc_t, kvs_t, kvw_t, ks0, ks1, vs0, vs1, kw_k, vw0, vw1, gates, z, cb, gm) = _inproj(
        h1, mod_p, gn[1], w_in_p, tm, tpb, True)
    kc2 = _cmp_prompt(kvc.reshape(nbp, t // CMP_STRIDE, CHUNK_W), wbig, pe8, w2bd)
    tiles_p = _bias_prompt(rel_bias, t // CMP_STRIDE)
    o_nsa = _attn_prompt(q, gates, (ks0, ks1), (vs0, vs1), kw_k, (vw0, vw1), kc2, tiles_p, nbp, t)
    h2 = _mixout(h1, o_nsa, z, None, cb, gm, mod_p, w_conv8, w_nsa_p, w_cv, w_o, tm, tpb)
    y_prompt = _ffn(h2, mod_p, 2, gn[2], wgu2, wd2, tm, tpb, g_final).reshape(nbp, t, d)

    kv_shape = (2, H_KV, HEAD_DIM)
    kv_out = lambda a: jnp.transpose(a.reshape((1, nbp) + kv_shape + (a.shape[-1],)), (0, 1, 5, 2, 3, 4))
    kv_cmp_p = kv_out(kvc_t)
    kv_sel_p = kv_out(kvs_t)
    keep = min(WINDOW, t)
    kv_win_p = kv_out(kvw_t[:, :, t - keep:])
    conv_p = z.reshape(1, nbp, t, -1)[:, :, t - (CONV_WIDTH - 1):]

    ns = nbs * s_len
    xs = x_sample.reshape(ns, d)
    h1s = _ffn(xs, mod_s, 0, gn[0], wgu1, wd1, ns, 1)
    qs, kvc_s, kvs_s, kvw_s, kvs_sb, kvw_sb, gates_s, z_s, cb_s, gm_s = _inproj(
        h1s, mod_s, gn[1], w_in_p, ns, 1, False)

    pos_minor = lambda a: jnp.transpose(a, (0, 2, 3, 4, 1)).reshape(a.shape[0], KV_W, a.shape[1])
    kvc_past = _cmp_sample(page_table, pos_minor(cache_kv_cmp[0]), wbig, pe8, w2bd)
    n_c = (past + s_len - CMP_BLOCK) // CMP_STRIDE + 1
    nsel = -(-(past + s_len) // SEL_BLOCK)
    nselp = -(-nsel // LANES) * LANES
    rows = N_HEADS * s_len
    head_of_row = [g * GROUP + r for r in range(GROUP) for g in range(H_KV) for _ in range(s_len)]
    tbl_rows = jnp.pad(rel_bias.T[jnp.array(head_of_row)], ((0, 0), (0, LANES - N_BUCKETS)))
    tiles_s = _bias_sample(tbl_rows, past, s_len, past // CMP_STRIDE, SAMPLE_CK, wb)
    msel = _sel_matrix(past // CMP_STRIDE - 1, nselp)

    q5 = qs.reshape(nbs, s_len, GROUP, H_KV, HEAD_DIM).transpose(0, 2, 3, 1, 4)
    q32 = jnp.zeros((nbs, GROUP, H_KV, s_len, H_KV, HEAD_DIM), BF16)
    for g in range(H_KV):
        q32 = q32.at[:, :, g, :, g, :].set(q5[:, :, g])
    q32 = q32.reshape(nbs, rows, LANES)
    g5 = gates_s[:, :3 * N_HEADS].reshape(nbs, s_len, 3, H_KV, GROUP).transpose(0, 4, 3, 1, 2)
    g32 = jnp.pad(g5.reshape(nbs, rows, 3), ((0, 0), (0, 0), (0, LANES - 3)))
    pad_new = lambda a: jnp.pad(a.reshape(nbs, s_len, KV_W), ((0, 0), (0, LANES - s_len), (0, 0)))
    o32 = _attn_sample(page_table, q32, g32, pad_new(kvs_sb), pad_new(kvw_sb),
                       pos_minor(state_kv_win[0]), kvc_past, tiles_s, msel,
                       pos_minor(cache_kv_sel[0]), past, s_len)
    o6 = o32.reshape(nbs, GROUP, H_KV, s_len, H_KV, HEAD_DIM)
    o_s = jnp.stack([o6[:, :, g, :, g, :] for g in range(H_KV)], axis=3)
    o_nsa_s = o_s.transpose(0, 2, 1, 3, 4).reshape(ns, N_HEADS * HEAD_DIM).astype(BF16)

    full = jnp.concatenate([state_conv[0], z_s.reshape(nbs, s_len, -1)], axis=1)
    z_shift = (full[:, 1:1 + s_len].reshape(ns, -1), full[:, 0:s_len].reshape(ns, -1))
    h2s = _mixout(h1s, o_nsa_s, z_s, z_shift, cb_s, gm_s, mod_s, w_conv8, w_nsa_p, w_cv, w_o, ns, 1)
    y_sample = _ffn(h2s, mod_s, 2, gn[2], wgu2, wd2, ns, 1, g_final).reshape(nbs, s_len, d)

    kv_cmp_s = kvc_s.reshape((1, nbs, s_len) + kv_shape)
    kv_sel_s = kvs_s.reshape((1, nbs, s_len) + kv_shape)
    win_full = jnp.concatenate([state_kv_win[0], kvw_s.reshape((nbs, s_len) + kv_shape)], axis=1)
    keep_s = min(WINDOW, wb + s_len)
    kv_win_s = win_full[None, :, wb + s_len - keep_s:]
    conv_s = full[None, :, s_len:]
    return (y_prompt, y_sample, kv_cmp_p, kv_sel_p, kv_win_p, conv_p, kv_cmp_s, kv_sel_s, kv_win_s, conv_s)
```

```python
import functools
import math

import jax
import jax.numpy as jnp
from jax import lax
from jax.experimental import pallas as pl
from jax.experimental.pallas import tpu as pltpu

F32 = jnp.float32
BF16 = jnp.bfloat16

HEAD_DIM = 64
N_HEADS = 8
H_KV = 2
GROUP = N_HEADS // H_KV
CMP_BLOCK = 32
CMP_STRIDE = 16
CMP_HIDDEN = 2 * HEAD_DIM
SEL_BLOCK = 64
TOP_N = 16
WINDOW = 512
CONV_WIDTH = 3
N_BUCKETS = 32
MAX_DISTANCE = 128
N_SUB = 3
EPS = 1e-6
NEG = -1e9
LOG2E = math.log2(math.e)
FORCE_BONUS = 1e3
KV_W = 2 * H_KV * HEAD_DIM
KV_HALF = H_KV * HEAD_DIM
LANES = 128
Q_TILE = 256
CMP_NEAR = 32
VMEM_LIMIT = 56 * 1024 * 1024


def _bucket_thresholds():
    max_exact = N_BUCKETS // 2

    def bucket(d):
        if d < max_exact:
            return d
        large = max_exact + int(math.log(d / max_exact) / math.log(MAX_DISTANCE / max_exact)
                                * (N_BUCKETS - max_exact))
        return min(large, N_BUCKETS - 1)

    thr, d = [], 0
    for b in range(N_BUCKETS):
        while bucket(d) < b:
            d += 1
        thr.append(d)
    return tuple(thr)


BUCKET_THR = _bucket_thresholds()
FAR_DIST = BUCKET_THR[-1]


def _cparams(sem):
    return pltpu.CompilerParams(dimension_semantics=sem, vmem_limit_bytes=VMEM_LIMIT)


def _resident(shape):
    nd = len(shape)
    return pl.BlockSpec(shape, lambda *_: (0,) * nd, pipeline_mode=pl.Buffered(1))


def _dot(a, b):
    return jnp.dot(a, b, preferred_element_type=F32)


def _dot_nt(a, b):
    return lax.dot_general(a, b, (((1,), (1,)), ((), ())), preferred_element_type=F32)


def _split3(x):
    h1 = x.astype(BF16)
    r1 = x - h1.astype(F32)
    h2 = r1.astype(BF16)
    h3 = (r1 - h2.astype(F32)).astype(BF16)
    return h1, h2, h3


def _modulated_norm(x, g, shift, scale):
    y = x * lax.rsqrt(jnp.mean(x * x, axis=-1, keepdims=True) + EPS)
    return (y * g) * (1.0 + scale) + shift


def _ada_kernel(c_ref, w_ref, b_ref, o_ref):
    c = c_ref[...]
    a = (c * jax.nn.sigmoid(c)).astype(BF16)
    o_ref[...] = _dot(a, w_ref[...].astype(BF16)) + b_ref[...]


def _ada(c_all, w_ada, b_ada):
    rows, d = c_all.shape
    n = w_ada.shape[1]
    tn = n // 8
    return pl.pallas_call(
        _ada_kernel,
        grid=(n // tn,),
        in_specs=[pl.BlockSpec((rows, d), lambda j: (0, 0)),
                  pl.BlockSpec((d, tn), lambda j: (0, j)),
                  pl.BlockSpec((1, tn), lambda j: (0, j))],
        out_specs=pl.BlockSpec((rows, tn), lambda j: (0, j)),
        out_shape=jax.ShapeDtypeStruct((rows, n), F32),
        compiler_params=_cparams(("arbitrary",)),
        name="ada",
    )(c_all, w_ada, b_ada.reshape(1, n))


def _ffn_kernel(*refs, d_ff, fc, final_norm):
    if final_norm:
        x_ref, sh_ref, sc_ref, gt_ref, gn_ref, wgu_ref, wd_ref, gf_ref, o_ref = refs
    else:
        x_ref, sh_ref, sc_ref, gt_ref, gn_ref, wgu_ref, wd_ref, o_ref = refs
    x = x_ref[...]
    ub = _modulated_norm(x, gn_ref[...], sh_ref[...], sc_ref[...]).astype(BF16)
    acc = jnp.zeros(x.shape, F32)
    for c in range(d_ff // fc):
        g = _dot(ub, wgu_ref[:, c * fc:(c + 1) * fc])
        v = _dot(ub, wgu_ref[:, d_ff + c * fc:d_ff + (c + 1) * fc])
        a = ((g * jax.nn.sigmoid(g)) * v).astype(BF16)
        acc = acc + _dot(a, wd_ref[c * fc:(c + 1) * fc, :])
    h = x + (0.5 * gt_ref[...]) * acc
    if final_norm:
        h = (h * lax.rsqrt(jnp.mean(h * h, axis=-1, keepdims=True) + EPS)) * gf_ref[...]
    o_ref[...] = h


def _mod_specs(mod, ks, tm, tiles_per_batch):
    if mod.ndim == 3 and mod.shape[1] == 1:
        d = mod.shape[-1]
        specs = [pl.BlockSpec((None, 1, d), lambda i, k=k: ((i // tiles_per_batch) * (3 * N_SUB) + k, 0, 0))
                 for k in ks]
        return specs, [mod] * len(ks)
    d = mod.shape[-1]
    specs = [pl.BlockSpec((None, tm, d), lambda i, k=k: (k, i, 0)) for k in ks]
    return specs, [mod] * len(ks)


def _ffn(x, mod, sub, gn_row, w_gu, w_down, tm, tiles_per_batch, g_final=None):
    n, d = x.shape
    d_ff = w_down.shape[0]
    fc = d_ff // 2 if (d_ff // 2) % LANES == 0 else d_ff
    final_norm = g_final is not None
    mspecs, mops = _mod_specs(mod, (3 * sub, 3 * sub + 1, 3 * sub + 2), tm, tiles_per_batch)
    in_specs = [pl.BlockSpec((tm, d), lambda i: (i, 0))] + mspecs + [
        pl.BlockSpec((1, d), lambda i: (0, 0)), _resident(w_gu.shape), _resident(w_down.shape)]
    ops = [x] + mops + [gn_row, w_gu, w_down]
    if final_norm:
        in_specs.append(pl.BlockSpec((1, d), lambda i: (0, 0)))
        ops.append(g_final.reshape(1, d))
    return pl.pallas_call(
        functools.partial(_ffn_kernel, d_ff=d_ff, fc=fc, final_norm=final_norm),
        grid=(n // tm,),
        in_specs=in_specs,
        out_specs=pl.BlockSpec((tm, d), lambda i: (i, 0)),
        out_shape=jax.ShapeDtypeStruct((n, d), F32),
        compiler_params=_cparams(("arbitrary",)),
        name="ffn_final" if final_norm else "ffn",
    )(*ops)


_Q_W = N_HEADS * HEAD_DIM
_G_W = LANES
_C_W = 512
_SEG = {}
_off = 0
for _name, _w in (("q", _Q_W), ("kc", KV_W), ("ks", KV_W), ("kw", KV_W), ("gn", _G_W),
                  ("ch", _C_W), ("cb", _C_W), ("cc", _C_W)):
    _SEG[_name] = (_off, _off + _w)
    _off += _w
_MG_OFF = _off


V_ROWS = HEAD_DIM + 16


def _aug_lane(g):
    return HEAD_DIM if g == 0 else 0


def _inproj_kernel(*refs, prompt):
    x_ref, sh_ref, sc_ref, gn_ref, w_ref = refs[:5]
    ub = _modulated_norm(x_ref[...], gn_ref[...], sh_ref[...], sc_ref[...]).astype(BF16)

    def seg(name):
        lo, hi = _SEG[name]
        return _dot(ub, w_ref[:, lo:hi])

    if prompt:
        (q_ref, kc_ref, kct_ref, kst_ref, kwt_ref, ks0_ref, ks1_ref, vs0_ref, vs1_ref, kwk_ref, vw0_ref, vw1_ref,
         g_ref, z_ref, cb_ref, gm_ref) = refs[5:]
        kc = seg("kc")
        kc_ref[...] = kc
        kct_ref[...] = kc.T
        tm = kc.shape[0]
        lane = lax.broadcasted_iota(jnp.int32, (tm, KV_HALF), 1)
        blk = lax.shift_right_logical(lax.broadcasted_iota(jnp.int32, (tm, KV_HALF), 0),
                                      int(math.log2(SEL_BLOCK))) & (Q_TILE // SEL_BLOCK - 1)
        ones_rows = jnp.where(lax.broadcasted_iota(jnp.int32, (V_ROWS - HEAD_DIM, tm), 0) == 0, 1.0, 0.0)
        for name, t_ref, k_refs, vt_refs in (("ks", kst_ref, (ks0_ref, ks1_ref), (vs0_ref, vs1_ref)),
                                             ("kw", kwt_ref, (kwk_ref,), (vw0_ref, vw1_ref))):
            kv = seg(name)
            kv_t = kv.T
            t_ref[...] = kv_t
            kk = kv[:, 0:KV_HALF]
            if len(k_refs) == 1:
                k_refs[0][...] = kk.astype(BF16)
            else:
                for g, k_ref in enumerate(k_refs):
                    own = (lane < HEAD_DIM) if g == 0 else (lane >= HEAD_DIM)
                    onehot = jnp.where(lane - _aug_lane(g) == blk, 1.0, 0.0)
                    k_ref[...] = jnp.where(own, kk, onehot).astype(BF16)
            for g, vt_ref in enumerate(vt_refs):
                vg = jnp.concatenate(
                    [kv_t[KV_HALF + g * HEAD_DIM:KV_HALF + (g + 1) * HEAD_DIM, :], ones_rows], axis=0)
                for c in range(vt_ref.shape[0]):
                    vt_ref[c] = vg[:, c * Q_TILE:(c + 1) * Q_TILE].astype(BF16)
    else:
        q_ref, kc_ref, ks_ref, kw_ref, ksb_ref, kwb_ref, g_ref, z_ref, cb_ref, gm_ref = refs[5:]
        kc_ref[...] = seg("kc")
        ks = seg("ks")
        ks_ref[...] = ks
        ksb_ref[...] = ks.astype(BF16)
        kw = seg("kw")
        kw_ref[...] = kw
        kwb_ref[...] = kw.astype(BF16)
    q_ref[...] = (seg("q") * (HEAD_DIM ** -0.5 * LOG2E)).astype(BF16)
    g_ref[...] = jax.nn.sigmoid(seg("gn"))
    z_ref[...] = seg("cc") * seg("ch")
    cb_ref[...] = seg("cb")
    d2 = gm_ref.shape[1]
    half = d2 // 2
    for c in range(2):
        gm_ref[:, c * half:(c + 1) * half] = jax.nn.sigmoid(
            _dot(ub, w_ref[:, _MG_OFF + c * half:_MG_OFF + (c + 1) * half]))


def _inproj(h, mod, gn_row, w_in_p, tm, tiles_per_batch, prompt):
    n, d = h.shape
    d_conv = _C_W
    mspecs, mops = _mod_specs(mod, (3, 4), tm, tiles_per_batch)
    rows = lambda w, dt: (pl.BlockSpec((tm, w), lambda i: (i, 0)), jax.ShapeDtypeStruct((n, w), dt))
    tail = [rows(_G_W, F32), rows(d_conv, F32), rows(d_conv, F32), rows(2 * d, F32)]
    if prompt:
        nb = n // (tm * tiles_per_batch)
        t = tm * tiles_per_batch
        cpt = tm // Q_TILE
        tr = (pl.BlockSpec((None, KV_W, tm), lambda i: (i // tiles_per_batch, 0, i % tiles_per_batch)),
              jax.ShapeDtypeStruct((nb, KV_W, t), F32))
        vt = (pl.BlockSpec((cpt, V_ROWS, Q_TILE), lambda i: (i, 0, 0)),
              jax.ShapeDtypeStruct((n // Q_TILE, V_ROWS, Q_TILE), BF16))
        kr = rows(KV_HALF, BF16)
        outs = [rows(_Q_W, BF16), rows(KV_W, F32), tr, tr, tr, kr, kr, vt, vt, kr, vt, vt] + tail
    else:
        outs = [rows(_Q_W, BF16), rows(KV_W, F32), rows(KV_W, F32), rows(KV_W, F32),
                rows(KV_W, BF16), rows(KV_W, BF16)] + tail
    return pl.pallas_call(
        functools.partial(_inproj_kernel, prompt=prompt),
        grid=(n // tm,),
        in_specs=[pl.BlockSpec((tm, d), lambda i: (i, 0))] + mspecs + [
            pl.BlockSpec((1, d), lambda i: (0, 0)), _resident(w_in_p.shape)],
        out_specs=[o[0] for o in outs],
        out_shape=[o[1] for o in outs],
        compiler_params=_cparams(("arbitrary",)),
        name="inproj_prompt" if prompt else "inproj",
    )(h, *mops, gn_row, w_in_p)


def _mixout_kernel(*refs, halo, tiles_per_batch):
    if halo:
        (h_ref, o_ref, z_ref, zp_ref, cb_ref, gm_ref, g2_ref, wc_ref,
         wn_ref, wcv_ref, wo_ref, out_ref) = refs
        z = z_ref[...]
        tm = z.shape[0]
        first = (pl.program_id(0) % tiles_per_batch) == 0
        prev = jnp.where(first, 0.0, zp_ref[...])
        row = lax.broadcasted_iota(jnp.int32, z.shape, 0)
        zm1 = jnp.where(row == 0, prev[7:8, :], pltpu.roll(z, 1, 0))
        zm2 = jnp.where(row == 0, prev[6:7, :], jnp.where(row == 1, prev[7:8, :], pltpu.roll(z, 2, 0)))
    else:
        (h_ref, o_ref, z_ref, zm1_ref, zm2_ref, cb_ref, gm_ref, g2_ref, wc_ref,
         wn_ref, wcv_ref, wo_ref, out_ref) = refs
        z, zm1, zm2 = z_ref[...], zm1_ref[...], zm2_ref[...]
    conv = wc_ref[0:1, :] * zm2 + wc_ref[1:2, :] * zm1 + wc_ref[2:3, :] * z
    y = (cb_ref[...] * conv).astype(BF16)
    d = h_ref.shape[1]
    merged = gm_ref[:, 0:d] * _dot(o_ref[...], wn_ref[...]) + gm_ref[:, d:2 * d] * _dot(y, wcv_ref[...])
    out_ref[...] = h_ref[...] + g2_ref[...] * _dot(merged.astype(BF16), wo_ref[...])


def _mixout(h, o_nsa, z, z_shift, cb, gm, mod, w_conv8, w_nsa_p, w_cv, w_o, tm, tiles_per_batch):
    n, d = h.shape
    dc = z.shape[1]
    halo = z_shift is None
    mspecs, mops = _mod_specs(mod, (5,), tm, tiles_per_batch)
    tok = lambda w: pl.BlockSpec((tm, w), lambda i: (i, 0))
    if halo:
        zspecs = [tok(dc), pl.BlockSpec((8, dc), lambda i: (jnp.maximum(i * (tm // 8) - 1, 0), 0))]
        zops = [z, z]
    else:
        zspecs = [tok(dc), tok(dc), tok(dc)]
        zops = [z, z_shift[0], z_shift[1]]
    return pl.pallas_call(
        functools.partial(_mixout_kernel, halo=halo, tiles_per_batch=tiles_per_batch),
        grid=(n // tm,),
        in_specs=[tok(d), tok(o_nsa.shape[1])] + zspecs + [tok(dc), tok(2 * d)] + mspecs + [
            pl.BlockSpec((8, dc), lambda i: (0, 0)),
            _resident(w_nsa_p.shape), _resident(w_cv.shape), _resident(w_o.shape)],
        out_specs=tok(d),
        out_shape=jax.ShapeDtypeStruct((n, d), F32),
        compiler_params=_cparams(("arbitrary",)),
        name="mixout",
    )(h, o_nsa, *zops, cb, gm, *mops, w_conv8, w_nsa_p, w_cv, w_o)


CHUNK_W = CMP_STRIDE * KV_W
CMP_OUT_W = 2 * H_KV * CMP_HIDDEN


def _gelu_tanh(x):
    return x * (0.5 * (1.0 + jnp.tanh(math.sqrt(2.0 / math.pi) * (x + 0.044715 * (x * x * x)))))


def _chunk_partials(tile, wc_ref):
    accs = []
    for c in range(2):
        acc = None
        for lp in range(CMP_STRIDE // 2):
            x2 = jnp.concatenate([tile(2 * lp, c), tile(2 * lp + 1, c)], axis=1)
            part = _dot(x2, wc_ref[c, lp])
            acc = part if acc is None else acc + part
        accs.append(acc)
    hw = CMP_OUT_W // 2
    return jnp.concatenate([accs[0][:, 0:hw], accs[1][:, 0:hw], accs[0][:, hw:], accs[1][:, hw:]], axis=1)


def _compress_tail(p, wc_ref, pe_ref, w2_ref):
    n_ch = p.shape[0]
    pb = _chunk_partials(
        lambda l, c: pe_ref[:, l * KV_W + c * KV_HALF:l * KV_W + (c + 1) * KV_HALF].astype(BF16), wc_ref)
    pre = (pb[0:1, 0:CMP_OUT_W] + pb[1:2, CMP_OUT_W:]) + p[:, 0:CMP_OUT_W] \
        + pltpu.roll(p[:, CMP_OUT_W:], n_ch - 1, 0)
    out = _dot(_gelu_tanh(pre).astype(BF16), w2_ref[...])
    row = lax.broadcasted_iota(jnp.int32, out.shape, 0)
    return jnp.where(row < n_ch - 1, out, 0.0)


def _cmp_prompt_kernel(x_ref, wc_ref, pe_ref, w2_ref, o_ref):
    n_ch = x_ref.shape[0]
    p = _chunk_partials(
        lambda l, c: x_ref[:, l * KV_W + c * KV_HALF:l * KV_W + (c + 1) * KV_HALF].astype(BF16), wc_ref)
    out = _compress_tail(p, wc_ref, pe_ref, w2_ref).astype(BF16)
    o_ref[0:n_ch, :] = out
    o_ref[n_ch:2 * n_ch, :] = out


def _cmp_prompt(rows, wbig, pe8, w2bd):
    nb, n_ch, _ = rows.shape
    return pl.pallas_call(
        _cmp_prompt_kernel,
        grid=(nb,),
        in_specs=[pl.BlockSpec((None, n_ch, CHUNK_W), lambda b: (b, 0, 0)),
                  _resident(wbig.shape), _resident(pe8.shape), _resident(w2bd.shape)],
        out_specs=pl.BlockSpec((None, 2 * n_ch, KV_W), lambda b: (b, 0, 0)),
        out_shape=jax.ShapeDtypeStruct((nb, 2 * n_ch, KV_W), BF16),
        compiler_params=_cparams(("arbitrary",)),
        name="cmp_prompt",
    )(rows, wbig, pe8, w2bd)


def _page_copies(cache_ref, pt_ref, b, buf, sem, slot, n_pages, page):
    return [pltpu.make_async_copy(cache_ref.at[pt_ref[b, p]],
                                  buf.at[slot, :, pl.ds(p * page, page)],
                                  sem.at[slot]) for p in range(n_pages)]


def _gather_pages(cache_ref, pt_ref, buf, sem, n_pages, page):
    b = pl.program_id(0)
    nb = pl.num_programs(0)
    slot = b % 2

    @pl.when(b == 0)
    def _():
        for c in _page_copies(cache_ref, pt_ref, 0, buf, sem, 0, n_pages, page):
            c.start()

    @pl.when(b + 1 < nb)
    def _():
        for c in _page_copies(cache_ref, pt_ref, b + 1, buf, sem, 1 - slot, n_pages, page):
            c.start()

    for c in _page_copies(cache_ref, pt_ref, b, buf, sem, slot, n_pages, page):
        c.wait()
    return slot


def _cmp_sample_kernel(pt_ref, cache_ref, wc_ref, pe_ref, w2_ref, o_ref, buf, xs_ref, sem, *, n_pages, page):
    slot = _gather_pages(cache_ref, pt_ref, buf, sem, n_pages, page)
    for c in range(2):
        xs_ref[c] = buf[slot, c * KV_HALF:(c + 1) * KV_HALF, :].T
    n_ch = xs_ref.shape[1] // CMP_STRIDE
    p = _chunk_partials(lambda l, c: xs_ref[c, pl.ds(l, n_ch, stride=CMP_STRIDE), :].astype(BF16), wc_ref)
    o_ref[...] = _compress_tail(p, wc_ref, pe_ref, w2_ref).astype(BF16)


def _cmp_sample(page_table, cache_t, wbig, pe8, w2bd):
    nb, n_pages = page_table.shape
    page = cache_t.shape[2]
    past = n_pages * page
    n_ch = past // CMP_STRIDE
    return pl.pallas_call(
        functools.partial(_cmp_sample_kernel, n_pages=n_pages, page=page),
        grid_spec=pltpu.PrefetchScalarGridSpec(
            num_scalar_prefetch=1,
            grid=(nb,),
            in_specs=[pl.BlockSpec(memory_space=pl.ANY),
                      _resident_sp(wbig.shape), _resident_sp(pe8.shape), _resident_sp(w2bd.shape)],
            out_specs=pl.BlockSpec((None, n_ch, KV_W), lambda b, pt: (b, 0, 0)),
            scratch_shapes=[pltpu.VMEM((2, KV_W, past), F32), pltpu.VMEM((2, past, KV_HALF), F32),
                            pltpu.SemaphoreType.DMA((2,))]),
        out_shape=jax.ShapeDtypeStruct((nb, n_ch, KV_W), BF16),
        compiler_params=_cparams(("arbitrary",)),
        name="cmp_sample",
    )(page_table, cache_t, wbig, pe8, w2bd)


def _resident_sp(shape):
    nd = len(shape)
    return pl.BlockSpec(shape, lambda *_: (0,) * nd, pipeline_mode=pl.Buffered(1))


def _bias_values(d, table):
    acc = jnp.zeros(d.shape, F32) + table(0)
    for b in range(1, N_BUCKETS):
        acc = jnp.where(d >= BUCKET_THR[b], table(b), acc)
    return jnp.where(d >= 0, (acc - table(N_BUCKETS - 1)) * LOG2E, NEG)


CMP_BACK = 16


def _bias_prompt_kernel(tbl_ref, bc_ref, t0_ref, t1_ref):
    h = pl.program_id(0)
    table = lambda b: tbl_ref[b, h]
    ncp, tq = bc_ref.shape
    kj = lax.broadcasted_iota(jnp.int32, (tq, tq), 0)
    qi = lax.broadcasted_iota(jnp.int32, (tq, tq), 1)
    t0_ref[...] = _bias_values(qi - kj, table)
    t1_ref[...] = _bias_values(qi - kj + tq, table)
    c = lax.broadcasted_iota(jnp.int32, (ncp, tq), 0)
    qi = lax.broadcasted_iota(jnp.int32, (ncp, tq), 1)
    d = qi - CMP_STRIDE * c + (CMP_STRIDE * CMP_BACK - (CMP_BLOCK - 1))
    bc_ref[...] = jnp.where(c < CMP_NEAR, _bias_values(d, table), 0.0)


def _bias_prompt(rel_bias, ncp):
    tq = Q_TILE
    return pl.pallas_call(
        _bias_prompt_kernel,
        grid=(N_HEADS,),
        in_specs=[pl.BlockSpec(memory_space=pltpu.SMEM)],
        out_specs=[pl.BlockSpec((None, ncp, tq), lambda h: (h, 0, 0)),
                   pl.BlockSpec((None, tq, tq), lambda h: (h, 0, 0)),
                   pl.BlockSpec((None, tq, tq), lambda h: (h, 0, 0))],
        out_shape=[jax.ShapeDtypeStruct((N_HEADS, ncp, tq), F32),
                   jax.ShapeDtypeStruct((N_HEADS, tq, tq), F32),
                   jax.ShapeDtypeStruct((N_HEADS, tq, tq), F32)],
        compiler_params=_cparams(("arbitrary",)),
        name="bias_prompt",
    )(rel_bias)


def _bias_sample_kernel(tbl_ref, bcs_ref, bsl_ref, bnew_ref, bw_ref, *, past, s_len, n_c):
    table = lambda b: tbl_ref[:, b:b + 1]

    def qpos(shape):
        return past + lax.rem(lax.broadcasted_iota(jnp.int32, shape, 0), s_len)

    def lane(shape):
        return lax.broadcasted_iota(jnp.int32, shape, 1)

    sh = bcs_ref.shape
    n = lane(sh)
    bcs_ref[...] = jnp.where(n < n_c, _bias_values(qpos(sh) - (CMP_STRIDE * n + CMP_BLOCK - 1), table), NEG)
    sh = bsl_ref.shape
    bsl_ref[...] = _bias_values(qpos(sh) - (past - sh[1] + lane(sh)), table)
    sh = bnew_ref.shape
    bnew_ref[...] = jnp.where(lane(sh) < s_len, _bias_values(qpos(sh) - (past + lane(sh)), table), NEG)
    sh = bw_ref.shape
    d = qpos(sh) - (past - sh[1] + lane(sh))
    bw_ref[...] = jnp.where(d < WINDOW, _bias_values(d, table), NEG)


def _bias_sample(tbl_rows, past, s_len, ncs, ck, wb):
    rows = tbl_rows.shape[0]
    widths = (ncs, ck, LANES, wb)
    return pl.pallas_call(
        functools.partial(_bias_sample_kernel, past=past, s_len=s_len, n_c=(past + s_len - CMP_BLOCK) // CMP_STRIDE + 1),
        out_shape=[jax.ShapeDtypeStruct((rows, w), F32) for w in widths],
        name="bias_sample",
    )(tbl_rows)


MASK_FLOOR = -1e8
M_INIT = -1e30
TAKEN = -3e38
SCORE_AHEAD = 3
VALUE_BEHIND = 2


def _sel_weights(v):
    return jnp.where((v >= 0) & (v <= 2), 2.0, jnp.where((v == -1) | (v == 3), 1.0, 0.0))


def _attn_prompt_kernel(q_ref, gate_ref, ks0_ref, ks1_ref, vs0_ref, vs1_ref, kwk_ref, vw0_ref, vw1_ref,
                        kc2_ref, bc_ref, t0_ref, t1_ref, o_ref,
                        qm_ref, selr_ref, oc_ref, ms_ref, as_ref, mw_ref, aw_ref, *, ncp, nsel):
    tq = Q_TILE
    i = pl.program_id(1)
    q0 = i * tq
    lane = lax.broadcasted_iota(jnp.int32, (tq, LANES), 1)
    heads = [(g, r) for g in range(H_KV) for r in range(GROUP)]

    for g, r in heads:
        gmask = (lane < HEAD_DIM) if g == 0 else (lane >= HEAD_DIM)
        qm_ref[g * GROUP + r] = jnp.where(gmask, q_ref[:, r * LANES:(r + 1) * LANES], jnp.zeros((tq, LANES), BF16))

    n0 = (tq // CMP_STRIDE) * i - CMP_BACK
    ws = pl.multiple_of(lax.rem(n0 + ncp, ncp), 8)
    cl = lax.broadcasted_iota(jnp.int32, (ncp, tq), 0)
    dm = jnp.where((cl >= CMP_NEAR) & (cl < ncp - n0), NEG, 0.0) + jnp.where(cl < -n0, NEG, 0.0)
    jrow = lax.broadcasted_iota(jnp.int32, (nsel, ncp), 0)
    nn = lax.broadcasted_iota(jnp.int32, (nsel, ncp), 1) + n0
    nn = jnp.where(nn < 0, nn + ncp, nn)
    nn = jnp.where(nn >= ncp, nn - ncp, nn)
    mit = _sel_weights(nn - (SEL_BLOCK // CMP_STRIDE) * jrow).astype(BF16)

    jr = lax.broadcasted_iota(jnp.int32, (nsel, tq), 0)
    qpos = q0 + lax.broadcasted_iota(jnp.int32, (nsel, tq), 1)
    cur = lax.shift_right_logical(qpos, int(math.log2(SEL_BLOCK)))
    invalid_pen = jnp.where(jr * SEL_BLOCK <= qpos, 0.0, NEG)
    bonus = jnp.where((jr == 0) | (jr == cur) | (jr == cur - 1), FORCE_BONUS, 0.0)
    pj = lax.broadcasted_iota(jnp.int32, (LANES, LANES), 0)
    pl_ = lax.broadcasted_iota(jnp.int32, (LANES, LANES), 1)
    kj = lax.broadcasted_iota(jnp.int32, (tq, tq), 0)
    qi = lax.broadcasted_iota(jnp.int32, (tq, tq), 1)
    wmask = jnp.where(kj > qi, 0.0, NEG)
    blocks_per_chunk = tq // SEL_BLOCK

    kc = kc2_ref[pl.ds(ws, ncp), 0:KV_HALF]
    vct = kc2_ref[pl.ds(ws, ncp), KV_HALF:KV_W].astype(F32).T.astype(BF16)
    for g in range(H_KV):
        psum = None
        for r in range(GROUP):
            h = g * GROUP + r
            s = _dot_nt(kc, qm_ref[h]) + bc_ref[h] + dm
            m = jnp.maximum(jnp.max(s, axis=0, keepdims=True), MASK_FLOOR)
            e = jnp.exp2(s - m)
            p = e * (1.0 / jnp.maximum(jnp.sum(e, axis=0, keepdims=True), 1e-30))
            oc_ref[h] = _dot(vct, p.astype(BF16))
            psum = p if psum is None else psum + p

        h1, h2, h3 = _split3(psum)
        pslc = (_dot(mit, h1) + _dot(mit, h2)) + _dot(mit, h3)
        score = jnp.where(invalid_pen < 0.0, NEG, pslc + bonus)
        def take_max(_, carry):
            work, sel_acc = carry
            mx = jnp.max(work, axis=0, keepdims=True)
            first = jnp.min(jnp.where(work == mx, jr, nsel), axis=0, keepdims=True)
            hit = jr == first
            return jnp.where(hit, TAKEN, work), jnp.where(hit, 1.0, sel_acc)

        _, sel_t = lax.fori_loop(0, min(TOP_N, nsel), take_max, (score, jnp.zeros((nsel, tq), F32)))
        if nsel < LANES:
            sel_t = jnp.concatenate([sel_t, jnp.zeros((LANES - nsel, tq), F32)], axis=0)
        selr_ref[g] = sel_t.T.astype(BF16)

    def masked_queries(t, g):
        off = pl_ - _aug_lane(g)
        place = jnp.where((pj - t * blocks_per_chunk == off) & (off >= 0) & (off < blocks_per_chunk), 1.0, 0.0)
        picked = _dot(selr_ref[g], place.astype(BF16))
        auglane = (lane >= _aug_lane(g)) & (lane < _aug_lane(g) + blocks_per_chunk)
        pen = jnp.where(auglane, (picked - 1.0) * (-NEG), 0.0).astype(BF16)
        return [qm_ref[g * GROUP + r] + pen for r in range(GROUP)]

    def run(chunks):
        items = []
        for k_of, vt_of, queries, extra, m_ref, a_ref in chunks:
            cache = {}
            items += [(h, k_of, vt_of, queries, extra, m_ref, a_ref, cache) for h in range(N_HEADS)]

        def kv(item):
            g, cache = item[0] // GROUP, item[7]
            if g not in cache:
                cache[g] = (item[1](g), item[2](g))
            return cache[g]

        def scores(item):
            h, queries, extra = item[0], item[3], item[4]
            s = _dot_nt(kv(item)[0], queries(h))
            return s if extra is None else s + extra(h)

        def softmax(item, s):
            h, m_ref = item[0], item[5]
            m_prev = m_ref[h]
            m_new = jnp.maximum(m_prev, jnp.max(s, axis=0, keepdims=True))
            m_ref[h] = m_new
            return jnp.exp2(s - m_new).astype(BF16), jnp.exp2(m_prev - m_new)

        def accumulate(item, p, alpha):
            h, a_ref = item[0], item[6]
            a_ref[h] = alpha * a_ref[h] + _dot(kv(item)[1], p)

        n = len(items)
        s = {j: scores(items[j]) for j in range(SCORE_AHEAD)}
        pa = {}
        for j in range(n):
            pa[j] = softmax(items[j], s.pop(j))
            if j + SCORE_AHEAD < n:
                s[j + SCORE_AHEAD] = scores(items[j + SCORE_AHEAD])
            if j >= VALUE_BEHIND:
                accumulate(items[j - VALUE_BEHIND], *pa.pop(j - VALUE_BEHIND))
        for j in range(n - VALUE_BEHIND, n):
            accumulate(items[j], *pa.pop(j))

    for m_ref, a_ref in ((ms_ref, as_ref), (mw_ref, aw_ref)):
        m_ref[...] = jnp.full(m_ref.shape, M_INIT, F32)
        a_ref[...] = jnp.zeros(a_ref.shape, F32)

    def sel_chunk(t, extra):
        sl = pl.ds(pl.multiple_of(t * tq, tq), tq)
        qs = [masked_queries(t, g) for g in range(H_KV)]
        return (lambda g: (ks0_ref, ks1_ref)[g][sl, :], lambda g: (vs0_ref, vs1_ref)[g][t],
                lambda h: qs[h // GROUP][h % GROUP], extra, ms_ref, as_ref)

    def win_chunk(t, extra):
        sl = pl.ds(pl.multiple_of(t * tq, tq), tq)
        return (lambda g: kwk_ref[sl, :], lambda g: (vw0_ref, vw1_ref)[g][t],
                lambda h: qm_ref[h], extra, mw_ref, aw_ref)

    n_far = jnp.maximum(i - 1, 0)

    def far_pair(u, carry):
        run([sel_chunk(2 * u, None), sel_chunk(2 * u + 1, None)])
        return carry

    lax.fori_loop(0, n_far // 2, far_pair, 0)

    @pl.when(n_far % 2 == 1)
    def _():
        run([sel_chunk(n_far - 1, None)])

    gone1 = jnp.where(i >= 1, 0.0, NEG)
    gone2 = jnp.where(i >= 2, 0.0, NEG)
    wmask2 = wmask + gone2
    run([sel_chunk(jnp.maximum(i - 1, 0), lambda h: t1_ref[h] + gone1),
         sel_chunk(i, lambda h: t0_ref[h]),
         win_chunk(jnp.maximum(i - 2, 0), lambda h: wmask2),
         win_chunk(jnp.maximum(i - 1, 0), lambda h: t1_ref[h] + gone1),
         win_chunk(i, lambda h: t0_ref[h])])

    gates_t = gate_ref[...].T
    for r in range(GROUP):
        per_g = []
        for g in range(H_KV):
            h = g * GROUP + r
            a_s, a_w = as_ref[h], aw_ref[h]
            o_sel = a_s[0:HEAD_DIM] * (1.0 / jnp.maximum(a_s[HEAD_DIM:HEAD_DIM + 1], 1e-30))
            o_win = a_w[0:HEAD_DIM] * (1.0 / jnp.maximum(a_w[HEAD_DIM:HEAD_DIM + 1], 1e-30))
            per_g.append(gates_t[h:h + 1, :] * oc_ref[h, g * HEAD_DIM:(g + 1) * HEAD_DIM, :]
                         + gates_t[N_HEADS + h:N_HEADS + h + 1, :] * o_sel
                         + gates_t[2 * N_HEADS + h:2 * N_HEADS + h + 1, :] * o_win)
        o_ref[:, r * LANES:(r + 1) * LANES] = jnp.concatenate(per_g, axis=0).T.astype(BF16)


def _attn_prompt(q, gates, ks_g, vs_g, kw_k, vw_g, kc2, bias_tiles, nb, t):
    tq = Q_TILE
    assert t % tq == 0 and WINDOW == 2 * tq
    nt = t // tq
    ncp = t // CMP_STRIDE
    nsel = t // SEL_BLOCK
    assert nsel <= LANES and ncp % LANES == 0
    bc, t0, t1 = bias_tiles
    tok = lambda w: pl.BlockSpec((tq, w), lambda b, i: (b * nt + i, 0))
    k_rows = pl.BlockSpec((None, t, KV_HALF), lambda b, i: (b, 0, 0))
    v_cols = pl.BlockSpec((None, nt, V_ROWS, tq), lambda b, i: (b, 0, 0, 0))
    state = [pltpu.VMEM((N_HEADS, 1, tq), F32), pltpu.VMEM((N_HEADS, V_ROWS, tq), F32)]
    kr = lambda a: a.reshape(nb, t, KV_HALF)
    vc = lambda a: a.reshape(nb, nt, V_ROWS, tq)
    return pl.pallas_call(
        functools.partial(_attn_prompt_kernel, ncp=ncp, nsel=nsel),
        grid=(nb, nt),
        in_specs=[tok(q.shape[1]), tok(LANES), k_rows, k_rows, v_cols, v_cols, k_rows, v_cols, v_cols,
                  pl.BlockSpec((None, 2 * ncp, KV_W), lambda b, i: (b, 0, 0)),
                  _resident(bc.shape), _resident(t0.shape), _resident(t1.shape)],
        out_specs=tok(q.shape[1]),
        out_shape=jax.ShapeDtypeStruct(q.shape, BF16),
        scratch_shapes=[pltpu.VMEM((N_HEADS, tq, LANES), BF16), pltpu.VMEM((H_KV, tq, LANES), BF16),
                        pltpu.VMEM((N_HEADS, LANES, tq), F32)] + state + state,
        compiler_params=_cparams(("arbitrary", "arbitrary")),
        name="attn_prompt",
    )(q, gates, kr(ks_g[0]), kr(ks_g[1]), vc(vs_g[0]), vc(vs_g[1]), kr(kw_k), vc(vw_g[0]), vc(vw_g[1]),
      kc2, bc, t0, t1)


SAMPLE_CK = 1024


def _softmax_part(s, v, transposed):
    m = jnp.max(s, axis=-1, keepdims=True)
    p = jnp.exp2(s - m)
    pv = _dot_nt(p.astype(BF16), v) if transposed else _dot(p.astype(BF16), v)
    return m, jnp.sum(p, axis=-1, keepdims=True), pv


def _merge_parts(parts):
    m = parts[0][0]
    for mt, _, _ in parts[1:]:
        m = jnp.maximum(m, mt)
    l, acc = None, None
    for mt, lt, at in parts:
        w = jnp.exp2(mt - m)
        l = w * lt if l is None else l + w * lt
        acc = w * at if acc is None else acc + w * at
    return acc / jnp.maximum(l, 1e-30)


def _attn_sample_kernel(pt_ref, q_ref, g_ref, knew_ref, wnew_ref, win_ref, kc_ref,
                        bcs_ref, bsl_ref, bnew_ref, bw_ref, msel_ref, cache_ref, o_ref, buf, sem,
                        *, n_pages, page, past, s_len, nsel):
    slot = _gather_pages(cache_ref, pt_ref, buf, sem, n_pages, page)
    q = q_ref[...]
    rows = q.shape[0]
    gs = rows // GROUP
    nselp = msel_ref.shape[1]

    s = _dot_nt(q, kc_ref[:, 0:KV_HALF]) + bcs_ref[...]
    m = jnp.maximum(jnp.max(s, axis=-1, keepdims=True), MASK_FLOOR)
    e = jnp.exp2(s - m)
    p = e / jnp.maximum(jnp.sum(e, axis=-1, keepdims=True), 1e-30)
    o_cmp = _dot(p.astype(BF16), kc_ref[:, KV_HALF:KV_W])
    psum = p[0:gs]
    for r in range(1, GROUP):
        psum = psum + p[r * gs:(r + 1) * gs]

    h1, h2, h3 = _split3(psum)
    msel = msel_ref[...]
    pslc = (_dot(h1, msel) + _dot(h2, msel)) + _dot(h3, msel)
    j = lax.broadcasted_iota(jnp.int32, (gs, nselp), 1)
    qpos = past + lax.rem(lax.broadcasted_iota(jnp.int32, (gs, nselp), 0), s_len)
    cur = lax.shift_right_logical(qpos, int(math.log2(SEL_BLOCK)))
    valid = j * SEL_BLOCK <= qpos
    forced = (j == 0) | (j == cur) | (j == cur - 1)
    score = jnp.where(valid, pslc + jnp.where(forced, FORCE_BONUS, 0.0), NEG)
    score = jnp.where(j < nsel, score, -3e38)
    score_t = jnp.concatenate([score, jnp.zeros((LANES - gs, nselp), F32)], axis=0).T
    jp = lax.broadcasted_iota(jnp.int32, (nselp, nselp), 0)
    jj = lax.broadcasted_iota(jnp.int32, (nselp, nselp), 1)
    sel_rows = []
    for r in range(gs):
        col = score_t[:, r:r + 1]
        row = score[r:r + 1, :]
        beats = jnp.where(jj > jp, jnp.where(col >= row, 1.0, 0.0), jnp.where(col > row, 1.0, 0.0))
        rank = jnp.sum(beats, axis=0, keepdims=True)
        sel_rows.append(jnp.where(rank < min(TOP_N, nsel), 1.0, 0.0))
    sel8 = jnp.concatenate(sel_rows, axis=0)
    sel = jnp.concatenate([sel8] * GROUP, axis=0).astype(BF16)

    ck = bsl_ref.shape[1]
    n_chunks = past // ck
    erow = lax.broadcasted_iota(jnp.int32, (nselp, ck), 0)
    eblk = lax.shift_right_logical(lax.broadcasted_iota(jnp.int32, (nselp, ck), 1), int(math.log2(SEL_BLOCK)))
    scs = []
    for t in range(n_chunks):
        kt = buf[slot, 0:KV_HALF, t * ck:(t + 1) * ck].astype(BF16)
        e_t = jnp.where(erow == t * (ck // SEL_BLOCK) + eblk, 1.0, 0.0).astype(BF16)
        sc = _dot(q, kt) + (_dot(sel, e_t) - 1.0) * (-NEG)
        scs.append(sc + bsl_ref[...] if t == n_chunks - 1 else sc)
    knew = knew_ref[...]
    wnew = wnew_ref[...]
    s_new = _dot_nt(q, knew[:, 0:KV_HALF]) + bnew_ref[...]
    s_win = _dot(q, win_ref[0:KV_HALF, :].astype(BF16)) + bw_ref[...]
    s_wnew = _dot_nt(q, wnew[:, 0:KV_HALF]) + bnew_ref[...]
    parts = [_softmax_part(scs[t], buf[slot, KV_HALF:KV_W, t * ck:(t + 1) * ck].astype(BF16), True)
             for t in range(n_chunks)]
    parts.append(_softmax_part(s_new, knew[:, KV_HALF:KV_W], False))
    o_sel = _merge_parts(parts)

    o_win = _merge_parts([_softmax_part(s_win, win_ref[KV_HALF:KV_W, :].astype(BF16), True),
                          _softmax_part(s_wnew, wnew[:, KV_HALF:KV_W], False)])

    g = g_ref[...]
    o_ref[...] = g[:, 0:1] * o_cmp + g[:, 1:2] * o_sel + g[:, 2:3] * o_win


def _attn_sample(page_table, q32, g32, knew, wnew, win_t, kvc, tiles, msel, cache_t, past, s_len):
    nb, n_pages = page_table.shape
    page = cache_t.shape[2]
    rows = q32.shape[1]
    nsel = -(-(past + s_len) // SEL_BLOCK)
    bcs, bsl, bnew, bw = tiles
    per_b = lambda r, w: pl.BlockSpec((None, r, w), lambda b, pt: (b, 0, 0))
    return pl.pallas_call(
        functools.partial(_attn_sample_kernel, n_pages=n_pages, page=page, past=past, s_len=s_len, nsel=nsel),
        grid_spec=pltpu.PrefetchScalarGridSpec(
            num_scalar_prefetch=1,
            grid=(nb,),
            in_specs=[per_b(rows, LANES), per_b(rows, LANES), per_b(LANES, KV_W), per_b(LANES, KV_W),
                      per_b(KV_W, win_t.shape[2]), per_b(kvc.shape[1], KV_W),
                      _resident_sp(bcs.shape), _resident_sp(bsl.shape), _resident_sp(bnew.shape),
                      _resident_sp(bw.shape), _resident_sp(msel.shape),
                      pl.BlockSpec(memory_space=pl.ANY)],
            out_specs=per_b(rows, LANES),
            scratch_shapes=[pltpu.VMEM((2, KV_W, past), F32), pltpu.SemaphoreType.DMA((2,))]),
        out_shape=jax.ShapeDtypeStruct((nb, rows, LANES), F32),
        compiler_params=_cparams(("arbitrary",)),
        name="attn_sample",
    )(page_table, q32, g32, knew, wnew, win_t, kvc, bcs, bsl, bnew, bw, msel, cache_t)


def _prep_weights(w_in, w_cmp1, w_cmp2, pe_cmp, w_conv, w_nsa_out):
    d = w_in.shape[0]
    sizes = (N_HEADS * HEAD_DIM, KV_W, KV_W, KV_W, 3 * N_HEADS, _C_W, _C_W, _C_W, 2 * d)
    offs = [0]
    for sz in sizes:
        offs.append(offs[-1] + sz)
    part = lambda k: w_in[:, offs[k]:offs[k + 1]]
    qp = part(0).reshape(d, H_KV, GROUP, HEAD_DIM).transpose(0, 2, 1, 3).reshape(d, N_HEADS * HEAD_DIM)
    gn = jnp.pad(part(4), ((0, 0), (0, _G_W - 3 * N_HEADS)))
    w_in_p = jnp.concatenate([qp, part(1), part(2), part(3), gn, part(5), part(6), part(7), part(8)],
                             axis=1).astype(BF16)
    w_nsa_p = w_nsa_out.reshape(H_KV, GROUP, HEAD_DIM, -1).transpose(1, 0, 2, 3).reshape(
        N_HEADS * HEAD_DIM, -1).astype(BF16)
    r = CMP_BLOCK // CMP_STRIDE
    w1r = w_cmp1.reshape(2, r, CMP_STRIDE // 2, 2, HEAD_DIM, CMP_HIDDEN)
    w1t = jnp.transpose(w1r, (0, 2, 3, 4, 1, 5))
    wb = jnp.zeros((2, CMP_STRIDE // 2, 2, H_KV, HEAD_DIM, r, H_KV, CMP_HIDDEN), F32)
    w2b = jnp.zeros((2, H_KV, CMP_HIDDEN, 2, H_KV, HEAD_DIM), F32)
    for g in range(H_KV):
        wb = wb.at[:, :, :, g, :, :, g, :].set(w1t)
        for c in range(2):
            w2b = w2b.at[c, g, :, c, g, :].set(w_cmp2[c])
    wbig = wb.reshape(2, CMP_STRIDE // 2, 2 * KV_HALF, CMP_OUT_W).astype(BF16)
    w2bd = w2b.reshape(CMP_OUT_W, KV_W).astype(BF16)
    pe_r = pe_cmp.reshape(2, r, CMP_STRIDE, HEAD_DIM).transpose(1, 2, 0, 3)
    pe_rows = jnp.broadcast_to(pe_r[:, :, :, None, :], (r, CMP_STRIDE, 2, H_KV, HEAD_DIM)).reshape(r, CHUNK_W)
    pe8 = jnp.pad(pe_rows, ((0, 8 - r), (0, 0)))
    w_conv8 = jnp.pad(w_conv, ((0, 8 - CONV_WIDTH), (0, 0)))
    return w_in_p, w_nsa_p, wbig, w2bd, pe8, w_conv8


def _sel_matrix(n_c, nselp):
    n = jnp.arange(n_c + 1)[:, None]
    j = jnp.arange(nselp)[None, :]
    return _sel_weights(n - (SEL_BLOCK // CMP_STRIDE) * j).astype(BF16)


def kernel(x_prompt, x_sample, cache_kv_cmp, cache_kv_sel, state_kv_win, state_conv, page_table,
           c_prompt, c_sample, w_ada, b_ada, g_norm, w_ffn1_gu, w_ffn1_down, w_ffn2_gu, w_ffn2_down,
           w_in, w_cmp1, w_cmp2, pe_cmp, w_conv, w_nsa_out, w_conv_out, w_out, rel_bias, g_final):
    assert w_ada.shape[0] == 1, "single-layer trunk"
    nbp, t, d = x_prompt.shape
    nbs, s_len, _ = x_sample.shape
    n_pages = page_table.shape[1]
    page = cache_kv_cmp.shape[2]
    past = n_pages * page
    n_phys = cache_kv_cmp.shape[1]
    wb = state_kv_win.shape[2]
    assert wb == WINDOW and past % SAMPLE_CK == 0

    w_in_p, w_nsa_p, wbig, w2bd, pe8, w_conv8 = _prep_weights(
        w_in[0], w_cmp1[0], w_cmp2[0], pe_cmp[0], w_conv[0], w_nsa_out[0])
    wgu1, wd1 = w_ffn1_gu[0].astype(BF16), w_ffn1_down[0].astype(BF16)
    wgu2, wd2 = w_ffn2_gu[0].astype(BF16), w_ffn2_down[0].astype(BF16)
    w_cv, w_o = w_conv_out[0].astype(BF16), w_out[0].astype(BF16)
    gn = [g_norm[0][k:k + 1] for k in range(N_SUB)]

    n_c_rows = nbp + nbs
    c_all = jnp.pad(jnp.concatenate([c_prompt, c_sample], axis=0), ((0, (-n_c_rows) % 8), (0, 0)))
    mod_all = _ada(c_all, w_ada[0], b_ada[0])
    mod_p = mod_all[:nbp].reshape(nbp * 3 * N_SUB, 1, d)
    mod_s = jnp.transpose(jnp.repeat(mod_all[nbp:n_c_rows].reshape(nbs, 3 * N_SUB, d), s_len, axis=0), (1, 0, 2))

    tm = 512 if t % 512 == 0 else t
    tpb = t // tm
    xp = x_prompt.reshape(nbp * t, d)
    h1 = _ffn(xp, mod_p, 0, gn[0], wgu1, wd1, tm, tpb)
    (q, kvc, kvc_t, kvs_t, kvw_t, ks0, ks1, vs0, vs1, kw_k, vw0, vw1, gates, z, cb, gm) = _inproj(
        h1, mod_p, gn[1], w_in_p, tm, tpb, True)
    kc2 = _cmp_prompt(kvc.reshape(nbp, t // CMP_STRIDE, CHUNK_W), wbig, pe8, w2bd)
    tiles_p = _bias_prompt(rel_bias, t // CMP_STRIDE)
    o_nsa = _attn_prompt(q, gates, (ks0, ks1), (vs0, vs1), kw_k, (vw0, vw1), kc2, tiles_p, nbp, t)
    h2 = _mixout(h1, o_nsa, z, None, cb, gm, mod_p, w_conv8, w_nsa_p, w_cv, w_o, tm, tpb)
    y_prompt = _ffn(h2, mod_p, 2, gn[2], wgu2, wd2, tm, tpb, g_final).reshape(nbp, t, d)

    kv_shape = (2, H_KV, HEAD_DIM)
    kv_out = lambda a: jnp.transpose(a.reshape((1, nbp) + kv_shape + (a.shape[-1],)), (0, 1, 5, 2, 3, 4))
    kv_cmp_p = kv_out(kvc_t)
    kv_sel_p = kv_out(kvs_t)
    keep = min(WINDOW, t)
    kv_win_p = kv_out(kvw_t[:, :, t - keep:])
    conv_p = z.reshape(1, nbp, t, -1)[:, :, t - (CONV_WIDTH - 1):]

    ns = nbs * s_len
    xs = x_sample.reshape(ns, d)
    h1s = _ffn(xs, mod_s, 0, gn[0], wgu1, wd1, ns, 1)
    qs, kvc_s, kvs_s, kvw_s, kvs_sb, kvw_sb, gates_s, z_s, cb_s, gm_s = _inproj(
        h1s, mod_s, gn[1], w_in_p, ns, 1, False)

    pos_minor = lambda a: jnp.transpose(a, (0, 2, 3, 4, 1)).reshape(a.shape[0], KV_W, a.shape[1])
    kvc_past = _cmp_sample(page_table, pos_minor(cache_kv_cmp[0]), wbig, pe8, w2bd)
    n_c = (past + s_len - CMP_BLOCK) // CMP_STRIDE + 1
    nsel = -(-(past + s_len) // SEL_BLOCK)
    nselp = -(-nsel // LANES) * LANES
    rows = N_HEADS * s_len
    head_of_row = [g * GROUP + r for r in range(GROUP) for g in range(H_KV) for _ in range(s_len)]
    tbl_rows = jnp.pad(rel_bias.T[jnp.array(head_of_row)], ((0, 0), (0, LANES - N_BUCKETS)))
    tiles_s = _bias_sample(tbl_rows, past, s_len, past // CMP_STRIDE, SAMPLE_CK, wb)
    msel = _sel_matrix(past // CMP_STRIDE - 1, nselp)

    q5 = qs.reshape(nbs, s_len, GROUP, H_KV, HEAD_DIM).transpose(0, 2, 3, 1, 4)
    q32 = jnp.zeros((nbs, GROUP, H_KV, s_len, H_KV, HEAD_DIM), BF16)
    for g in range(H_KV):
        q32 = q32.at[:, :, g, :, g, :].set(q5[:, :, g])
    q32 = q32.reshape(nbs, rows, LANES)
    g5 = gates_s[:, :3 * N_HEADS].reshape(nbs, s_len, 3, H_KV, GROUP).transpose(0, 4, 3, 1, 2)
    g32 = jnp.pad(g5.reshape(nbs, rows, 3), ((0, 0), (0, 0), (0, LANES - 3)))
    pad_new = lambda a: jnp.pad(a.reshape(nbs, s_len, KV_W), ((0, 0), (0, LANES - s_len), (0, 0)))
    o32 = _attn_sample(page_table, q32, g32, pad_new(kvs_sb), pad_new(kvw_sb),
                       pos_minor(state_kv_win[0]), kvc_past, tiles_s, msel,
                       pos_minor(cache_kv_sel[0]), past, s_len)
    o6 = o32.reshape(nbs, GROUP, H_KV, s_len, H_KV, HEAD_DIM)
    o_s = jnp.stack([o6[:, :, g, :, g, :] for g in range(H_KV)], axis=3)
    o_nsa_s = o_s.transpose(0, 2, 1, 3, 4).reshape(ns, N_HEADS * HEAD_DIM).astype(BF16)

    full = jnp.concatenate([state_conv[0], z_s.reshape(nbs, s_len, -1)], axis=1)
    z_shift = (full[:, 1:1 + s_len].reshape(ns, -1), full[:, 0:s_len].reshape(ns, -1))
    h2s = _mixout(h1s, o_nsa_s, z_s, z_shift, cb_s, gm_s, mod_s, w_conv8, w_nsa_p, w_cv, w_o, ns, 1)
    y_sample = _ffn(h2s, mod_s, 2, gn[2], wgu2, wd2, ns, 1, g_final).reshape(nbs, s_len, d)

    kv_cmp_s = kvc_s.reshape((1, nbs, s_len) + kv_shape)
    kv_sel_s = kvs_s.reshape((1, nbs, s_len) + kv_shape)
    win_full = jnp.concatenate([state_kv_win[0], kvw_s.reshape((nbs, s_len) + kv_shape)], axis=1)
    keep_s = min(WINDOW, wb + s_len)
    kv_win_s = win_full[None, :, wb + s_len - keep_s:]
    conv_s = full[None, :, s_len:]
    return (y_prompt, y_sample, kv_cmp_p, kv_sel_p, kv_win_p, conv_p, kv_cmp_s, kv_sel_s, kv_win_s, conv_s)
```

```python
import functools
import math

import jax
import jax.numpy as jnp
from jax import lax
from jax.experimental import pallas as pl
from jax.experimental.pallas import tpu as pltpu

F32 = jnp.float32
BF16 = jnp.bfloat16

HEAD_DIM = 64
N_HEADS = 8
H_KV = 2
GROUP = N_HEADS // H_KV
CMP_BLOCK = 32
CMP_STRIDE = 16
CMP_HIDDEN = 2 * HEAD_DIM
SEL_BLOCK = 64
TOP_N = 16
WINDOW = 512
CONV_WIDTH = 3
N_BUCKETS = 32
MAX_DISTANCE = 128
N_SUB = 3
EPS = 1e-6
NEG = -1e9
LOG2E = math.log2(math.e)
FORCE_BONUS = 1e3
KV_W = 2 * H_KV * HEAD_DIM
KV_HALF = H_KV * HEAD_DIM
LANES = 128
Q_TILE = 256
CMP_NEAR = 32
VMEM_LIMIT = 56 * 1024 * 1024


def _bucket_thresholds():
    max_exact = N_BUCKETS // 2

    def bucket(d):
        if d < max_exact:
            return d
        large = max_exact + int(math.log(d / max_exact) / math.log(MAX_DISTANCE / max_exact)
                                * (N_BUCKETS - max_exact))
        return min(large, N_BUCKETS - 1)

    thr, d = [], 0
    for b in range(N_BUCKETS):
        while bucket(d) < b:
            d += 1
        thr.append(d)
    return tuple(thr)


BUCKET_THR = _bucket_thresholds()
FAR_DIST = BUCKET_THR[-1]


def _cparams(sem):
    return pltpu.CompilerParams(dimension_semantics=sem, vmem_limit_bytes=VMEM_LIMIT)


def _resident(shape):
    nd = len(shape)
    return pl.BlockSpec(shape, lambda *_: (0,) * nd, pipeline_mode=pl.Buffered(1))


def _dot(a, b):
    return jnp.dot(a, b, preferred_element_type=F32)


def _dot_nt(a, b):
    return lax.dot_general(a, b, (((1,), (1,)), ((), ())), preferred_element_type=F32)


def _split3(x):
    h1 = x.astype(BF16)
    r1 = x - h1.astype(F32)
    h2 = r1.astype(BF16)
    h3 = (r1 - h2.astype(F32)).astype(BF16)
    return h1, h2, h3


def _modulated_norm(x, g, shift, scale):
    y = x * lax.rsqrt(jnp.mean(x * x, axis=-1, keepdims=True) + EPS)
    return (y * g) * (1.0 + scale) + shift


def _ada_kernel(c_ref, w_ref, b_ref, o_ref):
    c = c_ref[...]
    a = (c * jax.nn.sigmoid(c)).astype(BF16)
    o_ref[...] = _dot(a, w_ref[...].astype(BF16)) + b_ref[...]


def _ada(c_all, w_ada, b_ada):
    rows, d = c_all.shape
    n = w_ada.shape[1]
    tn = n // 8
    return pl.pallas_call(
        _ada_kernel,
        grid=(n // tn,),
        in_specs=[pl.BlockSpec((rows, d), lambda j: (0, 0)),
                  pl.BlockSpec((d, tn), lambda j: (0, j)),
                  pl.BlockSpec((1, tn), lambda j: (0, j))],
        out_specs=pl.BlockSpec((rows, tn), lambda j: (0, j)),
        out_shape=jax.ShapeDtypeStruct((rows, n), F32),
        compiler_params=_cparams(("arbitrary",)),
        name="ada",
    )(c_all, w_ada, b_ada.reshape(1, n))


def _ffn_kernel(*refs, d_ff, fc, final_norm):
    if final_norm:
        x_ref, sh_ref, sc_ref, gt_ref, gn_ref, wgu_ref, wd_ref, gf_ref, o_ref = refs
    else:
        x_ref, sh_ref, sc_ref, gt_ref, gn_ref, wgu_ref, wd_ref, o_ref = refs
    x = x_ref[...]
    ub = _modulated_norm(x, gn_ref[...], sh_ref[...], sc_ref[...]).astype(BF16)
    acc = jnp.zeros(x.shape, F32)
    for c in range(d_ff // fc):
        g = _dot(ub, wgu_ref[:, c * fc:(c + 1) * fc])
        v = _dot(ub, wgu_ref[:, d_ff + c * fc:d_ff + (c + 1) * fc])
        a = ((g * jax.nn.sigmoid(g)) * v).astype(BF16)
        acc = acc + _dot(a, wd_ref[c * fc:(c + 1) * fc, :])
    h = x + (0.5 * gt_ref[...]) * acc
    if final_norm:
        h = (h * lax.rsqrt(jnp.mean(h * h, axis=-1, keepdims=True) + EPS)) * gf_ref[...]
    o_ref[...] = h


def _mod_specs(mod, ks, tm, tiles_per_batch):
    if mod.ndim == 3 and mod.shape[1] == 1:
        d = mod.shape[-1]
        specs = [pl.BlockSpec((None, 1, d), lambda i, k=k: ((i // tiles_per_batch) * (3 * N_SUB) + k, 0, 0))
                 for k in ks]
        return specs, [mod] * len(ks)
    d = mod.shape[-1]
    specs = [pl.BlockSpec((None, tm, d), lambda i, k=k: (k, i, 0)) for k in ks]
    return specs, [mod] * len(ks)


def _ffn(x, mod, sub, gn_row, w_gu, w_down, tm, tiles_per_batch, g_final=None):
    n, d = x.shape
    d_ff = w_down.shape[0]
    fc = d_ff // 2 if (d_ff // 2) % LANES == 0 else d_ff
    final_norm = g_final is not None
    mspecs, mops = _mod_specs(mod, (3 * sub, 3 * sub + 1, 3 * sub + 2), tm, tiles_per_batch)
    in_specs = [pl.BlockSpec((tm, d), lambda i: (i, 0))] + mspecs + [
        pl.BlockSpec((1, d), lambda i: (0, 0)), _resident(w_gu.shape), _resident(w_down.shape)]
    ops = [x] + mops + [gn_row, w_gu, w_down]
    if final_norm:
        in_specs.append(pl.BlockSpec((1, d), lambda i: (0, 0)))
        ops.append(g_final.reshape(1, d))
    return pl.pallas_call(
        functools.partial(_ffn_kernel, d_ff=d_ff, fc=fc, final_norm=final_norm),
        grid=(n // tm,),
        in_specs=in_specs,
        out_specs=pl.BlockSpec((tm, d), lambda i: (i, 0)),
        out_shape=jax.ShapeDtypeStruct((n, d), F32),
        compiler_params=_cparams(("arbitrary",)),
        name="ffn_final" if final_norm else "ffn",
    )(*ops)


_Q_W = N_HEADS * HEAD_DIM
_G_W = LANES
_C_W = 512
_SEG = {}
_off = 0
for _name, _w in (("q", _Q_W), ("kc", KV_W), ("ks", KV_W), ("kw", KV_W), ("gn", _G_W),
                  ("ch", _C_W), ("cb", _C_W), ("cc", _C_W)):
    _SEG[_name] = (_off, _off + _w)
    _off += _w
_MG_OFF = _off


V_ROWS = HEAD_DIM + 16


def _aug_lane(g):
    return HEAD_DIM if g == 0 else 0


def _inproj_kernel(*refs, prompt):
    x_ref, sh_ref, sc_ref, gn_ref, w_ref = refs[:5]
    ub = _modulated_norm(x_ref[...], gn_ref[...], sh_ref[...], sc_ref[...]).astype(BF16)

    def seg(name):
        lo, hi = _SEG[name]
        return _dot(ub, w_ref[:, lo:hi])

    if prompt:
        (q_ref, kc_ref, kct_ref, kst_ref, kwt_ref, ks0_ref, ks1_ref, vs0_ref, vs1_ref, kwk_ref, vw0_ref, vw1_ref,
         g_ref, z_ref, cb_ref, gm_ref) = refs[5:]
        kc = seg("kc")
        kc_ref[...] = kc
        kct_ref[...] = kc.T
        tm = kc.shape[0]
        lane = lax.broadcasted_iota(jnp.int32, (tm, KV_HALF), 1)
        blk = lax.shift_right_logical(lax.broadcasted_iota(jnp.int32, (tm, KV_HALF), 0),
                                      int(math.log2(SEL_BLOCK))) & (Q_TILE // SEL_BLOCK - 1)
        ones_rows = jnp.where(lax.broadcasted_iota(jnp.int32, (V_ROWS - HEAD_DIM, tm), 0) == 0, 1.0, 0.0)
        for name, t_ref, k_refs, vt_refs in (("ks", kst_ref, (ks0_ref, ks1_ref), (vs0_ref, vs1_ref)),
                                             ("kw", kwt_ref, (kwk_ref,), (vw0_ref, vw1_ref))):
            kv = seg(name)
            kv_t = kv.T
            t_ref[...] = kv_t
            kk = kv[:, 0:KV_HALF]
            if len(k_refs) == 1:
                k_refs[0][...] = kk.astype(BF16)
            else:
                for g, k_ref in enumerate(k_refs):
                    own = (lane < HEAD_DIM) if g == 0 else (lane >= HEAD_DIM)
                    onehot = jnp.where(lane - _aug_lane(g) == blk, 1.0, 0.0)
                    k_ref[...] = jnp.where(own, kk, onehot).astype(BF16)
            for g, vt_ref in enumerate(vt_refs):
                vg = jnp.concatenate(
                    [kv_t[KV_HALF + g * HEAD_DIM:KV_HALF + (g + 1) * HEAD_DIM, :], ones_rows], axis=0)
                for c in range(vt_ref.shape[0]):
                    vt_ref[c] = vg[:, c * Q_TILE:(c + 1) * Q_TILE].astype(BF16)
    else:
        q_ref, kc_ref, ks_ref, kw_ref, ksb_ref, kwb_ref, g_ref, z_ref, cb_ref, gm_ref = refs[5:]
        kc_ref[...] = seg("kc")
        ks = seg("ks")
        ks_ref[...] = ks
        ksb_ref[...] = ks.astype(BF16)
        kw = seg("kw")
        kw_ref[...] = kw
        kwb_ref[...] = kw.astype(BF16)
    q_ref[...] = (seg("q") * (HEAD_DIM ** -0.5 * LOG2E)).astype(BF16)
    g_ref[...] = jax.nn.sigmoid(seg("gn"))
    z_ref[...] = seg("cc") * seg("ch")
    cb_ref[...] = seg("cb").astype(cb_ref.dtype)
    d2 = gm_ref.shape[1]
    half = d2 // 2
    for c in range(2):
        gm_ref[:, c * half:(c + 1) * half] = jax.nn.sigmoid(
            _dot(ub, w_ref[:, _MG_OFF + c * half:_MG_OFF + (c + 1) * half])).astype(gm_ref.dtype)


def _inproj(h, mod, gn_row, w_in_p, tm, tiles_per_batch, prompt):
    n, d = h.shape
    d_conv = _C_W
    mspecs, mops = _mod_specs(mod, (3, 4), tm, tiles_per_batch)
    rows = lambda w, dt: (pl.BlockSpec((tm, w), lambda i: (i, 0)), jax.ShapeDtypeStruct((n, w), dt))
    tail = [rows(_G_W, F32), rows(d_conv, F32), rows(d_conv, BF16), rows(2 * d, BF16)]
    if prompt:
        nb = n // (tm * tiles_per_batch)
        t = tm * tiles_per_batch
        cpt = tm // Q_TILE
        tr = (pl.BlockSpec((None, KV_W, tm), lambda i: (i // tiles_per_batch, 0, i % tiles_per_batch)),
              jax.ShapeDtypeStruct((nb, KV_W, t), F32))
        vt = (pl.BlockSpec((cpt, V_ROWS, Q_TILE), lambda i: (i, 0, 0)),
              jax.ShapeDtypeStruct((n // Q_TILE, V_ROWS, Q_TILE), BF16))
        kr = rows(KV_HALF, BF16)
        outs = [rows(_Q_W, BF16), rows(KV_W, F32), tr, tr, tr, kr, kr, vt, vt, kr, vt, vt] + tail
    else:
        outs = [rows(_Q_W, BF16), rows(KV_W, F32), rows(KV_W, F32), rows(KV_W, F32),
                rows(KV_W, BF16), rows(KV_W, BF16)] + tail
    return pl.pallas_call(
        functools.partial(_inproj_kernel, prompt=prompt),
        grid=(n // tm,),
        in_specs=[pl.BlockSpec((tm, d), lambda i: (i, 0))] + mspecs + [
            pl.BlockSpec((1, d), lambda i: (0, 0)), _resident(w_in_p.shape)],
        out_specs=[o[0] for o in outs],
        out_shape=[o[1] for o in outs],
        compiler_params=_cparams(("arbitrary",)),
        name="inproj_prompt" if prompt else "inproj",
    )(h, *mops, gn_row, w_in_p)


def _mixout_kernel(*refs, halo, tiles_per_batch):
    if halo:
        (h_ref, o_ref, z_ref, zp_ref, cb_ref, gm_ref, g2_ref, wc_ref,
         wn_ref, wcv_ref, wo_ref, out_ref) = refs
        z = z_ref[...]
        tm = z.shape[0]
        first = (pl.program_id(0) % tiles_per_batch) == 0
        prev = jnp.where(first, 0.0, zp_ref[...])
        row = lax.broadcasted_iota(jnp.int32, z.shape, 0)
        zm1 = jnp.where(row == 0, prev[7:8, :], pltpu.roll(z, 1, 0))
        zm2 = jnp.where(row == 0, prev[6:7, :], jnp.where(row == 1, prev[7:8, :], pltpu.roll(z, 2, 0)))
    else:
        (h_ref, o_ref, z_ref, zm1_ref, zm2_ref, cb_ref, gm_ref, g2_ref, wc_ref,
         wn_ref, wcv_ref, wo_ref, out_ref) = refs
        z, zm1, zm2 = z_ref[...], zm1_ref[...], zm2_ref[...]
    conv = wc_ref[0:1, :] * zm2 + wc_ref[1:2, :] * zm1 + wc_ref[2:3, :] * z
    y = (cb_ref[...].astype(F32) * conv).astype(BF16)
    d = h_ref.shape[1]
    merged = gm_ref[:, 0:d].astype(F32) * _dot(o_ref[...], wn_ref[...]) \
        + gm_ref[:, d:2 * d].astype(F32) * _dot(y, wcv_ref[...])
    out_ref[...] = h_ref[...] + g2_ref[...] * _dot(merged.astype(BF16), wo_ref[...])


def _mixout(h, o_nsa, z, z_shift, cb, gm, mod, w_conv8, w_nsa_p, w_cv, w_o, tm, tiles_per_batch):
    n, d = h.shape
    dc = z.shape[1]
    halo = z_shift is None
    mspecs, mops = _mod_specs(mod, (5,), tm, tiles_per_batch)
    tok = lambda w: pl.BlockSpec((tm, w), lambda i: (i, 0))
    if halo:
        zspecs = [tok(dc), pl.BlockSpec((8, dc), lambda i: (jnp.maximum(i * (tm // 8) - 1, 0), 0))]
        zops = [z, z]
    else:
        zspecs = [tok(dc), tok(dc), tok(dc)]
        zops = [z, z_shift[0], z_shift[1]]
    return pl.pallas_call(
        functools.partial(_mixout_kernel, halo=halo, tiles_per_batch=tiles_per_batch),
        grid=(n // tm,),
        in_specs=[tok(d), tok(o_nsa.shape[1])] + zspecs + [tok(dc), tok(2 * d)] + mspecs + [
            pl.BlockSpec((8, dc), lambda i: (0, 0)),
            _resident(w_nsa_p.shape), _resident(w_cv.shape), _resident(w_o.shape)],
        out_specs=tok(d),
        out_shape=jax.ShapeDtypeStruct((n, d), F32),
        compiler_params=_cparams(("arbitrary",)),
        name="mixout",
    )(h, o_nsa, *zops, cb, gm, *mops, w_conv8, w_nsa_p, w_cv, w_o)


CHUNK_W = CMP_STRIDE * KV_W
CMP_OUT_W = 2 * H_KV * CMP_HIDDEN


def _gelu_tanh(x):
    return x * (0.5 * (1.0 + jnp.tanh(math.sqrt(2.0 / math.pi) * (x + 0.044715 * (x * x * x)))))


def _chunk_partials(tile, wc_ref, after_step=None):
    accs = []
    for c in range(2):
        acc = None
        for lp in range(CMP_STRIDE // 2):
            x2 = jnp.concatenate([tile(2 * lp, c), tile(2 * lp + 1, c)], axis=1)
            part = _dot(x2, wc_ref[c, lp])
            acc = part if acc is None else acc + part
            if after_step is not None:
                after_step(c, lp)
        accs.append(acc)
    hw = CMP_OUT_W // 2
    return jnp.concatenate([accs[0][:, 0:hw], accs[1][:, 0:hw], accs[0][:, hw:], accs[1][:, hw:]], axis=1)


def _compress_tail(p, wc_ref, pe_ref, w2_ref):
    n_ch = p.shape[0]
    pb = _chunk_partials(
        lambda l, c: pe_ref[:, l * KV_W + c * KV_HALF:l * KV_W + (c + 1) * KV_HALF].astype(BF16), wc_ref)
    pre = (pb[0:1, 0:CMP_OUT_W] + pb[1:2, CMP_OUT_W:]) + p[:, 0:CMP_OUT_W] \
        + pltpu.roll(p[:, CMP_OUT_W:], n_ch - 1, 0)
    out = _dot(_gelu_tanh(pre).astype(BF16), w2_ref[...])
    row = lax.broadcasted_iota(jnp.int32, out.shape, 0)
    return jnp.where(row < n_ch - 1, out, 0.0)


def _cmp_prompt_kernel(x_ref, wc_ref, pe_ref, w2_ref, o_ref):
    n_ch = x_ref.shape[0]
    p = _chunk_partials(
        lambda l, c: x_ref[:, l * KV_W + c * KV_HALF:l * KV_W + (c + 1) * KV_HALF].astype(BF16), wc_ref)
    out = _compress_tail(p, wc_ref, pe_ref, w2_ref).astype(BF16)
    o_ref[0:n_ch, :] = out
    o_ref[n_ch:2 * n_ch, :] = out


def _cmp_prompt(rows, wbig, pe8, w2bd):
    nb, n_ch, _ = rows.shape
    return pl.pallas_call(
        _cmp_prompt_kernel,
        grid=(nb,),
        in_specs=[pl.BlockSpec((None, n_ch, CHUNK_W), lambda b: (b, 0, 0)),
                  _resident(wbig.shape), _resident(pe8.shape), _resident(w2bd.shape)],
        out_specs=pl.BlockSpec((None, 2 * n_ch, KV_W), lambda b: (b, 0, 0)),
        out_shape=jax.ShapeDtypeStruct((nb, 2 * n_ch, KV_W), BF16),
        compiler_params=_cparams(("arbitrary",)),
        name="cmp_prompt",
    )(rows, wbig, pe8, w2bd)


def _page_copies(cache_ref, pt_ref, b, buf, sem, slot, n_pages, page):
    return [pltpu.make_async_copy(cache_ref.at[pt_ref[b, p]],
                                  buf.at[slot, :, pl.ds(p * page, page)],
                                  sem.at[slot]) for p in range(n_pages)]


def _gather_pages(cache_ref, pt_ref, buf, sem, n_pages, page):
    b = pl.program_id(0)
    nb = pl.num_programs(0)
    slot = b % 2

    @pl.when(b == 0)
    def _():
        for c in _page_copies(cache_ref, pt_ref, 0, buf, sem, 0, n_pages, page):
            c.start()

    @pl.when(b + 1 < nb)
    def _():
        for c in _page_copies(cache_ref, pt_ref, b + 1, buf, sem, 1 - slot, n_pages, page):
            c.start()

    for c in _page_copies(cache_ref, pt_ref, b, buf, sem, slot, n_pages, page):
        c.wait()
    return slot


def _cmp_sample_kernel(pt_ref, cache_ref, wc_ref, pe_ref, w2_ref, o_ref, buf, xk_ref, xv_ref, sem,
                       *, n_pages, page):
    slot = _gather_pages(cache_ref, pt_ref, buf, sem, n_pages, page)
    xs = (xk_ref, xv_ref)
    past = xk_ref.shape[0]
    n_ch = past // CMP_STRIDE
    n_slabs = CMP_STRIDE // 2
    slab = past // n_slabs

    def transpose_slab(c, j):
        xs[c][j * slab:(j + 1) * slab, :] = buf[slot, c * KV_HALF:(c + 1) * KV_HALF, j * slab:(j + 1) * slab].T

    for j in range(n_slabs):
        transpose_slab(0, j)

    def v_half_transposes(c, lp):
        if c == 0:
            transpose_slab(1, lp)

    p = _chunk_partials(lambda l, c: xs[c][pl.ds(l, n_ch, stride=CMP_STRIDE), :].astype(BF16), wc_ref,
                        v_half_transposes)
    o_ref[...] = _compress_tail(p, wc_ref, pe_ref, w2_ref).astype(BF16)


def _cmp_sample(page_table, cache_t, wbig, pe8, w2bd):
    nb, n_pages = page_table.shape
    page = cache_t.shape[2]
    past = n_pages * page
    n_ch = past // CMP_STRIDE
    return pl.pallas_call(
        functools.partial(_cmp_sample_kernel, n_pages=n_pages, page=page),
        grid_spec=pltpu.PrefetchScalarGridSpec(
            num_scalar_prefetch=1,
            grid=(nb,),
            in_specs=[pl.BlockSpec(memory_space=pl.ANY),
                      _resident_sp(wbig.shape), _resident_sp(pe8.shape), _resident_sp(w2bd.shape)],
            out_specs=pl.BlockSpec((None, n_ch, KV_W), lambda b, pt: (b, 0, 0)),
            scratch_shapes=[pltpu.VMEM((2, KV_W, past), F32), pltpu.VMEM((past, KV_HALF), F32),
                            pltpu.VMEM((past, KV_HALF), F32), pltpu.SemaphoreType.DMA((2,))]),
        out_shape=jax.ShapeDtypeStruct((nb, n_ch, KV_W), BF16),
        compiler_params=_cparams(("arbitrary",)),
        name="cmp_sample",
    )(page_table, cache_t, wbig, pe8, w2bd)


def _resident_sp(shape):
    nd = len(shape)
    return pl.BlockSpec(shape, lambda *_: (0,) * nd, pipeline_mode=pl.Buffered(1))


def _bias_values(d, table):
    acc = jnp.zeros(d.shape, F32) + table(0)
    for b in range(1, N_BUCKETS):
        acc = jnp.where(d >= BUCKET_THR[b], table(b), acc)
    return jnp.where(d >= 0, (acc - table(N_BUCKETS - 1)) * LOG2E, NEG)


CMP_BACK = 16


def _bias_prompt_kernel(tbl_ref, bc_ref, t0_ref, t1_ref):
    h = pl.program_id(0)
    table = lambda b: tbl_ref[b, h]
    ncp, tq = bc_ref.shape
    kj = lax.broadcasted_iota(jnp.int32, (tq, tq), 0)
    qi = lax.broadcasted_iota(jnp.int32, (tq, tq), 1)
    t0_ref[...] = _bias_values(qi - kj, table)
    t1_ref[...] = _bias_values(qi - kj + tq, table)
    c = lax.broadcasted_iota(jnp.int32, (ncp, tq), 0)
    qi = lax.broadcasted_iota(jnp.int32, (ncp, tq), 1)
    d = qi - CMP_STRIDE * c + (CMP_STRIDE * CMP_BACK - (CMP_BLOCK - 1))
    bc_ref[...] = jnp.where(c < CMP_NEAR, _bias_values(d, table), 0.0)


def _bias_prompt(rel_bias, ncp):
    tq = Q_TILE
    return pl.pallas_call(
        _bias_prompt_kernel,
        grid=(N_HEADS,),
        in_specs=[pl.BlockSpec(memory_space=pltpu.SMEM)],
        out_specs=[pl.BlockSpec((None, ncp, tq), lambda h: (h, 0, 0)),
                   pl.BlockSpec((None, tq, tq), lambda h: (h, 0, 0)),
                   pl.BlockSpec((None, tq, tq), lambda h: (h, 0, 0))],
        out_shape=[jax.ShapeDtypeStruct((N_HEADS, ncp, tq), F32),
                   jax.ShapeDtypeStruct((N_HEADS, tq, tq), F32),
                   jax.ShapeDtypeStruct((N_HEADS, tq, tq), F32)],
        compiler_params=_cparams(("arbitrary",)),
        name="bias_prompt",
    )(rel_bias)


def _bias_sample_kernel(tbl_ref, bcs_ref, bsl_ref, bnew_ref, bw_ref, *, past, s_len, n_c):
    table = lambda b: tbl_ref[:, b:b + 1]

    def qpos(shape):
        return past + lax.rem(lax.broadcasted_iota(jnp.int32, shape, 0), s_len)

    def lane(shape):
        return lax.broadcasted_iota(jnp.int32, shape, 1)

    sh = bcs_ref.shape
    n = lane(sh)
    bcs_ref[...] = jnp.where(n < n_c, _bias_values(qpos(sh) - (CMP_STRIDE * n + CMP_BLOCK - 1), table), NEG)
    sh = bsl_ref.shape
    bsl_ref[...] = _bias_values(qpos(sh) - (past - sh[1] + lane(sh)), table)
    sh = bnew_ref.shape
    bnew_ref[...] = jnp.where(lane(sh) < s_len, _bias_values(qpos(sh) - (past + lane(sh)), table), NEG)
    sh = bw_ref.shape
    d = qpos(sh) - (past - sh[1] + lane(sh))
    bw_ref[...] = jnp.where(d < WINDOW, _bias_values(d, table), NEG)


def _bias_sample(tbl_rows, past, s_len, ncs, ck, wb):
    rows = tbl_rows.shape[0]
    widths = (ncs, ck, LANES, wb)
    return pl.pallas_call(
        functools.partial(_bias_sample_kernel, past=past, s_len=s_len, n_c=(past + s_len - CMP_BLOCK) // CMP_STRIDE + 1),
        out_shape=[jax.ShapeDtypeStruct((rows, w), F32) for w in widths],
        name="bias_sample",
    )(tbl_rows)


MASK_FLOOR = -1e8
M_INIT = -1e30
TAKEN = -3e38
SCORE_AHEAD = 3
VALUE_BEHIND = 2


def _sel_weights(v):
    return jnp.where((v >= 0) & (v <= 2), 2.0, jnp.where((v == -1) | (v == 3), 1.0, 0.0))


def _attn_prompt_kernel(q_ref, gate_ref, ks0_ref, ks1_ref, vs0_ref, vs1_ref, kwk_ref, vw0_ref, vw1_ref,
                        kc2_ref, bc_ref, t0_ref, t1_ref, o_ref,
                        qm_ref, selr_ref, oc_ref, ms_ref, as_ref, mw_ref, aw_ref, *, ncp, nsel):
    tq = Q_TILE
    i = pl.program_id(1)
    q0 = i * tq
    lane = lax.broadcasted_iota(jnp.int32, (tq, LANES), 1)
    heads = [(g, r) for g in range(H_KV) for r in range(GROUP)]

    for g, r in heads:
        gmask = (lane < HEAD_DIM) if g == 0 else (lane >= HEAD_DIM)
        qm_ref[g * GROUP + r] = jnp.where(gmask, q_ref[:, r * LANES:(r + 1) * LANES], jnp.zeros((tq, LANES), BF16))

    n0 = (tq // CMP_STRIDE) * i - CMP_BACK
    ws = pl.multiple_of(lax.rem(n0 + ncp, ncp), 8)
    cl = lax.broadcasted_iota(jnp.int32, (ncp, tq), 0)
    dm = jnp.where((cl >= CMP_NEAR) & (cl < ncp - n0), NEG, 0.0) + jnp.where(cl < -n0, NEG, 0.0)
    jrow = lax.broadcasted_iota(jnp.int32, (nsel, ncp), 0)
    nn = lax.broadcasted_iota(jnp.int32, (nsel, ncp), 1) + n0
    nn = jnp.where(nn < 0, nn + ncp, nn)
    nn = jnp.where(nn >= ncp, nn - ncp, nn)
    mit = _sel_weights(nn - (SEL_BLOCK // CMP_STRIDE) * jrow).astype(BF16)

    jr = lax.broadcasted_iota(jnp.int32, (nsel, tq), 0)
    qpos = q0 + lax.broadcasted_iota(jnp.int32, (nsel, tq), 1)
    cur = lax.shift_right_logical(qpos, int(math.log2(SEL_BLOCK)))
    invalid_pen = jnp.where(jr * SEL_BLOCK <= qpos, 0.0, NEG)
    bonus = jnp.where((jr == 0) | (jr == cur) | (jr == cur - 1), FORCE_BONUS, 0.0)
    pj = lax.broadcasted_iota(jnp.int32, (LANES, LANES), 0)
    pl_ = lax.broadcasted_iota(jnp.int32, (LANES, LANES), 1)
    kj = lax.broadcasted_iota(jnp.int32, (tq, tq), 0)
    qi = lax.broadcasted_iota(jnp.int32, (tq, tq), 1)
    wmask = jnp.where(kj > qi, 0.0, NEG)
    blocks_per_chunk = tq // SEL_BLOCK

    kc = kc2_ref[pl.ds(ws, ncp), 0:KV_HALF]
    vct = kc2_ref[pl.ds(ws, ncp), KV_HALF:KV_W].astype(F32).T.astype(BF16)
    def cmp_scores(h):
        return _dot_nt(kc, qm_ref[h]) + bc_ref[h] + dm

    ahead = {h: cmp_scores(h) for h in range(2)}
    psum = [None] * H_KV
    for h in range(N_HEADS):
        s = ahead.pop(h)
        if h + 2 < N_HEADS:
            ahead[h + 2] = cmp_scores(h + 2)
        m = jnp.maximum(jnp.max(s, axis=0, keepdims=True), MASK_FLOOR)
        e = jnp.exp2(s - m)
        p = e * (1.0 / jnp.maximum(jnp.sum(e, axis=0, keepdims=True), 1e-30))
        oc_ref[h] = _dot(vct, p.astype(BF16))
        psum[h // GROUP] = p if psum[h // GROUP] is None else psum[h // GROUP] + p

    scores = []
    for g in range(H_KV):
        h1, h2, h3 = _split3(psum[g])
        pslc = (_dot(mit, h1) + _dot(mit, h2)) + _dot(mit, h3)
        scores.append(jnp.where(invalid_pen < 0.0, NEG, pslc + bonus))

    def take_max(_, carry):
        out = []
        for work, sel_acc in carry:
            mx = jnp.max(work, axis=0, keepdims=True)
            first = jnp.min(jnp.where(work == mx, jr, nsel), axis=0, keepdims=True)
            hit = jr == first
            out.append((jnp.where(hit, TAKEN, work), jnp.where(hit, 1.0, sel_acc)))
        return tuple(out)

    picked = lax.fori_loop(0, min(TOP_N, nsel), take_max,
                           tuple((sc, jnp.zeros((nsel, tq), F32)) for sc in scores))
    for g in range(H_KV):
        sel_t = picked[g][1]
        if nsel < LANES:
            sel_t = jnp.concatenate([sel_t, jnp.zeros((LANES - nsel, tq), F32)], axis=0)
        selr_ref[g] = sel_t.T.astype(BF16)

    def masked_queries(t, g):
        off = pl_ - _aug_lane(g)
        place = jnp.where((pj - t * blocks_per_chunk == off) & (off >= 0) & (off < blocks_per_chunk), 1.0, 0.0)
        picked = _dot(selr_ref[g], place.astype(BF16))
        auglane = (lane >= _aug_lane(g)) & (lane < _aug_lane(g) + blocks_per_chunk)
        pen = jnp.where(auglane, (picked - 1.0) * (-NEG), 0.0).astype(BF16)
        return [qm_ref[g * GROUP + r] + pen for r in range(GROUP)]

    def run(chunks):
        items = []
        for k_of, vt_of, queries, extra, m_ref, a_ref in chunks:
            cache = {}
            items += [(h, k_of, vt_of, queries, extra, m_ref, a_ref, cache) for h in range(N_HEADS)]

        def kv(item):
            g, cache = item[0] // GROUP, item[7]
            if g not in cache:
                cache[g] = (item[1](g), item[2](g))
            return cache[g]

        def scores(item):
            h, queries, extra = item[0], item[3], item[4]
            s = _dot_nt(kv(item)[0], queries(h))
            return s if extra is None else s + extra(h)

        def softmax(item, s):
            h, m_ref = item[0], item[5]
            m_prev = m_ref[h]
            m_new = jnp.maximum(m_prev, jnp.max(s, axis=0, keepdims=True))
            m_ref[h] = m_new
            return jnp.exp2(s - m_new).astype(BF16), jnp.exp2(m_prev - m_new)

        def accumulate(item, p, alpha):
            h, a_ref = item[0], item[6]
            a_ref[h] = alpha * a_ref[h] + _dot(kv(item)[1], p)

        n = len(items)
        s = {j: scores(items[j]) for j in range(SCORE_AHEAD)}
        pa = {}
        for j in range(n):
            pa[j] = softmax(items[j], s.pop(j))
            if j + SCORE_AHEAD < n:
                s[j + SCORE_AHEAD] = scores(items[j + SCORE_AHEAD])
            if j >= VALUE_BEHIND:
                accumulate(items[j - VALUE_BEHIND], *pa.pop(j - VALUE_BEHIND))
        for j in range(n - VALUE_BEHIND, n):
            accumulate(items[j], *pa.pop(j))

    for m_ref, a_ref in ((ms_ref, as_ref), (mw_ref, aw_ref)):
        m_ref[...] = jnp.full(m_ref.shape, M_INIT, F32)
        a_ref[...] = jnp.zeros(a_ref.shape, F32)

    def sel_chunk(t, extra):
        sl = pl.ds(pl.multiple_of(t * tq, tq), tq)
        qs = [masked_queries(t, g) for g in range(H_KV)]
        return (lambda g: (ks0_ref, ks1_ref)[g][sl, :], lambda g: (vs0_ref, vs1_ref)[g][t],
                lambda h: qs[h // GROUP][h % GROUP], extra, ms_ref, as_ref)

    def win_chunk(t, extra):
        sl = pl.ds(pl.multiple_of(t * tq, tq), tq)
        return (lambda g: kwk_ref[sl, :], lambda g: (vw0_ref, vw1_ref)[g][t],
                lambda h: qm_ref[h], extra, mw_ref, aw_ref)

    n_far = jnp.maximum(i - 1, 0)

    def far_pair(u, carry):
        run([sel_chunk(2 * u, None), sel_chunk(2 * u + 1, None)])
        return carry

    lax.fori_loop(0, n_far // 2, far_pair, 0)

    @pl.when(n_far % 2 == 1)
    def _():
        run([sel_chunk(n_far - 1, None)])

    gone1 = jnp.where(i >= 1, 0.0, NEG)
    gone2 = jnp.where(i >= 2, 0.0, NEG)
    wmask2 = wmask + gone2
    run([sel_chunk(jnp.maximum(i - 1, 0), lambda h: t1_ref[h] + gone1),
         sel_chunk(i, lambda h: t0_ref[h]),
         win_chunk(jnp.maximum(i - 2, 0), lambda h: wmask2),
         win_chunk(jnp.maximum(i - 1, 0), lambda h: t1_ref[h] + gone1),
         win_chunk(i, lambda h: t0_ref[h])])

    gates_t = gate_ref[...].T
    for r in range(GROUP):
        per_g = []
        for g in range(H_KV):
            h = g * GROUP + r
            a_s, a_w = as_ref[h], aw_ref[h]
            o_sel = a_s[0:HEAD_DIM] * (1.0 / jnp.maximum(a_s[HEAD_DIM:HEAD_DIM + 1], 1e-30))
            o_win = a_w[0:HEAD_DIM] * (1.0 / jnp.maximum(a_w[HEAD_DIM:HEAD_DIM + 1], 1e-30))
            per_g.append(gates_t[h:h + 1, :] * oc_ref[h, g * HEAD_DIM:(g + 1) * HEAD_DIM, :]
                         + gates_t[N_HEADS + h:N_HEADS + h + 1, :] * o_sel
                         + gates_t[2 * N_HEADS + h:2 * N_HEADS + h + 1, :] * o_win)
        o_ref[:, r * LANES:(r + 1) * LANES] = jnp.concatenate(per_g, axis=0).T.astype(BF16)


def _attn_prompt(q, gates, ks_g, vs_g, kw_k, vw_g, kc2, bias_tiles, nb, t):
    tq = Q_TILE
    assert t % tq == 0 and WINDOW == 2 * tq
    nt = t // tq
    ncp = t // CMP_STRIDE
    nsel = t // SEL_BLOCK
    assert nsel <= LANES and ncp % LANES == 0
    bc, t0, t1 = bias_tiles
    tok = lambda w: pl.BlockSpec((tq, w), lambda b, i: (b * nt + i, 0))
    k_rows = pl.BlockSpec((None, t, KV_HALF), lambda b, i: (b, 0, 0))
    v_cols = pl.BlockSpec((None, nt, V_ROWS, tq), lambda b, i: (b, 0, 0, 0))
    state = [pltpu.VMEM((N_HEADS, 1, tq), F32), pltpu.VMEM((N_HEADS, V_ROWS, tq), F32)]
    kr = lambda a: a.reshape(nb, t, KV_HALF)
    vc = lambda a: a.reshape(nb, nt, V_ROWS, tq)
    return pl.pallas_call(
        functools.partial(_attn_prompt_kernel, ncp=ncp, nsel=nsel),
        grid=(nb, nt),
        in_specs=[tok(q.shape[1]), tok(LANES), k_rows, k_rows, v_cols, v_cols, k_rows, v_cols, v_cols,
                  pl.BlockSpec((None, 2 * ncp, KV_W), lambda b, i: (b, 0, 0)),
                  _resident(bc.shape), _resident(t0.shape), _resident(t1.shape)],
        out_specs=tok(q.shape[1]),
        out_shape=jax.ShapeDtypeStruct(q.shape, BF16),
        scratch_shapes=[pltpu.VMEM((N_HEADS, tq, LANES), BF16), pltpu.VMEM((H_KV, tq, LANES), BF16),
                        pltpu.VMEM((N_HEADS, LANES, tq), F32)] + state + state,
        compiler_params=_cparams(("arbitrary", "arbitrary")),
        name="attn_prompt",
    )(q, gates, kr(ks_g[0]), kr(ks_g[1]), vc(vs_g[0]), vc(vs_g[1]), kr(kw_k), vc(vw_g[0]), vc(vw_g[1]),
      kc2, bc, t0, t1)


SAMPLE_CK = 1024


def _softmax_part(s, v, transposed):
    m = jnp.max(s, axis=-1, keepdims=True)
    p = jnp.exp2(s - m)
    pv = _dot_nt(p.astype(BF16), v) if transposed else _dot(p.astype(BF16), v)
    return m, jnp.sum(p, axis=-1, keepdims=True), pv


def _merge_parts(parts):
    m = parts[0][0]
    for mt, _, _ in parts[1:]:
        m = jnp.maximum(m, mt)
    l, acc = None, None
    for mt, lt, at in parts:
        w = jnp.exp2(mt - m)
        l = w * lt if l is None else l + w * lt
        acc = w * at if acc is None else acc + w * at
    return acc / jnp.maximum(l, 1e-30)


def _attn_sample_kernel(pt_ref, q_ref, g_ref, knew_ref, wnew_ref, wnewt_ref, win_ref, kc_ref,
                        bcs_ref, bsl_ref, bnew_ref, bw_ref, msel_ref, cache_ref, o_ref, wout_ref, buf, sem,
                        *, n_pages, page, past, s_len, nsel):
    slot = _gather_pages(cache_ref, pt_ref, buf, sem, n_pages, page)

    wb = win_ref.shape[1]
    shifted = pltpu.roll(win_ref[...], wb - s_len, 1)
    fresh = pltpu.roll(wnewt_ref[...], LANES - s_len, 1)
    tail_lane = lax.broadcasted_iota(jnp.int32, fresh.shape, 1) >= LANES - s_len
    wout_ref[:, 0:wb - LANES] = shifted[:, 0:wb - LANES]
    wout_ref[:, wb - LANES:wb] = jnp.where(tail_lane, fresh, shifted[:, wb - LANES:wb])
    q = q_ref[...]
    rows = q.shape[0]
    gs = rows // GROUP
    nselp = msel_ref.shape[1]

    s = _dot_nt(q, kc_ref[:, 0:KV_HALF]) + bcs_ref[...]
    m = jnp.maximum(jnp.max(s, axis=-1, keepdims=True), MASK_FLOOR)
    e = jnp.exp2(s - m)
    p = e / jnp.maximum(jnp.sum(e, axis=-1, keepdims=True), 1e-30)
    o_cmp = _dot(p.astype(BF16), kc_ref[:, KV_HALF:KV_W])
    psum = p[0:gs]
    for r in range(1, GROUP):
        psum = psum + p[r * gs:(r + 1) * gs]

    h1, h2, h3 = _split3(psum)
    msel = msel_ref[...]
    pslc = (_dot(h1, msel) + _dot(h2, msel)) + _dot(h3, msel)
    j = lax.broadcasted_iota(jnp.int32, (gs, nselp), 1)
    qpos = past + lax.rem(lax.broadcasted_iota(jnp.int32, (gs, nselp), 0), s_len)
    cur = lax.shift_right_logical(qpos, int(math.log2(SEL_BLOCK)))
    valid = j * SEL_BLOCK <= qpos
    forced = (j == 0) | (j == cur) | (j == cur - 1)
    score = jnp.where(valid, pslc + jnp.where(forced, FORCE_BONUS, 0.0), NEG)
    score = jnp.where(j < nsel, score, -3e38)
    score_t = jnp.concatenate([score, jnp.zeros((LANES - gs, nselp), F32)], axis=0).T
    jp = lax.broadcasted_iota(jnp.int32, (nselp, nselp), 0)
    jj = lax.broadcasted_iota(jnp.int32, (nselp, nselp), 1)
    sel_rows = []
    for r in range(gs):
        col = score_t[:, r:r + 1]
        row = score[r:r + 1, :]
        beats = jnp.where(jj > jp, jnp.where(col >= row, 1.0, 0.0), jnp.where(col > row, 1.0, 0.0))
        rank = jnp.sum(beats, axis=0, keepdims=True)
        sel_rows.append(jnp.where(rank < min(TOP_N, nsel), 1.0, 0.0))
    sel8 = jnp.concatenate(sel_rows, axis=0)
    sel = jnp.concatenate([sel8] * GROUP, axis=0).astype(BF16)

    ck = bsl_ref.shape[1]
    n_chunks = past // ck
    erow = lax.broadcasted_iota(jnp.int32, (nselp, ck), 0)
    eblk = lax.shift_right_logical(lax.broadcasted_iota(jnp.int32, (nselp, ck), 1), int(math.log2(SEL_BLOCK)))
    scs = []
    for t in range(n_chunks):
        kt = buf[slot, 0:KV_HALF, t * ck:(t + 1) * ck].astype(BF16)
        e_t = jnp.where(erow == t * (ck // SEL_BLOCK) + eblk, 1.0, 0.0).astype(BF16)
        sc = _dot(q, kt) + (_dot(sel, e_t) - 1.0) * (-NEG)
        scs.append(sc + bsl_ref[...] if t == n_chunks - 1 else sc)
    knew = knew_ref[...]
    wnew = wnew_ref[...]
    s_new = _dot_nt(q, knew[:, 0:KV_HALF]) + bnew_ref[...]
    s_win = _dot(q, win_ref[0:KV_HALF, :].astype(BF16)) + bw_ref[...]
    s_wnew = _dot_nt(q, wnew[:, 0:KV_HALF]) + bnew_ref[...]
    parts = [_softmax_part(scs[t], buf[slot, KV_HALF:KV_W, t * ck:(t + 1) * ck].astype(BF16), True)
             for t in range(n_chunks)]
    parts.append(_softmax_part(s_new, knew[:, KV_HALF:KV_W], False))
    o_sel = _merge_parts(parts)

    o_win = _merge_parts([_softmax_part(s_win, win_ref[KV_HALF:KV_W, :].astype(BF16), True),
                          _softmax_part(s_wnew, wnew[:, KV_HALF:KV_W], False)])

    g = g_ref[...]
    o_ref[...] = g[:, 0:1] * o_cmp + g[:, 1:2] * o_sel + g[:, 2:3] * o_win


def _attn_sample(page_table, q32, g32, knew, wnew, wnew_t, win_t, kvc, tiles, msel, cache_t, past, s_len):
    nb, n_pages = page_table.shape
    page = cache_t.shape[2]
    rows = q32.shape[1]
    wb = win_t.shape[2]
    nsel = -(-(past + s_len) // SEL_BLOCK)
    bcs, bsl, bnew, bw = tiles
    per_b = lambda r, w: pl.BlockSpec((None, r, w), lambda b, pt: (b, 0, 0))
    return pl.pallas_call(
        functools.partial(_attn_sample_kernel, n_pages=n_pages, page=page, past=past, s_len=s_len, nsel=nsel),
        grid_spec=pltpu.PrefetchScalarGridSpec(
            num_scalar_prefetch=1,
            grid=(nb,),
            in_specs=[per_b(rows, LANES), per_b(rows, LANES), per_b(LANES, KV_W), per_b(LANES, KV_W),
                      per_b(KV_W, LANES), per_b(KV_W, wb), per_b(kvc.shape[1], KV_W),
                      _resident_sp(bcs.shape), _resident_sp(bsl.shape), _resident_sp(bnew.shape),
                      _resident_sp(bw.shape), _resident_sp(msel.shape),
                      pl.BlockSpec(memory_space=pl.ANY)],
            out_specs=[per_b(rows, LANES), per_b(KV_W, wb)],
            scratch_shapes=[pltpu.VMEM((2, KV_W, past), F32), pltpu.SemaphoreType.DMA((2,))]),
        out_shape=[jax.ShapeDtypeStruct((nb, rows, LANES), F32), jax.ShapeDtypeStruct((nb, KV_W, wb), F32)],
        compiler_params=_cparams(("arbitrary",)),
        name="attn_sample",
    )(page_table, q32, g32, knew, wnew, wnew_t, win_t, kvc, bcs, bsl, bnew, bw, msel, cache_t)


def _prep_weights(w_in, w_cmp1, w_cmp2, pe_cmp, w_conv, w_nsa_out):
    d = w_in.shape[0]
    sizes = (N_HEADS * HEAD_DIM, KV_W, KV_W, KV_W, 3 * N_HEADS, _C_W, _C_W, _C_W, 2 * d)
    offs = [0]
    for sz in sizes:
        offs.append(offs[-1] + sz)
    part = lambda k: w_in[:, offs[k]:offs[k + 1]]
    qp = part(0).reshape(d, H_KV, GROUP, HEAD_DIM).transpose(0, 2, 1, 3).reshape(d, N_HEADS * HEAD_DIM)
    gn = jnp.pad(part(4), ((0, 0), (0, _G_W - 3 * N_HEADS)))
    w_in_p = jnp.concatenate([qp, part(1), part(2), part(3), gn, part(5), part(6), part(7), part(8)],
                             axis=1).astype(BF16)
    w_nsa_p = w_nsa_out.reshape(H_KV, GROUP, HEAD_DIM, -1).transpose(1, 0, 2, 3).reshape(
        N_HEADS * HEAD_DIM, -1).astype(BF16)
    r = CMP_BLOCK // CMP_STRIDE
    w1r = w_cmp1.reshape(2, r, CMP_STRIDE // 2, 2, HEAD_DIM, CMP_HIDDEN)
    w1t = jnp.transpose(w1r, (0, 2, 3, 4, 1, 5))
    w1e = w1t[:, :, :, :, :, None, :].astype(BF16)
    z1 = jnp.zeros_like(w1e)
    wb = jnp.stack([jnp.concatenate([w1e if gp == g else z1 for gp in range(H_KV)], axis=5)
                    for g in range(H_KV)], axis=3)
    wbig = wb.reshape(2, CMP_STRIDE // 2, 2 * KV_HALF, CMP_OUT_W)
    z2 = jnp.zeros((CMP_HIDDEN, HEAD_DIM), F32)
    w2bd = jnp.concatenate(
        [jnp.concatenate([w_cmp2[c] if (cp, gp) == (c, g) else z2 for cp in range(2) for gp in range(H_KV)], axis=1)
         for c in range(2) for g in range(H_KV)], axis=0).astype(BF16)
    pe_r = pe_cmp.reshape(2, r, CMP_STRIDE, HEAD_DIM).transpose(1, 2, 0, 3)
    pe_rows = jnp.broadcast_to(pe_r[:, :, :, None, :], (r, CMP_STRIDE, 2, H_KV, HEAD_DIM)).reshape(r, CHUNK_W)
    pe8 = jnp.pad(pe_rows, ((0, 8 - r), (0, 0)))
    w_conv8 = jnp.pad(w_conv, ((0, 8 - CONV_WIDTH), (0, 0)))
    return w_in_p, w_nsa_p, wbig, w2bd, pe8, w_conv8


def _sel_matrix(n_c, nselp):
    n = jnp.arange(n_c + 1)[:, None]
    j = jnp.arange(nselp)[None, :]
    return _sel_weights(n - (SEL_BLOCK // CMP_STRIDE) * j).astype(BF16)


def kernel(x_prompt, x_sample, cache_kv_cmp, cache_kv_sel, state_kv_win, state_conv, page_table,
           c_prompt, c_sample, w_ada, b_ada, g_norm, w_ffn1_gu, w_ffn1_down, w_ffn2_gu, w_ffn2_down,
           w_in, w_cmp1, w_cmp2, pe_cmp, w_conv, w_nsa_out, w_conv_out, w_out, rel_bias, g_final):
    assert w_ada.shape[0] == 1, "single-layer trunk"
    nbp, t, d = x_prompt.shape
    nbs, s_len, _ = x_sample.shape
    n_pages = page_table.shape[1]
    page = cache_kv_cmp.shape[2]
    past = n_pages * page
    n_phys = cache_kv_cmp.shape[1]
    wb = state_kv_win.shape[2]
    assert wb == WINDOW and past % SAMPLE_CK == 0

    w_in_p, w_nsa_p, wbig, w2bd, pe8, w_conv8 = _prep_weights(
        w_in[0], w_cmp1[0], w_cmp2[0], pe_cmp[0], w_conv[0], w_nsa_out[0])
    wgu1, wd1 = w_ffn1_gu[0].astype(BF16), w_ffn1_down[0].astype(BF16)
    wgu2, wd2 = w_ffn2_gu[0].astype(BF16), w_ffn2_down[0].astype(BF16)
    w_cv, w_o = w_conv_out[0].astype(BF16), w_out[0].astype(BF16)
    gn = [g_norm[0][k:k + 1] for k in range(N_SUB)]

    n_c_rows = nbp + nbs
    c_all = jnp.pad(jnp.concatenate([c_prompt, c_sample], axis=0), ((0, (-n_c_rows) % 8), (0, 0)))
    mod_all = _ada(c_all, w_ada[0], b_ada[0])
    mod_p = mod_all[:nbp].reshape(nbp * 3 * N_SUB, 1, d)
    mod_s = jnp.transpose(jnp.repeat(mod_all[nbp:n_c_rows].reshape(nbs, 3 * N_SUB, d), s_len, axis=0), (1, 0, 2))

    tm = 512 if t % 512 == 0 else t
    tpb = t // tm
    xp = x_prompt.reshape(nbp * t, d)
    h1 = _ffn(xp, mod_p, 0, gn[0], wgu1, wd1, tm, tpb)
    (q, kvc, kvc_t, kvs_t, kvw_t, ks0, ks1, vs0, vs1, kw_k, vw0, vw1, gates, z, cb, gm) = _inproj(
        h1, mod_p, gn[1], w_in_p, tm, tpb, True)
    kc2 = _cmp_prompt(kvc.reshape(nbp, t // CMP_STRIDE, CHUNK_W), wbig, pe8, w2bd)
    tiles_p = _bias_prompt(rel_bias, t // CMP_STRIDE)
    o_nsa = _attn_prompt(q, gates, (ks0, ks1), (vs0, vs1), kw_k, (vw0, vw1), kc2, tiles_p, nbp, t)
    h2 = _mixout(h1, o_nsa, z, None, cb, gm, mod_p, w_conv8, w_nsa_p, w_cv, w_o, tm, tpb)
    y_prompt = _ffn(h2, mod_p, 2, gn[2], wgu2, wd2, tm, tpb, g_final).reshape(nbp, t, d)

    kv_shape = (2, H_KV, HEAD_DIM)
    kv_out = lambda a: jnp.transpose(a.reshape((1, nbp) + kv_shape + (a.shape[-1],)), (0, 1, 5, 2, 3, 4))
    kv_cmp_p = kv_out(kvc_t)
    kv_sel_p = kv_out(kvs_t)
    keep = min(WINDOW, t)
    kv_win_p = kv_out(kvw_t[:, :, t - keep:])
    conv_p = z.reshape(1, nbp, t, -1)[:, :, t - (CONV_WIDTH - 1):]

    ns = nbs * s_len
    xs = x_sample.reshape(ns, d)
    h1s = _ffn(xs, mod_s, 0, gn[0], wgu1, wd1, ns, 1)
    qs, kvc_s, kvs_s, kvw_s, kvs_sb, kvw_sb, gates_s, z_s, cb_s, gm_s = _inproj(
        h1s, mod_s, gn[1], w_in_p, ns, 1, False)

    pos_minor = lambda a: jnp.transpose(a, (0, 2, 3, 4, 1)).reshape(a.shape[0], KV_W, a.shape[1])
    kvc_past = _cmp_sample(page_table, pos_minor(cache_kv_cmp[0]), wbig, pe8, w2bd)
    n_c = (past + s_len - CMP_BLOCK) // CMP_STRIDE + 1
    nsel = -(-(past + s_len) // SEL_BLOCK)
    nselp = -(-nsel // LANES) * LANES
    rows = N_HEADS * s_len
    head_of_row = [g * GROUP + r for r in range(GROUP) for g in range(H_KV) for _ in range(s_len)]
    tbl_rows = jnp.pad(rel_bias.T[jnp.array(head_of_row)], ((0, 0), (0, LANES - N_BUCKETS)))
    tiles_s = _bias_sample(tbl_rows, past, s_len, past // CMP_STRIDE, SAMPLE_CK, wb)
    msel = _sel_matrix(past // CMP_STRIDE - 1, nselp)

    q5 = qs.reshape(nbs, s_len, GROUP, H_KV, HEAD_DIM).transpose(0, 2, 3, 1, 4)
    q32 = jnp.zeros((nbs, GROUP, H_KV, s_len, H_KV, HEAD_DIM), BF16)
    for g in range(H_KV):
        q32 = q32.at[:, :, g, :, g, :].set(q5[:, :, g])
    q32 = q32.reshape(nbs, rows, LANES)
    g5 = gates_s[:, :3 * N_HEADS].reshape(nbs, s_len, 3, H_KV, GROUP).transpose(0, 4, 3, 1, 2)
    g32 = jnp.pad(g5.reshape(nbs, rows, 3), ((0, 0), (0, 0), (0, LANES - 3)))
    pad_new = lambda a: jnp.pad(a.reshape(nbs, s_len, KV_W), ((0, 0), (0, LANES - s_len), (0, 0)))
    wnew_t = jnp.pad(jnp.transpose(kvw_s.reshape(nbs, s_len, KV_W), (0, 2, 1)), ((0, 0), (0, 0), (0, LANES - s_len)))
    o32, win_next = _attn_sample(page_table, q32, g32, pad_new(kvs_sb), pad_new(kvw_sb), wnew_t,
                                 pos_minor(state_kv_win[0]), kvc_past, tiles_s, msel,
                                 pos_minor(cache_kv_sel[0]), past, s_len)
    o6 = o32.reshape(nbs, GROUP, H_KV, s_len, H_KV, HEAD_DIM)
    o_s = jnp.stack([o6[:, :, g, :, g, :] for g in range(H_KV)], axis=3)
    o_nsa_s = o_s.transpose(0, 2, 1, 3, 4).reshape(ns, N_HEADS * HEAD_DIM).astype(BF16)

    full = jnp.concatenate([state_conv[0], z_s.reshape(nbs, s_len, -1)], axis=1)
    z_shift = (full[:, 1:1 + s_len].reshape(ns, -1), full[:, 0:s_len].reshape(ns, -1))
    h2s = _mixout(h1s, o_nsa_s, z_s, z_shift, cb_s, gm_s, mod_s, w_conv8, w_nsa_p, w_cv, w_o, ns, 1)
    y_sample = _ffn(h2s, mod_s, 2, gn[2], wgu2, wd2, ns, 1, g_final).reshape(nbs, s_len, d)

    kv_cmp_s = kvc_s.reshape((1, nbs, s_len) + kv_shape)
    kv_sel_s = kvs_s.reshape((1, nbs, s_len) + kv_shape)
    kv_win_s = jnp.transpose(win_next.reshape((1, nbs) + kv_shape + (wb,)), (0, 1, 5, 2, 3, 4))
    conv_s = full[None, :, s_len:]
    return (y_prompt, y_sample, kv_cmp_p, kv_sel_p, kv_win_p, conv_p, kv_cmp_s, kv_sel_s, kv_win_s, conv_s)
```

```python
import functools
import math

import jax
import jax.numpy as jnp
from jax import lax
from jax.experimental import pallas as pl
from jax.experimental.pallas import tpu as pltpu

F32 = jnp.float32
BF16 = jnp.bfloat16

HEAD_DIM = 64
N_HEADS = 8
H_KV = 2
GROUP = N_HEADS // H_KV
CMP_BLOCK = 32
CMP_STRIDE = 16
CMP_HIDDEN = 2 * HEAD_DIM
SEL_BLOCK = 64
TOP_N = 16
WINDOW = 512
CONV_WIDTH = 3
N_BUCKETS = 32
MAX_DISTANCE = 128
N_SUB = 3
EPS = 1e-6
NEG = -1e9
LOG2E = math.log2(math.e)
FORCE_BONUS = 1e3
KV_W = 2 * H_KV * HEAD_DIM
KV_HALF = H_KV * HEAD_DIM
LANES = 128
Q_TILE = 256
CMP_NEAR = 32
VMEM_LIMIT = 56 * 1024 * 1024


def _bucket_thresholds():
    max_exact = N_BUCKETS // 2

    def bucket(d):
        if d < max_exact:
            return d
        large = max_exact + int(math.log(d / max_exact) / math.log(MAX_DISTANCE / max_exact)
                                * (N_BUCKETS - max_exact))
        return min(large, N_BUCKETS - 1)

    thr, d = [], 0
    for b in range(N_BUCKETS):
        while bucket(d) < b:
            d += 1
        thr.append(d)
    return tuple(thr)


BUCKET_THR = _bucket_thresholds()
FAR_DIST = BUCKET_THR[-1]


def _cparams(sem):
    return pltpu.CompilerParams(dimension_semantics=sem, vmem_limit_bytes=VMEM_LIMIT)


def _resident(shape):
    nd = len(shape)
    return pl.BlockSpec(shape, lambda *_: (0,) * nd, pipeline_mode=pl.Buffered(1))


def _dot(a, b):
    return jnp.dot(a, b, preferred_element_type=F32)


def _dot_nt(a, b):
    return lax.dot_general(a, b, (((1,), (1,)), ((), ())), preferred_element_type=F32)


def _split3(x):
    h1 = x.astype(BF16)
    r1 = x - h1.astype(F32)
    h2 = r1.astype(BF16)
    h3 = (r1 - h2.astype(F32)).astype(BF16)
    return h1, h2, h3


def _modulated_norm(x, g, shift, scale):
    y = x * lax.rsqrt(jnp.mean(x * x, axis=-1, keepdims=True) + EPS)
    return (y * g) * (1.0 + scale) + shift


def _ada_kernel(c_ref, w_ref, b_ref, o_ref):
    c = c_ref[...]
    a = (c * jax.nn.sigmoid(c)).astype(BF16)
    o_ref[...] = _dot(a, w_ref[...].astype(BF16)) + b_ref[...]


def _ada(c_all, w_ada, b_ada):
    rows, d = c_all.shape
    n = w_ada.shape[1]
    tn = n // 8
    return pl.pallas_call(
        _ada_kernel,
        grid=(n // tn,),
        in_specs=[pl.BlockSpec((rows, d), lambda j: (0, 0)),
                  pl.BlockSpec((d, tn), lambda j: (0, j)),
                  pl.BlockSpec((1, tn), lambda j: (0, j))],
        out_specs=pl.BlockSpec((rows, tn), lambda j: (0, j)),
        out_shape=jax.ShapeDtypeStruct((rows, n), F32),
        compiler_params=_cparams(("arbitrary",)),
        name="ada",
    )(c_all, w_ada, b_ada.reshape(1, n))


def _ffn_kernel(*refs, d_ff, fc, final_norm):
    if final_norm:
        x_ref, sh_ref, sc_ref, gt_ref, gn_ref, wgu_ref, wd_ref, gf_ref, o_ref = refs
    else:
        x_ref, sh_ref, sc_ref, gt_ref, gn_ref, wgu_ref, wd_ref, o_ref = refs
    x = x_ref[...]
    ub = _modulated_norm(x, gn_ref[...], sh_ref[...], sc_ref[...]).astype(BF16)
    acc = jnp.zeros(x.shape, F32)
    for c in range(d_ff // fc):
        g = _dot(ub, wgu_ref[:, c * fc:(c + 1) * fc])
        v = _dot(ub, wgu_ref[:, d_ff + c * fc:d_ff + (c + 1) * fc])
        a = ((g * jax.nn.sigmoid(g)) * v).astype(BF16)
        acc = acc + _dot(a, wd_ref[c * fc:(c + 1) * fc, :])
    h = x + (0.5 * gt_ref[...]) * acc
    if final_norm:
        h = (h * lax.rsqrt(jnp.mean(h * h, axis=-1, keepdims=True) + EPS)) * gf_ref[...]
    o_ref[...] = h


def _mod_specs(mod, ks, tm, tiles_per_batch):
    if mod.ndim == 3 and mod.shape[1] == 1:
        d = mod.shape[-1]
        specs = [pl.BlockSpec((None, 1, d), lambda i, k=k: ((i // tiles_per_batch) * (3 * N_SUB) + k, 0, 0))
                 for k in ks]
        return specs, [mod] * len(ks)
    d = mod.shape[-1]
    specs = [pl.BlockSpec((None, tm, d), lambda i, k=k: (k, i, 0)) for k in ks]
    return specs, [mod] * len(ks)


def _ffn(x, mod, sub, gn_row, w_gu, w_down, tm, tiles_per_batch, g_final=None):
    n, d = x.shape
    d_ff = w_down.shape[0]
    fc = d_ff // 2 if (d_ff // 2) % LANES == 0 else d_ff
    final_norm = g_final is not None
    mspecs, mops = _mod_specs(mod, (3 * sub, 3 * sub + 1, 3 * sub + 2), tm, tiles_per_batch)
    in_specs = [pl.BlockSpec((tm, d), lambda i: (i, 0))] + mspecs + [
        pl.BlockSpec((1, d), lambda i: (0, 0)), _resident(w_gu.shape), _resident(w_down.shape)]
    ops = [x] + mops + [gn_row, w_gu, w_down]
    if final_norm:
        in_specs.append(pl.BlockSpec((1, d), lambda i: (0, 0)))
        ops.append(g_final.reshape(1, d))
    return pl.pallas_call(
        functools.partial(_ffn_kernel, d_ff=d_ff, fc=fc, final_norm=final_norm),
        grid=(n // tm,),
        in_specs=in_specs,
        out_specs=pl.BlockSpec((tm, d), lambda i: (i, 0)),
        out_shape=jax.ShapeDtypeStruct((n, d), F32),
        compiler_params=_cparams(("arbitrary",)),
        name="ffn_final" if final_norm else "ffn",
    )(*ops)


_Q_W = N_HEADS * HEAD_DIM
_G_W = LANES
_C_W = 512
_SEG = {}
_off = 0
for _name, _w in (("q", _Q_W), ("kc", KV_W), ("ks", KV_W), ("kw", KV_W), ("gn", _G_W),
                  ("ch", _C_W), ("cb", _C_W), ("cc", _C_W)):
    _SEG[_name] = (_off, _off + _w)
    _off += _w
_MG_OFF = _off


V_ROWS = HEAD_DIM + 16


def _aug_lane(g):
    return HEAD_DIM if g == 0 else 0


def _inproj_kernel(*refs, prompt):
    x_ref, sh_ref, sc_ref, gn_ref, w_ref = refs[:5]
    ub = _modulated_norm(x_ref[...], gn_ref[...], sh_ref[...], sc_ref[...]).astype(BF16)

    def seg(name):
        lo, hi = _SEG[name]
        return _dot(ub, w_ref[:, lo:hi])

    if prompt:
        (q_ref, kck_ref, kcv_ref, kct_ref, kst_ref, kwt_ref, ks0_ref, ks1_ref, vs0_ref, vs1_ref, kwk_ref,
         vw0_ref, vw1_ref, g_ref, z_ref, cb_ref, gm_ref) = refs[5:]
        kc = seg("kc")
        kck_ref[...] = kc[:, 0:KV_HALF]
        kcv_ref[...] = kc[:, KV_HALF:KV_W]
        kct_ref[...] = kc.T
        tm = kc.shape[0]
        lane = lax.broadcasted_iota(jnp.int32, (tm, KV_HALF), 1)
        blk = lax.shift_right_logical(lax.broadcasted_iota(jnp.int32, (tm, KV_HALF), 0),
                                      int(math.log2(SEL_BLOCK))) & (Q_TILE // SEL_BLOCK - 1)
        ones_rows = jnp.where(lax.broadcasted_iota(jnp.int32, (V_ROWS - HEAD_DIM, tm), 0) == 0, 1.0, 0.0)
        for name, t_ref, k_refs, vt_refs in (("ks", kst_ref, (ks0_ref, ks1_ref), (vs0_ref, vs1_ref)),
                                             ("kw", kwt_ref, (kwk_ref,), (vw0_ref, vw1_ref))):
            kv = seg(name)
            kv_t = kv.T
            t_ref[...] = kv_t
            kk = kv[:, 0:KV_HALF]
            if len(k_refs) == 1:
                k_refs[0][...] = kk.astype(BF16)
            else:
                for g, k_ref in enumerate(k_refs):
                    own = (lane < HEAD_DIM) if g == 0 else (lane >= HEAD_DIM)
                    onehot = jnp.where(lane - _aug_lane(g) == blk, 1.0, 0.0)
                    k_ref[...] = jnp.where(own, kk, onehot).astype(BF16)
            for g, vt_ref in enumerate(vt_refs):
                vg = jnp.concatenate(
                    [kv_t[KV_HALF + g * HEAD_DIM:KV_HALF + (g + 1) * HEAD_DIM, :], ones_rows], axis=0)
                for c in range(vt_ref.shape[0]):
                    vt_ref[c] = vg[:, c * Q_TILE:(c + 1) * Q_TILE].astype(BF16)
    else:
        q_ref, kc_ref, ks_ref, kw_ref, ksb_ref, kwb_ref, g_ref, z_ref, cb_ref, gm_ref = refs[5:]
        kc_ref[...] = seg("kc")
        ks = seg("ks")
        ks_ref[...] = ks
        ksb_ref[...] = ks.astype(BF16)
        kw = seg("kw")
        kw_ref[...] = kw
        kwb_ref[...] = kw.astype(BF16)
    q_ref[...] = (seg("q") * (HEAD_DIM ** -0.5 * LOG2E)).astype(BF16)
    g_ref[...] = jax.nn.sigmoid(seg("gn"))
    z_ref[...] = seg("cc") * seg("ch")
    cb_ref[...] = seg("cb").astype(cb_ref.dtype)
    d2 = gm_ref.shape[1]
    half = d2 // 2
    for c in range(2):
        gm_ref[:, c * half:(c + 1) * half] = jax.nn.sigmoid(
            _dot(ub, w_ref[:, _MG_OFF + c * half:_MG_OFF + (c + 1) * half])).astype(gm_ref.dtype)


def _inproj(h, mod, gn_row, w_in_p, tm, tiles_per_batch, prompt):
    n, d = h.shape
    d_conv = _C_W
    mspecs, mops = _mod_specs(mod, (3, 4), tm, tiles_per_batch)
    rows = lambda w, dt: (pl.BlockSpec((tm, w), lambda i: (i, 0)), jax.ShapeDtypeStruct((n, w), dt))
    tail = [rows(_G_W, F32), rows(d_conv, F32), rows(d_conv, BF16), rows(2 * d, BF16)]
    if prompt:
        nb = n // (tm * tiles_per_batch)
        t = tm * tiles_per_batch
        cpt = tm // Q_TILE
        tr = (pl.BlockSpec((None, KV_W, tm), lambda i: (i // tiles_per_batch, 0, i % tiles_per_batch)),
              jax.ShapeDtypeStruct((nb, KV_W, t), F32))
        vt = (pl.BlockSpec((cpt, V_ROWS, Q_TILE), lambda i: (i, 0, 0)),
              jax.ShapeDtypeStruct((n // Q_TILE, V_ROWS, Q_TILE), BF16))
        kr = rows(KV_HALF, BF16)
        outs = [rows(_Q_W, BF16), rows(KV_HALF, F32), rows(KV_HALF, F32), tr, tr, tr,
                kr, kr, vt, vt, kr, vt, vt] + tail
    else:
        outs = [rows(_Q_W, BF16), rows(KV_W, F32), rows(KV_W, F32), rows(KV_W, F32),
                rows(KV_W, BF16), rows(KV_W, BF16)] + tail
    return pl.pallas_call(
        functools.partial(_inproj_kernel, prompt=prompt),
        grid=(n // tm,),
        in_specs=[pl.BlockSpec((tm, d), lambda i: (i, 0))] + mspecs + [
            pl.BlockSpec((1, d), lambda i: (0, 0)), _resident(w_in_p.shape)],
        out_specs=[o[0] for o in outs],
        out_shape=[o[1] for o in outs],
        compiler_params=_cparams(("arbitrary",)),
        name="inproj_prompt" if prompt else "inproj",
    )(h, *mops, gn_row, w_in_p)


def _mixout_kernel(*refs, halo, tiles_per_batch):
    if halo:
        (h_ref, o_ref, z_ref, zp_ref, cb_ref, gm_ref, g2_ref, wc_ref,
         wn_ref, wcv_ref, wo_ref, out_ref) = refs
        z = z_ref[...]
        tm = z.shape[0]
        first = (pl.program_id(0) % tiles_per_batch) == 0
        prev = jnp.where(first, 0.0, zp_ref[...])
        row = lax.broadcasted_iota(jnp.int32, z.shape, 0)
        zm1 = jnp.where(row == 0, prev[7:8, :], pltpu.roll(z, 1, 0))
        zm2 = jnp.where(row == 0, prev[6:7, :], jnp.where(row == 1, prev[7:8, :], pltpu.roll(z, 2, 0)))
    else:
        (h_ref, o_ref, z_ref, zm1_ref, zm2_ref, cb_ref, gm_ref, g2_ref, wc_ref,
         wn_ref, wcv_ref, wo_ref, out_ref) = refs
        z, zm1, zm2 = z_ref[...], zm1_ref[...], zm2_ref[...]
    conv = wc_ref[0:1, :] * zm2 + wc_ref[1:2, :] * zm1 + wc_ref[2:3, :] * z
    y = (cb_ref[...].astype(F32) * conv).astype(BF16)
    d = h_ref.shape[1]
    merged = gm_ref[:, 0:d].astype(F32) * _dot(o_ref[...], wn_ref[...]) \
        + gm_ref[:, d:2 * d].astype(F32) * _dot(y, wcv_ref[...])
    out_ref[...] = h_ref[...] + g2_ref[...] * _dot(merged.astype(BF16), wo_ref[...])


def _mixout(h, o_nsa, z, z_shift, cb, gm, mod, w_conv8, w_nsa_p, w_cv, w_o, tm, tiles_per_batch):
    n, d = h.shape
    dc = z.shape[1]
    halo = z_shift is None
    mspecs, mops = _mod_specs(mod, (5,), tm, tiles_per_batch)
    tok = lambda w: pl.BlockSpec((tm, w), lambda i: (i, 0))
    if halo:
        zspecs = [tok(dc), pl.BlockSpec((8, dc), lambda i: (jnp.maximum(i * (tm // 8) - 1, 0), 0))]
        zops = [z, z]
    else:
        zspecs = [tok(dc), tok(dc), tok(dc)]
        zops = [z, z_shift[0], z_shift[1]]
    return pl.pallas_call(
        functools.partial(_mixout_kernel, halo=halo, tiles_per_batch=tiles_per_batch),
        grid=(n // tm,),
        in_specs=[tok(d), tok(o_nsa.shape[1])] + zspecs + [tok(dc), tok(2 * d)] + mspecs + [
            pl.BlockSpec((8, dc), lambda i: (0, 0)),
            _resident(w_nsa_p.shape), _resident(w_cv.shape), _resident(w_o.shape)],
        out_specs=tok(d),
        out_shape=jax.ShapeDtypeStruct((n, d), F32),
        compiler_params=_cparams(("arbitrary",)),
        name="mixout",
    )(h, o_nsa, *zops, cb, gm, *mops, w_conv8, w_nsa_p, w_cv, w_o)


CHUNK_W = CMP_STRIDE * KV_W
CMP_OUT_W = 2 * H_KV * CMP_HIDDEN


def _gelu_tanh(x):
    return x * (0.5 * (1.0 + jnp.tanh(math.sqrt(2.0 / math.pi) * (x + 0.044715 * (x * x * x)))))


def _chunk_partials(tile, wc_ref, after_step=None):
    accs = []
    for c in range(2):
        acc = None
        for lp in range(CMP_STRIDE // 2):
            x2 = jnp.concatenate([tile(2 * lp, c), tile(2 * lp + 1, c)], axis=1)
            part = _dot(x2, wc_ref[c, lp])
            acc = part if acc is None else acc + part
            if after_step is not None:
                after_step(c, lp)
        accs.append(acc)
    hw = CMP_OUT_W // 2
    return jnp.concatenate([accs[0][:, 0:hw], accs[1][:, 0:hw], accs[0][:, hw:], accs[1][:, hw:]], axis=1)


def _compress_tail(p, wc_ref, pe_ref, w2_ref):
    n_ch = p.shape[0]
    pb = _chunk_partials(
        lambda l, c: pe_ref[:, l * KV_W + c * KV_HALF:l * KV_W + (c + 1) * KV_HALF].astype(BF16), wc_ref)
    pre = (pb[0:1, 0:CMP_OUT_W] + pb[1:2, CMP_OUT_W:]) + p[:, 0:CMP_OUT_W] \
        + pltpu.roll(p[:, CMP_OUT_W:], n_ch - 1, 0)
    out = _dot(_gelu_tanh(pre).astype(BF16), w2_ref[...])
    row = lax.broadcasted_iota(jnp.int32, out.shape, 0)
    return jnp.where(row < n_ch - 1, out, 0.0)


def _cmp_prompt_kernel(xk_ref, xv_ref, wc_ref, pe_ref, w2_ref, o_ref):
    xs = (xk_ref, xv_ref)
    n_ch = xk_ref.shape[0] // CMP_STRIDE
    p = _chunk_partials(lambda l, c: xs[c][pl.ds(l, n_ch, stride=CMP_STRIDE), :].astype(BF16), wc_ref)
    out = _compress_tail(p, wc_ref, pe_ref, w2_ref).astype(BF16)
    o_ref[0:n_ch, :] = out
    o_ref[n_ch:2 * n_ch, :] = out


def _cmp_prompt(rows_k, rows_v, t, wbig, pe8, w2bd):
    nb = rows_k.shape[0] // t
    n_ch = t // CMP_STRIDE
    half = pl.BlockSpec((t, KV_HALF), lambda b: (b, 0))
    return pl.pallas_call(
        _cmp_prompt_kernel,
        grid=(nb,),
        in_specs=[half, half, _resident(wbig.shape), _resident(pe8.shape), _resident(w2bd.shape)],
        out_specs=pl.BlockSpec((None, 2 * n_ch, KV_W), lambda b: (b, 0, 0)),
        out_shape=jax.ShapeDtypeStruct((nb, 2 * n_ch, KV_W), BF16),
        compiler_params=_cparams(("arbitrary",)),
        name="cmp_prompt",
    )(rows_k, rows_v, wbig, pe8, w2bd)


def _page_copies(cache_ref, pt_ref, b, buf, sem, slot, n_pages, page):
    return [pltpu.make_async_copy(cache_ref.at[pt_ref[b, p]],
                                  buf.at[slot, :, pl.ds(p * page, page)],
                                  sem.at[slot]) for p in range(n_pages)]


def _gather_pages(cache_ref, pt_ref, buf, sem, n_pages, page):
    b = pl.program_id(0)
    nb = pl.num_programs(0)
    slot = b % 2

    @pl.when(b == 0)
    def _():
        for c in _page_copies(cache_ref, pt_ref, 0, buf, sem, 0, n_pages, page):
            c.start()

    @pl.when(b + 1 < nb)
    def _():
        for c in _page_copies(cache_ref, pt_ref, b + 1, buf, sem, 1 - slot, n_pages, page):
            c.start()

    for c in _page_copies(cache_ref, pt_ref, b, buf, sem, slot, n_pages, page):
        c.wait()
    return slot


def _cmp_sample_kernel(pt_ref, cache_ref, wc_ref, pe_ref, w2_ref, o_ref, buf, xk_ref, xv_ref, sem,
                       *, n_pages, page):
    slot = _gather_pages(cache_ref, pt_ref, buf, sem, n_pages, page)
    xs = (xk_ref, xv_ref)
    past = xk_ref.shape[0]
    n_ch = past // CMP_STRIDE
    n_slabs = CMP_STRIDE // 2
    slab = past // n_slabs

    def transpose_slab(c, j):
        xs[c][j * slab:(j + 1) * slab, :] = buf[slot, c * KV_HALF:(c + 1) * KV_HALF, j * slab:(j + 1) * slab].T

    for j in range(n_slabs):
        transpose_slab(0, j)

    def v_half_transposes(c, lp):
        if c == 0:
            transpose_slab(1, lp)

    p = _chunk_partials(lambda l, c: xs[c][pl.ds(l, n_ch, stride=CMP_STRIDE), :].astype(BF16), wc_ref,
                        v_half_transposes)
    o_ref[...] = _compress_tail(p, wc_ref, pe_ref, w2_ref).astype(BF16)


def _cmp_sample(page_table, cache_t, wbig, pe8, w2bd):
    nb, n_pages = page_table.shape
    page = cache_t.shape[2]
    past = n_pages * page
    n_ch = past // CMP_STRIDE
    return pl.pallas_call(
        functools.partial(_cmp_sample_kernel, n_pages=n_pages, page=page),
        grid_spec=pltpu.PrefetchScalarGridSpec(
            num_scalar_prefetch=1,
            grid=(nb,),
            in_specs=[pl.BlockSpec(memory_space=pl.ANY),
                      _resident_sp(wbig.shape), _resident_sp(pe8.shape), _resident_sp(w2bd.shape)],
            out_specs=pl.BlockSpec((None, n_ch, KV_W), lambda b, pt: (b, 0, 0)),
            scratch_shapes=[pltpu.VMEM((2, KV_W, past), F32), pltpu.VMEM((past, KV_HALF), F32),
                            pltpu.VMEM((past, KV_HALF), F32), pltpu.SemaphoreType.DMA((2,))]),
        out_shape=jax.ShapeDtypeStruct((nb, n_ch, KV_W), BF16),
        compiler_params=_cparams(("arbitrary",)),
        name="cmp_sample",
    )(page_table, cache_t, wbig, pe8, w2bd)


def _resident_sp(shape):
    nd = len(shape)
    return pl.BlockSpec(shape, lambda *_: (0,) * nd, pipeline_mode=pl.Buffered(1))


def _bias_values(d, table):
    acc = jnp.zeros(d.shape, F32) + table(0)
    for b in range(1, N_BUCKETS):
        acc = jnp.where(d >= BUCKET_THR[b], table(b), acc)
    return jnp.where(d >= 0, (acc - table(N_BUCKETS - 1)) * LOG2E, NEG)


CMP_BACK = 16


def _bias_prompt_kernel(tbl_ref, bc_ref, t0_ref, t1_ref):
    h = pl.program_id(0)
    table = lambda b: tbl_ref[b, h]
    ncp, tq = bc_ref.shape
    kj = lax.broadcasted_iota(jnp.int32, (tq, tq), 0)
    qi = lax.broadcasted_iota(jnp.int32, (tq, tq), 1)
    t0_ref[...] = _bias_values(qi - kj, table)
    t1_ref[...] = _bias_values(qi - kj + tq, table)
    c = lax.broadcasted_iota(jnp.int32, (ncp, tq), 0)
    qi = lax.broadcasted_iota(jnp.int32, (ncp, tq), 1)
    d = qi - CMP_STRIDE * c + (CMP_STRIDE * CMP_BACK - (CMP_BLOCK - 1))
    bc_ref[...] = jnp.where(c < CMP_NEAR, _bias_values(d, table), 0.0)


def _bias_prompt(rel_bias, ncp):
    tq = Q_TILE
    return pl.pallas_call(
        _bias_prompt_kernel,
        grid=(N_HEADS,),
        in_specs=[pl.BlockSpec(memory_space=pltpu.SMEM)],
        out_specs=[pl.BlockSpec((None, ncp, tq), lambda h: (h, 0, 0)),
                   pl.BlockSpec((None, tq, tq), lambda h: (h, 0, 0)),
                   pl.BlockSpec((None, tq, tq), lambda h: (h, 0, 0))],
        out_shape=[jax.ShapeDtypeStruct((N_HEADS, ncp, tq), F32),
                   jax.ShapeDtypeStruct((N_HEADS, tq, tq), F32),
                   jax.ShapeDtypeStruct((N_HEADS, tq, tq), F32)],
        compiler_params=_cparams(("arbitrary",)),
        name="bias_prompt",
    )(rel_bias)


def _bias_sample_kernel(tbl_ref, bcs_ref, bsl_ref, bnew_ref, bw_ref, *, past, s_len, n_c):
    table = lambda b: tbl_ref[:, b:b + 1]

    def qpos(shape):
        return past + lax.rem(lax.broadcasted_iota(jnp.int32, shape, 0), s_len)

    def lane(shape):
        return lax.broadcasted_iota(jnp.int32, shape, 1)

    sh = bcs_ref.shape
    n = lane(sh)
    bcs_ref[...] = jnp.where(n < n_c, _bias_values(qpos(sh) - (CMP_STRIDE * n + CMP_BLOCK - 1), table), NEG)
    sh = bsl_ref.shape
    bsl_ref[...] = _bias_values(qpos(sh) - (past - sh[1] + lane(sh)), table)
    sh = bnew_ref.shape
    bnew_ref[...] = jnp.where(lane(sh) < s_len, _bias_values(qpos(sh) - (past + lane(sh)), table), NEG)
    sh = bw_ref.shape
    d = qpos(sh) - (past - sh[1] + lane(sh))
    bw_ref[...] = jnp.where(d < WINDOW, _bias_values(d, table), NEG)


def _bias_sample(tbl_rows, past, s_len, ncs, ck, wb):
    rows = tbl_rows.shape[0]
    widths = (ncs, ck, LANES, wb)
    return pl.pallas_call(
        functools.partial(_bias_sample_kernel, past=past, s_len=s_len, n_c=(past + s_len - CMP_BLOCK) // CMP_STRIDE + 1),
        out_shape=[jax.ShapeDtypeStruct((rows, w), F32) for w in widths],
        name="bias_sample",
    )(tbl_rows)


MASK_FLOOR = -1e8
M_INIT = -1e30
TAKEN = -3e38
SCORE_AHEAD = 3
FAR_PER_ITER = 3
VALUE_BEHIND = 2


def _sel_weights(v):
    return jnp.where((v >= 0) & (v <= 2), 2.0, jnp.where((v == -1) | (v == 3), 1.0, 0.0))


def _attn_prompt_kernel(q_ref, gate_ref, ks0_ref, ks1_ref, vs0_ref, vs1_ref, kwk_ref, vw0_ref, vw1_ref,
                        kc2_ref, bc_ref, t0_ref, t1_ref, o_ref,
                        qm_ref, selr_ref, oc_ref, ms_ref, as_ref, mw_ref, aw_ref, *, ncp, nsel):
    tq = Q_TILE
    i = pl.program_id(1)
    q0 = i * tq
    lane = lax.broadcasted_iota(jnp.int32, (tq, LANES), 1)
    heads = [(g, r) for g in range(H_KV) for r in range(GROUP)]

    for g, r in heads:
        gmask = (lane < HEAD_DIM) if g == 0 else (lane >= HEAD_DIM)
        qm_ref[g * GROUP + r] = jnp.where(gmask, q_ref[:, r * LANES:(r + 1) * LANES], jnp.zeros((tq, LANES), BF16))

    n0 = (tq // CMP_STRIDE) * i - CMP_BACK
    ws = pl.multiple_of(lax.rem(n0 + ncp, ncp), 8)
    cl = lax.broadcasted_iota(jnp.int32, (ncp, tq), 0)
    dm = jnp.where((cl >= CMP_NEAR) & (cl < ncp - n0), NEG, 0.0) + jnp.where(cl < -n0, NEG, 0.0)
    jrow = lax.broadcasted_iota(jnp.int32, (nsel, ncp), 0)
    nn = lax.broadcasted_iota(jnp.int32, (nsel, ncp), 1) + n0
    nn = jnp.where(nn < 0, nn + ncp, nn)
    nn = jnp.where(nn >= ncp, nn - ncp, nn)
    mit = _sel_weights(nn - (SEL_BLOCK // CMP_STRIDE) * jrow).astype(BF16)

    jr = lax.broadcasted_iota(jnp.int32, (nsel, tq), 0)
    qpos = q0 + lax.broadcasted_iota(jnp.int32, (nsel, tq), 1)
    cur = lax.shift_right_logical(qpos, int(math.log2(SEL_BLOCK)))
    invalid_pen = jnp.where(jr * SEL_BLOCK <= qpos, 0.0, NEG)
    bonus = jnp.where((jr == 0) | (jr == cur) | (jr == cur - 1), FORCE_BONUS, 0.0)
    pj = lax.broadcasted_iota(jnp.int32, (LANES, LANES), 0)
    pl_ = lax.broadcasted_iota(jnp.int32, (LANES, LANES), 1)
    kj = lax.broadcasted_iota(jnp.int32, (tq, tq), 0)
    qi = lax.broadcasted_iota(jnp.int32, (tq, tq), 1)
    wmask = jnp.where(kj > qi, 0.0, NEG)
    blocks_per_chunk = tq // SEL_BLOCK

    kc = kc2_ref[pl.ds(ws, ncp), 0:KV_HALF]
    vct = kc2_ref[pl.ds(ws, ncp), KV_HALF:KV_W].astype(F32).T.astype(BF16)
    def cmp_scores(h):
        return _dot_nt(kc, qm_ref[h]) + bc_ref[h] + dm

    ahead = {h: cmp_scores(h) for h in range(2)}
    psum = [None] * H_KV
    for h in range(N_HEADS):
        s = ahead.pop(h)
        if h + 2 < N_HEADS:
            ahead[h + 2] = cmp_scores(h + 2)
        m = jnp.maximum(jnp.max(s, axis=0, keepdims=True), MASK_FLOOR)
        e = jnp.exp2(s - m)
        p = e * (1.0 / jnp.maximum(jnp.sum(e, axis=0, keepdims=True), 1e-30))
        oc_ref[h] = _dot(vct, p.astype(BF16))
        psum[h // GROUP] = p if psum[h // GROUP] is None else psum[h // GROUP] + p

    scores = []
    for g in range(H_KV):
        h1, h2, h3 = _split3(psum[g])
        pslc = (_dot(mit, h1) + _dot(mit, h2)) + _dot(mit, h3)
        scores.append(jnp.where(invalid_pen < 0.0, NEG, pslc + bonus))

    def take_max(_, carry):
        out = []
        for work, sel_acc in carry:
            mx = jnp.max(work, axis=0, keepdims=True)
            first = jnp.min(jnp.where(work == mx, jr, nsel), axis=0, keepdims=True)
            hit = jr == first
            out.append((jnp.where(hit, TAKEN, work), jnp.where(hit, 1.0, sel_acc)))
        return tuple(out)

    picked = lax.fori_loop(0, min(TOP_N, nsel), take_max,
                           tuple((sc, jnp.zeros((nsel, tq), F32)) for sc in scores))
    for g in range(H_KV):
        sel_t = picked[g][1]
        if nsel < LANES:
            sel_t = jnp.concatenate([sel_t, jnp.zeros((LANES - nsel, tq), F32)], axis=0)
        selr_ref[g] = sel_t.T.astype(BF16)

    def masked_queries(t, g):
        off = pl_ - _aug_lane(g)
        place = jnp.where((pj - t * blocks_per_chunk == off) & (off >= 0) & (off < blocks_per_chunk), 1.0, 0.0)
        picked = _dot(selr_ref[g], place.astype(BF16))
        auglane = (lane >= _aug_lane(g)) & (lane < _aug_lane(g) + blocks_per_chunk)
        pen = jnp.where(auglane, (picked - 1.0) * (-NEG), 0.0).astype(BF16)
        return [qm_ref[g * GROUP + r] + pen for r in range(GROUP)]

    def run(chunks):
        items = []
        for k_of, vt_of, queries, extra, m_ref, a_ref in chunks:
            cache = {}
            items += [(h, k_of, vt_of, queries, extra, m_ref, a_ref, cache) for h in range(N_HEADS)]

        def kv(item):
            g, cache = item[0] // GROUP, item[7]
            if g not in cache:
                cache[g] = (item[1](g), item[2](g))
            return cache[g]

        def scores(item):
            h, queries, extra = item[0], item[3], item[4]
            s = _dot_nt(kv(item)[0], queries(h))
            return s if extra is None else s + extra(h)

        def softmax(item, s):
            h, m_ref = item[0], item[5]
            m_prev = m_ref[h]
            m_new = jnp.maximum(m_prev, jnp.max(s, axis=0, keepdims=True))
            m_ref[h] = m_new
            return jnp.exp2(s - m_new).astype(BF16), jnp.exp2(m_prev - m_new)

        def accumulate(item, p, alpha):
            h, a_ref = item[0], item[6]
            a_ref[h] = alpha * a_ref[h] + _dot(kv(item)[1], p)

        n = len(items)
        s = {j: scores(items[j]) for j in range(SCORE_AHEAD)}
        pa = {}
        for j in range(n):
            pa[j] = softmax(items[j], s.pop(j))
            if j + SCORE_AHEAD < n:
                s[j + SCORE_AHEAD] = scores(items[j + SCORE_AHEAD])
            if j >= VALUE_BEHIND:
                accumulate(items[j - VALUE_BEHIND], *pa.pop(j - VALUE_BEHIND))
        for j in range(n - VALUE_BEHIND, n):
            accumulate(items[j], *pa.pop(j))

    for m_ref, a_ref in ((ms_ref, as_ref), (mw_ref, aw_ref)):
        m_ref[...] = jnp.full(m_ref.shape, M_INIT, F32)
        a_ref[...] = jnp.zeros(a_ref.shape, F32)

    def sel_chunk(t, extra):
        sl = pl.ds(pl.multiple_of(t * tq, tq), tq)
        qs = [masked_queries(t, g) for g in range(H_KV)]
        return (lambda g: (ks0_ref, ks1_ref)[g][sl, :], lambda g: (vs0_ref, vs1_ref)[g][t],
                lambda h: qs[h // GROUP][h % GROUP], extra, ms_ref, as_ref)

    def win_chunk(t, extra):
        sl = pl.ds(pl.multiple_of(t * tq, tq), tq)
        return (lambda g: kwk_ref[sl, :], lambda g: (vw0_ref, vw1_ref)[g][t],
                lambda h: qm_ref[h], extra, mw_ref, aw_ref)

    n_far = jnp.maximum(i - 1, 0)
    n_iter = n_far // FAR_PER_ITER

    def far_group(u, carry):
        run([sel_chunk(FAR_PER_ITER * u + j, None) for j in range(FAR_PER_ITER)])
        return carry

    lax.fori_loop(0, n_iter, far_group, 0)
    for rem in range(1, FAR_PER_ITER):
        @pl.when(n_far - FAR_PER_ITER * n_iter == rem)
        def _(rem=rem):
            run([sel_chunk(FAR_PER_ITER * n_iter + j, None) for j in range(rem)])

    gone1 = jnp.where(i >= 1, 0.0, NEG)
    gone2 = jnp.where(i >= 2, 0.0, NEG)
    wmask2 = wmask + gone2
    run([sel_chunk(jnp.maximum(i - 1, 0), lambda h: t1_ref[h] + gone1),
         sel_chunk(i, lambda h: t0_ref[h]),
         win_chunk(jnp.maximum(i - 2, 0), lambda h: wmask2),
         win_chunk(jnp.maximum(i - 1, 0), lambda h: t1_ref[h] + gone1),
         win_chunk(i, lambda h: t0_ref[h])])

    gates_t = gate_ref[...].T
    for r in range(GROUP):
        per_g = []
        for g in range(H_KV):
            h = g * GROUP + r
            a_s, a_w = as_ref[h], aw_ref[h]
            o_sel = a_s[0:HEAD_DIM] * (1.0 / jnp.maximum(a_s[HEAD_DIM:HEAD_DIM + 1], 1e-30))
            o_win = a_w[0:HEAD_DIM] * (1.0 / jnp.maximum(a_w[HEAD_DIM:HEAD_DIM + 1], 1e-30))
            per_g.append(gates_t[h:h + 1, :] * oc_ref[h, g * HEAD_DIM:(g + 1) * HEAD_DIM, :]
                         + gates_t[N_HEADS + h:N_HEADS + h + 1, :] * o_sel
                         + gates_t[2 * N_HEADS + h:2 * N_HEADS + h + 1, :] * o_win)
        o_ref[:, r * LANES:(r + 1) * LANES] = jnp.concatenate(per_g, axis=0).T.astype(BF16)


def _attn_prompt(q, gates, ks_g, vs_g, kw_k, vw_g, kc2, bias_tiles, nb, t):
    tq = Q_TILE
    assert t % tq == 0 and WINDOW == 2 * tq
    nt = t // tq
    ncp = t // CMP_STRIDE
    nsel = t // SEL_BLOCK
    assert nsel <= LANES and ncp % LANES == 0
    bc, t0, t1 = bias_tiles
    tok = lambda w: pl.BlockSpec((tq, w), lambda b, i: (b * nt + i, 0))
    k_rows = pl.BlockSpec((None, t, KV_HALF), lambda b, i: (b, 0, 0))
    v_cols = pl.BlockSpec((None, nt, V_ROWS, tq), lambda b, i: (b, 0, 0, 0))
    state = [pltpu.VMEM((N_HEADS, 1, tq), F32), pltpu.VMEM((N_HEADS, V_ROWS, tq), F32)]
    kr = lambda a: a.reshape(nb, t, KV_HALF)
    vc = lambda a: a.reshape(nb, nt, V_ROWS, tq)
    return pl.pallas_call(
        functools.partial(_attn_prompt_kernel, ncp=ncp, nsel=nsel),
        grid=(nb, nt),
        in_specs=[tok(q.shape[1]), tok(LANES), k_rows, k_rows, v_cols, v_cols, k_rows, v_cols, v_cols,
                  pl.BlockSpec((None, 2 * ncp, KV_W), lambda b, i: (b, 0, 0)),
                  _resident(bc.shape), _resident(t0.shape), _resident(t1.shape)],
        out_specs=tok(q.shape[1]),
        out_shape=jax.ShapeDtypeStruct(q.shape, BF16),
        scratch_shapes=[pltpu.VMEM((N_HEADS, tq, LANES), BF16), pltpu.VMEM((H_KV, tq, LANES), BF16),
                        pltpu.VMEM((N_HEADS, LANES, tq), F32)] + state + state,
        compiler_params=_cparams(("arbitrary", "arbitrary")),
        name="attn_prompt",
    )(q, gates, kr(ks_g[0]), kr(ks_g[1]), vc(vs_g[0]), vc(vs_g[1]), kr(kw_k), vc(vw_g[0]), vc(vw_g[1]),
      kc2, bc, t0, t1)


SAMPLE_CK = 1024


def _softmax_part(s, v, transposed):
    m = jnp.max(s, axis=-1, keepdims=True)
    p = jnp.exp2(s - m)
    pv = _dot_nt(p.astype(BF16), v) if transposed else _dot(p.astype(BF16), v)
    return m, jnp.sum(p, axis=-1, keepdims=True), pv


def _merge_parts(parts):
    m = parts[0][0]
    for mt, _, _ in parts[1:]:
        m = jnp.maximum(m, mt)
    l, acc = None, None
    for mt, lt, at in parts:
        w = jnp.exp2(mt - m)
        l = w * lt if l is None else l + w * lt
        acc = w * at if acc is None else acc + w * at
    return acc / jnp.maximum(l, 1e-30)


def _attn_sample_kernel(pt_ref, q_ref, g_ref, knew_ref, wnew_ref, wnewt_ref, win_ref, kc_ref,
                        bcs_ref, bsl_ref, bnew_ref, bw_ref, msel_ref, cache_ref, o_ref, wout_ref, buf, sem,
                        *, n_pages, page, past, s_len, nsel):
    slot = _gather_pages(cache_ref, pt_ref, buf, sem, n_pages, page)

    wb = win_ref.shape[1]
    shifted = pltpu.roll(win_ref[...], wb - s_len, 1)
    fresh = pltpu.roll(wnewt_ref[...], LANES - s_len, 1)
    tail_lane = lax.broadcasted_iota(jnp.int32, fresh.shape, 1) >= LANES - s_len
    wout_ref[:, 0:wb - LANES] = shifted[:, 0:wb - LANES]
    wout_ref[:, wb - LANES:wb] = jnp.where(tail_lane, fresh, shifted[:, wb - LANES:wb])
    q = q_ref[...]
    rows = q.shape[0]
    gs = rows // GROUP
    nselp = msel_ref.shape[1]

    s = _dot_nt(q, kc_ref[:, 0:KV_HALF]) + bcs_ref[...]
    m = jnp.maximum(jnp.max(s, axis=-1, keepdims=True), MASK_FLOOR)
    e = jnp.exp2(s - m)
    p = e / jnp.maximum(jnp.sum(e, axis=-1, keepdims=True), 1e-30)
    o_cmp = _dot(p.astype(BF16), kc_ref[:, KV_HALF:KV_W])
    psum = p[0:gs]
    for r in range(1, GROUP):
        psum = psum + p[r * gs:(r + 1) * gs]

    h1, h2, h3 = _split3(psum)
    msel = msel_ref[...]
    pslc = (_dot(h1, msel) + _dot(h2, msel)) + _dot(h3, msel)
    j = lax.broadcasted_iota(jnp.int32, (gs, nselp), 1)
    qpos = past + lax.rem(lax.broadcasted_iota(jnp.int32, (gs, nselp), 0), s_len)
    cur = lax.shift_right_logical(qpos, int(math.log2(SEL_BLOCK)))
    valid = j * SEL_BLOCK <= qpos
    forced = (j == 0) | (j == cur) | (j == cur - 1)
    score = jnp.where(valid, pslc + jnp.where(forced, FORCE_BONUS, 0.0), NEG)
    score = jnp.where(j < nsel, score, -3e38)
    score_t = jnp.concatenate([score, jnp.zeros((LANES - gs, nselp), F32)], axis=0).T
    jp = lax.broadcasted_iota(jnp.int32, (nselp, nselp), 0)
    jj = lax.broadcasted_iota(jnp.int32, (nselp, nselp), 1)
    sel_rows = []
    for r in range(gs):
        col = score_t[:, r:r + 1]
        row = score[r:r + 1, :]
        beats = jnp.where(jj > jp, jnp.where(col >= row, 1.0, 0.0), jnp.where(col > row, 1.0, 0.0))
        rank = jnp.sum(beats, axis=0, keepdims=True)
        sel_rows.append(jnp.where(rank < min(TOP_N, nsel), 1.0, 0.0))
    sel8 = jnp.concatenate(sel_rows, axis=0)
    sel = jnp.concatenate([sel8] * GROUP, axis=0).astype(BF16)

    ck = bsl_ref.shape[1]
    n_chunks = past // ck
    erow = lax.broadcasted_iota(jnp.int32, (nselp, ck), 0)
    eblk = lax.shift_right_logical(lax.broadcasted_iota(jnp.int32, (nselp, ck), 1), int(math.log2(SEL_BLOCK)))
    scs = []
    for t in range(n_chunks):
        kt = buf[slot, 0:KV_HALF, t * ck:(t + 1) * ck].astype(BF16)
        e_t = jnp.where(erow == t * (ck // SEL_BLOCK) + eblk, 1.0, 0.0).astype(BF16)
        sc = _dot(q, kt) + (_dot(sel, e_t) - 1.0) * (-NEG)
        scs.append(sc + bsl_ref[...] if t == n_chunks - 1 else sc)
    knew = knew_ref[...]
    wnew = wnew_ref[...]
    s_new = _dot_nt(q, knew[:, 0:KV_HALF]) + bnew_ref[...]
    s_win = _dot(q, win_ref[0:KV_HALF, :].astype(BF16)) + bw_ref[...]
    s_wnew = _dot_nt(q, wnew[:, 0:KV_HALF]) + bnew_ref[...]
    parts = [_softmax_part(scs[t], buf[slot, KV_HALF:KV_W, t * ck:(t + 1) * ck].astype(BF16), True)
             for t in range(n_chunks)]
    parts.append(_softmax_part(s_new, knew[:, KV_HALF:KV_W], False))
    o_sel = _merge_parts(parts)

    o_win = _merge_parts([_softmax_part(s_win, win_ref[KV_HALF:KV_W, :].astype(BF16), True),
                          _softmax_part(s_wnew, wnew[:, KV_HALF:KV_W], False)])

    g = g_ref[...]
    o_ref[...] = g[:, 0:1] * o_cmp + g[:, 1:2] * o_sel + g[:, 2:3] * o_win


def _attn_sample(page_table, q32, g32, knew, wnew, wnew_t, win_t, kvc, tiles, msel, cache_t, past, s_len):
    nb, n_pages = page_table.shape
    page = cache_t.shape[2]
    rows = q32.shape[1]
    wb = win_t.shape[2]
    nsel = -(-(past + s_len) // SEL_BLOCK)
    bcs, bsl, bnew, bw = tiles
    per_b = lambda r, w: pl.BlockSpec((None, r, w), lambda b, pt: (b, 0, 0))
    return pl.pallas_call(
        functools.partial(_attn_sample_kernel, n_pages=n_pages, page=page, past=past, s_len=s_len, nsel=nsel),
        grid_spec=pltpu.PrefetchScalarGridSpec(
            num_scalar_prefetch=1,
            grid=(nb,),
            in_specs=[per_b(rows, LANES), per_b(rows, LANES), per_b(LANES, KV_W), per_b(LANES, KV_W),
                      per_b(KV_W, LANES), per_b(KV_W, wb), per_b(kvc.shape[1], KV_W),
                      _resident_sp(bcs.shape), _resident_sp(bsl.shape), _resident_sp(bnew.shape),
                      _resident_sp(bw.shape), _resident_sp(msel.shape),
                      pl.BlockSpec(memory_space=pl.ANY)],
            out_specs=[per_b(rows, LANES), per_b(KV_W, wb)],
            scratch_shapes=[pltpu.VMEM((2, KV_W, past), F32), pltpu.SemaphoreType.DMA((2,))]),
        out_shape=[jax.ShapeDtypeStruct((nb, rows, LANES), F32), jax.ShapeDtypeStruct((nb, KV_W, wb), F32)],
        compiler_params=_cparams(("arbitrary",)),
        name="attn_sample",
    )(page_table, q32, g32, knew, wnew, wnew_t, win_t, kvc, bcs, bsl, bnew, bw, msel, cache_t)


def _prep_weights(w_in, w_cmp1, w_cmp2, pe_cmp, w_conv, w_nsa_out):
    d = w_in.shape[0]
    sizes = (N_HEADS * HEAD_DIM, KV_W, KV_W, KV_W, 3 * N_HEADS, _C_W, _C_W, _C_W, 2 * d)
    offs = [0]
    for sz in sizes:
        offs.append(offs[-1] + sz)
    part = lambda k: w_in[:, offs[k]:offs[k + 1]]
    qp = part(0).reshape(d, H_KV, GROUP, HEAD_DIM).transpose(0, 2, 1, 3).reshape(d, N_HEADS * HEAD_DIM)
    gn = jnp.pad(part(4), ((0, 0), (0, _G_W - 3 * N_HEADS)))
    w_in_p = jnp.concatenate([qp, part(1), part(2), part(3), gn, part(5), part(6), part(7), part(8)],
                             axis=1).astype(BF16)
    w_nsa_p = w_nsa_out.reshape(H_KV, GROUP, HEAD_DIM, -1).transpose(1, 0, 2, 3).reshape(
        N_HEADS * HEAD_DIM, -1).astype(BF16)
    r = CMP_BLOCK // CMP_STRIDE
    w1r = w_cmp1.reshape(2, r, CMP_STRIDE // 2, 2, HEAD_DIM, CMP_HIDDEN)
    w1t = jnp.transpose(w1r, (0, 2, 3, 4, 1, 5))
    w1e = w1t[:, :, :, :, :, None, :].astype(BF16)
    z1 = jnp.zeros_like(w1e)
    wb = jnp.stack([jnp.concatenate([w1e if gp == g else z1 for gp in range(H_KV)], axis=5)
                    for g in range(H_KV)], axis=3)
    wbig = wb.reshape(2, CMP_STRIDE // 2, 2 * KV_HALF, CMP_OUT_W)
    z2 = jnp.zeros((CMP_HIDDEN, HEAD_DIM), F32)
    w2bd = jnp.concatenate(
        [jnp.concatenate([w_cmp2[c] if (cp, gp) == (c, g) else z2 for cp in range(2) for gp in range(H_KV)], axis=1)
         for c in range(2) for g in range(H_KV)], axis=0).astype(BF16)
    pe_r = pe_cmp.reshape(2, r, CMP_STRIDE, HEAD_DIM).transpose(1, 2, 0, 3)
    pe_rows = jnp.broadcast_to(pe_r[:, :, :, None, :], (r, CMP_STRIDE, 2, H_KV, HEAD_DIM)).reshape(r, CHUNK_W)
    pe8 = jnp.pad(pe_rows, ((0, 8 - r), (0, 0)))
    w_conv8 = jnp.pad(w_conv, ((0, 8 - CONV_WIDTH), (0, 0)))
    return w_in_p, w_nsa_p, wbig, w2bd, pe8, w_conv8


def _sel_matrix(n_c, nselp):
    n = jnp.arange(n_c + 1)[:, None]
    j = jnp.arange(nselp)[None, :]
    return _sel_weights(n - (SEL_BLOCK // CMP_STRIDE) * j).astype(BF16)


def kernel(x_prompt, x_sample, cache_kv_cmp, cache_kv_sel, state_kv_win, state_conv, page_table,
           c_prompt, c_sample, w_ada, b_ada, g_norm, w_ffn1_gu, w_ffn1_down, w_ffn2_gu, w_ffn2_down,
           w_in, w_cmp1, w_cmp2, pe_cmp, w_conv, w_nsa_out, w_conv_out, w_out, rel_bias, g_final):
    assert w_ada.shape[0] == 1, "single-layer trunk"
    nbp, t, d = x_prompt.shape
    nbs, s_len, _ = x_sample.shape
    n_pages = page_table.shape[1]
    page = cache_kv_cmp.shape[2]
    past = n_pages * page
    n_phys = cache_kv_cmp.shape[1]
    wb = state_kv_win.shape[2]
    assert wb == WINDOW and past % SAMPLE_CK == 0

    w_in_p, w_nsa_p, wbig, w2bd, pe8, w_conv8 = _prep_weights(
        w_in[0], w_cmp1[0], w_cmp2[0], pe_cmp[0], w_conv[0], w_nsa_out[0])
    wgu1, wd1 = w_ffn1_gu[0].astype(BF16), w_ffn1_down[0].astype(BF16)
    wgu2, wd2 = w_ffn2_gu[0].astype(BF16), w_ffn2_down[0].astype(BF16)
    w_cv, w_o = w_conv_out[0].astype(BF16), w_out[0].astype(BF16)
    gn = [g_norm[0][k:k + 1] for k in range(N_SUB)]

    n_c_rows = nbp + nbs
    c_all = jnp.pad(jnp.concatenate([c_prompt, c_sample], axis=0), ((0, (-n_c_rows) % 8), (0, 0)))
    mod_all = _ada(c_all, w_ada[0], b_ada[0])
    mod_p = mod_all[:nbp].reshape(nbp * 3 * N_SUB, 1, d)
    mod_s = jnp.transpose(jnp.repeat(mod_all[nbp:n_c_rows].reshape(nbs, 3 * N_SUB, d), s_len, axis=0), (1, 0, 2))

    tm = 512 if t % 512 == 0 else t
    tpb = t // tm
    xp = x_prompt.reshape(nbp * t, d)
    h1 = _ffn(xp, mod_p, 0, gn[0], wgu1, wd1, tm, tpb)
    (q, kc_k, kc_v, kvc_t, kvs_t, kvw_t, ks0, ks1, vs0, vs1, kw_k, vw0, vw1, gates, z, cb, gm) = _inproj(
        h1, mod_p, gn[1], w_in_p, tm, tpb, True)
    kc2 = _cmp_prompt(kc_k, kc_v, t, wbig, pe8, w2bd)
    tiles_p = _bias_prompt(rel_bias, t // CMP_STRIDE)
    o_nsa = _attn_prompt(q, gates, (ks0, ks1), (vs0, vs1), kw_k, (vw0, vw1), kc2, tiles_p, nbp, t)
    h2 = _mixout(h1, o_nsa, z, None, cb, gm, mod_p, w_conv8, w_nsa_p, w_cv, w_o, tm, tpb)
    y_prompt = _ffn(h2, mod_p, 2, gn[2], wgu2, wd2, tm, tpb, g_final).reshape(nbp, t, d)

    kv_shape = (2, H_KV, HEAD_DIM)
    kv_out = lambda a: jnp.transpose(a.reshape((1, nbp) + kv_shape + (a.shape[-1],)), (0, 1, 5, 2, 3, 4))
    kv_cmp_p = kv_out(kvc_t)
    kv_sel_p = kv_out(kvs_t)
    keep = min(WINDOW, t)
    kv_win_p = kv_out(kvw_t[:, :, t - keep:])
    conv_p = z.reshape(1, nbp, t, -1)[:, :, t - (CONV_WIDTH - 1):]

    ns = nbs * s_len
    xs = x_sample.reshape(ns, d)
    h1s = _ffn(xs, mod_s, 0, gn[0], wgu1, wd1, ns, 1)
    qs, kvc_s, kvs_s, kvw_s, kvs_sb, kvw_sb, gates_s, z_s, cb_s, gm_s = _inproj(
        h1s, mod_s, gn[1], w_in_p, ns, 1, False)

    pos_minor = lambda a: jnp.transpose(a, (0, 2, 3, 4, 1)).reshape(a.shape[0], KV_W, a.shape[1])
    kvc_past = _cmp_sample(page_table, pos_minor(cache_kv_cmp[0]), wbig, pe8, w2bd)
    n_c = (past + s_len - CMP_BLOCK) // CMP_STRIDE + 1
    nsel = -(-(past + s_len) // SEL_BLOCK)
    nselp = -(-nsel // LANES) * LANES
    rows = N_HEADS * s_len
    head_of_row = [g * GROUP + r for r in range(GROUP) for g in range(H_KV) for _ in range(s_len)]
    tbl_rows = jnp.pad(rel_bias.T[jnp.array(head_of_row)], ((0, 0), (0, LANES - N_BUCKETS)))
    tiles_s = _bias_sample(tbl_rows, past, s_len, past // CMP_STRIDE, SAMPLE_CK, wb)
    msel = _sel_matrix(past // CMP_STRIDE - 1, nselp)

    q5 = qs.reshape(nbs, s_len, GROUP, H_KV, HEAD_DIM).transpose(0, 2, 3, 1, 4)
    q32 = jnp.zeros((nbs, GROUP, H_KV, s_len, H_KV, HEAD_DIM), BF16)
    for g in range(H_KV):
        q32 = q32.at[:, :, g, :, g, :].set(q5[:, :, g])
    q32 = q32.reshape(nbs, rows, LANES)
    g5 = gates_s[:, :3 * N_HEADS].reshape(nbs, s_len, 3, H_KV, GROUP).transpose(0, 4, 3, 1, 2)
    g32 = jnp.pad(g5.reshape(nbs, rows, 3), ((0, 0), (0, 0), (0, LANES - 3)))
    pad_new = lambda a: jnp.pad(a.reshape(nbs, s_len, KV_W), ((0, 0), (0, LANES - s_len), (0, 0)))
    wnew_t = jnp.pad(jnp.transpose(kvw_s.reshape(nbs, s_len, KV_W), (0, 2, 1)), ((0, 0), (0, 0), (0, LANES - s_len)))
    o32, win_next = _attn_sample(page_table, q32, g32, pad_new(kvs_sb), pad_new(kvw_sb), wnew_t,
                                 pos_minor(state_kv_win[0]), kvc_past, tiles_s, msel,
                                 pos_minor(cache_kv_sel[0]), past, s_len)
    o6 = o32.reshape(nbs, GROUP, H_KV, s_len, H_KV, HEAD_DIM)
    o_s = jnp.stack([o6[:, :, g, :, g, :] for g in range(H_KV)], axis=3)
    o_nsa_s = o_s.transpose(0, 2, 1, 3, 4).reshape(ns, N_HEADS * HEAD_DIM).astype(BF16)

    full = jnp.concatenate([state_conv[0], z_s.reshape(nbs, s_len, -1)], axis=1)
    z_shift = (full[:, 1:1 + s_len].reshape(ns, -1), full[:, 0:s_len].reshape(ns, -1))
    h2s = _mixout(h1s, o_nsa_s, z_s, z_shift, cb_s, gm_s, mod_s, w_conv8, w_nsa_p, w_cv, w_o, ns, 1)
    y_sample = _ffn(h2s, mod_s, 2, gn[2], wgu2, wd2, ns, 1, g_final).reshape(nbs, s_len, d)

    kv_cmp_s = kvc_s.reshape((1, nbs, s_len) + kv_shape)
    kv_sel_s = kvs_s.reshape((1, nbs, s_len) + kv_shape)
    kv_win_s = jnp.transpose(win_next.reshape((1, nbs) + kv_shape + (wb,)), (0, 1, 5, 2, 3, 4))
    conv_s = full[None, :, s_len:]
    return (y_prompt, y_sample, kv_cmp_p, kv_sel_p, kv_win_p, conv_p, kv_cmp_s, kv_sel_s, kv_win_s, conv_s)
```

```python
import functools
import math

import jax
import jax.numpy as jnp
from jax import lax
from jax.experimental import pallas as pl
from jax.experimental.pallas import tpu as pltpu

F32 = jnp.float32
BF16 = jnp.bfloat16

HEAD_DIM = 64
N_HEADS = 8
H_KV = 2
GROUP = N_HEADS // H_KV
CMP_BLOCK = 32
CMP_STRIDE = 16
CMP_HIDDEN = 2 * HEAD_DIM
SEL_BLOCK = 64
TOP_N = 16
WINDOW = 512
CONV_WIDTH = 3
N_BUCKETS = 32
MAX_DISTANCE = 128
N_SUB = 3
EPS = 1e-6
NEG = -1e9
LOG2E = math.log2(math.e)
FORCE_BONUS = 1e3
KV_W = 2 * H_KV * HEAD_DIM
KV_HALF = H_KV * HEAD_DIM
LANES = 128
Q_TILE = 256
CMP_NEAR = 32
VMEM_LIMIT = 56 * 1024 * 1024


def _bucket_thresholds():
    max_exact = N_BUCKETS // 2

    def bucket(d):
        if d < max_exact:
            return d
        large = max_exact + int(math.log(d / max_exact) / math.log(MAX_DISTANCE / max_exact)
                                * (N_BUCKETS - max_exact))
        return min(large, N_BUCKETS - 1)

    thr, d = [], 0
    for b in range(N_BUCKETS):
        while bucket(d) < b:
            d += 1
        thr.append(d)
    return tuple(thr)


BUCKET_THR = _bucket_thresholds()
FAR_DIST = BUCKET_THR[-1]


def _cparams(sem):
    return pltpu.CompilerParams(dimension_semantics=sem, vmem_limit_bytes=VMEM_LIMIT)


def _resident(shape):
    nd = len(shape)
    return pl.BlockSpec(shape, lambda *_: (0,) * nd, pipeline_mode=pl.Buffered(1))


def _dot(a, b):
    return jnp.dot(a, b, preferred_element_type=F32)


def _dot_nt(a, b):
    return lax.dot_general(a, b, (((1,), (1,)), ((), ())), preferred_element_type=F32)


def _split3(x):
    h1 = x.astype(BF16)
    r1 = x - h1.astype(F32)
    h2 = r1.astype(BF16)
    h3 = (r1 - h2.astype(F32)).astype(BF16)
    return h1, h2, h3


def _modulated_norm(x, g, shift, scale):
    y = x * lax.rsqrt(jnp.mean(x * x, axis=-1, keepdims=True) + EPS)
    return (y * g) * (1.0 + scale) + shift


def _ada_kernel(c_ref, w_ref, b_ref, o_ref):
    c = c_ref[...]
    a = (c * jax.nn.sigmoid(c)).astype(BF16)
    o_ref[...] = _dot(a, w_ref[...].astype(BF16)) + b_ref[...]


def _ada(c_all, w_ada, b_ada):
    rows, d = c_all.shape
    n = w_ada.shape[1]
    tn = n // 8
    return pl.pallas_call(
        _ada_kernel,
        grid=(n // tn,),
        in_specs=[pl.BlockSpec((rows, d), lambda j: (0, 0)),
                  pl.BlockSpec((d, tn), lambda j: (0, j)),
                  pl.BlockSpec((1, tn), lambda j: (0, j))],
        out_specs=pl.BlockSpec((rows, tn), lambda j: (0, j)),
        out_shape=jax.ShapeDtypeStruct((rows, n), F32),
        compiler_params=_cparams(("arbitrary",)),
        name="ada",
    )(c_all, w_ada, b_ada.reshape(1, n))


def _ffn_kernel(*refs, d_ff, fc, final_norm):
    if final_norm:
        x_ref, sh_ref, sc_ref, gt_ref, gn_ref, wgu_ref, wd_ref, gf_ref, o_ref = refs
    else:
        x_ref, sh_ref, sc_ref, gt_ref, gn_ref, wgu_ref, wd_ref, o_ref = refs
    x = x_ref[...]
    ub = _modulated_norm(x, gn_ref[...], sh_ref[...], sc_ref[...]).astype(BF16)
    acc = jnp.zeros(x.shape, F32)
    for c in range(d_ff // fc):
        g = _dot(ub, wgu_ref[:, c * fc:(c + 1) * fc])
        v = _dot(ub, wgu_ref[:, d_ff + c * fc:d_ff + (c + 1) * fc])
        a = ((g * jax.nn.sigmoid(g)) * v).astype(BF16)
        acc = acc + _dot(a, wd_ref[c * fc:(c + 1) * fc, :])
    h = x + (0.5 * gt_ref[...]) * acc
    if final_norm:
        h = (h * lax.rsqrt(jnp.mean(h * h, axis=-1, keepdims=True) + EPS)) * gf_ref[...]
    o_ref[...] = h


def _mod_specs(mod, ks, tm, tiles_per_batch):
    if mod.ndim == 3 and mod.shape[1] == 1:
        d = mod.shape[-1]
        specs = [pl.BlockSpec((None, 1, d), lambda i, k=k: ((i // tiles_per_batch) * (3 * N_SUB) + k, 0, 0))
                 for k in ks]
        return specs, [mod] * len(ks)
    d = mod.shape[-1]
    specs = [pl.BlockSpec((None, tm, d), lambda i, k=k: (k, i, 0)) for k in ks]
    return specs, [mod] * len(ks)


def _ffn(x, mod, sub, gn_row, w_gu, w_down, tm, tiles_per_batch, g_final=None):
    n, d = x.shape
    d_ff = w_down.shape[0]
    fc = d_ff // 2 if (d_ff // 2) % LANES == 0 else d_ff
    final_norm = g_final is not None
    mspecs, mops = _mod_specs(mod, (3 * sub, 3 * sub + 1, 3 * sub + 2), tm, tiles_per_batch)
    in_specs = [pl.BlockSpec((tm, d), lambda i: (i, 0))] + mspecs + [
        pl.BlockSpec((1, d), lambda i: (0, 0)), _resident(w_gu.shape), _resident(w_down.shape)]
    ops = [x] + mops + [gn_row, w_gu, w_down]
    if final_norm:
        in_specs.append(pl.BlockSpec((1, d), lambda i: (0, 0)))
        ops.append(g_final.reshape(1, d))
    return pl.pallas_call(
        functools.partial(_ffn_kernel, d_ff=d_ff, fc=fc, final_norm=final_norm),
        grid=(n // tm,),
        in_specs=in_specs,
        out_specs=pl.BlockSpec((tm, d), lambda i: (i, 0)),
        out_shape=jax.ShapeDtypeStruct((n, d), F32),
        compiler_params=_cparams(("arbitrary",)),
        name="ffn_final" if final_norm else "ffn",
    )(*ops)


_Q_W = N_HEADS * HEAD_DIM
_G_W = LANES
_C_W = 512
_SEG = {}
_off = 0
for _name, _w in (("q", _Q_W), ("kc", KV_W), ("ks", KV_W), ("kw", KV_W), ("gn", _G_W),
                  ("ch", _C_W), ("cb", _C_W), ("cc", _C_W)):
    _SEG[_name] = (_off, _off + _w)
    _off += _w
_MG_OFF = _off


V_ROWS = HEAD_DIM + 16


def _aug_lane(g):
    return HEAD_DIM if g == 0 else 0


def _inproj_kernel(*refs, prompt):
    x_ref, sh_ref, sc_ref, gn_ref, w_ref = refs[:5]
    ub = _modulated_norm(x_ref[...], gn_ref[...], sh_ref[...], sc_ref[...]).astype(BF16)

    def seg(name):
        lo, hi = _SEG[name]
        return _dot(ub, w_ref[:, lo:hi])

    if prompt:
        (q_ref, kck_ref, kcv_ref, kct_ref, kst_ref, kwt_ref, ks0_ref, ks1_ref, vs0_ref, vs1_ref, kwk_ref,
         vw0_ref, vw1_ref, g_ref, z_ref, cb_ref, gm_ref) = refs[5:]
        kc = seg("kc")
        kck_ref[...] = kc[:, 0:KV_HALF]
        kcv_ref[...] = kc[:, KV_HALF:KV_W]
        kct_ref[...] = kc.T
        tm = kc.shape[0]
        lane = lax.broadcasted_iota(jnp.int32, (tm, KV_HALF), 1)
        blk = lax.shift_right_logical(lax.broadcasted_iota(jnp.int32, (tm, KV_HALF), 0),
                                      int(math.log2(SEL_BLOCK))) & (Q_TILE // SEL_BLOCK - 1)
        ones_rows = jnp.where(lax.broadcasted_iota(jnp.int32, (V_ROWS - HEAD_DIM, tm), 0) == 0, 1.0, 0.0)
        for name, t_ref, k_refs, vt_refs in (("ks", kst_ref, (ks0_ref, ks1_ref), (vs0_ref, vs1_ref)),
                                             ("kw", kwt_ref, (kwk_ref,), (vw0_ref, vw1_ref))):
            kv = seg(name)
            kv_t = kv.T
            t_ref[...] = kv_t
            kk = kv[:, 0:KV_HALF]
            if len(k_refs) == 1:
                k_refs[0][...] = kk.astype(BF16)
            else:
                for g, k_ref in enumerate(k_refs):
                    own = (lane < HEAD_DIM) if g == 0 else (lane >= HEAD_DIM)
                    onehot = jnp.where(lane - _aug_lane(g) == blk, 1.0, 0.0)
                    k_ref[...] = jnp.where(own, kk, onehot).astype(BF16)
            for g, vt_ref in enumerate(vt_refs):
                vg = jnp.concatenate(
                    [kv_t[KV_HALF + g * HEAD_DIM:KV_HALF + (g + 1) * HEAD_DIM, :], ones_rows], axis=0)
                for c in range(vt_ref.shape[0]):
                    vt_ref[c] = vg[:, c * Q_TILE:(c + 1) * Q_TILE].astype(BF16)
    else:
        q_ref, kc_ref, ks_ref, kw_ref, ksb_ref, kwb_ref, g_ref, z_ref, cb_ref, gm_ref = refs[5:]
        kc_ref[...] = seg("kc")
        ks = seg("ks")
        ks_ref[...] = ks
        ksb_ref[...] = ks.astype(BF16)
        kw = seg("kw")
        kw_ref[...] = kw
        kwb_ref[...] = kw.astype(BF16)
    q_ref[...] = (seg("q") * (HEAD_DIM ** -0.5 * LOG2E)).astype(BF16)
    g_ref[...] = jax.nn.sigmoid(seg("gn"))
    z_ref[...] = seg("cc") * seg("ch")
    cb_ref[...] = seg("cb").astype(cb_ref.dtype)
    d2 = gm_ref.shape[1]
    half = d2 // 2
    for c in range(2):
        gm_ref[:, c * half:(c + 1) * half] = jax.nn.sigmoid(
            _dot(ub, w_ref[:, _MG_OFF + c * half:_MG_OFF + (c + 1) * half])).astype(gm_ref.dtype)


def _inproj(h, mod, gn_row, w_in_p, tm, tiles_per_batch, prompt):
    n, d = h.shape
    d_conv = _C_W
    mspecs, mops = _mod_specs(mod, (3, 4), tm, tiles_per_batch)
    rows = lambda w, dt: (pl.BlockSpec((tm, w), lambda i: (i, 0)), jax.ShapeDtypeStruct((n, w), dt))
    tail = [rows(_G_W, F32), rows(d_conv, F32), rows(d_conv, BF16), rows(2 * d, BF16)]
    if prompt:
        nb = n // (tm * tiles_per_batch)
        t = tm * tiles_per_batch
        cpt = tm // Q_TILE
        tr = (pl.BlockSpec((None, KV_W, tm), lambda i: (i // tiles_per_batch, 0, i % tiles_per_batch)),
              jax.ShapeDtypeStruct((nb, KV_W, t), F32))
        vt = (pl.BlockSpec((cpt, V_ROWS, Q_TILE), lambda i: (i, 0, 0)),
              jax.ShapeDtypeStruct((n // Q_TILE, V_ROWS, Q_TILE), BF16))
        kr = rows(KV_HALF, BF16)
        outs = [rows(_Q_W, BF16), rows(KV_HALF, F32), rows(KV_HALF, F32), tr, tr, tr,
                kr, kr, vt, vt, kr, vt, vt] + tail
    else:
        outs = [rows(_Q_W, BF16), rows(KV_W, F32), rows(KV_W, F32), rows(KV_W, F32),
                rows(KV_W, BF16), rows(KV_W, BF16)] + tail
    return pl.pallas_call(
        functools.partial(_inproj_kernel, prompt=prompt),
        grid=(n // tm,),
        in_specs=[pl.BlockSpec((tm, d), lambda i: (i, 0))] + mspecs + [
            pl.BlockSpec((1, d), lambda i: (0, 0)), _resident(w_in_p.shape)],
        out_specs=[o[0] for o in outs],
        out_shape=[o[1] for o in outs],
        compiler_params=_cparams(("arbitrary",)),
        name="inproj_prompt" if prompt else "inproj",
    )(h, *mops, gn_row, w_in_p)


def _mixout_kernel(*refs, halo, tiles_per_batch):
    if halo:
        (h_ref, o_ref, z_ref, zp_ref, cb_ref, gm_ref, g2_ref, wc_ref,
         wn_ref, wcv_ref, wo_ref, out_ref) = refs
        z = z_ref[...]
        tm = z.shape[0]
        first = (pl.program_id(0) % tiles_per_batch) == 0
        prev = jnp.where(first, 0.0, zp_ref[...])
        row = lax.broadcasted_iota(jnp.int32, z.shape, 0)
        zm1 = jnp.where(row == 0, prev[7:8, :], pltpu.roll(z, 1, 0))
        zm2 = jnp.where(row == 0, prev[6:7, :], jnp.where(row == 1, prev[7:8, :], pltpu.roll(z, 2, 0)))
    else:
        (h_ref, o_ref, z_ref, zm1_ref, zm2_ref, cb_ref, gm_ref, g2_ref, wc_ref,
         wn_ref, wcv_ref, wo_ref, out_ref) = refs
        z, zm1, zm2 = z_ref[...], zm1_ref[...], zm2_ref[...]
    conv = wc_ref[0:1, :] * zm2 + wc_ref[1:2, :] * zm1 + wc_ref[2:3, :] * z
    y = (cb_ref[...].astype(F32) * conv).astype(BF16)
    d = h_ref.shape[1]
    merged = gm_ref[:, 0:d].astype(F32) * _dot(o_ref[...], wn_ref[...]) \
        + gm_ref[:, d:2 * d].astype(F32) * _dot(y, wcv_ref[...])
    out_ref[...] = h_ref[...] + g2_ref[...] * _dot(merged.astype(BF16), wo_ref[...])


def _mixout(h, o_nsa, z, z_shift, cb, gm, mod, w_conv8, w_nsa_p, w_cv, w_o, tm, tiles_per_batch):
    n, d = h.shape
    dc = z.shape[1]
    halo = z_shift is None
    mspecs, mops = _mod_specs(mod, (5,), tm, tiles_per_batch)
    tok = lambda w: pl.BlockSpec((tm, w), lambda i: (i, 0))
    if halo:
        zspecs = [tok(dc), pl.BlockSpec((8, dc), lambda i: (jnp.maximum(i * (tm // 8) - 1, 0), 0))]
        zops = [z, z]
    else:
        zspecs = [tok(dc), tok(dc), tok(dc)]
        zops = [z, z_shift[0], z_shift[1]]
    return pl.pallas_call(
        functools.partial(_mixout_kernel, halo=halo, tiles_per_batch=tiles_per_batch),
        grid=(n // tm,),
        in_specs=[tok(d), tok(o_nsa.shape[1])] + zspecs + [tok(dc), tok(2 * d)] + mspecs + [
            pl.BlockSpec((8, dc), lambda i: (0, 0)),
            _resident(w_nsa_p.shape), _resident(w_cv.shape), _resident(w_o.shape)],
        out_specs=tok(d),
        out_shape=jax.ShapeDtypeStruct((n, d), F32),
        compiler_params=_cparams(("arbitrary",)),
        name="mixout",
    )(h, o_nsa, *zops, cb, gm, *mops, w_conv8, w_nsa_p, w_cv, w_o)


CHUNK_W = CMP_STRIDE * KV_W
CMP_OUT_W = 2 * H_KV * CMP_HIDDEN


def _gelu_tanh(x):
    return x * (0.5 * (1.0 + jnp.tanh(math.sqrt(2.0 / math.pi) * (x + 0.044715 * (x * x * x)))))


def _chunk_partials(tile, wc_ref, after_step=None):
    accs = []
    for c in range(2):
        acc = None
        for lp in range(CMP_STRIDE // 2):
            x2 = jnp.concatenate([tile(2 * lp, c), tile(2 * lp + 1, c)], axis=1)
            part = _dot(x2, wc_ref[c, lp])
            acc = part if acc is None else acc + part
            if after_step is not None:
                after_step(c, lp)
        accs.append(acc)
    hw = CMP_OUT_W // 2
    return jnp.concatenate([accs[0][:, 0:hw], accs[1][:, 0:hw], accs[0][:, hw:], accs[1][:, hw:]], axis=1)


def _compress_tail(p, wc_ref, pe_ref, w2_ref):
    n_ch = p.shape[0]
    pb = _chunk_partials(
        lambda l, c: pe_ref[:, l * KV_W + c * KV_HALF:l * KV_W + (c + 1) * KV_HALF].astype(BF16), wc_ref)
    pre = (pb[0:1, 0:CMP_OUT_W] + pb[1:2, CMP_OUT_W:]) + p[:, 0:CMP_OUT_W] \
        + pltpu.roll(p[:, CMP_OUT_W:], n_ch - 1, 0)
    out = _dot(_gelu_tanh(pre).astype(BF16), w2_ref[...])
    row = lax.broadcasted_iota(jnp.int32, out.shape, 0)
    return jnp.where(row < n_ch - 1, out, 0.0)


def _cmp_prompt_kernel(xk_ref, xv_ref, wc_ref, pe_ref, w2_ref, o_ref):
    xs = (xk_ref, xv_ref)
    n_ch = xk_ref.shape[0] // CMP_STRIDE
    p = _chunk_partials(lambda l, c: xs[c][pl.ds(l, n_ch, stride=CMP_STRIDE), :].astype(BF16), wc_ref)
    out = _compress_tail(p, wc_ref, pe_ref, w2_ref).astype(BF16)
    o_ref[0:n_ch, :] = out
    o_ref[n_ch:2 * n_ch, :] = out


def _cmp_prompt(rows_k, rows_v, t, wbig, pe8, w2bd):
    nb = rows_k.shape[0] // t
    n_ch = t // CMP_STRIDE
    half = pl.BlockSpec((t, KV_HALF), lambda b: (b, 0))
    return pl.pallas_call(
        _cmp_prompt_kernel,
        grid=(nb,),
        in_specs=[half, half, _resident(wbig.shape), _resident(pe8.shape), _resident(w2bd.shape)],
        out_specs=pl.BlockSpec((None, 2 * n_ch, KV_W), lambda b: (b, 0, 0)),
        out_shape=jax.ShapeDtypeStruct((nb, 2 * n_ch, KV_W), BF16),
        compiler_params=_cparams(("arbitrary",)),
        name="cmp_prompt",
    )(rows_k, rows_v, wbig, pe8, w2bd)


def _page_copies(cache_ref, pt_ref, b, dst, sem, n_pages, page):
    return [pltpu.make_async_copy(cache_ref.at[pt_ref[b, p]], dst.at[:, pl.ds(p * page, page)], sem)
            for p in range(n_pages)]


def _gather_pages(cache_ref, pt_ref, buf, sem, n_pages, page):
    b = pl.program_id(0)
    nb = pl.num_programs(0)
    slot = b % 2
    copies = lambda bb, sl: _page_copies(cache_ref, pt_ref, bb, buf.at[sl], sem.at[sl], n_pages, page)

    @pl.when(b == 0)
    def _():
        for c in copies(0, 0):
            c.start()

    @pl.when(b + 1 < nb)
    def _():
        for c in copies(b + 1, 1 - slot):
            c.start()

    for c in copies(b, slot):
        c.wait()
    return slot


def _cmp_sample_kernel(pt_ref, cache_ref, wc_ref, pe_ref, w2_ref, o_ref,
                       buf_a, buf_b, xk_a, xv_a, xk_b, xv_b, sem, *, n_pages, page, nb):
    b = pl.program_id(0)
    bufs = (buf_a, buf_b)
    xs = ((xk_a, xv_a), (xk_b, xv_b))
    past = xk_a.shape[0]
    n_ch = past // CMP_STRIDE
    n_slabs = CMP_STRIDE // 2
    slab = past // n_slabs
    copies = lambda bb, par: _page_copies(cache_ref, pt_ref, bb, bufs[par], sem.at[par], n_pages, page)

    def transpose_slab(par, c, j):
        xs[par][c][j * slab:(j + 1) * slab, :] = bufs[par][c * KV_HALF:(c + 1) * KV_HALF, j * slab:(j + 1) * slab].T

    @pl.when(b == 0)
    def _():
        for cp in copies(0, 0):
            cp.start()
        if nb > 1:
            for cp in copies(1, 1):
                cp.start()
        for cp in copies(0, 0):
            cp.wait()
        for c in range(2):
            for j in range(n_slabs):
                transpose_slab(0, c, j)

    def step(par):
        nxt = 1 - par
        hook = None
        if nb > 1:
            @pl.when(b + 1 < nb)
            def _():
                for cp in copies(b + 1, nxt):
                    cp.wait()

            @pl.when(b + 2 < nb)
            def _():
                for cp in copies(b + 2, par):
                    cp.start()

            hook = lambda c, lp: transpose_slab(nxt, c, lp)
        p = _chunk_partials(lambda l, c: xs[par][c][pl.ds(l, n_ch, stride=CMP_STRIDE), :].astype(BF16), wc_ref, hook)
        o_ref[...] = _compress_tail(p, wc_ref, pe_ref, w2_ref).astype(BF16)

    for par in range(2):
        pl.when(b % 2 == par)(functools.partial(step, par))


def _cmp_sample(page_table, cache_t, wbig, pe8, w2bd):
    nb, n_pages = page_table.shape
    page = cache_t.shape[2]
    past = n_pages * page
    n_ch = past // CMP_STRIDE
    return pl.pallas_call(
        functools.partial(_cmp_sample_kernel, n_pages=n_pages, page=page, nb=nb),
        grid_spec=pltpu.PrefetchScalarGridSpec(
            num_scalar_prefetch=1,
            grid=(nb,),
            in_specs=[pl.BlockSpec(memory_space=pl.ANY),
                      _resident_sp(wbig.shape), _resident_sp(pe8.shape), _resident_sp(w2bd.shape)],
            out_specs=pl.BlockSpec((None, n_ch, KV_W), lambda b, pt: (b, 0, 0)),
            scratch_shapes=[pltpu.VMEM((KV_W, past), F32)] * 2 + [pltpu.VMEM((past, KV_HALF), F32)] * 4
            + [pltpu.SemaphoreType.DMA((2,))]),
        out_shape=jax.ShapeDtypeStruct((nb, n_ch, KV_W), BF16),
        compiler_params=_cparams(("arbitrary",)),
        name="cmp_sample",
    )(page_table, cache_t, wbig, pe8, w2bd)


def _resident_sp(shape):
    nd = len(shape)
    return pl.BlockSpec(shape, lambda *_: (0,) * nd, pipeline_mode=pl.Buffered(1))


def _bias_values(d, table):
    acc = jnp.zeros(d.shape, F32) + table(0)
    for b in range(1, N_BUCKETS):
        acc = jnp.where(d >= BUCKET_THR[b], table(b), acc)
    return jnp.where(d >= 0, (acc - table(N_BUCKETS - 1)) * LOG2E, NEG)


CMP_BACK = 16


def _bias_prompt_kernel(tbl_ref, bc_ref, t0_ref, t1_ref):
    h = pl.program_id(0)
    table = lambda b: tbl_ref[b, h]
    ncp, tq = bc_ref.shape
    kj = lax.broadcasted_iota(jnp.int32, (tq, tq), 0)
    qi = lax.broadcasted_iota(jnp.int32, (tq, tq), 1)
    t0_ref[...] = _bias_values(qi - kj, table)
    t1_ref[...] = _bias_values(qi - kj + tq, table)
    c = lax.broadcasted_iota(jnp.int32, (ncp, tq), 0)
    qi = lax.broadcasted_iota(jnp.int32, (ncp, tq), 1)
    d = qi - CMP_STRIDE * c + (CMP_STRIDE * CMP_BACK - (CMP_BLOCK - 1))
    bc_ref[...] = jnp.where(c < CMP_NEAR, _bias_values(d, table), 0.0)


def _bias_prompt(rel_bias, ncp):
    tq = Q_TILE
    return pl.pallas_call(
        _bias_prompt_kernel,
        grid=(N_HEADS,),
        in_specs=[pl.BlockSpec(memory_space=pltpu.SMEM)],
        out_specs=[pl.BlockSpec((None, ncp, tq), lambda h: (h, 0, 0)),
                   pl.BlockSpec((None, tq, tq), lambda h: (h, 0, 0)),
                   pl.BlockSpec((None, tq, tq), lambda h: (h, 0, 0))],
        out_shape=[jax.ShapeDtypeStruct((N_HEADS, ncp, tq), F32),
                   jax.ShapeDtypeStruct((N_HEADS, tq, tq), F32),
                   jax.ShapeDtypeStruct((N_HEADS, tq, tq), F32)],
        compiler_params=_cparams(("arbitrary",)),
        name="bias_prompt",
    )(rel_bias)


def _bias_sample_kernel(tbl_ref, bcs_ref, bsl_ref, bnew_ref, bw_ref, *, past, s_len, n_c):
    table = lambda b: tbl_ref[:, b:b + 1]

    def qpos(shape):
        return past + lax.rem(lax.broadcasted_iota(jnp.int32, shape, 0), s_len)

    def lane(shape):
        return lax.broadcasted_iota(jnp.int32, shape, 1)

    sh = bcs_ref.shape
    n = lane(sh)
    bcs_ref[...] = jnp.where(n < n_c, _bias_values(qpos(sh) - (CMP_STRIDE * n + CMP_BLOCK - 1), table), NEG)
    sh = bsl_ref.shape
    bsl_ref[...] = _bias_values(qpos(sh) - (past - sh[1] + lane(sh)), table)
    sh = bnew_ref.shape
    bnew_ref[...] = jnp.where(lane(sh) < s_len, _bias_values(qpos(sh) - (past + lane(sh)), table), NEG)
    sh = bw_ref.shape
    d = qpos(sh) - (past - sh[1] + lane(sh))
    bw_ref[...] = jnp.where(d < WINDOW, _bias_values(d, table), NEG)


def _bias_sample(tbl_rows, past, s_len, ncs, ck, wb):
    rows = tbl_rows.shape[0]
    widths = (ncs, ck, LANES, wb)
    return pl.pallas_call(
        functools.partial(_bias_sample_kernel, past=past, s_len=s_len, n_c=(past + s_len - CMP_BLOCK) // CMP_STRIDE + 1),
        out_shape=[jax.ShapeDtypeStruct((rows, w), F32) for w in widths],
        name="bias_sample",
    )(tbl_rows)


MASK_FLOOR = -1e8
M_INIT = -1e30
TAKEN = -3e38
SCORE_AHEAD = 3
FAR_PER_ITER = 3
VALUE_BEHIND = 2


def _sel_weights(v):
    return jnp.where((v >= 0) & (v <= 2), 2.0, jnp.where((v == -1) | (v == 3), 1.0, 0.0))


def _attn_prompt_kernel(q_ref, gate_ref, ks0_ref, ks1_ref, vs0_ref, vs1_ref, kwk_ref, vw0_ref, vw1_ref,
                        kc2_ref, bc_ref, t0_ref, t1_ref, o_ref,
                        qm_ref, selr_ref, oc_ref, ms_ref, as_ref, mw_ref, aw_ref, *, ncp, nsel):
    tq = Q_TILE
    i = pl.program_id(1)
    q0 = i * tq
    lane = lax.broadcasted_iota(jnp.int32, (tq, LANES), 1)
    heads = [(g, r) for g in range(H_KV) for r in range(GROUP)]

    for g, r in heads:
        gmask = (lane < HEAD_DIM) if g == 0 else (lane >= HEAD_DIM)
        qm_ref[g * GROUP + r] = jnp.where(gmask, q_ref[:, r * LANES:(r + 1) * LANES], jnp.zeros((tq, LANES), BF16))

    n0 = (tq // CMP_STRIDE) * i - CMP_BACK
    ws = pl.multiple_of(lax.rem(n0 + ncp, ncp), 8)
    cl = lax.broadcasted_iota(jnp.int32, (ncp, tq), 0)
    dm = jnp.where((cl >= CMP_NEAR) & (cl < ncp - n0), NEG, 0.0) + jnp.where(cl < -n0, NEG, 0.0)
    jrow = lax.broadcasted_iota(jnp.int32, (nsel, ncp), 0)
    nn = lax.broadcasted_iota(jnp.int32, (nsel, ncp), 1) + n0
    nn = jnp.where(nn < 0, nn + ncp, nn)
    nn = jnp.where(nn >= ncp, nn - ncp, nn)
    mit = _sel_weights(nn - (SEL_BLOCK // CMP_STRIDE) * jrow).astype(BF16)

    jr = lax.broadcasted_iota(jnp.int32, (nsel, tq), 0)
    qpos = q0 + lax.broadcasted_iota(jnp.int32, (nsel, tq), 1)
    cur = lax.shift_right_logical(qpos, int(math.log2(SEL_BLOCK)))
    invalid_pen = jnp.where(jr * SEL_BLOCK <= qpos, 0.0, NEG)
    bonus = jnp.where((jr == 0) | (jr == cur) | (jr == cur - 1), FORCE_BONUS, 0.0)
    pj = lax.broadcasted_iota(jnp.int32, (LANES, LANES), 0)
    pl_ = lax.broadcasted_iota(jnp.int32, (LANES, LANES), 1)
    kj = lax.broadcasted_iota(jnp.int32, (tq, tq), 0)
    qi = lax.broadcasted_iota(jnp.int32, (tq, tq), 1)
    wmask = jnp.where(kj > qi, 0.0, NEG)
    blocks_per_chunk = tq // SEL_BLOCK

    kc = kc2_ref[pl.ds(ws, ncp), 0:KV_HALF]
    vct = kc2_ref[pl.ds(ws, ncp), KV_HALF:KV_W].astype(F32).T.astype(BF16)
    def cmp_scores(h):
        return _dot_nt(kc, qm_ref[h]) + bc_ref[h] + dm

    ahead = {h: cmp_scores(h) for h in range(2)}
    psum = [None] * H_KV
    for h in range(N_HEADS):
        s = ahead.pop(h)
        if h + 2 < N_HEADS:
            ahead[h + 2] = cmp_scores(h + 2)
        m = jnp.maximum(jnp.max(s, axis=0, keepdims=True), MASK_FLOOR)
        e = jnp.exp2(s - m)
        p = e * (1.0 / jnp.maximum(jnp.sum(e, axis=0, keepdims=True), 1e-30))
        oc_ref[h] = _dot(vct, p.astype(BF16))
        psum[h // GROUP] = p if psum[h // GROUP] is None else psum[h // GROUP] + p

    scores = []
    for g in range(H_KV):
        h1, h2, h3 = _split3(psum[g])
        pslc = (_dot(mit, h1) + _dot(mit, h2)) + _dot(mit, h3)
        scores.append(jnp.where(invalid_pen < 0.0, NEG, pslc + bonus))

    def take_max(_, carry):
        out = []
        for work, sel_acc in carry:
            mx = jnp.max(work, axis=0, keepdims=True)
            first = jnp.min(jnp.where(work == mx, jr, nsel), axis=0, keepdims=True)
            hit = jr == first
            out.append((jnp.where(hit, TAKEN, work), jnp.where(hit, 1.0, sel_acc)))
        return tuple(out)

    picked = lax.fori_loop(0, min(TOP_N, nsel), take_max,
                           tuple((sc, jnp.zeros((nsel, tq), F32)) for sc in scores))
    for g in range(H_KV):
        sel_t = picked[g][1]
        if nsel < LANES:
            sel_t = jnp.concatenate([sel_t, jnp.zeros((LANES - nsel, tq), F32)], axis=0)
        selr_ref[g] = sel_t.T.astype(BF16)

    def masked_queries(t, g):
        off = pl_ - _aug_lane(g)
        place = jnp.where((pj - t * blocks_per_chunk == off) & (off >= 0) & (off < blocks_per_chunk), 1.0, 0.0)
        picked = _dot(selr_ref[g], place.astype(BF16))
        auglane = (lane >= _aug_lane(g)) & (lane < _aug_lane(g) + blocks_per_chunk)
        pen = jnp.where(auglane, (picked - 1.0) * (-NEG), 0.0).astype(BF16)
        return [qm_ref[g * GROUP + r] + pen for r in range(GROUP)]

    def run(chunks):
        items = []
        for k_of, vt_of, queries, extra, m_ref, a_ref in chunks:
            cache = {}
            items += [(h, k_of, vt_of, queries, extra, m_ref, a_ref, cache) for h in range(N_HEADS)]

        def kv(item):
            g, cache = item[0] // GROUP, item[7]
            if g not in cache:
                cache[g] = (item[1](g), item[2](g))
            return cache[g]

        def scores(item):
            h, queries, extra = item[0], item[3], item[4]
            s = _dot_nt(kv(item)[0], queries(h))
            return s if extra is None else s + extra(h)

        def softmax(item, s):
            h, m_ref = item[0], item[5]
            m_prev = m_ref[h]
            m_new = jnp.maximum(m_prev, jnp.max(s, axis=0, keepdims=True))
            m_ref[h] = m_new
            return jnp.exp2(s - m_new).astype(BF16), jnp.exp2(m_prev - m_new)

        def accumulate(item, p, alpha):
            h, a_ref = item[0], item[6]
            a_ref[h] = alpha * a_ref[h] + _dot(kv(item)[1], p)

        n = len(items)
        s = {j: scores(items[j]) for j in range(SCORE_AHEAD)}
        pa = {}
        for j in range(n):
            pa[j] = softmax(items[j], s.pop(j))
            if j + SCORE_AHEAD < n:
                s[j + SCORE_AHEAD] = scores(items[j + SCORE_AHEAD])
            if j >= VALUE_BEHIND:
                accumulate(items[j - VALUE_BEHIND], *pa.pop(j - VALUE_BEHIND))
        for j in range(n - VALUE_BEHIND, n):
            accumulate(items[j], *pa.pop(j))

    for m_ref, a_ref in ((ms_ref, as_ref), (mw_ref, aw_ref)):
        m_ref[...] = jnp.full(m_ref.shape, M_INIT, F32)
        a_ref[...] = jnp.zeros(a_ref.shape, F32)

    def sel_chunk(t, extra):
        sl = pl.ds(pl.multiple_of(t * tq, tq), tq)
        qs = [masked_queries(t, g) for g in range(H_KV)]
        return (lambda g: (ks0_ref, ks1_ref)[g][sl, :], lambda g: (vs0_ref, vs1_ref)[g][t],
                lambda h: qs[h // GROUP][h % GROUP], extra, ms_ref, as_ref)

    def win_chunk(t, extra):
        sl = pl.ds(pl.multiple_of(t * tq, tq), tq)
        return (lambda g: kwk_ref[sl, :], lambda g: (vw0_ref, vw1_ref)[g][t],
                lambda h: qm_ref[h], extra, mw_ref, aw_ref)

    n_far = jnp.maximum(i - 1, 0)
    n_iter = n_far // FAR_PER_ITER

    def far_group(u, carry):
        run([sel_chunk(FAR_PER_ITER * u + j, None) for j in range(FAR_PER_ITER)])
        return carry

    lax.fori_loop(0, n_iter, far_group, 0)
    for rem in range(1, FAR_PER_ITER):
        @pl.when(n_far - FAR_PER_ITER * n_iter == rem)
        def _(rem=rem):
            run([sel_chunk(FAR_PER_ITER * n_iter + j, None) for j in range(rem)])

    gone1 = jnp.where(i >= 1, 0.0, NEG)
    gone2 = jnp.where(i >= 2, 0.0, NEG)
    wmask2 = wmask + gone2
    run([sel_chunk(jnp.maximum(i - 1, 0), lambda h: t1_ref[h] + gone1),
         sel_chunk(i, lambda h: t0_ref[h]),
         win_chunk(jnp.maximum(i - 2, 0), lambda h: wmask2),
         win_chunk(jnp.maximum(i - 1, 0), lambda h: t1_ref[h] + gone1),
         win_chunk(i, lambda h: t0_ref[h])])

    gates_t = gate_ref[...].T
    for r in range(GROUP):
        per_g = []
        for g in range(H_KV):
            h = g * GROUP + r
            a_s, a_w = as_ref[h], aw_ref[h]
            o_sel = a_s[0:HEAD_DIM] * (1.0 / jnp.maximum(a_s[HEAD_DIM:HEAD_DIM + 1], 1e-30))
            o_win = a_w[0:HEAD_DIM] * (1.0 / jnp.maximum(a_w[HEAD_DIM:HEAD_DIM + 1], 1e-30))
            per_g.append(gates_t[h:h + 1, :] * oc_ref[h, g * HEAD_DIM:(g + 1) * HEAD_DIM, :]
                         + gates_t[N_HEADS + h:N_HEADS + h + 1, :] * o_sel
                         + gates_t[2 * N_HEADS + h:2 * N_HEADS + h + 1, :] * o_win)
        o_ref[:, r * LANES:(r + 1) * LANES] = jnp.concatenate(per_g, axis=0).T.astype(BF16)


def _attn_prompt(q, gates, ks_g, vs_g, kw_k, vw_g, kc2, bias_tiles, nb, t):
    tq = Q_TILE
    assert t % tq == 0 and WINDOW == 2 * tq
    nt = t // tq
    ncp = t // CMP_STRIDE
    nsel = t // SEL_BLOCK
    assert nsel <= LANES and ncp % LANES == 0
    bc, t0, t1 = bias_tiles
    tok = lambda w: pl.BlockSpec((tq, w), lambda b, i: (b * nt + i, 0))
    k_rows = pl.BlockSpec((None, t, KV_HALF), lambda b, i: (b, 0, 0))
    v_cols = pl.BlockSpec((None, nt, V_ROWS, tq), lambda b, i: (b, 0, 0, 0))
    state = [pltpu.VMEM((N_HEADS, 1, tq), F32), pltpu.VMEM((N_HEADS, V_ROWS, tq), F32)]
    kr = lambda a: a.reshape(nb, t, KV_HALF)
    vc = lambda a: a.reshape(nb, nt, V_ROWS, tq)
    return pl.pallas_call(
        functools.partial(_attn_prompt_kernel, ncp=ncp, nsel=nsel),
        grid=(nb, nt),
        in_specs=[tok(q.shape[1]), tok(LANES), k_rows, k_rows, v_cols, v_cols, k_rows, v_cols, v_cols,
                  pl.BlockSpec((None, 2 * ncp, KV_W), lambda b, i: (b, 0, 0)),
                  _resident(bc.shape), _resident(t0.shape), _resident(t1.shape)],
        out_specs=tok(q.shape[1]),
        out_shape=jax.ShapeDtypeStruct(q.shape, BF16),
        scratch_shapes=[pltpu.VMEM((N_HEADS, tq, LANES), BF16), pltpu.VMEM((H_KV, tq, LANES), BF16),
                        pltpu.VMEM((N_HEADS, LANES, tq), F32)] + state + state,
        compiler_params=_cparams(("arbitrary", "arbitrary")),
        name="attn_prompt",
    )(q, gates, kr(ks_g[0]), kr(ks_g[1]), vc(vs_g[0]), vc(vs_g[1]), kr(kw_k), vc(vw_g[0]), vc(vw_g[1]),
      kc2, bc, t0, t1)


SAMPLE_CK = 1024


def _softmax_part(s, v, transposed):
    m = jnp.max(s, axis=-1, keepdims=True)
    p = jnp.exp2(s - m)
    pv = _dot_nt(p.astype(BF16), v) if transposed else _dot(p.astype(BF16), v)
    return m, jnp.sum(p, axis=-1, keepdims=True), pv


def _merge_parts(parts):
    m = parts[0][0]
    for mt, _, _ in parts[1:]:
        m = jnp.maximum(m, mt)
    l, acc = None, None
    for mt, lt, at in parts:
        w = jnp.exp2(mt - m)
        l = w * lt if l is None else l + w * lt
        acc = w * at if acc is None else acc + w * at
    return acc / jnp.maximum(l, 1e-30)


def _attn_sample_kernel(pt_ref, q_ref, g_ref, knew_ref, wnew_ref, wnewt_ref, win_ref, kc_ref,
                        bcs_ref, bsl_ref, bnew_ref, bw_ref, msel_ref, cache_ref, o_ref, wout_ref, buf, sem,
                        *, n_pages, page, past, s_len, nsel):
    slot = _gather_pages(cache_ref, pt_ref, buf, sem, n_pages, page)

    wb = win_ref.shape[1]
    shifted = pltpu.roll(win_ref[...], wb - s_len, 1)
    fresh = pltpu.roll(wnewt_ref[...], LANES - s_len, 1)
    tail_lane = lax.broadcasted_iota(jnp.int32, fresh.shape, 1) >= LANES - s_len
    wout_ref[:, 0:wb - LANES] = shifted[:, 0:wb - LANES]
    wout_ref[:, wb - LANES:wb] = jnp.where(tail_lane, fresh, shifted[:, wb - LANES:wb])
    q = q_ref[...]
    rows = q.shape[0]
    gs = rows // GROUP
    nselp = msel_ref.shape[1]

    s = _dot_nt(q, kc_ref[:, 0:KV_HALF]) + bcs_ref[...]
    m = jnp.maximum(jnp.max(s, axis=-1, keepdims=True), MASK_FLOOR)
    e = jnp.exp2(s - m)
    p = e / jnp.maximum(jnp.sum(e, axis=-1, keepdims=True), 1e-30)
    o_cmp = _dot(p.astype(BF16), kc_ref[:, KV_HALF:KV_W])
    psum = p[0:gs]
    for r in range(1, GROUP):
        psum = psum + p[r * gs:(r + 1) * gs]

    h1, h2, h3 = _split3(psum)
    msel = msel_ref[...]
    pslc = (_dot(h1, msel) + _dot(h2, msel)) + _dot(h3, msel)
    j = lax.broadcasted_iota(jnp.int32, (gs, nselp), 1)
    qpos = past + lax.rem(lax.broadcasted_iota(jnp.int32, (gs, nselp), 0), s_len)
    cur = lax.shift_right_logical(qpos, int(math.log2(SEL_BLOCK)))
    valid = j * SEL_BLOCK <= qpos
    forced = (j == 0) | (j == cur) | (j == cur - 1)
    score = jnp.where(valid, pslc + jnp.where(forced, FORCE_BONUS, 0.0), NEG)
    score = jnp.where(j < nsel, score, -3e38)
    score_t = jnp.concatenate([score, jnp.zeros((LANES - gs, nselp), F32)], axis=0).T
    jp = lax.broadcasted_iota(jnp.int32, (nselp, nselp), 0)
    jj = lax.broadcasted_iota(jnp.int32, (nselp, nselp), 1)
    sel_rows = []
    for r in range(gs):
        col = score_t[:, r:r + 1]
        row = score[r:r + 1, :]
        beats = jnp.where(jj > jp, jnp.where(col >= row, 1.0, 0.0), jnp.where(col > row, 1.0, 0.0))
        rank = jnp.sum(beats, axis=0, keepdims=True)
        sel_rows.append(jnp.where(rank < min(TOP_N, nsel), 1.0, 0.0))
    sel8 = jnp.concatenate(sel_rows, axis=0)
    sel = jnp.concatenate([sel8] * GROUP, axis=0).astype(BF16)

    ck = bsl_ref.shape[1]
    n_chunks = past // ck
    erow = lax.broadcasted_iota(jnp.int32, (nselp, ck), 0)
    eblk = lax.shift_right_logical(lax.broadcasted_iota(jnp.int32, (nselp, ck), 1), int(math.log2(SEL_BLOCK)))
    scs = []
    for t in range(n_chunks):
        kt = buf[slot, 0:KV_HALF, t * ck:(t + 1) * ck].astype(BF16)
        e_t = jnp.where(erow == t * (ck // SEL_BLOCK) + eblk, 1.0, 0.0).astype(BF16)
        sc = _dot(q, kt) + (_dot(sel, e_t) - 1.0) * (-NEG)
        scs.append(sc + bsl_ref[...] if t == n_chunks - 1 else sc)
    knew = knew_ref[...]
    wnew = wnew_ref[...]
    s_new = _dot_nt(q, knew[:, 0:KV_HALF]) + bnew_ref[...]
    s_win = _dot(q, win_ref[0:KV_HALF, :].astype(BF16)) + bw_ref[...]
    s_wnew = _dot_nt(q, wnew[:, 0:KV_HALF]) + bnew_ref[...]
    parts = [_softmax_part(scs[t], buf[slot, KV_HALF:KV_W, t * ck:(t + 1) * ck].astype(BF16), True)
             for t in range(n_chunks)]
    parts.append(_softmax_part(s_new, knew[:, KV_HALF:KV_W], False))
    o_sel = _merge_parts(parts)

    o_win = _merge_parts([_softmax_part(s_win, win_ref[KV_HALF:KV_W, :].astype(BF16), True),
                          _softmax_part(s_wnew, wnew[:, KV_HALF:KV_W], False)])

    g = g_ref[...]
    o_ref[...] = g[:, 0:1] * o_cmp + g[:, 1:2] * o_sel + g[:, 2:3] * o_win


def _attn_sample(page_table, q32, g32, knew, wnew, wnew_t, win_t, kvc, tiles, msel, cache_t, past, s_len):
    nb, n_pages = page_table.shape
    page = cache_t.shape[2]
    rows = q32.shape[1]
    wb = win_t.shape[2]
    nsel = -(-(past + s_len) // SEL_BLOCK)
    bcs, bsl, bnew, bw = tiles
    per_b = lambda r, w: pl.BlockSpec((None, r, w), lambda b, pt: (b, 0, 0))
    return pl.pallas_call(
        functools.partial(_attn_sample_kernel, n_pages=n_pages, page=page, past=past, s_len=s_len, nsel=nsel),
        grid_spec=pltpu.PrefetchScalarGridSpec(
            num_scalar_prefetch=1,
            grid=(nb,),
            in_specs=[per_b(rows, LANES), per_b(rows, LANES), per_b(LANES, KV_W), per_b(LANES, KV_W),
                      per_b(KV_W, LANES), per_b(KV_W, wb), per_b(kvc.shape[1], KV_W),
                      _resident_sp(bcs.shape), _resident_sp(bsl.shape), _resident_sp(bnew.shape),
                      _resident_sp(bw.shape), _resident_sp(msel.shape),
                      pl.BlockSpec(memory_space=pl.ANY)],
            out_specs=[per_b(rows, LANES), per_b(KV_W, wb)],
            scratch_shapes=[pltpu.VMEM((2, KV_W, past), F32), pltpu.SemaphoreType.DMA((2,))]),
        out_shape=[jax.ShapeDtypeStruct((nb, rows, LANES), F32), jax.ShapeDtypeStruct((nb, KV_W, wb), F32)],
        compiler_params=_cparams(("arbitrary",)),
        name="attn_sample",
    )(page_table, q32, g32, knew, wnew, wnew_t, win_t, kvc, bcs, bsl, bnew, bw, msel, cache_t)


def _prep_weights(w_in, w_cmp1, w_cmp2, pe_cmp, w_conv, w_nsa_out):
    d = w_in.shape[0]
    sizes = (N_HEADS * HEAD_DIM, KV_W, KV_W, KV_W, 3 * N_HEADS, _C_W, _C_W, _C_W, 2 * d)
    offs = [0]
    for sz in sizes:
        offs.append(offs[-1] + sz)
    part = lambda k: w_in[:, offs[k]:offs[k + 1]]
    qp = part(0).reshape(d, H_KV, GROUP, HEAD_DIM).transpose(0, 2, 1, 3).reshape(d, N_HEADS * HEAD_DIM)
    gn = jnp.pad(part(4), ((0, 0), (0, _G_W - 3 * N_HEADS)))
    w_in_p = jnp.concatenate([qp, part(1), part(2), part(3), gn, part(5), part(6), part(7), part(8)],
                             axis=1).astype(BF16)
    w_nsa_p = w_nsa_out.reshape(H_KV, GROUP, HEAD_DIM, -1).transpose(1, 0, 2, 3).reshape(
        N_HEADS * HEAD_DIM, -1).astype(BF16)
    r = CMP_BLOCK // CMP_STRIDE
    w1r = w_cmp1.reshape(2, r, CMP_STRIDE // 2, 2, HEAD_DIM, CMP_HIDDEN)
    w1t = jnp.transpose(w1r, (0, 2, 3, 4, 1, 5))
    w1e = w1t[:, :, :, :, :, None, :].astype(BF16)
    z1 = jnp.zeros_like(w1e)
    wb = jnp.stack([jnp.concatenate([w1e if gp == g else z1 for gp in range(H_KV)], axis=5)
                    for g in range(H_KV)], axis=3)
    wbig = wb.reshape(2, CMP_STRIDE // 2, 2 * KV_HALF, CMP_OUT_W)
    z2 = jnp.zeros((CMP_HIDDEN, HEAD_DIM), F32)
    w2bd = jnp.concatenate(
        [jnp.concatenate([w_cmp2[c] if (cp, gp) == (c, g) else z2 for cp in range(2) for gp in range(H_KV)], axis=1)
         for c in range(2) for g in range(H_KV)], axis=0).astype(BF16)
    pe_r = pe_cmp.reshape(2, r, CMP_STRIDE, HEAD_DIM).transpose(1, 2, 0, 3)
    pe_rows = jnp.broadcast_to(pe_r[:, :, :, None, :], (r, CMP_STRIDE, 2, H_KV, HEAD_DIM)).reshape(r, CHUNK_W)
    pe8 = jnp.pad(pe_rows, ((0, 8 - r), (0, 0)))
    w_conv8 = jnp.pad(w_conv, ((0, 8 - CONV_WIDTH), (0, 0)))
    return w_in_p, w_nsa_p, wbig, w2bd, pe8, w_conv8


def _sel_matrix(n_c, nselp):
    n = jnp.arange(n_c + 1)[:, None]
    j = jnp.arange(nselp)[None, :]
    return _sel_weights(n - (SEL_BLOCK // CMP_STRIDE) * j).astype(BF16)


def kernel(x_prompt, x_sample, cache_kv_cmp, cache_kv_sel, state_kv_win, state_conv, page_table,
           c_prompt, c_sample, w_ada, b_ada, g_norm, w_ffn1_gu, w_ffn1_down, w_ffn2_gu, w_ffn2_down,
           w_in, w_cmp1, w_cmp2, pe_cmp, w_conv, w_nsa_out, w_conv_out, w_out, rel_bias, g_final):
    assert w_ada.shape[0] == 1, "single-layer trunk"
    nbp, t, d = x_prompt.shape
    nbs, s_len, _ = x_sample.shape
    n_pages = page_table.shape[1]
    page = cache_kv_cmp.shape[2]
    past = n_pages * page
    n_phys = cache_kv_cmp.shape[1]
    wb = state_kv_win.shape[2]
    assert wb == WINDOW and past % SAMPLE_CK == 0

    w_in_p, w_nsa_p, wbig, w2bd, pe8, w_conv8 = _prep_weights(
        w_in[0], w_cmp1[0], w_cmp2[0], pe_cmp[0], w_conv[0], w_nsa_out[0])
    wgu1, wd1 = w_ffn1_gu[0].astype(BF16), w_ffn1_down[0].astype(BF16)
    wgu2, wd2 = w_ffn2_gu[0].astype(BF16), w_ffn2_down[0].astype(BF16)
    w_cv, w_o = w_conv_out[0].astype(BF16), w_out[0].astype(BF16)
    gn = [g_norm[0][k:k + 1] for k in range(N_SUB)]

    n_c_rows = nbp + nbs
    c_all = jnp.pad(jnp.concatenate([c_prompt, c_sample], axis=0), ((0, (-n_c_rows) % 8), (0, 0)))
    mod_all = _ada(c_all, w_ada[0], b_ada[0])
    mod_p = mod_all[:nbp].reshape(nbp * 3 * N_SUB, 1, d)
    mod_s = jnp.transpose(jnp.repeat(mod_all[nbp:n_c_rows].reshape(nbs, 3 * N_SUB, d), s_len, axis=0), (1, 0, 2))

    tm = 512 if t % 512 == 0 else t
    tpb = t // tm
    xp = x_prompt.reshape(nbp * t, d)
    h1 = _ffn(xp, mod_p, 0, gn[0], wgu1, wd1, tm, tpb)
    (q, kc_k, kc_v, kvc_t, kvs_t, kvw_t, ks0, ks1, vs0, vs1, kw_k, vw0, vw1, gates, z, cb, gm) = _inproj(
        h1, mod_p, gn[1], w_in_p, tm, tpb, True)
    kc2 = _cmp_prompt(kc_k, kc_v, t, wbig, pe8, w2bd)
    tiles_p = _bias_prompt(rel_bias, t // CMP_STRIDE)
    o_nsa = _attn_prompt(q, gates, (ks0, ks1), (vs0, vs1), kw_k, (vw0, vw1), kc2, tiles_p, nbp, t)
    h2 = _mixout(h1, o_nsa, z, None, cb, gm, mod_p, w_conv8, w_nsa_p, w_cv, w_o, tm, tpb)
    y_prompt = _ffn(h2, mod_p, 2, gn[2], wgu2, wd2, tm, tpb, g_final).reshape(nbp, t, d)

    kv_shape = (2, H_KV, HEAD_DIM)
    kv_out = lambda a: jnp.transpose(a.reshape((1, nbp) + kv_shape + (a.shape[-1],)), (0, 1, 5, 2, 3, 4))
    kv_cmp_p = kv_out(kvc_t)
    kv_sel_p = kv_out(kvs_t)
    keep = min(WINDOW, t)
    kv_win_p = kv_out(kvw_t[:, :, t - keep:])
    conv_p = z.reshape(1, nbp, t, -1)[:, :, t - (CONV_WIDTH - 1):]

    ns = nbs * s_len
    xs = x_sample.reshape(ns, d)
    h1s = _ffn(xs, mod_s, 0, gn[0], wgu1, wd1, ns, 1)
    qs, kvc_s, kvs_s, kvw_s, kvs_sb, kvw_sb, gates_s, z_s, cb_s, gm_s = _inproj(
        h1s, mod_s, gn[1], w_in_p, ns, 1, False)

    pos_minor = lambda a: jnp.transpose(a, (0, 2, 3, 4, 1)).reshape(a.shape[0], KV_W, a.shape[1])
    kvc_past = _cmp_sample(page_table, pos_minor(cache_kv_cmp[0]), wbig, pe8, w2bd)
    n_c = (past + s_len - CMP_BLOCK) // CMP_STRIDE + 1
    nsel = -(-(past + s_len) // SEL_BLOCK)
    nselp = -(-nsel // LANES) * LANES
    rows = N_HEADS * s_len
    head_of_row = [g * GROUP + r for r in range(GROUP) for g in range(H_KV) for _ in range(s_len)]
    tbl_rows = jnp.pad(rel_bias.T[jnp.array(head_of_row)], ((0, 0), (0, LANES - N_BUCKETS)))
    tiles_s = _bias_sample(tbl_rows, past, s_len, past // CMP_STRIDE, SAMPLE_CK, wb)
    msel = _sel_matrix(past // CMP_STRIDE - 1, nselp)

    q5 = qs.reshape(nbs, s_len, GROUP, H_KV, HEAD_DIM).transpose(0, 2, 3, 1, 4)
    q32 = jnp.zeros((nbs, GROUP, H_KV, s_len, H_KV, HEAD_DIM), BF16)
    for g in range(H_KV):
        q32 = q32.at[:, :, g, :, g, :].set(q5[:, :, g])
    q32 = q32.reshape(nbs, rows, LANES)
    g5 = gates_s[:, :3 * N_HEADS].reshape(nbs, s_len, 3, H_KV, GROUP).transpose(0, 4, 3, 1, 2)
    g32 = jnp.pad(g5.reshape(nbs, rows, 3), ((0, 0), (0, 0), (0, LANES - 3)))
    pad_new = lambda a: jnp.pad(a.reshape(nbs, s_len, KV_W), ((0, 0), (0, LANES - s_len), (0, 0)))
    wnew_t = jnp.pad(jnp.transpose(kvw_s.reshape(nbs, s_len, KV_W), (0, 2, 1)), ((0, 0), (0, 0), (0, LANES - s_len)))
    o32, win_next = _attn_sample(page_table, q32, g32, pad_new(kvs_sb), pad_new(kvw_sb), wnew_t,
                                 pos_minor(state_kv_win[0]), kvc_past, tiles_s, msel,
                                 pos_minor(cache_kv_sel[0]), past, s_len)
    o6 = o32.reshape(nbs, GROUP, H_KV, s_len, H_KV, HEAD_DIM)
    o_s = jnp.stack([o6[:, :, g, :, g, :] for g in range(H_KV)], axis=3)
    o_nsa_s = o_s.transpose(0, 2, 1, 3, 4).reshape(ns, N_HEADS * HEAD_DIM).astype(BF16)

    full = jnp.concatenate([state_conv[0], z_s.reshape(nbs, s_len, -1)], axis=1)
    z_shift = (full[:, 1:1 + s_len].reshape(ns, -1), full[:, 0:s_len].reshape(ns, -1))
    h2s = _mixout(h1s, o_nsa_s, z_s, z_shift, cb_s, gm_s, mod_s, w_conv8, w_nsa_p, w_cv, w_o, ns, 1)
    y_sample = _ffn(h2s, mod_s, 2, gn[2], wgu2, wd2, ns, 1, g_final).reshape(nbs, s_len, d)

    kv_cmp_s = kvc_s.reshape((1, nbs, s_len) + kv_shape)
    kv_sel_s = kvs_s.reshape((1, nbs, s_len) + kv_shape)
    kv_win_s = jnp.transpose(win_next.reshape((1, nbs) + kv_shape + (wb,)), (0, 1, 5, 2, 3, 4))
    conv_s = full[None, :, s_len:]
    return (y_prompt, y_sample, kv_cmp_p, kv_sel_p, kv_win_p, conv_p, kv_cmp_s, kv_sel_s, kv_win_s, conv_s)
```

```python
import functools
import math

import jax
import jax.numpy as jnp
from jax import lax
from jax.experimental import pallas as pl
from jax.experimental.pallas import tpu as pltpu

F32 = jnp.float32
BF16 = jnp.bfloat16

HEAD_DIM = 64
N_HEADS = 8
H_KV = 2
GROUP = N_HEADS // H_KV
CMP_BLOCK = 32
CMP_STRIDE = 16
CMP_HIDDEN = 2 * HEAD_DIM
SEL_BLOCK = 64
TOP_N = 16
WINDOW = 512
CONV_WIDTH = 3
N_BUCKETS = 32
MAX_DISTANCE = 128
N_SUB = 3
EPS = 1e-6
NEG = -1e9
LOG2E = math.log2(math.e)
FORCE_BONUS = 1e3
KV_W = 2 * H_KV * HEAD_DIM
KV_HALF = H_KV * HEAD_DIM
LANES = 128
Q_TILE = 256
CMP_NEAR = 32
VMEM_LIMIT = 56 * 1024 * 1024


def _bucket_thresholds():
    max_exact = N_BUCKETS // 2

    def bucket(d):
        if d < max_exact:
            return d
        large = max_exact + int(math.log(d / max_exact) / math.log(MAX_DISTANCE / max_exact)
                                * (N_BUCKETS - max_exact))
        return min(large, N_BUCKETS - 1)

    thr, d = [], 0
    for b in range(N_BUCKETS):
        while bucket(d) < b:
            d += 1
        thr.append(d)
    return tuple(thr)


BUCKET_THR = _bucket_thresholds()
FAR_DIST = BUCKET_THR[-1]


def _cparams(sem):
    return pltpu.CompilerParams(dimension_semantics=sem, vmem_limit_bytes=VMEM_LIMIT)


def _resident(shape):
    nd = len(shape)
    return pl.BlockSpec(shape, lambda *_: (0,) * nd, pipeline_mode=pl.Buffered(1))


def _dot(a, b):
    return jnp.dot(a, b, preferred_element_type=F32)


def _dot_nt(a, b):
    return lax.dot_general(a, b, (((1,), (1,)), ((), ())), preferred_element_type=F32)


def _split3(x):
    h1 = x.astype(BF16)
    r1 = x - h1.astype(F32)
    h2 = r1.astype(BF16)
    h3 = (r1 - h2.astype(F32)).astype(BF16)
    return h1, h2, h3


def _modulated_norm(x, g, shift, scale):
    y = x * lax.rsqrt(jnp.mean(x * x, axis=-1, keepdims=True) + EPS)
    return (y * g) * (1.0 + scale) + shift


def _ada_kernel(c_ref, w_ref, b_ref, o_ref):
    c = c_ref[...]
    a = (c * jax.nn.sigmoid(c)).astype(BF16)
    o_ref[...] = _dot(a, w_ref[...].astype(BF16)) + b_ref[...]


def _ada(c_all, w_ada, b_ada):
    rows, d = c_all.shape
    n = w_ada.shape[1]
    tn = n // 8
    return pl.pallas_call(
        _ada_kernel,
        grid=(n // tn,),
        in_specs=[pl.BlockSpec((rows, d), lambda j: (0, 0)),
                  pl.BlockSpec((d, tn), lambda j: (0, j)),
                  pl.BlockSpec((1, tn), lambda j: (0, j))],
        out_specs=pl.BlockSpec((rows, tn), lambda j: (0, j)),
        out_shape=jax.ShapeDtypeStruct((rows, n), F32),
        compiler_params=_cparams(("arbitrary",)),
        name="ada",
    )(c_all, w_ada, b_ada.reshape(1, n))


def _ffn_kernel(*refs, d_ff, fc, final_norm):
    if final_norm:
        x_ref, sh_ref, sc_ref, gt_ref, gn_ref, wgu_ref, wd_ref, gf_ref, o_ref = refs
    else:
        x_ref, sh_ref, sc_ref, gt_ref, gn_ref, wgu_ref, wd_ref, o_ref = refs
    x = x_ref[...]
    ub = _modulated_norm(x, gn_ref[...], sh_ref[...], sc_ref[...]).astype(BF16)
    acc = jnp.zeros(x.shape, F32)
    for c in range(d_ff // fc):
        g = _dot(ub, wgu_ref[:, c * fc:(c + 1) * fc])
        v = _dot(ub, wgu_ref[:, d_ff + c * fc:d_ff + (c + 1) * fc])
        a = ((g * jax.nn.sigmoid(g)) * v).astype(BF16)
        acc = acc + _dot(a, wd_ref[c * fc:(c + 1) * fc, :])
    h = x + (0.5 * gt_ref[...]) * acc
    if final_norm:
        h = (h * lax.rsqrt(jnp.mean(h * h, axis=-1, keepdims=True) + EPS)) * gf_ref[...]
    o_ref[...] = h


def _mod_specs(mod, ks, tm, tiles_per_batch):
    if mod.ndim == 3 and mod.shape[1] == 1:
        d = mod.shape[-1]
        specs = [pl.BlockSpec((None, 1, d), lambda i, k=k: ((i // tiles_per_batch) * (3 * N_SUB) + k, 0, 0))
                 for k in ks]
        return specs, [mod] * len(ks)
    d = mod.shape[-1]
    specs = [pl.BlockSpec((None, tm, d), lambda i, k=k: (k, i, 0)) for k in ks]
    return specs, [mod] * len(ks)


def _ffn(x, mod, sub, gn_row, w_gu, w_down, tm, tiles_per_batch, g_final=None):
    n, d = x.shape
    d_ff = w_down.shape[0]
    fc = d_ff // 2 if (d_ff // 2) % LANES == 0 else d_ff
    final_norm = g_final is not None
    mspecs, mops = _mod_specs(mod, (3 * sub, 3 * sub + 1, 3 * sub + 2), tm, tiles_per_batch)
    in_specs = [pl.BlockSpec((tm, d), lambda i: (i, 0))] + mspecs + [
        pl.BlockSpec((1, d), lambda i: (0, 0)), _resident(w_gu.shape), _resident(w_down.shape)]
    ops = [x] + mops + [gn_row, w_gu, w_down]
    if final_norm:
        in_specs.append(pl.BlockSpec((1, d), lambda i: (0, 0)))
        ops.append(g_final.reshape(1, d))
    return pl.pallas_call(
        functools.partial(_ffn_kernel, d_ff=d_ff, fc=fc, final_norm=final_norm),
        grid=(n // tm,),
        in_specs=in_specs,
        out_specs=pl.BlockSpec((tm, d), lambda i: (i, 0)),
        out_shape=jax.ShapeDtypeStruct((n, d), F32),
        compiler_params=_cparams(("arbitrary",)),
        name="ffn_final" if final_norm else "ffn",
    )(*ops)


_Q_W = N_HEADS * HEAD_DIM
_G_W = LANES
_C_W = 512
_SEG = {}
_off = 0
for _name, _w in (("q", _Q_W), ("kc", KV_W), ("ks", KV_W), ("kw", KV_W), ("gn", _G_W),
                  ("ch", _C_W), ("cb", _C_W), ("cc", _C_W)):
    _SEG[_name] = (_off, _off + _w)
    _off += _w
_MG_OFF = _off


V_ROWS = HEAD_DIM + 16


def _aug_lane(g):
    return HEAD_DIM if g == 0 else 0


def _inproj_kernel(*refs, prompt):
    x_ref, sh_ref, sc_ref, gn_ref, w_ref = refs[:5]
    ub = _modulated_norm(x_ref[...], gn_ref[...], sh_ref[...], sc_ref[...]).astype(BF16)

    def seg(name):
        lo, hi = _SEG[name]
        return _dot(ub, w_ref[:, lo:hi])

    if prompt:
        (q_ref, kck_ref, kcv_ref, kct_ref, kst_ref, kwt_ref, ks0_ref, ks1_ref, vs0_ref, vs1_ref, kwk_ref,
         vw0_ref, vw1_ref, g_ref, z_ref, cb_ref, gm_ref) = refs[5:]
        kc = seg("kc")
        kck_ref[...] = kc[:, 0:KV_HALF]
        kcv_ref[...] = kc[:, KV_HALF:KV_W]
        kct_ref[...] = kc.T
        tm = kc.shape[0]
        lane = lax.broadcasted_iota(jnp.int32, (tm, KV_HALF), 1)
        blk = lax.shift_right_logical(lax.broadcasted_iota(jnp.int32, (tm, KV_HALF), 0),
                                      int(math.log2(SEL_BLOCK))) & (Q_TILE // SEL_BLOCK - 1)
        ones_rows = jnp.where(lax.broadcasted_iota(jnp.int32, (V_ROWS - HEAD_DIM, tm), 0) == 0, 1.0, 0.0)
        for name, t_ref, k_refs, vt_refs in (("ks", kst_ref, (ks0_ref, ks1_ref), (vs0_ref, vs1_ref)),
                                             ("kw", kwt_ref, (kwk_ref,), (vw0_ref, vw1_ref))):
            kv = seg(name)
            kv_t = kv.T
            t_ref[...] = kv_t
            kk = kv[:, 0:KV_HALF]
            if len(k_refs) == 1:
                k_refs[0][...] = kk.astype(BF16)
            else:
                for g, k_ref in enumerate(k_refs):
                    own = (lane < HEAD_DIM) if g == 0 else (lane >= HEAD_DIM)
                    onehot = jnp.where(lane - _aug_lane(g) == blk, 1.0, 0.0)
                    k_ref[...] = jnp.where(own, kk, onehot).astype(BF16)
            for g, vt_ref in enumerate(vt_refs):
                vg = jnp.concatenate(
                    [kv_t[KV_HALF + g * HEAD_DIM:KV_HALF + (g + 1) * HEAD_DIM, :], ones_rows], axis=0)
                for c in range(vt_ref.shape[0]):
                    vt_ref[c] = vg[:, c * Q_TILE:(c + 1) * Q_TILE].astype(BF16)
    else:
        q_ref, kc_ref, ks_ref, kw_ref, ksb_ref, kwb_ref, g_ref, z_ref, cb_ref, gm_ref = refs[5:]
        kc_ref[...] = seg("kc")
        ks = seg("ks")
        ks_ref[...] = ks
        ksb_ref[...] = ks.astype(BF16)
        kw = seg("kw")
        kw_ref[...] = kw
        kwb_ref[...] = kw.astype(BF16)
    q_ref[...] = (seg("q") * (HEAD_DIM ** -0.5 * LOG2E)).astype(BF16)
    g_ref[...] = jax.nn.sigmoid(seg("gn"))
    z_ref[...] = seg("cc") * seg("ch")
    cb_ref[...] = seg("cb").astype(cb_ref.dtype)
    d2 = gm_ref.shape[1]
    half = d2 // 2
    for c in range(2):
        gm_ref[:, c * half:(c + 1) * half] = jax.nn.sigmoid(
            _dot(ub, w_ref[:, _MG_OFF + c * half:_MG_OFF + (c + 1) * half])).astype(gm_ref.dtype)


def _inproj(h, mod, gn_row, w_in_p, tm, tiles_per_batch, prompt):
    n, d = h.shape
    d_conv = _C_W
    mspecs, mops = _mod_specs(mod, (3, 4), tm, tiles_per_batch)
    rows = lambda w, dt: (pl.BlockSpec((tm, w), lambda i: (i, 0)), jax.ShapeDtypeStruct((n, w), dt))
    tail = [rows(_G_W, F32), rows(d_conv, F32), rows(d_conv, BF16), rows(2 * d, BF16)]
    if prompt:
        nb = n // (tm * tiles_per_batch)
        t = tm * tiles_per_batch
        cpt = tm // Q_TILE
        tr = (pl.BlockSpec((None, KV_W, tm), lambda i: (i // tiles_per_batch, 0, i % tiles_per_batch)),
              jax.ShapeDtypeStruct((nb, KV_W, t), F32))
        vt = (pl.BlockSpec((cpt, V_ROWS, Q_TILE), lambda i: (i, 0, 0)),
              jax.ShapeDtypeStruct((n // Q_TILE, V_ROWS, Q_TILE), BF16))
        kr = rows(KV_HALF, BF16)
        outs = [rows(_Q_W, BF16), rows(KV_HALF, F32), rows(KV_HALF, F32), tr, tr, tr,
                kr, kr, vt, vt, kr, vt, vt] + tail
    else:
        outs = [rows(_Q_W, BF16), rows(KV_W, F32), rows(KV_W, F32), rows(KV_W, F32),
                rows(KV_W, BF16), rows(KV_W, BF16)] + tail
    return pl.pallas_call(
        functools.partial(_inproj_kernel, prompt=prompt),
        grid=(n // tm,),
        in_specs=[pl.BlockSpec((tm, d), lambda i: (i, 0))] + mspecs + [
            pl.BlockSpec((1, d), lambda i: (0, 0)), _resident(w_in_p.shape)],
        out_specs=[o[0] for o in outs],
        out_shape=[o[1] for o in outs],
        compiler_params=_cparams(("arbitrary",)),
        name="inproj_prompt" if prompt else "inproj",
    )(h, *mops, gn_row, w_in_p)


def _mixout_kernel(*refs, halo, tiles_per_batch):
    if halo:
        (h_ref, o_ref, z_ref, zp_ref, cb_ref, gm_ref, g2_ref, wc_ref,
         wn_ref, wcv_ref, wo_ref, out_ref) = refs
        z = z_ref[...]
        tm = z.shape[0]
        first = (pl.program_id(0) % tiles_per_batch) == 0
        prev = jnp.where(first, 0.0, zp_ref[...])
        row = lax.broadcasted_iota(jnp.int32, z.shape, 0)
        zm1 = jnp.where(row == 0, prev[7:8, :], pltpu.roll(z, 1, 0))
        zm2 = jnp.where(row == 0, prev[6:7, :], jnp.where(row == 1, prev[7:8, :], pltpu.roll(z, 2, 0)))
    else:
        (h_ref, o_ref, z_ref, zm1_ref, zm2_ref, cb_ref, gm_ref, g2_ref, wc_ref,
         wn_ref, wcv_ref, wo_ref, out_ref) = refs
        z, zm1, zm2 = z_ref[...], zm1_ref[...], zm2_ref[...]
    conv = wc_ref[0:1, :] * zm2 + wc_ref[1:2, :] * zm1 + wc_ref[2:3, :] * z
    y = (cb_ref[...].astype(F32) * conv).astype(BF16)
    d = h_ref.shape[1]
    merged = gm_ref[:, 0:d].astype(F32) * _dot(o_ref[...], wn_ref[...]) \
        + gm_ref[:, d:2 * d].astype(F32) * _dot(y, wcv_ref[...])
    out_ref[...] = h_ref[...] + g2_ref[...] * _dot(merged.astype(BF16), wo_ref[...])


def _mixout(h, o_nsa, z, z_shift, cb, gm, mod, w_conv8, w_nsa_p, w_cv, w_o, tm, tiles_per_batch):
    n, d = h.shape
    dc = z.shape[1]
    halo = z_shift is None
    mspecs, mops = _mod_specs(mod, (5,), tm, tiles_per_batch)
    tok = lambda w: pl.BlockSpec((tm, w), lambda i: (i, 0))
    if halo:
        zspecs = [tok(dc), pl.BlockSpec((8, dc), lambda i: (jnp.maximum(i * (tm // 8) - 1, 0), 0))]
        zops = [z, z]
    else:
        zspecs = [tok(dc), tok(dc), tok(dc)]
        zops = [z, z_shift[0], z_shift[1]]
    return pl.pallas_call(
        functools.partial(_mixout_kernel, halo=halo, tiles_per_batch=tiles_per_batch),
        grid=(n // tm,),
        in_specs=[tok(d), tok(o_nsa.shape[1])] + zspecs + [tok(dc), tok(2 * d)] + mspecs + [
            pl.BlockSpec((8, dc), lambda i: (0, 0)),
            _resident(w_nsa_p.shape), _resident(w_cv.shape), _resident(w_o.shape)],
        out_specs=tok(d),
        out_shape=jax.ShapeDtypeStruct((n, d), F32),
        compiler_params=_cparams(("arbitrary",)),
        name="mixout",
    )(h, o_nsa, *zops, cb, gm, *mops, w_conv8, w_nsa_p, w_cv, w_o)


CHUNK_W = CMP_STRIDE * KV_W
CMP_OUT_W = 2 * H_KV * CMP_HIDDEN


def _gelu_tanh(x):
    return x * (0.5 * (1.0 + jnp.tanh(math.sqrt(2.0 / math.pi) * (x + 0.044715 * (x * x * x)))))


def _chunk_partials(tile, wc_ref, after_step=None):
    accs = []
    for c in range(2):
        acc = None
        for lp in range(CMP_STRIDE // 2):
            x2 = jnp.concatenate([tile(2 * lp, c), tile(2 * lp + 1, c)], axis=1)
            part = _dot(x2, wc_ref[c, lp])
            acc = part if acc is None else acc + part
            if after_step is not None:
                after_step(c, lp)
        accs.append(acc)
    hw = CMP_OUT_W // 2
    return jnp.concatenate([accs[0][:, 0:hw], accs[1][:, 0:hw], accs[0][:, hw:], accs[1][:, hw:]], axis=1)


def _compress_tail(p, wc_ref, pe_ref, w2_ref):
    n_ch = p.shape[0]
    pb = _chunk_partials(
        lambda l, c: pe_ref[:, l * KV_W + c * KV_HALF:l * KV_W + (c + 1) * KV_HALF].astype(BF16), wc_ref)
    pre = (pb[0:1, 0:CMP_OUT_W] + pb[1:2, CMP_OUT_W:]) + p[:, 0:CMP_OUT_W] \
        + pltpu.roll(p[:, CMP_OUT_W:], n_ch - 1, 0)
    out = _dot(_gelu_tanh(pre).astype(BF16), w2_ref[...])
    row = lax.broadcasted_iota(jnp.int32, out.shape, 0)
    return jnp.where(row < n_ch - 1, out, 0.0)


def _cmp_prompt_kernel(xk_ref, xv_ref, wc_ref, pe_ref, w2_ref, o_ref):
    xs = (xk_ref, xv_ref)
    n_ch = xk_ref.shape[0] // CMP_STRIDE
    p = _chunk_partials(lambda l, c: xs[c][pl.ds(l, n_ch, stride=CMP_STRIDE), :].astype(BF16), wc_ref)
    out = _compress_tail(p, wc_ref, pe_ref, w2_ref).astype(BF16)
    o_ref[0:n_ch, :] = out
    o_ref[n_ch:2 * n_ch, :] = out


def _cmp_prompt(rows_k, rows_v, t, wbig, pe8, w2bd):
    nb = rows_k.shape[0] // t
    n_ch = t // CMP_STRIDE
    half = pl.BlockSpec((t, KV_HALF), lambda b: (b, 0))
    return pl.pallas_call(
        _cmp_prompt_kernel,
        grid=(nb,),
        in_specs=[half, half, _resident(wbig.shape), _resident(pe8.shape), _resident(w2bd.shape)],
        out_specs=pl.BlockSpec((None, 2 * n_ch, KV_W), lambda b: (b, 0, 0)),
        out_shape=jax.ShapeDtypeStruct((nb, 2 * n_ch, KV_W), BF16),
        compiler_params=_cparams(("arbitrary",)),
        name="cmp_prompt",
    )(rows_k, rows_v, wbig, pe8, w2bd)


def _page_copies(cache_ref, pt_ref, b, dst, sem, n_pages, page):
    return [pltpu.make_async_copy(cache_ref.at[pt_ref[b, p]], dst.at[:, pl.ds(p * page, page)], sem)
            for p in range(n_pages)]


def _gather_pages(cache_ref, pt_ref, buf, sem, n_pages, page):
    b = pl.program_id(0)
    nb = pl.num_programs(0)
    slot = b % 2
    copies = lambda bb, sl: _page_copies(cache_ref, pt_ref, bb, buf.at[sl], sem.at[sl], n_pages, page)

    @pl.when(b == 0)
    def _():
        for c in copies(0, 0):
            c.start()

    @pl.when(b + 1 < nb)
    def _():
        for c in copies(b + 1, 1 - slot):
            c.start()

    for c in copies(b, slot):
        c.wait()
    return slot


def _cmp_sample_kernel(pt_ref, cache_ref, wc_ref, pe_ref, w2_ref, o_ref,
                       buf_a, buf_b, xk_a, xv_a, xk_b, xv_b, sem, *, n_pages, page, nb):
    b = pl.program_id(0)
    bufs = (buf_a, buf_b)
    xs = ((xk_a, xv_a), (xk_b, xv_b))
    past = xk_a.shape[0]
    n_ch = past // CMP_STRIDE
    n_slabs = CMP_STRIDE // 2
    slab = past // n_slabs
    copies = lambda bb, par: _page_copies(cache_ref, pt_ref, bb, bufs[par], sem.at[par], n_pages, page)

    def transpose_slab(par, c, j):
        xs[par][c][j * slab:(j + 1) * slab, :] = bufs[par][c * KV_HALF:(c + 1) * KV_HALF, j * slab:(j + 1) * slab].T

    @pl.when(b == 0)
    def _():
        for cp in copies(0, 0):
            cp.start()
        if nb > 1:
            for cp in copies(1, 1):
                cp.start()
        for cp in copies(0, 0):
            cp.wait()
        for c in range(2):
            for j in range(n_slabs):
                transpose_slab(0, c, j)

    def step(par):
        nxt = 1 - par
        hook = None
        if nb > 1:
            @pl.when(b + 1 < nb)
            def _():
                for cp in copies(b + 1, nxt):
                    cp.wait()

            @pl.when(b + 2 < nb)
            def _():
                for cp in copies(b + 2, par):
                    cp.start()

            hook = lambda c, lp: transpose_slab(nxt, c, lp)
        p = _chunk_partials(lambda l, c: xs[par][c][pl.ds(l, n_ch, stride=CMP_STRIDE), :].astype(BF16), wc_ref, hook)
        o_ref[...] = _compress_tail(p, wc_ref, pe_ref, w2_ref).astype(BF16)

    for par in range(2):
        pl.when(b % 2 == par)(functools.partial(step, par))


def _cmp_sample(page_table, cache_t, wbig, pe8, w2bd):
    nb, n_pages = page_table.shape
    page = cache_t.shape[2]
    past = n_pages * page
    n_ch = past // CMP_STRIDE
    return pl.pallas_call(
        functools.partial(_cmp_sample_kernel, n_pages=n_pages, page=page, nb=nb),
        grid_spec=pltpu.PrefetchScalarGridSpec(
            num_scalar_prefetch=1,
            grid=(nb,),
            in_specs=[pl.BlockSpec(memory_space=pl.ANY),
                      _resident_sp(wbig.shape), _resident_sp(pe8.shape), _resident_sp(w2bd.shape)],
            out_specs=pl.BlockSpec((None, n_ch, KV_W), lambda b, pt: (b, 0, 0)),
            scratch_shapes=[pltpu.VMEM((KV_W, past), F32)] * 2 + [pltpu.VMEM((past, KV_HALF), F32)] * 4
            + [pltpu.SemaphoreType.DMA((2,))]),
        out_shape=jax.ShapeDtypeStruct((nb, n_ch, KV_W), BF16),
        compiler_params=_cparams(("arbitrary",)),
        name="cmp_sample",
    )(page_table, cache_t, wbig, pe8, w2bd)


def _resident_sp(shape):
    nd = len(shape)
    return pl.BlockSpec(shape, lambda *_: (0,) * nd, pipeline_mode=pl.Buffered(1))


def _bias_values(d, table):
    acc = jnp.zeros(d.shape, F32) + table(0)
    for b in range(1, N_BUCKETS):
        acc = jnp.where(d >= BUCKET_THR[b], table(b), acc)
    return jnp.where(d >= 0, (acc - table(N_BUCKETS - 1)) * LOG2E, NEG)


CMP_BACK = 16


def _bias_prompt_kernel(tbl_ref, bc_ref, t0_ref, t1_ref):
    h = pl.program_id(0)
    table = lambda b: tbl_ref[b, h]
    ncp, tq = bc_ref.shape
    kj = lax.broadcasted_iota(jnp.int32, (tq, tq), 0)
    qi = lax.broadcasted_iota(jnp.int32, (tq, tq), 1)
    t0_ref[...] = _bias_values(qi - kj, table)
    t1_ref[...] = _bias_values(qi - kj + tq, table)
    c = lax.broadcasted_iota(jnp.int32, (ncp, tq), 0)
    qi = lax.broadcasted_iota(jnp.int32, (ncp, tq), 1)
    d = qi - CMP_STRIDE * c + (CMP_STRIDE * CMP_BACK - (CMP_BLOCK - 1))
    bc_ref[...] = jnp.where(c < CMP_NEAR, _bias_values(d, table), 0.0)


def _bias_prompt(rel_bias, ncp):
    tq = Q_TILE
    return pl.pallas_call(
        _bias_prompt_kernel,
        grid=(N_HEADS,),
        in_specs=[pl.BlockSpec(memory_space=pltpu.SMEM)],
        out_specs=[pl.BlockSpec((None, ncp, tq), lambda h: (h, 0, 0)),
                   pl.BlockSpec((None, tq, tq), lambda h: (h, 0, 0)),
                   pl.BlockSpec((None, tq, tq), lambda h: (h, 0, 0))],
        out_shape=[jax.ShapeDtypeStruct((N_HEADS, ncp, tq), F32),
                   jax.ShapeDtypeStruct((N_HEADS, tq, tq), F32),
                   jax.ShapeDtypeStruct((N_HEADS, tq, tq), F32)],
        compiler_params=_cparams(("arbitrary",)),
        name="bias_prompt",
    )(rel_bias)


def _bias_sample_kernel(tbl_ref, bcs_ref, bsl_ref, bnew_ref, bw_ref, *, past, s_len, n_c):
    table = lambda b: tbl_ref[:, b:b + 1]

    def qpos(shape):
        return past + lax.rem(lax.broadcasted_iota(jnp.int32, shape, 0), s_len)

    def lane(shape):
        return lax.broadcasted_iota(jnp.int32, shape, 1)

    sh = bcs_ref.shape
    n = lane(sh)
    bcs_ref[...] = jnp.where(n < n_c, _bias_values(qpos(sh) - (CMP_STRIDE * n + CMP_BLOCK - 1), table), NEG)
    sh = bsl_ref.shape
    bsl_ref[...] = _bias_values(qpos(sh) - (past - sh[1] + lane(sh)), table)
    sh = bnew_ref.shape
    bnew_ref[...] = jnp.where(lane(sh) < s_len, _bias_values(qpos(sh) - (past + lane(sh)), table), NEG)
    sh = bw_ref.shape
    d = qpos(sh) - (past - sh[1] + lane(sh))
    bw_ref[...] = jnp.where(d < WINDOW, _bias_values(d, table), NEG)


def _bias_sample(tbl_rows, past, s_len, ncs, ck, wb):
    rows = tbl_rows.shape[0]
    widths = (ncs, ck, LANES, wb)
    return pl.pallas_call(
        functools.partial(_bias_sample_kernel, past=past, s_len=s_len, n_c=(past + s_len - CMP_BLOCK) // CMP_STRIDE + 1),
        out_shape=[jax.ShapeDtypeStruct((rows, w), F32) for w in widths],
        name="bias_sample",
    )(tbl_rows)


MASK_FLOOR = -1e8
M_INIT = -1e30
TAKEN = -3e38
SCORE_AHEAD = 3
FAR_PER_ITER = 3
VALUE_BEHIND = 2


def _sel_weights(v):
    return jnp.where((v >= 0) & (v <= 2), 2.0, jnp.where((v == -1) | (v == 3), 1.0, 0.0))


def _attn_prompt_kernel(q_ref, gate_ref, ks0_ref, ks1_ref, vs0_ref, vs1_ref, kwk_ref, vw0_ref, vw1_ref,
                        kc2_ref, bc_ref, t0_ref, t1_ref, o_ref,
                        qm_ref, selr_ref, oc_ref, ms_ref, as_ref, mw_ref, aw_ref, *, ncp, nsel):
    tq = Q_TILE
    i = pl.program_id(1)
    q0 = i * tq
    lane = lax.broadcasted_iota(jnp.int32, (tq, LANES), 1)
    heads = [(g, r) for g in range(H_KV) for r in range(GROUP)]

    for g, r in heads:
        gmask = (lane < HEAD_DIM) if g == 0 else (lane >= HEAD_DIM)
        qm_ref[g * GROUP + r] = jnp.where(gmask, q_ref[:, r * LANES:(r + 1) * LANES], jnp.zeros((tq, LANES), BF16))

    n0 = (tq // CMP_STRIDE) * i - CMP_BACK
    ws = pl.multiple_of(lax.rem(n0 + ncp, ncp), 8)
    cl = lax.broadcasted_iota(jnp.int32, (ncp, tq), 0)
    dm = jnp.where((cl >= CMP_NEAR) & (cl < ncp - n0), NEG, 0.0) + jnp.where(cl < -n0, NEG, 0.0)
    jrow = lax.broadcasted_iota(jnp.int32, (nsel, ncp), 0)
    nn = lax.broadcasted_iota(jnp.int32, (nsel, ncp), 1) + n0
    nn = jnp.where(nn < 0, nn + ncp, nn)
    nn = jnp.where(nn >= ncp, nn - ncp, nn)
    mit = _sel_weights(nn - (SEL_BLOCK // CMP_STRIDE) * jrow).astype(BF16)

    jr = lax.broadcasted_iota(jnp.int32, (nsel, tq), 0)
    qpos = q0 + lax.broadcasted_iota(jnp.int32, (nsel, tq), 1)
    cur = lax.shift_right_logical(qpos, int(math.log2(SEL_BLOCK)))
    invalid_pen = jnp.where(jr * SEL_BLOCK <= qpos, 0.0, NEG)
    bonus = jnp.where((jr == 0) | (jr == cur) | (jr == cur - 1), FORCE_BONUS, 0.0)
    pj = lax.broadcasted_iota(jnp.int32, (LANES, LANES), 0)
    pl_ = lax.broadcasted_iota(jnp.int32, (LANES, LANES), 1)
    kj = lax.broadcasted_iota(jnp.int32, (tq, tq), 0)
    qi = lax.broadcasted_iota(jnp.int32, (tq, tq), 1)
    wmask = jnp.where(kj > qi, 0.0, NEG)
    blocks_per_chunk = tq // SEL_BLOCK

    kc = kc2_ref[pl.ds(ws, ncp), 0:KV_HALF]
    vct = kc2_ref[pl.ds(ws, ncp), KV_HALF:KV_W].astype(F32).T.astype(BF16)
    def cmp_scores(h):
        return _dot_nt(kc, qm_ref[h]) + bc_ref[h] + dm

    ahead = {h: cmp_scores(h) for h in range(2)}
    psum = [None] * H_KV
    for h in range(N_HEADS):
        s = ahead.pop(h)
        if h + 2 < N_HEADS:
            ahead[h + 2] = cmp_scores(h + 2)
        m = jnp.maximum(jnp.max(s, axis=0, keepdims=True), MASK_FLOOR)
        e = jnp.exp2(s - m)
        p = e * (1.0 / jnp.maximum(jnp.sum(e, axis=0, keepdims=True), 1e-30))
        oc_ref[h] = _dot(vct, p.astype(BF16))
        psum[h // GROUP] = p if psum[h // GROUP] is None else psum[h // GROUP] + p

    scores = []
    for g in range(H_KV):
        h1, h2, h3 = _split3(psum[g])
        pslc = (_dot(mit, h1) + _dot(mit, h2)) + _dot(mit, h3)
        scores.append(jnp.where(invalid_pen < 0.0, NEG, pslc + bonus))

    def take_max(_, carry):
        out = []
        for work, sel_acc in carry:
            mx = jnp.max(work, axis=0, keepdims=True)
            first = jnp.min(jnp.where(work == mx, jr, nsel), axis=0, keepdims=True)
            hit = jr == first
            out.append((jnp.where(hit, TAKEN, work), jnp.where(hit, 1.0, sel_acc)))
        return tuple(out)

    picked = lax.fori_loop(0, min(TOP_N, nsel), take_max,
                           tuple((sc, jnp.zeros((nsel, tq), F32)) for sc in scores))
    for g in range(H_KV):
        sel_t = picked[g][1]
        if nsel < LANES:
            sel_t = jnp.concatenate([sel_t, jnp.zeros((LANES - nsel, tq), F32)], axis=0)
        selr_ref[g] = sel_t.T.astype(BF16)

    def masked_queries(t, g):
        off = pl_ - _aug_lane(g)
        place = jnp.where((pj - t * blocks_per_chunk == off) & (off >= 0) & (off < blocks_per_chunk), 1.0, 0.0)
        picked = _dot(selr_ref[g], place.astype(BF16))
        auglane = (lane >= _aug_lane(g)) & (lane < _aug_lane(g) + blocks_per_chunk)
        pen = jnp.where(auglane, (picked - 1.0) * (-NEG), 0.0).astype(BF16)
        return [qm_ref[g * GROUP + r] + pen for r in range(GROUP)]

    def run(chunks):
        items = []
        for k_of, vt_of, queries, extra, m_ref, a_ref in chunks:
            cache = {}
            items += [(h, k_of, vt_of, queries, extra, m_ref, a_ref, cache) for h in range(N_HEADS)]

        def kv(item):
            g, cache = item[0] // GROUP, item[7]
            if g not in cache:
                cache[g] = (item[1](g), item[2](g))
            return cache[g]

        def scores(item):
            h, queries, extra = item[0], item[3], item[4]
            s = _dot_nt(kv(item)[0], queries(h))
            return s if extra is None else s + extra(h)

        def softmax(item, s):
            h, m_ref = item[0], item[5]
            m_prev = m_ref[h]
            m_new = jnp.maximum(m_prev, jnp.max(s, axis=0, keepdims=True))
            m_ref[h] = m_new
            return jnp.exp2(s - m_new).astype(BF16), jnp.exp2(m_prev - m_new)

        def accumulate(item, p, alpha):
            h, a_ref = item[0], item[6]
            a_ref[h] = alpha * a_ref[h] + _dot(kv(item)[1], p)

        n = len(items)
        s = {j: scores(items[j]) for j in range(SCORE_AHEAD)}
        pa = {}
        for j in range(n):
            pa[j] = softmax(items[j], s.pop(j))
            if j + SCORE_AHEAD < n:
                s[j + SCORE_AHEAD] = scores(items[j + SCORE_AHEAD])
            if j >= VALUE_BEHIND:
                accumulate(items[j - VALUE_BEHIND], *pa.pop(j - VALUE_BEHIND))
        for j in range(n - VALUE_BEHIND, n):
            accumulate(items[j], *pa.pop(j))

    for m_ref, a_ref in ((ms_ref, as_ref), (mw_ref, aw_ref)):
        m_ref[...] = jnp.full(m_ref.shape, M_INIT, F32)
        a_ref[...] = jnp.zeros(a_ref.shape, F32)

    def sel_chunk(t, extra):
        sl = pl.ds(pl.multiple_of(t * tq, tq), tq)
        qs = [masked_queries(t, g) for g in range(H_KV)]
        return (lambda g: (ks0_ref, ks1_ref)[g][sl, :], lambda g: (vs0_ref, vs1_ref)[g][t],
                lambda h: qs[h // GROUP][h % GROUP], extra, ms_ref, as_ref)

    def win_chunk(t, extra):
        sl = pl.ds(pl.multiple_of(t * tq, tq), tq)
        return (lambda g: kwk_ref[sl, :], lambda g: (vw0_ref, vw1_ref)[g][t],
                lambda h: qm_ref[h], extra, mw_ref, aw_ref)

    n_far = jnp.maximum(i - 1, 0)
    n_iter = n_far // FAR_PER_ITER

    def far_group(u, carry):
        run([sel_chunk(FAR_PER_ITER * u + j, None) for j in range(FAR_PER_ITER)])
        return carry

    lax.fori_loop(0, n_iter, far_group, 0)
    for rem in range(1, FAR_PER_ITER):
        @pl.when(n_far - FAR_PER_ITER * n_iter == rem)
        def _(rem=rem):
            run([sel_chunk(FAR_PER_ITER * n_iter + j, None) for j in range(rem)])

    gone1 = jnp.where(i >= 1, 0.0, NEG)
    gone2 = jnp.where(i >= 2, 0.0, NEG)
    wmask2 = wmask + gone2
    run([sel_chunk(jnp.maximum(i - 1, 0), lambda h: t1_ref[h] + gone1),
         sel_chunk(i, lambda h: t0_ref[h]),
         win_chunk(jnp.maximum(i - 2, 0), lambda h: wmask2),
         win_chunk(jnp.maximum(i - 1, 0), lambda h: t1_ref[h] + gone1),
         win_chunk(i, lambda h: t0_ref[h])])

    gates_t = gate_ref[...].T
    for r in range(GROUP):
        per_g = []
        for g in range(H_KV):
            h = g * GROUP + r
            a_s, a_w = as_ref[h], aw_ref[h]
            o_sel = a_s[0:HEAD_DIM] * (1.0 / jnp.maximum(a_s[HEAD_DIM:HEAD_DIM + 1], 1e-30))
            o_win = a_w[0:HEAD_DIM] * (1.0 / jnp.maximum(a_w[HEAD_DIM:HEAD_DIM + 1], 1e-30))
            per_g.append(gates_t[h:h + 1, :] * oc_ref[h, g * HEAD_DIM:(g + 1) * HEAD_DIM, :]
                         + gates_t[N_HEADS + h:N_HEADS + h + 1, :] * o_sel
                         + gates_t[2 * N_HEADS + h:2 * N_HEADS + h + 1, :] * o_win)
        o_ref[:, r * LANES:(r + 1) * LANES] = jnp.concatenate(per_g, axis=0).T.astype(BF16)


def _attn_prompt(q, gates, ks_g, vs_g, kw_k, vw_g, kc2, bias_tiles, nb, t):
    tq = Q_TILE
    assert t % tq == 0 and WINDOW == 2 * tq
    nt = t // tq
    ncp = t // CMP_STRIDE
    nsel = t // SEL_BLOCK
    assert nsel <= LANES and ncp % LANES == 0
    bc, t0, t1 = bias_tiles
    tok = lambda w: pl.BlockSpec((tq, w), lambda b, i: (b * nt + i, 0))
    k_rows = pl.BlockSpec((None, t, KV_HALF), lambda b, i: (b, 0, 0))
    v_cols = pl.BlockSpec((None, nt, V_ROWS, tq), lambda b, i: (b, 0, 0, 0))
    state = [pltpu.VMEM((N_HEADS, 1, tq), F32), pltpu.VMEM((N_HEADS, V_ROWS, tq), F32)]
    kr = lambda a: a.reshape(nb, t, KV_HALF)
    vc = lambda a: a.reshape(nb, nt, V_ROWS, tq)
    return pl.pallas_call(
        functools.partial(_attn_prompt_kernel, ncp=ncp, nsel=nsel),
        grid=(nb, nt),
        in_specs=[tok(q.shape[1]), tok(LANES), k_rows, k_rows, v_cols, v_cols, k_rows, v_cols, v_cols,
                  pl.BlockSpec((None, 2 * ncp, KV_W), lambda b, i: (b, 0, 0)),
                  _resident(bc.shape), _resident(t0.shape), _resident(t1.shape)],
        out_specs=tok(q.shape[1]),
        out_shape=jax.ShapeDtypeStruct(q.shape, BF16),
        scratch_shapes=[pltpu.VMEM((N_HEADS, tq, LANES), BF16), pltpu.VMEM((H_KV, tq, LANES), BF16),
                        pltpu.VMEM((N_HEADS, LANES, tq), F32)] + state + state,
        compiler_params=_cparams(("arbitrary", "arbitrary")),
        name="attn_prompt",
    )(q, gates, kr(ks_g[0]), kr(ks_g[1]), vc(vs_g[0]), vc(vs_g[1]), kr(kw_k), vc(vw_g[0]), vc(vw_g[1]),
      kc2, bc, t0, t1)


SAMPLE_CK = 1024


def _softmax_part(s, v, transposed):
    m = jnp.max(s, axis=-1, keepdims=True)
    p = jnp.exp2(s - m)
    pv = _dot_nt(p.astype(BF16), v) if transposed else _dot(p.astype(BF16), v)
    return m, jnp.sum(p, axis=-1, keepdims=True), pv


def _merge_parts(parts):
    m = parts[0][0]
    for mt, _, _ in parts[1:]:
        m = jnp.maximum(m, mt)
    l, acc = None, None
    for mt, lt, at in parts:
        w = jnp.exp2(mt - m)
        l = w * lt if l is None else l + w * lt
        acc = w * at if acc is None else acc + w * at
    return acc / jnp.maximum(l, 1e-30)


def _attn_sample_kernel(pt_ref, q_ref, g_ref, knew_ref, wnew_ref, wnewt_ref, win_ref, kc_ref,
                        bcs_ref, bsl_ref, bnew_ref, bw_ref, msel_ref, cache_ref, o_ref, wout_ref, buf, sem,
                        *, n_pages, page, past, s_len, nsel):
    slot = _gather_pages(cache_ref, pt_ref, buf, sem, n_pages, page)

    wb = win_ref.shape[1]
    shifted = pltpu.roll(win_ref[...], wb - s_len, 1)
    fresh = pltpu.roll(wnewt_ref[...], LANES - s_len, 1)
    tail_lane = lax.broadcasted_iota(jnp.int32, fresh.shape, 1) >= LANES - s_len
    wout_ref[:, 0:wb - LANES] = shifted[:, 0:wb - LANES]
    wout_ref[:, wb - LANES:wb] = jnp.where(tail_lane, fresh, shifted[:, wb - LANES:wb])
    q = q_ref[...]
    rows = q.shape[0]
    gs = rows // GROUP
    nselp = msel_ref.shape[1]

    s = _dot_nt(q, kc_ref[:, 0:KV_HALF]) + bcs_ref[...]
    m = jnp.maximum(jnp.max(s, axis=-1, keepdims=True), MASK_FLOOR)
    e = jnp.exp2(s - m)
    p = e / jnp.maximum(jnp.sum(e, axis=-1, keepdims=True), 1e-30)
    o_cmp = _dot(p.astype(BF16), kc_ref[:, KV_HALF:KV_W])
    psum = p[0:gs]
    for r in range(1, GROUP):
        psum = psum + p[r * gs:(r + 1) * gs]

    h1, h2, h3 = _split3(psum)
    msel = msel_ref[...]
    pslc = (_dot(h1, msel) + _dot(h2, msel)) + _dot(h3, msel)
    j = lax.broadcasted_iota(jnp.int32, (gs, nselp), 1)
    qpos = past + lax.rem(lax.broadcasted_iota(jnp.int32, (gs, nselp), 0), s_len)
    cur = lax.shift_right_logical(qpos, int(math.log2(SEL_BLOCK)))
    valid = j * SEL_BLOCK <= qpos
    forced = (j == 0) | (j == cur) | (j == cur - 1)
    score = jnp.where(valid, pslc + jnp.where(forced, FORCE_BONUS, 0.0), NEG)
    score = jnp.where(j < nsel, score, -3e38)
    score_t = jnp.concatenate([score, jnp.zeros((LANES - gs, nselp), F32)], axis=0).T
    jp = lax.broadcasted_iota(jnp.int32, (nselp, nselp), 0)
    jj = lax.broadcasted_iota(jnp.int32, (nselp, nselp), 1)
    sel_rows = []
    for r in range(gs):
        col = score_t[:, r:r + 1]
        row = score[r:r + 1, :]
        beats = jnp.where(jj > jp, jnp.where(col >= row, 1.0, 0.0), jnp.where(col > row, 1.0, 0.0))
        rank = jnp.sum(beats, axis=0, keepdims=True)
        sel_rows.append(jnp.where(rank < min(TOP_N, nsel), 1.0, 0.0))
    sel8 = jnp.concatenate(sel_rows, axis=0)
    sel = jnp.concatenate([sel8] * GROUP, axis=0).astype(BF16)

    ck = bsl_ref.shape[1]
    n_chunks = past // ck
    erow = lax.broadcasted_iota(jnp.int32, (nselp, ck), 0)
    eblk = lax.shift_right_logical(lax.broadcasted_iota(jnp.int32, (nselp, ck), 1), int(math.log2(SEL_BLOCK)))
    scs = []
    for t in range(n_chunks):
        kt = buf[slot, 0:KV_HALF, t * ck:(t + 1) * ck].astype(BF16)
        e_t = jnp.where(erow == t * (ck // SEL_BLOCK) + eblk, 1.0, 0.0).astype(BF16)
        sc = _dot(q, kt) + (_dot(sel, e_t) - 1.0) * (-NEG)
        scs.append(sc + bsl_ref[...] if t == n_chunks - 1 else sc)
    knew = knew_ref[...]
    wnew = wnew_ref[...]
    s_new = _dot_nt(q, knew[:, 0:KV_HALF]) + bnew_ref[...]
    s_win = _dot(q, win_ref[0:KV_HALF, :].astype(BF16)) + bw_ref[...]
    s_wnew = _dot_nt(q, wnew[:, 0:KV_HALF]) + bnew_ref[...]
    parts = [_softmax_part(scs[t], buf[slot, KV_HALF:KV_W, t * ck:(t + 1) * ck].astype(BF16), True)
             for t in range(n_chunks)]
    parts.append(_softmax_part(s_new, knew[:, KV_HALF:KV_W], False))
    o_sel = _merge_parts(parts)

    o_win = _merge_parts([_softmax_part(s_win, win_ref[KV_HALF:KV_W, :].astype(BF16), True),
                          _softmax_part(s_wnew, wnew[:, KV_HALF:KV_W], False)])

    g = g_ref[...]
    o_ref[...] = g[:, 0:1] * o_cmp + g[:, 1:2] * o_sel + g[:, 2:3] * o_win


def _attn_sample(page_table, q32, g32, knew, wnew, wnew_t, win_t, kvc, tiles, msel, cache_t, past, s_len):
    nb, n_pages = page_table.shape
    page = cache_t.shape[2]
    rows = q32.shape[1]
    wb = win_t.shape[2]
    nsel = -(-(past + s_len) // SEL_BLOCK)
    bcs, bsl, bnew, bw = tiles
    per_b = lambda r, w: pl.BlockSpec((None, r, w), lambda b, pt: (b, 0, 0))
    return pl.pallas_call(
        functools.partial(_attn_sample_kernel, n_pages=n_pages, page=page, past=past, s_len=s_len, nsel=nsel),
        grid_spec=pltpu.PrefetchScalarGridSpec(
            num_scalar_prefetch=1,
            grid=(nb,),
            in_specs=[per_b(rows, LANES), per_b(rows, LANES), per_b(LANES, KV_W), per_b(LANES, KV_W),
                      per_b(KV_W, LANES), per_b(KV_W, wb), per_b(kvc.shape[1], KV_W),
                      _resident_sp(bcs.shape), _resident_sp(bsl.shape), _resident_sp(bnew.shape),
                      _resident_sp(bw.shape), _resident_sp(msel.shape),
                      pl.BlockSpec(memory_space=pl.ANY)],
            out_specs=[per_b(rows, LANES), per_b(KV_W, wb)],
            scratch_shapes=[pltpu.VMEM((2, KV_W, past), F32), pltpu.SemaphoreType.DMA((2,))]),
        out_shape=[jax.ShapeDtypeStruct((nb, rows, LANES), F32), jax.ShapeDtypeStruct((nb, KV_W, wb), F32)],
        compiler_params=_cparams(("arbitrary",)),
        name="attn_sample",
    )(page_table, q32, g32, knew, wnew, wnew_t, win_t, kvc, bcs, bsl, bnew, bw, msel, cache_t)


def _prep_weights(w_in, w_cmp1, w_cmp2, pe_cmp, w_conv, w_nsa_out):
    d = w_in.shape[0]
    sizes = (N_HEADS * HEAD_DIM, KV_W, KV_W, KV_W, 3 * N_HEADS, _C_W, _C_W, _C_W, 2 * d)
    offs = [0]
    for sz in sizes:
        offs.append(offs[-1] + sz)
    part = lambda k: w_in[:, offs[k]:offs[k + 1]]
    qp = part(0).reshape(d, H_KV, GROUP, HEAD_DIM).transpose(0, 2, 1, 3).reshape(d, N_HEADS * HEAD_DIM)
    gn = jnp.pad(part(4), ((0, 0), (0, _G_W - 3 * N_HEADS)))
    w_in_p = jnp.concatenate([qp, part(1), part(2), part(3), gn, part(5), part(6), part(7), part(8)],
                             axis=1).astype(BF16)
    w_nsa_p = w_nsa_out.reshape(H_KV, GROUP, HEAD_DIM, -1).transpose(1, 0, 2, 3).reshape(
        N_HEADS * HEAD_DIM, -1).astype(BF16)
    r = CMP_BLOCK // CMP_STRIDE
    w1r = w_cmp1.reshape(2, r, CMP_STRIDE // 2, 2, HEAD_DIM, CMP_HIDDEN)
    w1t = jnp.transpose(w1r, (0, 2, 3, 4, 1, 5))
    w1e = w1t[:, :, :, :, :, None, :].astype(BF16)
    z1 = jnp.zeros_like(w1e)
    wb = jnp.stack([jnp.concatenate([w1e if gp == g else z1 for gp in range(H_KV)], axis=5)
                    for g in range(H_KV)], axis=3)
    wbig = wb.reshape(2, CMP_STRIDE // 2, 2 * KV_HALF, CMP_OUT_W)
    z2 = jnp.zeros((CMP_HIDDEN, HEAD_DIM), F32)
    w2bd = jnp.concatenate(
        [jnp.concatenate([w_cmp2[c] if (cp, gp) == (c, g) else z2 for cp in range(2) for gp in range(H_KV)], axis=1)
         for c in range(2) for g in range(H_KV)], axis=0).astype(BF16)
    pe_r = pe_cmp.reshape(2, r, CMP_STRIDE, HEAD_DIM).transpose(1, 2, 0, 3)
    pe_rows = jnp.broadcast_to(pe_r[:, :, :, None, :], (r, CMP_STRIDE, 2, H_KV, HEAD_DIM)).reshape(r, CHUNK_W)
    pe8 = jnp.pad(pe_rows, ((0, 8 - r), (0, 0)))
    w_conv8 = jnp.pad(w_conv, ((0, 8 - CONV_WIDTH), (0, 0)))
    return w_in_p, w_nsa_p, wbig, w2bd, pe8, w_conv8


def _sel_matrix(n_c, nselp):
    n = jnp.arange(n_c + 1)[:, None]
    j = jnp.arange(nselp)[None, :]
    return _sel_weights(n - (SEL_BLOCK // CMP_STRIDE) * j).astype(BF16)


def kernel(x_prompt, x_sample, cache_kv_cmp, cache_kv_sel, state_kv_win, state_conv, page_table,
           c_prompt, c_sample, w_ada, b_ada, g_norm, w_ffn1_gu, w_ffn1_down, w_ffn2_gu, w_ffn2_down,
           w_in, w_cmp1, w_cmp2, pe_cmp, w_conv, w_nsa_out, w_conv_out, w_out, rel_bias, g_final):
    assert w_ada.shape[0] == 1, "single-layer trunk"
    nbp, t, d = x_prompt.shape
    nbs, s_len, _ = x_sample.shape
    n_pages = page_table.shape[1]
    page = cache_kv_cmp.shape[2]
    past = n_pages * page
    n_phys = cache_kv_cmp.shape[1]
    wb = state_kv_win.shape[2]
    assert wb == WINDOW and past % SAMPLE_CK == 0

    w_in_p, w_nsa_p, wbig, w2bd, pe8, w_conv8 = _prep_weights(
        w_in[0], w_cmp1[0], w_cmp2[0], pe_cmp[0], w_conv[0], w_nsa_out[0])
    wgu1, wd1 = w_ffn1_gu[0].astype(BF16), w_ffn1_down[0].astype(BF16)
    wgu2, wd2 = w_ffn2_gu[0].astype(BF16), w_ffn2_down[0].astype(BF16)
    w_cv, w_o = w_conv_out[0].astype(BF16), w_out[0].astype(BF16)
    gn = [g_norm[0][k:k + 1] for k in range(N_SUB)]

    n_c_rows = nbp + nbs
    c_all = jnp.pad(jnp.concatenate([c_prompt, c_sample], axis=0), ((0, (-n_c_rows) % 8), (0, 0)))
    mod_all = _ada(c_all, w_ada[0], b_ada[0])
    mod_p = mod_all[:nbp].reshape(nbp * 3 * N_SUB, 1, d)
    mod_s = jnp.transpose(jnp.repeat(mod_all[nbp:n_c_rows].reshape(nbs, 3 * N_SUB, d), s_len, axis=0), (1, 0, 2))

    tm = 512 if t % 512 == 0 else t
    tpb = t // tm
    xp = x_prompt.reshape(nbp * t, d)
    h1 = _ffn(xp, mod_p, 0, gn[0], wgu1, wd1, tm, tpb)
    (q, kc_k, kc_v, kvc_t, kvs_t, kvw_t, ks0, ks1, vs0, vs1, kw_k, vw0, vw1, gates, z, cb, gm) = _inproj(
        h1, mod_p, gn[1], w_in_p, tm, tpb, True)
    kc2 = _cmp_prompt(kc_k, kc_v, t, wbig, pe8, w2bd)
    tiles_p = _bias_prompt(rel_bias, t // CMP_STRIDE)
    o_nsa = _attn_prompt(q, gates, (ks0, ks1), (vs0, vs1), kw_k, (vw0, vw1), kc2, tiles_p, nbp, t)
    tm_mix = 2 * tm if t % (2 * tm) == 0 else tm
    h2 = _mixout(h1, o_nsa, z, None, cb, gm, mod_p, w_conv8, w_nsa_p, w_cv, w_o, tm_mix, t // tm_mix)
    y_prompt = _ffn(h2, mod_p, 2, gn[2], wgu2, wd2, tm, tpb, g_final).reshape(nbp, t, d)

    kv_shape = (2, H_KV, HEAD_DIM)
    kv_out = lambda a: jnp.transpose(a.reshape((1, nbp) + kv_shape + (a.shape[-1],)), (0, 1, 5, 2, 3, 4))
    kv_cmp_p = kv_out(kvc_t)
    kv_sel_p = kv_out(kvs_t)
    keep = min(WINDOW, t)
    kv_win_p = kv_out(kvw_t[:, :, t - keep:])
    conv_p = z.reshape(1, nbp, t, -1)[:, :, t - (CONV_WIDTH - 1):]

    ns = nbs * s_len
    xs = x_sample.reshape(ns, d)
    h1s = _ffn(xs, mod_s, 0, gn[0], wgu1, wd1, ns, 1)
    qs, kvc_s, kvs_s, kvw_s, kvs_sb, kvw_sb, gates_s, z_s, cb_s, gm_s = _inproj(
        h1s, mod_s, gn[1], w_in_p, ns, 1, False)

    pos_minor = lambda a: jnp.transpose(a, (0, 2, 3, 4, 1)).reshape(a.shape[0], KV_W, a.shape[1])
    kvc_past = _cmp_sample(page_table, pos_minor(cache_kv_cmp[0]), wbig, pe8, w2bd)
    n_c = (past + s_len - CMP_BLOCK) // CMP_STRIDE + 1
    nsel = -(-(past + s_len) // SEL_BLOCK)
    nselp = -(-nsel // LANES) * LANES
    rows = N_HEADS * s_len
    head_of_row = [g * GROUP + r for r in range(GROUP) for g in range(H_KV) for _ in range(s_len)]
    tbl_rows = jnp.pad(rel_bias.T[jnp.array(head_of_row)], ((0, 0), (0, LANES - N_BUCKETS)))
    tiles_s = _bias_sample(tbl_rows, past, s_len, past // CMP_STRIDE, SAMPLE_CK, wb)
    msel = _sel_matrix(past // CMP_STRIDE - 1, nselp)

    q5 = qs.reshape(nbs, s_len, GROUP, H_KV, HEAD_DIM).transpose(0, 2, 3, 1, 4)
    q32 = jnp.zeros((nbs, GROUP, H_KV, s_len, H_KV, HEAD_DIM), BF16)
    for g in range(H_KV):
        q32 = q32.at[:, :, g, :, g, :].set(q5[:, :, g])
    q32 = q32.reshape(nbs, rows, LANES)
    g5 = gates_s[:, :3 * N_HEADS].reshape(nbs, s_len, 3, H_KV, GROUP).transpose(0, 4, 3, 1, 2)
    g32 = jnp.pad(g5.reshape(nbs, rows, 3), ((0, 0), (0, 0), (0, LANES - 3)))
    pad_new = lambda a: jnp.pad(a.reshape(nbs, s_len, KV_W), ((0, 0), (0, LANES - s_len), (0, 0)))
    wnew_t = jnp.pad(jnp.transpose(kvw_s.reshape(nbs, s_len, KV_W), (0, 2, 1)), ((0, 0), (0, 0), (0, LANES - s_len)))
    o32, win_next = _attn_sample(page_table, q32, g32, pad_new(kvs_sb), pad_new(kvw_sb), wnew_t,
                                 pos_minor(state_kv_win[0]), kvc_past, tiles_s, msel,
                                 pos_minor(cache_kv_sel[0]), past, s_len)
    o6 = o32.reshape(nbs, GROUP, H_KV, s_len, H_KV, HEAD_DIM)
    o_s = jnp.stack([o6[:, :, g, :, g, :] for g in range(H_KV)], axis=3)
    o_nsa_s = o_s.transpose(0, 2, 1, 3, 4).reshape(ns, N_HEADS * HEAD_DIM).astype(BF16)

    full = jnp.concatenate([state_conv[0], z_s.reshape(nbs, s_len, -1)], axis=1)
    z_shift = (full[:, 1:1 + s_len].reshape(ns, -1), full[:, 0:s_len].reshape(ns, -1))
    h2s = _mixout(h1s, o_nsa_s, z_s, z_shift, cb_s, gm_s, mod_s, w_conv8, w_nsa_p, w_cv, w_o, ns, 1)
    y_sample = _ffn(h2s, mod_s, 2, gn[2], wgu2, wd2, ns, 1, g_final).reshape(nbs, s_len, d)

    kv_cmp_s = kvc_s.reshape((1, nbs, s_len) + kv_shape)
    kv_sel_s = kvs_s.reshape((1, nbs, s_len) + kv_shape)
    kv_win_s = jnp.transpose(win_next.reshape((1, nbs) + kv_shape + (wb,)), (0, 1, 5, 2, 3, 4))
    conv_s = full[None, :, s_len:]
    return (y_prompt, y_sample, kv_cmp_p, kv_sel_p, kv_win_p, conv_p, kv_cmp_s, kv_sel_s, kv_win_s, conv_s)
```

```python
import functools
import math

import jax
import jax.numpy as jnp
from jax import lax
from jax.experimental import pallas as pl
from jax.experimental.pallas import tpu as pltpu

F32 = jnp.float32
BF16 = jnp.bfloat16

HEAD_DIM = 64
N_HEADS = 8
H_KV = 2
GROUP = N_HEADS // H_KV
CMP_BLOCK = 32
CMP_STRIDE = 16
CMP_HIDDEN = 2 * HEAD_DIM
SEL_BLOCK = 64
TOP_N = 16
WINDOW = 512
CONV_WIDTH = 3
N_BUCKETS = 32
MAX_DISTANCE = 128
N_SUB = 3
EPS = 1e-6
NEG = -1e9
LOG2E = math.log2(math.e)
FORCE_BONUS = 1e3
KV_W = 2 * H_KV * HEAD_DIM
KV_HALF = H_KV * HEAD_DIM
LANES = 128
Q_TILE = 256
CMP_NEAR = 32
VMEM_LIMIT = 56 * 1024 * 1024
TOKEN_TILE = 512
ADA_COL_TILES = 8


def _bucket_thresholds():
    max_exact = N_BUCKETS // 2

    def bucket(d):
        if d < max_exact:
            return d
        large = max_exact + int(math.log(d / max_exact) / math.log(MAX_DISTANCE / max_exact)
                                * (N_BUCKETS - max_exact))
        return min(large, N_BUCKETS - 1)

    thr, d = [], 0
    for b in range(N_BUCKETS):
        while bucket(d) < b:
            d += 1
        thr.append(d)
    return tuple(thr)


BUCKET_THR = _bucket_thresholds()


def _cparams(sem):
    return pltpu.CompilerParams(dimension_semantics=sem, vmem_limit_bytes=VMEM_LIMIT)


def _resident(shape):
    nd = len(shape)
    return pl.BlockSpec(shape, lambda *_: (0,) * nd, pipeline_mode=pl.Buffered(1))


def _dot(a, b):
    return jnp.dot(a, b, preferred_element_type=F32)


def _dot_nt(a, b):
    return lax.dot_general(a, b, (((1,), (1,)), ((), ())), preferred_element_type=F32)


def _split3(x):
    h1 = x.astype(BF16)
    r1 = x - h1.astype(F32)
    h2 = r1.astype(BF16)
    h3 = (r1 - h2.astype(F32)).astype(BF16)
    return h1, h2, h3


def _modulated_norm(x, g, shift, scale):
    y = x * lax.rsqrt(jnp.mean(x * x, axis=-1, keepdims=True) + EPS)
    return (y * g) * (1.0 + scale) + shift


def _ada_kernel(c_ref, w_ref, b_ref, o_ref):
    c = c_ref[...]
    a = (c * jax.nn.sigmoid(c)).astype(BF16)
    o_ref[...] = _dot(a, w_ref[...].astype(BF16)) + b_ref[...]


def _ada(c_all, w_ada, b_ada):
    rows, d = c_all.shape
    n = w_ada.shape[1]
    tn = n // ADA_COL_TILES
    return pl.pallas_call(
        _ada_kernel,
        grid=(n // tn,),
        in_specs=[pl.BlockSpec((rows, d), lambda j: (0, 0)),
                  pl.BlockSpec((d, tn), lambda j: (0, j)),
                  pl.BlockSpec((1, tn), lambda j: (0, j))],
        out_specs=pl.BlockSpec((rows, tn), lambda j: (0, j)),
        out_shape=jax.ShapeDtypeStruct((rows, n), F32),
        compiler_params=_cparams(("arbitrary",)),
        name="ada",
    )(c_all, w_ada, b_ada.reshape(1, n))


def _ffn_kernel(*refs, d_ff, fc, final_norm):
    if final_norm:
        x_ref, sh_ref, sc_ref, gt_ref, gn_ref, wgu_ref, wd_ref, gf_ref, o_ref = refs
    else:
        x_ref, sh_ref, sc_ref, gt_ref, gn_ref, wgu_ref, wd_ref, o_ref = refs
    x = x_ref[...]
    ub = _modulated_norm(x, gn_ref[...], sh_ref[...], sc_ref[...]).astype(BF16)
    acc = jnp.zeros(x.shape, F32)
    for c in range(d_ff // fc):
        g = _dot(ub, wgu_ref[:, c * fc:(c + 1) * fc])
        v = _dot(ub, wgu_ref[:, d_ff + c * fc:d_ff + (c + 1) * fc])
        a = ((g * jax.nn.sigmoid(g)) * v).astype(BF16)
        acc = acc + _dot(a, wd_ref[c * fc:(c + 1) * fc, :])
    h = x + (0.5 * gt_ref[...]) * acc
    if final_norm:
        h = (h * lax.rsqrt(jnp.mean(h * h, axis=-1, keepdims=True) + EPS)) * gf_ref[...]
    o_ref[...] = h


def _mod_specs(mod, ks, tm, tiles_per_batch):
    if mod.ndim == 3 and mod.shape[1] == 1:
        d = mod.shape[-1]
        specs = [pl.BlockSpec((None, 1, d), lambda i, k=k: ((i // tiles_per_batch) * (3 * N_SUB) + k, 0, 0))
                 for k in ks]
        return specs, [mod] * len(ks)
    d = mod.shape[-1]
    specs = [pl.BlockSpec((None, tm, d), lambda i, k=k: (k, i, 0)) for k in ks]
    return specs, [mod] * len(ks)


def _ffn(x, mod, sub, gn_row, w_gu, w_down, tm, tiles_per_batch, g_final=None):
    n, d = x.shape
    d_ff = w_down.shape[0]
    fc = d_ff // 2 if (d_ff // 2) % LANES == 0 else d_ff
    final_norm = g_final is not None
    mspecs, mops = _mod_specs(mod, (3 * sub, 3 * sub + 1, 3 * sub + 2), tm, tiles_per_batch)
    in_specs = [pl.BlockSpec((tm, d), lambda i: (i, 0))] + mspecs + [
        pl.BlockSpec((1, d), lambda i: (0, 0)), _resident(w_gu.shape), _resident(w_down.shape)]
    ops = [x] + mops + [gn_row, w_gu, w_down]
    if final_norm:
        in_specs.append(pl.BlockSpec((1, d), lambda i: (0, 0)))
        ops.append(g_final.reshape(1, d))
    return pl.pallas_call(
        functools.partial(_ffn_kernel, d_ff=d_ff, fc=fc, final_norm=final_norm),
        grid=(n // tm,),
        in_specs=in_specs,
        out_specs=pl.BlockSpec((tm, d), lambda i: (i, 0)),
        out_shape=jax.ShapeDtypeStruct((n, d), F32),
        compiler_params=_cparams(("arbitrary",)),
        name="ffn_final" if final_norm else "ffn",
    )(*ops)


_Q_W = N_HEADS * HEAD_DIM
_G_W = LANES
_C_W = 512
_SEG = {}
_off = 0
for _name, _w in (("q", _Q_W), ("kc", KV_W), ("ks", KV_W), ("kw", KV_W), ("gn", _G_W),
                  ("ch", _C_W), ("cb", _C_W), ("cc", _C_W)):
    _SEG[_name] = (_off, _off + _w)
    _off += _w
_MG_OFF = _off


V_ROWS = HEAD_DIM + 16


def _aug_lane(g):
    return HEAD_DIM if g == 0 else 0


def _inproj_kernel(*refs, prompt):
    x_ref, sh_ref, sc_ref, gn_ref, w_ref = refs[:5]
    ub = _modulated_norm(x_ref[...], gn_ref[...], sh_ref[...], sc_ref[...]).astype(BF16)

    def seg(name):
        lo, hi = _SEG[name]
        return _dot(ub, w_ref[:, lo:hi])

    if prompt:
        (q_ref, kck_ref, kcv_ref, kct_ref, kst_ref, kwt_ref, ks0_ref, ks1_ref, vs0_ref, vs1_ref, kwk_ref,
         vw0_ref, vw1_ref, g_ref, z_ref, cb_ref, gm_ref) = refs[5:]
        kc = seg("kc")
        kck_ref[...] = kc[:, 0:KV_HALF]
        kcv_ref[...] = kc[:, KV_HALF:KV_W]
        kct_ref[...] = kc.T
        tm = kc.shape[0]
        lane = lax.broadcasted_iota(jnp.int32, (tm, KV_HALF), 1)
        blk = lax.shift_right_logical(lax.broadcasted_iota(jnp.int32, (tm, KV_HALF), 0),
                                      int(math.log2(SEL_BLOCK))) & (Q_TILE // SEL_BLOCK - 1)
        ones_rows = jnp.where(lax.broadcasted_iota(jnp.int32, (V_ROWS - HEAD_DIM, tm), 0) == 0, 1.0, 0.0)
        for name, t_ref, k_refs, vt_refs in (("ks", kst_ref, (ks0_ref, ks1_ref), (vs0_ref, vs1_ref)),
                                             ("kw", kwt_ref, (kwk_ref,), (vw0_ref, vw1_ref))):
            kv = seg(name)
            kv_t = kv.T
            t_ref[...] = kv_t
            kk = kv[:, 0:KV_HALF]
            if len(k_refs) == 1:
                k_refs[0][...] = kk.astype(BF16)
            else:
                for g, k_ref in enumerate(k_refs):
                    own = (lane < HEAD_DIM) if g == 0 else (lane >= HEAD_DIM)
                    onehot = jnp.where(lane - _aug_lane(g) == blk, 1.0, 0.0)
                    k_ref[...] = jnp.where(own, kk, onehot).astype(BF16)
            for g, vt_ref in enumerate(vt_refs):
                vg = jnp.concatenate(
                    [kv_t[KV_HALF + g * HEAD_DIM:KV_HALF + (g + 1) * HEAD_DIM, :], ones_rows], axis=0)
                for c in range(vt_ref.shape[0]):
                    vt_ref[c] = vg[:, c * Q_TILE:(c + 1) * Q_TILE].astype(BF16)
    else:
        q_ref, kc_ref, ks_ref, kw_ref, ksb_ref, kwb_ref, g_ref, z_ref, cb_ref, gm_ref = refs[5:]
        kc_ref[...] = seg("kc")
        ks = seg("ks")
        ks_ref[...] = ks
        ksb_ref[...] = ks.astype(BF16)
        kw = seg("kw")
        kw_ref[...] = kw
        kwb_ref[...] = kw.astype(BF16)
    q_ref[...] = (seg("q") * (HEAD_DIM ** -0.5 * LOG2E)).astype(BF16)
    g_ref[...] = jax.nn.sigmoid(seg("gn"))
    z_ref[...] = seg("cc") * seg("ch")
    cb_ref[...] = seg("cb").astype(cb_ref.dtype)
    d2 = gm_ref.shape[1]
    half = d2 // 2
    for c in range(2):
        gm_ref[:, c * half:(c + 1) * half] = jax.nn.sigmoid(
            _dot(ub, w_ref[:, _MG_OFF + c * half:_MG_OFF + (c + 1) * half])).astype(gm_ref.dtype)


def _inproj(h, mod, gn_row, w_in_p, tm, tiles_per_batch, prompt):
    n, d = h.shape
    d_conv = _C_W
    mspecs, mops = _mod_specs(mod, (3, 4), tm, tiles_per_batch)
    rows = lambda w, dt: (pl.BlockSpec((tm, w), lambda i: (i, 0)), jax.ShapeDtypeStruct((n, w), dt))
    tail = [rows(_G_W, F32), rows(d_conv, F32), rows(d_conv, BF16), rows(2 * d, BF16)]
    if prompt:
        nb = n // (tm * tiles_per_batch)
        t = tm * tiles_per_batch
        cpt = tm // Q_TILE
        tr = (pl.BlockSpec((None, KV_W, tm), lambda i: (i // tiles_per_batch, 0, i % tiles_per_batch)),
              jax.ShapeDtypeStruct((nb, KV_W, t), F32))
        vt = (pl.BlockSpec((cpt, V_ROWS, Q_TILE), lambda i: (i, 0, 0)),
              jax.ShapeDtypeStruct((n // Q_TILE, V_ROWS, Q_TILE), BF16))
        kr = rows(KV_HALF, BF16)
        outs = [rows(_Q_W, BF16), rows(KV_HALF, F32), rows(KV_HALF, F32), tr, tr, tr,
                kr, kr, vt, vt, kr, vt, vt] + tail
    else:
        outs = [rows(_Q_W, BF16), rows(KV_W, F32), rows(KV_W, F32), rows(KV_W, F32),
                rows(KV_W, BF16), rows(KV_W, BF16)] + tail
    return pl.pallas_call(
        functools.partial(_inproj_kernel, prompt=prompt),
        grid=(n // tm,),
        in_specs=[pl.BlockSpec((tm, d), lambda i: (i, 0))] + mspecs + [
            pl.BlockSpec((1, d), lambda i: (0, 0)), _resident(w_in_p.shape)],
        out_specs=[o[0] for o in outs],
        out_shape=[o[1] for o in outs],
        compiler_params=_cparams(("arbitrary",)),
        name="inproj_prompt" if prompt else "inproj",
    )(h, *mops, gn_row, w_in_p)


def _mixout_kernel(*refs, halo, tiles_per_batch):
    if halo:
        (h_ref, o_ref, z_ref, zp_ref, cb_ref, gm_ref, g2_ref, wc_ref,
         wn_ref, wcv_ref, wo_ref, out_ref) = refs
        z = z_ref[...]
        tm = z.shape[0]
        first = (pl.program_id(0) % tiles_per_batch) == 0
        prev = jnp.where(first, 0.0, zp_ref[...])
        row = lax.broadcasted_iota(jnp.int32, z.shape, 0)
        zm1 = jnp.where(row == 0, prev[7:8, :], pltpu.roll(z, 1, 0))
        zm2 = jnp.where(row == 0, prev[6:7, :], jnp.where(row == 1, prev[7:8, :], pltpu.roll(z, 2, 0)))
    else:
        (h_ref, o_ref, z_ref, zm1_ref, zm2_ref, cb_ref, gm_ref, g2_ref, wc_ref,
         wn_ref, wcv_ref, wo_ref, out_ref) = refs
        z, zm1, zm2 = z_ref[...], zm1_ref[...], zm2_ref[...]
    conv = wc_ref[0:1, :] * zm2 + wc_ref[1:2, :] * zm1 + wc_ref[2:3, :] * z
    y = (cb_ref[...].astype(F32) * conv).astype(BF16)
    d = h_ref.shape[1]
    merged = gm_ref[:, 0:d].astype(F32) * _dot(o_ref[...], wn_ref[...]) \
        + gm_ref[:, d:2 * d].astype(F32) * _dot(y, wcv_ref[...])
    out_ref[...] = h_ref[...] + g2_ref[...] * _dot(merged.astype(BF16), wo_ref[...])


def _mixout(h, o_nsa, z, z_shift, cb, gm, mod, w_conv8, w_nsa_p, w_cv, w_o, tm, tiles_per_batch):
    n, d = h.shape
    dc = z.shape[1]
    halo = z_shift is None
    mspecs, mops = _mod_specs(mod, (5,), tm, tiles_per_batch)
    tok = lambda w: pl.BlockSpec((tm, w), lambda i: (i, 0))
    if halo:
        zspecs = [tok(dc), pl.BlockSpec((8, dc), lambda i: (jnp.maximum(i * (tm // 8) - 1, 0), 0))]
        zops = [z, z]
    else:
        zspecs = [tok(dc), tok(dc), tok(dc)]
        zops = [z, z_shift[0], z_shift[1]]
    return pl.pallas_call(
        functools.partial(_mixout_kernel, halo=halo, tiles_per_batch=tiles_per_batch),
        grid=(n // tm,),
        in_specs=[tok(d), tok(o_nsa.shape[1])] + zspecs + [tok(dc), tok(2 * d)] + mspecs + [
            pl.BlockSpec((8, dc), lambda i: (0, 0)),
            _resident(w_nsa_p.shape), _resident(w_cv.shape), _resident(w_o.shape)],
        out_specs=tok(d),
        out_shape=jax.ShapeDtypeStruct((n, d), F32),
        compiler_params=_cparams(("arbitrary",)),
        name="mixout",
    )(h, o_nsa, *zops, cb, gm, *mops, w_conv8, w_nsa_p, w_cv, w_o)


CHUNK_W = CMP_STRIDE * KV_W
CMP_OUT_W = 2 * H_KV * CMP_HIDDEN


def _gelu_tanh(x):
    return x * (0.5 * (1.0 + jnp.tanh(math.sqrt(2.0 / math.pi) * (x + 0.044715 * (x * x * x)))))


def _chunk_partials(tile, wc_ref, after_step=None):
    accs = []
    for c in range(2):
        acc = None
        for lp in range(CMP_STRIDE // 2):
            x2 = jnp.concatenate([tile(2 * lp, c), tile(2 * lp + 1, c)], axis=1)
            part = _dot(x2, wc_ref[c, lp])
            acc = part if acc is None else acc + part
            if after_step is not None:
                after_step(c, lp)
        accs.append(acc)
    hw = CMP_OUT_W // 2
    return jnp.concatenate([accs[0][:, 0:hw], accs[1][:, 0:hw], accs[0][:, hw:], accs[1][:, hw:]], axis=1)


def _compress_tail(p, wc_ref, pe_ref, w2_ref):
    n_ch = p.shape[0]
    pb = _chunk_partials(
        lambda l, c: pe_ref[:, l * KV_W + c * KV_HALF:l * KV_W + (c + 1) * KV_HALF].astype(BF16), wc_ref)
    pre = (pb[0:1, 0:CMP_OUT_W] + pb[1:2, CMP_OUT_W:]) + p[:, 0:CMP_OUT_W] \
        + pltpu.roll(p[:, CMP_OUT_W:], n_ch - 1, 0)
    out = _dot(_gelu_tanh(pre).astype(BF16), w2_ref[...])
    row = lax.broadcasted_iota(jnp.int32, out.shape, 0)
    return jnp.where(row < n_ch - 1, out, 0.0)


def _cmp_prompt_kernel(xk_ref, xv_ref, wc_ref, pe_ref, w2_ref, o_ref):
    xs = (xk_ref, xv_ref)
    n_ch = xk_ref.shape[0] // CMP_STRIDE
    p = _chunk_partials(lambda l, c: xs[c][pl.ds(l, n_ch, stride=CMP_STRIDE), :].astype(BF16), wc_ref)
    out = _compress_tail(p, wc_ref, pe_ref, w2_ref).astype(BF16)
    o_ref[0:n_ch, :] = out
    o_ref[n_ch:2 * n_ch, :] = out


def _cmp_prompt(rows_k, rows_v, t, wbig, pe8, w2bd):
    nb = rows_k.shape[0] // t
    n_ch = t // CMP_STRIDE
    half = pl.BlockSpec((t, KV_HALF), lambda b: (b, 0))
    return pl.pallas_call(
        _cmp_prompt_kernel,
        grid=(nb,),
        in_specs=[half, half, _resident(wbig.shape), _resident(pe8.shape), _resident(w2bd.shape)],
        out_specs=pl.BlockSpec((None, 2 * n_ch, KV_W), lambda b: (b, 0, 0)),
        out_shape=jax.ShapeDtypeStruct((nb, 2 * n_ch, KV_W), BF16),
        compiler_params=_cparams(("arbitrary",)),
        name="cmp_prompt",
    )(rows_k, rows_v, wbig, pe8, w2bd)


def _page_copies(cache_ref, pt_ref, b, dst, sem, n_pages, page):
    return [pltpu.make_async_copy(cache_ref.at[pt_ref[b, p]], dst.at[:, pl.ds(p * page, page)], sem)
            for p in range(n_pages)]


def _gather_pages(cache_ref, pt_ref, buf, sem, n_pages, page):
    b = pl.program_id(0)
    nb = pl.num_programs(0)
    slot = b % 2
    copies = lambda bb, sl: _page_copies(cache_ref, pt_ref, bb, buf.at[sl], sem.at[sl], n_pages, page)

    @pl.when(b == 0)
    def _():
        for c in copies(0, 0):
            c.start()

    @pl.when(b + 1 < nb)
    def _():
        for c in copies(b + 1, 1 - slot):
            c.start()

    for c in copies(b, slot):
        c.wait()
    return slot


def _cmp_sample_kernel(pt_ref, cache_ref, wc_ref, pe_ref, w2_ref, o_ref,
                       buf_a, buf_b, xk_a, xv_a, xk_b, xv_b, sem, *, n_pages, page, nb):
    b = pl.program_id(0)
    bufs = (buf_a, buf_b)
    xs = ((xk_a, xv_a), (xk_b, xv_b))
    past = xk_a.shape[0]
    n_ch = past // CMP_STRIDE
    n_slabs = CMP_STRIDE // 2
    slab = past // n_slabs
    copies = lambda bb, par: _page_copies(cache_ref, pt_ref, bb, bufs[par], sem.at[par], n_pages, page)

    def transpose_slab(par, c, j):
        xs[par][c][j * slab:(j + 1) * slab, :] = bufs[par][c * KV_HALF:(c + 1) * KV_HALF, j * slab:(j + 1) * slab].T

    @pl.when(b == 0)
    def _():
        for cp in copies(0, 0):
            cp.start()
        if nb > 1:
            for cp in copies(1, 1):
                cp.start()
        for cp in copies(0, 0):
            cp.wait()
        for c in range(2):
            for j in range(n_slabs):
                transpose_slab(0, c, j)

    def step(par):
        nxt = 1 - par
        hook = None
        if nb > 1:
            @pl.when(b + 1 < nb)
            def _():
                for cp in copies(b + 1, nxt):
                    cp.wait()

            @pl.when(b + 2 < nb)
            def _():
                for cp in copies(b + 2, par):
                    cp.start()

            hook = lambda c, lp: transpose_slab(nxt, c, lp)
        p = _chunk_partials(lambda l, c: xs[par][c][pl.ds(l, n_ch, stride=CMP_STRIDE), :].astype(BF16), wc_ref, hook)
        o_ref[...] = _compress_tail(p, wc_ref, pe_ref, w2_ref).astype(BF16)

    for par in range(2):
        pl.when(b % 2 == par)(functools.partial(step, par))


def _cmp_sample(page_table, cache_t, wbig, pe8, w2bd):
    nb, n_pages = page_table.shape
    page = cache_t.shape[2]
    past = n_pages * page
    n_ch = past // CMP_STRIDE
    return pl.pallas_call(
        functools.partial(_cmp_sample_kernel, n_pages=n_pages, page=page, nb=nb),
        grid_spec=pltpu.PrefetchScalarGridSpec(
            num_scalar_prefetch=1,
            grid=(nb,),
            in_specs=[pl.BlockSpec(memory_space=pl.ANY),
                      _resident(wbig.shape), _resident(pe8.shape), _resident(w2bd.shape)],
            out_specs=pl.BlockSpec((None, n_ch, KV_W), lambda b, pt: (b, 0, 0)),
            scratch_shapes=[pltpu.VMEM((KV_W, past), F32)] * 2 + [pltpu.VMEM((past, KV_HALF), F32)] * 4
            + [pltpu.SemaphoreType.DMA((2,))]),
        out_shape=jax.ShapeDtypeStruct((nb, n_ch, KV_W), BF16),
        compiler_params=_cparams(("arbitrary",)),
        name="cmp_sample",
    )(page_table, cache_t, wbig, pe8, w2bd)


def _bias_values(d, table):
    acc = jnp.zeros(d.shape, F32) + table(0)
    for b in range(1, N_BUCKETS):
        acc = jnp.where(d >= BUCKET_THR[b], table(b), acc)
    return jnp.where(d >= 0, (acc - table(N_BUCKETS - 1)) * LOG2E, NEG)


CMP_BACK = 16


def _bias_prompt_kernel(tbl_ref, bc_ref, t0_ref, t1_ref):
    h = pl.program_id(0)
    table = lambda b: tbl_ref[b, h]
    ncp, tq = bc_ref.shape
    kj = lax.broadcasted_iota(jnp.int32, (tq, tq), 0)
    qi = lax.broadcasted_iota(jnp.int32, (tq, tq), 1)
    t0_ref[...] = _bias_values(qi - kj, table)
    t1_ref[...] = _bias_values(qi - kj + tq, table)
    c = lax.broadcasted_iota(jnp.int32, (ncp, tq), 0)
    qi = lax.broadcasted_iota(jnp.int32, (ncp, tq), 1)
    d = qi - CMP_STRIDE * c + (CMP_STRIDE * CMP_BACK - (CMP_BLOCK - 1))
    bc_ref[...] = jnp.where(c < CMP_NEAR, _bias_values(d, table), 0.0)


def _bias_prompt(rel_bias, ncp):
    tq = Q_TILE
    return pl.pallas_call(
        _bias_prompt_kernel,
        grid=(N_HEADS,),
        in_specs=[pl.BlockSpec(memory_space=pltpu.SMEM)],
        out_specs=[pl.BlockSpec((None, ncp, tq), lambda h: (h, 0, 0)),
                   pl.BlockSpec((None, tq, tq), lambda h: (h, 0, 0)),
                   pl.BlockSpec((None, tq, tq), lambda h: (h, 0, 0))],
        out_shape=[jax.ShapeDtypeStruct((N_HEADS, ncp, tq), F32),
                   jax.ShapeDtypeStruct((N_HEADS, tq, tq), F32),
                   jax.ShapeDtypeStruct((N_HEADS, tq, tq), F32)],
        compiler_params=_cparams(("arbitrary",)),
        name="bias_prompt",
    )(rel_bias)


def _bias_sample_kernel(tbl_ref, bcs_ref, bsl_ref, bnew_ref, bw_ref, *, past, s_len, n_c):
    table = lambda b: tbl_ref[:, b:b + 1]

    def qpos(shape):
        return past + lax.rem(lax.broadcasted_iota(jnp.int32, shape, 0), s_len)

    def lane(shape):
        return lax.broadcasted_iota(jnp.int32, shape, 1)

    sh = bcs_ref.shape
    n = lane(sh)
    bcs_ref[...] = jnp.where(n < n_c, _bias_values(qpos(sh) - (CMP_STRIDE * n + CMP_BLOCK - 1), table), NEG)
    sh = bsl_ref.shape
    bsl_ref[...] = _bias_values(qpos(sh) - (past - sh[1] + lane(sh)), table)
    sh = bnew_ref.shape
    bnew_ref[...] = jnp.where(lane(sh) < s_len, _bias_values(qpos(sh) - (past + lane(sh)), table), NEG)
    sh = bw_ref.shape
    d = qpos(sh) - (past - sh[1] + lane(sh))
    bw_ref[...] = jnp.where(d < WINDOW, _bias_values(d, table), NEG)


def _bias_sample(tbl_rows, past, s_len, ncs, ck, wb):
    rows = tbl_rows.shape[0]
    widths = (ncs, ck, LANES, wb)
    return pl.pallas_call(
        functools.partial(_bias_sample_kernel, past=past, s_len=s_len, n_c=(past + s_len - CMP_BLOCK) // CMP_STRIDE + 1),
        out_shape=[jax.ShapeDtypeStruct((rows, w), F32) for w in widths],
        name="bias_sample",
    )(tbl_rows)


MASK_FLOOR = -1e8
M_INIT = -1e30
TAKEN = -3e38
SCORE_AHEAD = 3
FAR_PER_ITER = 4
VALUE_BEHIND = 2


def _sel_weights(v):
    return jnp.where((v >= 0) & (v <= 2), 2.0, jnp.where((v == -1) | (v == 3), 1.0, 0.0))


def _attn_prompt_kernel(q_ref, gate_ref, ks0_ref, ks1_ref, vs0_ref, vs1_ref, kwk_ref, vw0_ref, vw1_ref,
                        kc2_ref, bc_ref, t0_ref, t1_ref, o_ref,
                        qm_ref, selr_ref, oc_ref, ms_ref, as_ref, mw_ref, aw_ref, *, ncp, nsel):
    tq = Q_TILE
    i = pl.program_id(1)
    q0 = i * tq
    lane = lax.broadcasted_iota(jnp.int32, (tq, LANES), 1)
    heads = [(g, r) for g in range(H_KV) for r in range(GROUP)]

    for g, r in heads:
        gmask = (lane < HEAD_DIM) if g == 0 else (lane >= HEAD_DIM)
        qm_ref[g * GROUP + r] = jnp.where(gmask, q_ref[:, r * LANES:(r + 1) * LANES], jnp.zeros((tq, LANES), BF16))

    n0 = (tq // CMP_STRIDE) * i - CMP_BACK
    ws = pl.multiple_of(lax.rem(n0 + ncp, ncp), 8)
    cl = lax.broadcasted_iota(jnp.int32, (ncp, tq), 0)
    dm = jnp.where((cl >= CMP_NEAR) & (cl < ncp - n0), NEG, 0.0) + jnp.where(cl < -n0, NEG, 0.0)
    jrow = lax.broadcasted_iota(jnp.int32, (nsel, ncp), 0)
    nn = lax.broadcasted_iota(jnp.int32, (nsel, ncp), 1) + n0
    nn = jnp.where(nn < 0, nn + ncp, nn)
    nn = jnp.where(nn >= ncp, nn - ncp, nn)
    mit = _sel_weights(nn - (SEL_BLOCK // CMP_STRIDE) * jrow).astype(BF16)

    jr = lax.broadcasted_iota(jnp.int32, (nsel, tq), 0)
    qpos = q0 + lax.broadcasted_iota(jnp.int32, (nsel, tq), 1)
    cur = lax.shift_right_logical(qpos, int(math.log2(SEL_BLOCK)))
    invalid_pen = jnp.where(jr * SEL_BLOCK <= qpos, 0.0, NEG)
    bonus = jnp.where((jr == 0) | (jr == cur) | (jr == cur - 1), FORCE_BONUS, 0.0)
    pj = lax.broadcasted_iota(jnp.int32, (LANES, LANES), 0)
    pl_ = lax.broadcasted_iota(jnp.int32, (LANES, LANES), 1)
    kj = lax.broadcasted_iota(jnp.int32, (tq, tq), 0)
    qi = lax.broadcasted_iota(jnp.int32, (tq, tq), 1)
    wmask = jnp.where(kj > qi, 0.0, NEG)
    blocks_per_chunk = tq // SEL_BLOCK

    kc = kc2_ref[pl.ds(ws, ncp), 0:KV_HALF]
    vct = kc2_ref[pl.ds(ws, ncp), KV_HALF:KV_W].astype(F32).T.astype(BF16)
    def cmp_scores(h):
        return _dot_nt(kc, qm_ref[h]) + bc_ref[h] + dm

    ahead = {h: cmp_scores(h) for h in range(2)}
    psum = [None] * H_KV
    for h in range(N_HEADS):
        s = ahead.pop(h)
        if h + 2 < N_HEADS:
            ahead[h + 2] = cmp_scores(h + 2)
        m = jnp.maximum(jnp.max(s, axis=0, keepdims=True), MASK_FLOOR)
        e = jnp.exp2(s - m)
        p = e * (1.0 / jnp.maximum(jnp.sum(e, axis=0, keepdims=True), 1e-30))
        oc_ref[h] = _dot(vct, p.astype(BF16))
        psum[h // GROUP] = p if psum[h // GROUP] is None else psum[h // GROUP] + p

    scores = []
    for g in range(H_KV):
        h1, h2, h3 = _split3(psum[g])
        pslc = (_dot(mit, h1) + _dot(mit, h2)) + _dot(mit, h3)
        scores.append(jnp.where(invalid_pen < 0.0, NEG, pslc + bonus))

    def take_max(_, carry):
        out = []
        for work, sel_acc in carry:
            mx = jnp.max(work, axis=0, keepdims=True)
            first = jnp.min(jnp.where(work == mx, jr, nsel), axis=0, keepdims=True)
            hit = jr == first
            out.append((jnp.where(hit, TAKEN, work), jnp.where(hit, 1.0, sel_acc)))
        return tuple(out)

    picked = lax.fori_loop(0, min(TOP_N, nsel), take_max,
                           tuple((sc, jnp.zeros((nsel, tq), F32)) for sc in scores))
    for g in range(H_KV):
        sel_t = picked[g][1]
        if nsel < LANES:
            sel_t = jnp.concatenate([sel_t, jnp.zeros((LANES - nsel, tq), F32)], axis=0)
        selr_ref[g] = sel_t.T.astype(BF16)

    def masked_queries(t, g):
        off = pl_ - _aug_lane(g)
        place = jnp.where((pj - t * blocks_per_chunk == off) & (off >= 0) & (off < blocks_per_chunk), 1.0, 0.0)
        picked = _dot(selr_ref[g], place.astype(BF16))
        auglane = (lane >= _aug_lane(g)) & (lane < _aug_lane(g) + blocks_per_chunk)
        pen = jnp.where(auglane, (picked - 1.0) * (-NEG), 0.0).astype(BF16)
        return [qm_ref[g * GROUP + r] + pen for r in range(GROUP)]

    def run(chunks):
        items = []
        for k_of, vt_of, queries, extra, m_ref, a_ref in chunks:
            cache = {}
            items += [(h, k_of, vt_of, queries, extra, m_ref, a_ref, cache) for h in range(N_HEADS)]

        def kv(item):
            g, cache = item[0] // GROUP, item[7]
            if g not in cache:
                cache[g] = (item[1](g), item[2](g))
            return cache[g]

        def scores(item):
            h, queries, extra = item[0], item[3], item[4]
            s = _dot_nt(kv(item)[0], queries(h))
            return s if extra is None else s + extra(h)

        def softmax(item, s):
            h, m_ref = item[0], item[5]
            m_prev = m_ref[h]
            m_new = jnp.maximum(m_prev, jnp.max(s, axis=0, keepdims=True))
            m_ref[h] = m_new
            return jnp.exp2(s - m_new).astype(BF16), jnp.exp2(m_prev - m_new)

        def accumulate(item, p, alpha):
            h, a_ref = item[0], item[6]
            a_ref[h] = alpha * a_ref[h] + _dot(kv(item)[1], p)

        n = len(items)
        s = {j: scores(items[j]) for j in range(SCORE_AHEAD)}
        pa = {}
        for j in range(n):
            pa[j] = softmax(items[j], s.pop(j))
            if j + SCORE_AHEAD < n:
                s[j + SCORE_AHEAD] = scores(items[j + SCORE_AHEAD])
            if j >= VALUE_BEHIND:
                accumulate(items[j - VALUE_BEHIND], *pa.pop(j - VALUE_BEHIND))
        for j in range(n - VALUE_BEHIND, n):
            accumulate(items[j], *pa.pop(j))

    for m_ref, a_ref in ((ms_ref, as_ref), (mw_ref, aw_ref)):
        m_ref[...] = jnp.full(m_ref.shape, M_INIT, F32)
        a_ref[...] = jnp.zeros(a_ref.shape, F32)

    def sel_chunk(t, extra):
        sl = pl.ds(pl.multiple_of(t * tq, tq), tq)
        qs = [masked_queries(t, g) for g in range(H_KV)]
        return (lambda g: (ks0_ref, ks1_ref)[g][sl, :], lambda g: (vs0_ref, vs1_ref)[g][t],
                lambda h: qs[h // GROUP][h % GROUP], extra, ms_ref, as_ref)

    def win_chunk(t, extra):
        sl = pl.ds(pl.multiple_of(t * tq, tq), tq)
        return (lambda g: kwk_ref[sl, :], lambda g: (vw0_ref, vw1_ref)[g][t],
                lambda h: qm_ref[h], extra, mw_ref, aw_ref)

    n_far = jnp.maximum(i - 1, 0)
    n_iter = n_far // FAR_PER_ITER

    def far_group(u, carry):
        run([sel_chunk(FAR_PER_ITER * u + j, None) for j in range(FAR_PER_ITER)])
        return carry

    lax.fori_loop(0, n_iter, far_group, 0)
    for rem in range(1, FAR_PER_ITER):
        @pl.when(n_far - FAR_PER_ITER * n_iter == rem)
        def _(rem=rem):
            run([sel_chunk(FAR_PER_ITER * n_iter + j, None) for j in range(rem)])

    gone1 = jnp.where(i >= 1, 0.0, NEG)
    gone2 = jnp.where(i >= 2, 0.0, NEG)
    wmask2 = wmask + gone2
    run([sel_chunk(jnp.maximum(i - 1, 0), lambda h: t1_ref[h] + gone1),
         sel_chunk(i, lambda h: t0_ref[h]),
         win_chunk(jnp.maximum(i - 2, 0), lambda h: wmask2),
         win_chunk(jnp.maximum(i - 1, 0), lambda h: t1_ref[h] + gone1),
         win_chunk(i, lambda h: t0_ref[h])])

    gates_t = gate_ref[...].T
    for r in range(GROUP):
        per_g = []
        for g in range(H_KV):
            h = g * GROUP + r
            a_s, a_w = as_ref[h], aw_ref[h]
            o_sel = a_s[0:HEAD_DIM] * (1.0 / jnp.maximum(a_s[HEAD_DIM:HEAD_DIM + 1], 1e-30))
            o_win = a_w[0:HEAD_DIM] * (1.0 / jnp.maximum(a_w[HEAD_DIM:HEAD_DIM + 1], 1e-30))
            per_g.append(gates_t[h:h + 1, :] * oc_ref[h, g * HEAD_DIM:(g + 1) * HEAD_DIM, :]
                         + gates_t[N_HEADS + h:N_HEADS + h + 1, :] * o_sel
                         + gates_t[2 * N_HEADS + h:2 * N_HEADS + h + 1, :] * o_win)
        o_ref[:, r * LANES:(r + 1) * LANES] = jnp.concatenate(per_g, axis=0).T.astype(BF16)


def _attn_prompt(q, gates, ks_g, vs_g, kw_k, vw_g, kc2, bias_tiles, nb, t):
    tq = Q_TILE
    assert t % tq == 0 and WINDOW == 2 * tq
    nt = t // tq
    ncp = t // CMP_STRIDE
    nsel = t // SEL_BLOCK
    assert nsel <= LANES and ncp % LANES == 0
    bc, t0, t1 = bias_tiles
    tok = lambda w: pl.BlockSpec((tq, w), lambda b, i: (b * nt + i, 0))
    k_rows = pl.BlockSpec((None, t, KV_HALF), lambda b, i: (b, 0, 0))
    v_cols = pl.BlockSpec((None, nt, V_ROWS, tq), lambda b, i: (b, 0, 0, 0))
    state = [pltpu.VMEM((N_HEADS, 1, tq), F32), pltpu.VMEM((N_HEADS, V_ROWS, tq), F32)]
    kr = lambda a: a.reshape(nb, t, KV_HALF)
    vc = lambda a: a.reshape(nb, nt, V_ROWS, tq)
    return pl.pallas_call(
        functools.partial(_attn_prompt_kernel, ncp=ncp, nsel=nsel),
        grid=(nb, nt),
        in_specs=[tok(q.shape[1]), tok(LANES), k_rows, k_rows, v_cols, v_cols, k_rows, v_cols, v_cols,
                  pl.BlockSpec((None, 2 * ncp, KV_W), lambda b, i: (b, 0, 0)),
                  _resident(bc.shape), _resident(t0.shape), _resident(t1.shape)],
        out_specs=tok(q.shape[1]),
        out_shape=jax.ShapeDtypeStruct(q.shape, BF16),
        scratch_shapes=[pltpu.VMEM((N_HEADS, tq, LANES), BF16), pltpu.VMEM((H_KV, tq, LANES), BF16),
                        pltpu.VMEM((N_HEADS, LANES, tq), F32)] + state + state,
        compiler_params=_cparams(("arbitrary", "arbitrary")),
        name="attn_prompt",
    )(q, gates, kr(ks_g[0]), kr(ks_g[1]), vc(vs_g[0]), vc(vs_g[1]), kr(kw_k), vc(vw_g[0]), vc(vw_g[1]),
      kc2, bc, t0, t1)


SAMPLE_CK = 1024


def _softmax_part(s, v, transposed):
    m = jnp.max(s, axis=-1, keepdims=True)
    p = jnp.exp2(s - m)
    pv = _dot_nt(p.astype(BF16), v) if transposed else _dot(p.astype(BF16), v)
    return m, jnp.sum(p, axis=-1, keepdims=True), pv


def _merge_parts(parts):
    m = parts[0][0]
    for mt, _, _ in parts[1:]:
        m = jnp.maximum(m, mt)
    l, acc = None, None
    for mt, lt, at in parts:
        w = jnp.exp2(mt - m)
        l = w * lt if l is None else l + w * lt
        acc = w * at if acc is None else acc + w * at
    return acc / jnp.maximum(l, 1e-30)


def _attn_sample_kernel(pt_ref, q_ref, g_ref, knew_ref, wnew_ref, wnewt_ref, win_ref, kc_ref,
                        bcs_ref, bsl_ref, bnew_ref, bw_ref, msel_ref, cache_ref, o_ref, wout_ref, buf, sem,
                        *, n_pages, page, past, s_len, nsel):
    slot = _gather_pages(cache_ref, pt_ref, buf, sem, n_pages, page)

    wb = win_ref.shape[1]
    shifted = pltpu.roll(win_ref[...], wb - s_len, 1)
    fresh = pltpu.roll(wnewt_ref[...], LANES - s_len, 1)
    tail_lane = lax.broadcasted_iota(jnp.int32, fresh.shape, 1) >= LANES - s_len
    wout_ref[:, 0:wb - LANES] = shifted[:, 0:wb - LANES]
    wout_ref[:, wb - LANES:wb] = jnp.where(tail_lane, fresh, shifted[:, wb - LANES:wb])
    q = q_ref[...]
    rows = q.shape[0]
    gs = rows // GROUP
    nselp = msel_ref.shape[1]

    s = _dot_nt(q, kc_ref[:, 0:KV_HALF]) + bcs_ref[...]
    m = jnp.maximum(jnp.max(s, axis=-1, keepdims=True), MASK_FLOOR)
    e = jnp.exp2(s - m)
    p = e / jnp.maximum(jnp.sum(e, axis=-1, keepdims=True), 1e-30)
    o_cmp = _dot(p.astype(BF16), kc_ref[:, KV_HALF:KV_W])
    psum = p[0:gs]
    for r in range(1, GROUP):
        psum = psum + p[r * gs:(r + 1) * gs]

    h1, h2, h3 = _split3(psum)
    msel = msel_ref[...]
    pslc = (_dot(h1, msel) + _dot(h2, msel)) + _dot(h3, msel)
    j = lax.broadcasted_iota(jnp.int32, (gs, nselp), 1)
    qpos = past + lax.rem(lax.broadcasted_iota(jnp.int32, (gs, nselp), 0), s_len)
    cur = lax.shift_right_logical(qpos, int(math.log2(SEL_BLOCK)))
    valid = j * SEL_BLOCK <= qpos
    forced = (j == 0) | (j == cur) | (j == cur - 1)
    score = jnp.where(valid, pslc + jnp.where(forced, FORCE_BONUS, 0.0), NEG)
    score = jnp.where(j < nsel, score, -3e38)
    score_t = jnp.concatenate([score, jnp.zeros((LANES - gs, nselp), F32)], axis=0).T
    jp = lax.broadcasted_iota(jnp.int32, (nselp, nselp), 0)
    jj = lax.broadcasted_iota(jnp.int32, (nselp, nselp), 1)
    sel_rows = []
    for r in range(gs):
        col = score_t[:, r:r + 1]
        row = score[r:r + 1, :]
        beats = jnp.where(jj > jp, jnp.where(col >= row, 1.0, 0.0), jnp.where(col > row, 1.0, 0.0))
        rank = jnp.sum(beats, axis=0, keepdims=True)
        sel_rows.append(jnp.where(rank < min(TOP_N, nsel), 1.0, 0.0))
    sel8 = jnp.concatenate(sel_rows, axis=0)
    sel = jnp.concatenate([sel8] * GROUP, axis=0).astype(BF16)

    ck = bsl_ref.shape[1]
    n_chunks = past // ck
    erow = lax.broadcasted_iota(jnp.int32, (nselp, ck), 0)
    eblk = lax.shift_right_logical(lax.broadcasted_iota(jnp.int32, (nselp, ck), 1), int(math.log2(SEL_BLOCK)))
    scs = []
    for t in range(n_chunks):
        kt = buf[slot, 0:KV_HALF, t * ck:(t + 1) * ck].astype(BF16)
        e_t = jnp.where(erow == t * (ck // SEL_BLOCK) + eblk, 1.0, 0.0).astype(BF16)
        sc = _dot(q, kt) + (_dot(sel, e_t) - 1.0) * (-NEG)
        scs.append(sc + bsl_ref[...] if t == n_chunks - 1 else sc)
    knew = knew_ref[...]
    wnew = wnew_ref[...]
    s_new = _dot_nt(q, knew[:, 0:KV_HALF]) + bnew_ref[...]
    s_win = _dot(q, win_ref[0:KV_HALF, :].astype(BF16)) + bw_ref[...]
    s_wnew = _dot_nt(q, wnew[:, 0:KV_HALF]) + bnew_ref[...]
    parts = [_softmax_part(scs[t], buf[slot, KV_HALF:KV_W, t * ck:(t + 1) * ck].astype(BF16), True)
             for t in range(n_chunks)]
    parts.append(_softmax_part(s_new, knew[:, KV_HALF:KV_W], False))
    o_sel = _merge_parts(parts)

    o_win = _merge_parts([_softmax_part(s_win, win_ref[KV_HALF:KV_W, :].astype(BF16), True),
                          _softmax_part(s_wnew, wnew[:, KV_HALF:KV_W], False)])

    g = g_ref[...]
    o_ref[...] = g[:, 0:1] * o_cmp + g[:, 1:2] * o_sel + g[:, 2:3] * o_win


def _attn_sample(page_table, q32, g32, knew, wnew, wnew_t, win_t, kvc, tiles, msel, cache_t, past, s_len):
    nb, n_pages = page_table.shape
    page = cache_t.shape[2]
    rows = q32.shape[1]
    wb = win_t.shape[2]
    nsel = -(-(past + s_len) // SEL_BLOCK)
    bcs, bsl, bnew, bw = tiles
    per_b = lambda r, w: pl.BlockSpec((None, r, w), lambda b, pt: (b, 0, 0))
    return pl.pallas_call(
        functools.partial(_attn_sample_kernel, n_pages=n_pages, page=page, past=past, s_len=s_len, nsel=nsel),
        grid_spec=pltpu.PrefetchScalarGridSpec(
            num_scalar_prefetch=1,
            grid=(nb,),
            in_specs=[per_b(rows, LANES), per_b(rows, LANES), per_b(LANES, KV_W), per_b(LANES, KV_W),
                      per_b(KV_W, LANES), per_b(KV_W, wb), per_b(kvc.shape[1], KV_W),
                      _resident(bcs.shape), _resident(bsl.shape), _resident(bnew.shape),
                      _resident(bw.shape), _resident(msel.shape),
                      pl.BlockSpec(memory_space=pl.ANY)],
            out_specs=[per_b(rows, LANES), per_b(KV_W, wb)],
            scratch_shapes=[pltpu.VMEM((2, KV_W, past), F32), pltpu.SemaphoreType.DMA((2,))]),
        out_shape=[jax.ShapeDtypeStruct((nb, rows, LANES), F32), jax.ShapeDtypeStruct((nb, KV_W, wb), F32)],
        compiler_params=_cparams(("arbitrary",)),
        name="attn_sample",
    )(page_table, q32, g32, knew, wnew, wnew_t, win_t, kvc, bcs, bsl, bnew, bw, msel, cache_t)


def _prep_weights(w_in, w_cmp1, w_cmp2, pe_cmp, w_conv, w_nsa_out):
    d = w_in.shape[0]
    sizes = (N_HEADS * HEAD_DIM, KV_W, KV_W, KV_W, 3 * N_HEADS, _C_W, _C_W, _C_W, 2 * d)
    offs = [0]
    for sz in sizes:
        offs.append(offs[-1] + sz)
    part = lambda k: w_in[:, offs[k]:offs[k + 1]]
    qp = part(0).reshape(d, H_KV, GROUP, HEAD_DIM).transpose(0, 2, 1, 3).reshape(d, N_HEADS * HEAD_DIM)
    gn = jnp.pad(part(4), ((0, 0), (0, _G_W - 3 * N_HEADS)))
    w_in_p = jnp.concatenate([qp, part(1), part(2), part(3), gn, part(5), part(6), part(7), part(8)],
                             axis=1).astype(BF16)
    w_nsa_p = w_nsa_out.reshape(H_KV, GROUP, HEAD_DIM, -1).transpose(1, 0, 2, 3).reshape(
        N_HEADS * HEAD_DIM, -1).astype(BF16)
    r = CMP_BLOCK // CMP_STRIDE
    w1r = w_cmp1.reshape(2, r, CMP_STRIDE // 2, 2, HEAD_DIM, CMP_HIDDEN)
    w1t = jnp.transpose(w1r, (0, 2, 3, 4, 1, 5))
    w1e = w1t[:, :, :, :, :, None, :].astype(BF16)
    z1 = jnp.zeros_like(w1e)
    wb = jnp.stack([jnp.concatenate([w1e if gp == g else z1 for gp in range(H_KV)], axis=5)
                    for g in range(H_KV)], axis=3)
    wbig = wb.reshape(2, CMP_STRIDE // 2, 2 * KV_HALF, CMP_OUT_W)
    z2 = jnp.zeros((CMP_HIDDEN, HEAD_DIM), F32)
    w2bd = jnp.concatenate(
        [jnp.concatenate([w_cmp2[c] if (cp, gp) == (c, g) else z2 for cp in range(2) for gp in range(H_KV)], axis=1)
         for c in range(2) for g in range(H_KV)], axis=0).astype(BF16)
    pe_r = pe_cmp.reshape(2, r, CMP_STRIDE, HEAD_DIM).transpose(1, 2, 0, 3)
    pe_rows = jnp.broadcast_to(pe_r[:, :, :, None, :], (r, CMP_STRIDE, 2, H_KV, HEAD_DIM)).reshape(r, CHUNK_W)
    pe8 = jnp.pad(pe_rows, ((0, 8 - r), (0, 0)))
    w_conv8 = jnp.pad(w_conv, ((0, 8 - CONV_WIDTH), (0, 0)))
    return w_in_p, w_nsa_p, wbig, w2bd, pe8, w_conv8


def _sel_matrix(n_c, nselp):
    n = jnp.arange(n_c + 1)[:, None]
    j = jnp.arange(nselp)[None, :]
    return _sel_weights(n - (SEL_BLOCK // CMP_STRIDE) * j).astype(BF16)


def kernel(x_prompt, x_sample, cache_kv_cmp, cache_kv_sel, state_kv_win, state_conv, page_table,
           c_prompt, c_sample, w_ada, b_ada, g_norm, w_ffn1_gu, w_ffn1_down, w_ffn2_gu, w_ffn2_down,
           w_in, w_cmp1, w_cmp2, pe_cmp, w_conv, w_nsa_out, w_conv_out, w_out, rel_bias, g_final):
    assert w_ada.shape[0] == 1, "single-layer trunk"
    nbp, t, d = x_prompt.shape
    nbs, s_len, _ = x_sample.shape
    n_pages = page_table.shape[1]
    page = cache_kv_cmp.shape[2]
    past = n_pages * page
    n_phys = cache_kv_cmp.shape[1]
    wb = state_kv_win.shape[2]
    assert wb == WINDOW and past % SAMPLE_CK == 0

    w_in_p, w_nsa_p, wbig, w2bd, pe8, w_conv8 = _prep_weights(
        w_in[0], w_cmp1[0], w_cmp2[0], pe_cmp[0], w_conv[0], w_nsa_out[0])
    wgu1, wd1 = w_ffn1_gu[0].astype(BF16), w_ffn1_down[0].astype(BF16)
    wgu2, wd2 = w_ffn2_gu[0].astype(BF16), w_ffn2_down[0].astype(BF16)
    w_cv, w_o = w_conv_out[0].astype(BF16), w_out[0].astype(BF16)
    gn = [g_norm[0][k:k + 1] for k in range(N_SUB)]

    n_c_rows = nbp + nbs
    c_all = jnp.pad(jnp.concatenate([c_prompt, c_sample], axis=0), ((0, (-n_c_rows) % 8), (0, 0)))
    mod_all = _ada(c_all, w_ada[0], b_ada[0])
    mod_p = mod_all[:nbp].reshape(nbp * 3 * N_SUB, 1, d)
    mod_s = jnp.transpose(jnp.repeat(mod_all[nbp:n_c_rows].reshape(nbs, 3 * N_SUB, d), s_len, axis=0), (1, 0, 2))

    tm = TOKEN_TILE if t % TOKEN_TILE == 0 else t
    tpb = t // tm
    xp = x_prompt.reshape(nbp * t, d)
    h1 = _ffn(xp, mod_p, 0, gn[0], wgu1, wd1, tm, tpb)
    (q, kc_k, kc_v, kvc_t, kvs_t, kvw_t, ks0, ks1, vs0, vs1, kw_k, vw0, vw1, gates, z, cb, gm) = _inproj(
        h1, mod_p, gn[1], w_in_p, tm, tpb, True)
    kc2 = _cmp_prompt(kc_k, kc_v, t, wbig, pe8, w2bd)
    tiles_p = _bias_prompt(rel_bias, t // CMP_STRIDE)
    o_nsa = _attn_prompt(q, gates, (ks0, ks1), (vs0, vs1), kw_k, (vw0, vw1), kc2, tiles_p, nbp, t)
    tm_mix = 2 * tm if t % (2 * tm) == 0 else tm
    h2 = _mixout(h1, o_nsa, z, None, cb, gm, mod_p, w_conv8, w_nsa_p, w_cv, w_o, tm_mix, t // tm_mix)
    y_prompt = _ffn(h2, mod_p, 2, gn[2], wgu2, wd2, tm, tpb, g_final).reshape(nbp, t, d)

    kv_shape = (2, H_KV, HEAD_DIM)
    kv_out = lambda a: jnp.transpose(a.reshape((1, nbp) + kv_shape + (a.shape[-1],)), (0, 1, 5, 2, 3, 4))
    kv_cmp_p = kv_out(kvc_t)
    kv_sel_p = kv_out(kvs_t)
    keep = min(WINDOW, t)
    kv_win_p = kv_out(kvw_t[:, :, t - keep:])
    conv_p = z.reshape(1, nbp, t, -1)[:, :, t - (CONV_WIDTH - 1):]

    ns = nbs * s_len
    xs = x_sample.reshape(ns, d)
    h1s = _ffn(xs, mod_s, 0, gn[0], wgu1, wd1, ns, 1)
    qs, kvc_s, kvs_s, kvw_s, kvs_sb, kvw_sb, gates_s, z_s, cb_s, gm_s = _inproj(
        h1s, mod_s, gn[1], w_in_p, ns, 1, False)

    pos_minor = lambda a: jnp.transpose(a, (0, 2, 3, 4, 1)).reshape(a.shape[0], KV_W, a.shape[1])
    kvc_past = _cmp_sample(page_table, pos_minor(cache_kv_cmp[0]), wbig, pe8, w2bd)
    n_c = (past + s_len - CMP_BLOCK) // CMP_STRIDE + 1
    nsel = -(-(past + s_len) // SEL_BLOCK)
    nselp = -(-nsel // LANES) * LANES
    rows = N_HEADS * s_len
    head_of_row = [g * GROUP + r for r in range(GROUP) for g in range(H_KV) for _ in range(s_len)]
    tbl_rows = jnp.pad(rel_bias.T[jnp.array(head_of_row)], ((0, 0), (0, LANES - N_BUCKETS)))
    tiles_s = _bias_sample(tbl_rows, past, s_len, past // CMP_STRIDE, SAMPLE_CK, wb)
    msel = _sel_matrix(past // CMP_STRIDE - 1, nselp)

    q5 = qs.reshape(nbs, s_len, GROUP, H_KV, HEAD_DIM).transpose(0, 2, 3, 1, 4)
    q32 = jnp.zeros((nbs, GROUP, H_KV, s_len, H_KV, HEAD_DIM), BF16)
    for g in range(H_KV):
        q32 = q32.at[:, :, g, :, g, :].set(q5[:, :, g])
    q32 = q32.reshape(nbs, rows, LANES)
    g5 = gates_s[:, :3 * N_HEADS].reshape(nbs, s_len, 3, H_KV, GROUP).transpose(0, 4, 3, 1, 2)
    g32 = jnp.pad(g5.reshape(nbs, rows, 3), ((0, 0), (0, 0), (0, LANES - 3)))
    pad_new = lambda a: jnp.pad(a.reshape(nbs, s_len, KV_W), ((0, 0), (0, LANES - s_len), (0, 0)))
    wnew_t = jnp.pad(jnp.transpose(kvw_s.reshape(nbs, s_len, KV_W), (0, 2, 1)), ((0, 0), (0, 0), (0, LANES - s_len)))
    o32, win_next = _attn_sample(page_table, q32, g32, pad_new(kvs_sb), pad_new(kvw_sb), wnew_t,
                                 pos_minor(state_kv_win[0]), kvc_past, tiles_s, msel,
                                 pos_minor(cache_kv_sel[0]), past, s_len)
    o6 = o32.reshape(nbs, GROUP, H_KV, s_len, H_KV, HEAD_DIM)
    o_s = jnp.stack([o6[:, :, g, :, g, :] for g in range(H_KV)], axis=3)
    o_nsa_s = o_s.transpose(0, 2, 1, 3, 4).reshape(ns, N_HEADS * HEAD_DIM).astype(BF16)

    full = jnp.concatenate([state_conv[0], z_s.reshape(nbs, s_len, -1)], axis=1)
    z_shift = (full[:, 1:1 + s_len].reshape(ns, -1), full[:, 0:s_len].reshape(ns, -1))
    h2s = _mixout(h1s, o_nsa_s, z_s, z_shift, cb_s, gm_s, mod_s, w_conv8, w_nsa_p, w_cv, w_o, ns, 1)
    y_sample = _ffn(h2s, mod_s, 2, gn[2], wgu2, wd2, ns, 1, g_final).reshape(nbs, s_len, d)

    kv_cmp_s = kvc_s.reshape((1, nbs, s_len) + kv_shape)
    kv_sel_s = kvs_s.reshape((1, nbs, s_len) + kv_shape)
    kv_win_s = jnp.transpose(win_next.reshape((1, nbs) + kv_shape + (wb,)), (0, 1, 5, 2, 3, 4))
    conv_s = full[None, :, s_len:]
    return (y_prompt, y_sample, kv_cmp_p, kv_sel_p, kv_win_p, conv_p, kv_cmp_s, kv_sel_s, kv_win_s, conv_s)
```

```python
import functools
import math

import jax
import jax.numpy as jnp
from jax import lax
from jax.experimental import pallas as pl
from jax.experimental.pallas import tpu as pltpu

F32 = jnp.float32
BF16 = jnp.bfloat16

HEAD_DIM = 64
N_HEADS = 8
H_KV = 2
GROUP = N_HEADS // H_KV
CMP_BLOCK = 32
CMP_STRIDE = 16
CMP_HIDDEN = 2 * HEAD_DIM
SEL_BLOCK = 64
TOP_N = 16
WINDOW = 512
CONV_WIDTH = 3
N_BUCKETS = 32
MAX_DISTANCE = 128
N_SUB = 3
EPS = 1e-6
NEG = -1e9
LOG2E = math.log2(math.e)
FORCE_BONUS = 1e3
KV_W = 2 * H_KV * HEAD_DIM
KV_HALF = H_KV * HEAD_DIM
LANES = 128
Q_TILE = 256
CMP_NEAR = 32
VMEM_LIMIT = 56 * 1024 * 1024
TOKEN_TILE = 512
ADA_COL_TILES = 8


def _bucket_thresholds():
    max_exact = N_BUCKETS // 2

    def bucket(d):
        if d < max_exact:
            return d
        large = max_exact + int(math.log(d / max_exact) / math.log(MAX_DISTANCE / max_exact)
                                * (N_BUCKETS - max_exact))
        return min(large, N_BUCKETS - 1)

    thr, d = [], 0
    for b in range(N_BUCKETS):
        while bucket(d) < b:
            d += 1
        thr.append(d)
    return tuple(thr)


BUCKET_THR = _bucket_thresholds()


def _cparams(sem):
    return pltpu.CompilerParams(dimension_semantics=sem, vmem_limit_bytes=VMEM_LIMIT)


def _resident(shape):
    nd = len(shape)
    return pl.BlockSpec(shape, lambda *_: (0,) * nd, pipeline_mode=pl.Buffered(1))


def _dot(a, b):
    return jnp.dot(a, b, preferred_element_type=F32)


def _dot_nt(a, b):
    return lax.dot_general(a, b, (((1,), (1,)), ((), ())), preferred_element_type=F32)


def _split3(x):
    h1 = x.astype(BF16)
    r1 = x - h1.astype(F32)
    h2 = r1.astype(BF16)
    h3 = (r1 - h2.astype(F32)).astype(BF16)
    return h1, h2, h3


def _modulated_norm(x, g, shift, scale):
    y = x * lax.rsqrt(jnp.mean(x * x, axis=-1, keepdims=True) + EPS)
    return (y * g) * (1.0 + scale) + shift


def _ada_kernel(c_ref, w_ref, b_ref, o_ref):
    c = c_ref[...]
    a = (c * jax.nn.sigmoid(c)).astype(BF16)
    o_ref[...] = _dot(a, w_ref[...].astype(BF16)) + b_ref[...]


def _ada(c_all, w_ada, b_ada):
    rows, d = c_all.shape
    n = w_ada.shape[1]
    tn = n // ADA_COL_TILES
    return pl.pallas_call(
        _ada_kernel,
        grid=(n // tn,),
        in_specs=[pl.BlockSpec((rows, d), lambda j: (0, 0)),
                  pl.BlockSpec((d, tn), lambda j: (0, j)),
                  pl.BlockSpec((1, tn), lambda j: (0, j))],
        out_specs=pl.BlockSpec((rows, tn), lambda j: (0, j)),
        out_shape=jax.ShapeDtypeStruct((rows, n), F32),
        compiler_params=_cparams(("arbitrary",)),
        name="ada",
    )(c_all, w_ada, b_ada.reshape(1, n))


def _ffn_kernel(*refs, d_ff, fc, final_norm):
    if final_norm:
        x_ref, sh_ref, sc_ref, gt_ref, gn_ref, wgu_ref, wd_ref, gf_ref, o_ref = refs
    else:
        x_ref, sh_ref, sc_ref, gt_ref, gn_ref, wgu_ref, wd_ref, o_ref = refs
    x = x_ref[...]
    ub = _modulated_norm(x, gn_ref[...], sh_ref[...], sc_ref[...]).astype(BF16)
    acc = jnp.zeros(x.shape, F32)
    for c in range(d_ff // fc):
        g = _dot(ub, wgu_ref[:, c * fc:(c + 1) * fc])
        v = _dot(ub, wgu_ref[:, d_ff + c * fc:d_ff + (c + 1) * fc])
        a = ((g * jax.nn.sigmoid(g)) * v).astype(BF16)
        acc = acc + _dot(a, wd_ref[c * fc:(c + 1) * fc, :])
    h = x + (0.5 * gt_ref[...]) * acc
    if final_norm:
        h = (h * lax.rsqrt(jnp.mean(h * h, axis=-1, keepdims=True) + EPS)) * gf_ref[...]
    o_ref[...] = h


def _mod_specs(mod, ks, tm, tiles_per_batch):
    if mod.ndim == 3 and mod.shape[1] == 1:
        d = mod.shape[-1]
        specs = [pl.BlockSpec((None, 1, d), lambda i, k=k: ((i // tiles_per_batch) * (3 * N_SUB) + k, 0, 0))
                 for k in ks]
        return specs, [mod] * len(ks)
    d = mod.shape[-1]
    specs = [pl.BlockSpec((None, tm, d), lambda i, k=k: (k, i, 0)) for k in ks]
    return specs, [mod] * len(ks)


def _ffn(x, mod, sub, gn_row, w_gu, w_down, tm, tiles_per_batch, g_final=None):
    n, d = x.shape
    d_ff = w_down.shape[0]
    fc = d_ff // 2 if (d_ff // 2) % LANES == 0 else d_ff
    final_norm = g_final is not None
    mspecs, mops = _mod_specs(mod, (3 * sub, 3 * sub + 1, 3 * sub + 2), tm, tiles_per_batch)
    in_specs = [pl.BlockSpec((tm, d), lambda i: (i, 0))] + mspecs + [
        pl.BlockSpec((1, d), lambda i: (0, 0)), _resident(w_gu.shape), _resident(w_down.shape)]
    ops = [x] + mops + [gn_row, w_gu, w_down]
    if final_norm:
        in_specs.append(pl.BlockSpec((1, d), lambda i: (0, 0)))
        ops.append(g_final.reshape(1, d))
    return pl.pallas_call(
        functools.partial(_ffn_kernel, d_ff=d_ff, fc=fc, final_norm=final_norm),
        grid=(n // tm,),
        in_specs=in_specs,
        out_specs=pl.BlockSpec((tm, d), lambda i: (i, 0)),
        out_shape=jax.ShapeDtypeStruct((n, d), F32),
        compiler_params=_cparams(("arbitrary",)),
        name="ffn_final" if final_norm else "ffn",
    )(*ops)


_Q_W = N_HEADS * HEAD_DIM
_G_W = LANES
_C_W = 512
_SEG = {}
_off = 0
for _name, _w in (("q", _Q_W), ("kc", KV_W), ("ks", KV_W), ("kw", KV_W), ("gn", _G_W),
                  ("ch", _C_W), ("cb", _C_W), ("cc", _C_W)):
    _SEG[_name] = (_off, _off + _w)
    _off += _w
_MG_OFF = _off


V_ROWS = HEAD_DIM + 16


def _aug_lane(g):
    return HEAD_DIM if g == 0 else 0


def _inproj_kernel(*refs, prompt):
    x_ref, sh_ref, sc_ref, gn_ref, w_ref = refs[:5]
    ub = _modulated_norm(x_ref[...], gn_ref[...], sh_ref[...], sc_ref[...]).astype(BF16)

    def seg(name):
        lo, hi = _SEG[name]
        return _dot(ub, w_ref[:, lo:hi])

    if prompt:
        (q_ref, kck_ref, kcv_ref, kct_ref, kst_ref, kwt_ref, ks0_ref, ks1_ref, vs0_ref, vs1_ref, kwk_ref,
         vw0_ref, vw1_ref, g_ref, z_ref, cb_ref, gm_ref) = refs[5:]
        kc = seg("kc")
        kck_ref[...] = kc[:, 0:KV_HALF]
        kcv_ref[...] = kc[:, KV_HALF:KV_W]
        kct_ref[...] = kc.T
        tm = kc.shape[0]
        lane = lax.broadcasted_iota(jnp.int32, (tm, KV_HALF), 1)
        blk = lax.shift_right_logical(lax.broadcasted_iota(jnp.int32, (tm, KV_HALF), 0),
                                      int(math.log2(SEL_BLOCK))) & (Q_TILE // SEL_BLOCK - 1)
        ones_rows = jnp.where(lax.broadcasted_iota(jnp.int32, (V_ROWS - HEAD_DIM, tm), 0) == 0, 1.0, 0.0)
        for name, t_ref, k_refs, vt_refs in (("ks", kst_ref, (ks0_ref, ks1_ref), (vs0_ref, vs1_ref)),
                                             ("kw", kwt_ref, (kwk_ref,), (vw0_ref, vw1_ref))):
            kv = seg(name)
            kv_t = kv.T
            t_ref[...] = kv_t
            kk = kv[:, 0:KV_HALF]
            if len(k_refs) == 1:
                k_refs[0][...] = kk.astype(BF16)
            else:
                for g, k_ref in enumerate(k_refs):
                    own = (lane < HEAD_DIM) if g == 0 else (lane >= HEAD_DIM)
                    onehot = jnp.where(lane - _aug_lane(g) == blk, 1.0, 0.0)
                    k_ref[...] = jnp.where(own, kk, onehot).astype(BF16)
            for g, vt_ref in enumerate(vt_refs):
                vg = jnp.concatenate(
                    [kv_t[KV_HALF + g * HEAD_DIM:KV_HALF + (g + 1) * HEAD_DIM, :], ones_rows], axis=0)
                for c in range(vt_ref.shape[0]):
                    vt_ref[c] = vg[:, c * Q_TILE:(c + 1) * Q_TILE].astype(BF16)
    else:
        q_ref, kc_ref, ks_ref, kw_ref, ksb_ref, kwb_ref, g_ref, z_ref, cb_ref, gm_ref = refs[5:]
        kc_ref[...] = seg("kc")
        ks = seg("ks")
        ks_ref[...] = ks
        ksb_ref[...] = ks.astype(BF16)
        kw = seg("kw")
        kw_ref[...] = kw
        kwb_ref[...] = kw.astype(BF16)
    q_ref[...] = (seg("q") * (HEAD_DIM ** -0.5 * LOG2E)).astype(BF16)
    g_ref[...] = jax.nn.sigmoid(seg("gn"))
    z_ref[...] = seg("cc") * seg("ch")
    cb_ref[...] = seg("cb").astype(cb_ref.dtype)
    d2 = gm_ref.shape[1]
    half = d2 // 2
    for c in range(2):
        gm_ref[:, c * half:(c + 1) * half] = jax.nn.sigmoid(
            _dot(ub, w_ref[:, _MG_OFF + c * half:_MG_OFF + (c + 1) * half])).astype(gm_ref.dtype)


def _inproj(h, mod, gn_row, w_in_p, tm, tiles_per_batch, prompt):
    n, d = h.shape
    d_conv = _C_W
    mspecs, mops = _mod_specs(mod, (3, 4), tm, tiles_per_batch)
    rows = lambda w, dt: (pl.BlockSpec((tm, w), lambda i: (i, 0)), jax.ShapeDtypeStruct((n, w), dt))
    tail = [rows(_G_W, F32), rows(d_conv, F32), rows(d_conv, BF16), rows(2 * d, BF16)]
    if prompt:
        nb = n // (tm * tiles_per_batch)
        t = tm * tiles_per_batch
        cpt = tm // Q_TILE
        tr = (pl.BlockSpec((None, KV_W, tm), lambda i: (i // tiles_per_batch, 0, i % tiles_per_batch)),
              jax.ShapeDtypeStruct((nb, KV_W, t), F32))
        vt = (pl.BlockSpec((cpt, V_ROWS, Q_TILE), lambda i: (i, 0, 0)),
              jax.ShapeDtypeStruct((n // Q_TILE, V_ROWS, Q_TILE), BF16))
        kr = rows(KV_HALF, BF16)
        outs = [rows(_Q_W, BF16), rows(KV_HALF, F32), rows(KV_HALF, F32), tr, tr, tr,
                kr, kr, vt, vt, kr, vt, vt] + tail
    else:
        outs = [rows(_Q_W, BF16), rows(KV_W, F32), rows(KV_W, F32), rows(KV_W, F32),
                rows(KV_W, BF16), rows(KV_W, BF16)] + tail
    return pl.pallas_call(
        functools.partial(_inproj_kernel, prompt=prompt),
        grid=(n // tm,),
        in_specs=[pl.BlockSpec((tm, d), lambda i: (i, 0))] + mspecs + [
            pl.BlockSpec((1, d), lambda i: (0, 0)), _resident(w_in_p.shape)],
        out_specs=[o[0] for o in outs],
        out_shape=[o[1] for o in outs],
        compiler_params=_cparams(("arbitrary",)),
        name="inproj_prompt" if prompt else "inproj",
    )(h, *mops, gn_row, w_in_p)


def _mixout_kernel(*refs, halo, tiles_per_batch):
    if halo:
        (h_ref, o_ref, z_ref, zp_ref, cb_ref, gm_ref, g2_ref, wc_ref,
         wn_ref, wcv_ref, wo_ref, out_ref) = refs
        z = z_ref[...]
        tm = z.shape[0]
        first = (pl.program_id(0) % tiles_per_batch) == 0
        prev = jnp.where(first, 0.0, zp_ref[...])
        row = lax.broadcasted_iota(jnp.int32, z.shape, 0)
        zm1 = jnp.where(row == 0, prev[7:8, :], pltpu.roll(z, 1, 0))
        zm2 = jnp.where(row == 0, prev[6:7, :], jnp.where(row == 1, prev[7:8, :], pltpu.roll(z, 2, 0)))
    else:
        (h_ref, o_ref, z_ref, zm1_ref, zm2_ref, cb_ref, gm_ref, g2_ref, wc_ref,
         wn_ref, wcv_ref, wo_ref, out_ref) = refs
        z, zm1, zm2 = z_ref[...], zm1_ref[...], zm2_ref[...]
    conv = wc_ref[0:1, :] * zm2 + wc_ref[1:2, :] * zm1 + wc_ref[2:3, :] * z
    y = (cb_ref[...].astype(F32) * conv).astype(BF16)
    d = h_ref.shape[1]
    merged = gm_ref[:, 0:d].astype(F32) * _dot(o_ref[...], wn_ref[...]) \
        + gm_ref[:, d:2 * d].astype(F32) * _dot(y, wcv_ref[...])
    out_ref[...] = h_ref[...] + g2_ref[...] * _dot(merged.astype(BF16), wo_ref[...])


def _mixout(h, o_nsa, z, z_shift, cb, gm, mod, w_conv8, w_nsa_p, w_cv, w_o, tm, tiles_per_batch):
    n, d = h.shape
    dc = z.shape[1]
    halo = z_shift is None
    mspecs, mops = _mod_specs(mod, (5,), tm, tiles_per_batch)
    tok = lambda w: pl.BlockSpec((tm, w), lambda i: (i, 0))
    if halo:
        zspecs = [tok(dc), pl.BlockSpec((8, dc), lambda i: (jnp.maximum(i * (tm // 8) - 1, 0), 0))]
        zops = [z, z]
    else:
        zspecs = [tok(dc), tok(dc), tok(dc)]
        zops = [z, z_shift[0], z_shift[1]]
    return pl.pallas_call(
        functools.partial(_mixout_kernel, halo=halo, tiles_per_batch=tiles_per_batch),
        grid=(n // tm,),
        in_specs=[tok(d), tok(o_nsa.shape[1])] + zspecs + [tok(dc), tok(2 * d)] + mspecs + [
            pl.BlockSpec((8, dc), lambda i: (0, 0)),
            _resident(w_nsa_p.shape), _resident(w_cv.shape), _resident(w_o.shape)],
        out_specs=tok(d),
        out_shape=jax.ShapeDtypeStruct((n, d), F32),
        compiler_params=_cparams(("arbitrary",)),
        name="mixout",
    )(h, o_nsa, *zops, cb, gm, *mops, w_conv8, w_nsa_p, w_cv, w_o)


CHUNK_W = CMP_STRIDE * KV_W
CMP_OUT_W = 2 * H_KV * CMP_HIDDEN


def _gelu_tanh(x):
    return x * (0.5 * (1.0 + jnp.tanh(math.sqrt(2.0 / math.pi) * (x + 0.044715 * (x * x * x)))))


def _chunk_partials(tile, wc_ref, after_step=None):
    accs = []
    for c in range(2):
        acc = None
        for lp in range(CMP_STRIDE // 2):
            x2 = jnp.concatenate([tile(2 * lp, c), tile(2 * lp + 1, c)], axis=1)
            part = _dot(x2, wc_ref[c, lp])
            acc = part if acc is None else acc + part
            if after_step is not None:
                after_step(c, lp)
        accs.append(acc)
    hw = CMP_OUT_W // 2
    return jnp.concatenate([accs[0][:, 0:hw], accs[1][:, 0:hw], accs[0][:, hw:], accs[1][:, hw:]], axis=1)


def _compress_tail(p, wc_ref, pe_ref, w2_ref):
    n_ch = p.shape[0]
    pb = _chunk_partials(
        lambda l, c: pe_ref[:, l * KV_W + c * KV_HALF:l * KV_W + (c + 1) * KV_HALF].astype(BF16), wc_ref)
    pre = (pb[0:1, 0:CMP_OUT_W] + pb[1:2, CMP_OUT_W:]) + p[:, 0:CMP_OUT_W] \
        + pltpu.roll(p[:, CMP_OUT_W:], n_ch - 1, 0)
    out = _dot(_gelu_tanh(pre).astype(BF16), w2_ref[...])
    row = lax.broadcasted_iota(jnp.int32, out.shape, 0)
    return jnp.where(row < n_ch - 1, out, 0.0)


def _cmp_prompt_kernel(xk_ref, xv_ref, wc_ref, pe_ref, w2_ref, o_ref):
    xs = (xk_ref, xv_ref)
    n_ch = xk_ref.shape[0] // CMP_STRIDE
    p = _chunk_partials(lambda l, c: xs[c][pl.ds(l, n_ch, stride=CMP_STRIDE), :].astype(BF16), wc_ref)
    out = _compress_tail(p, wc_ref, pe_ref, w2_ref).astype(BF16)
    o_ref[0:n_ch, :] = out
    o_ref[n_ch:2 * n_ch, :] = out


def _cmp_prompt(rows_k, rows_v, t, wbig, pe8, w2bd):
    nb = rows_k.shape[0] // t
    n_ch = t // CMP_STRIDE
    half = pl.BlockSpec((t, KV_HALF), lambda b: (b, 0))
    return pl.pallas_call(
        _cmp_prompt_kernel,
        grid=(nb,),
        in_specs=[half, half, _resident(wbig.shape), _resident(pe8.shape), _resident(w2bd.shape)],
        out_specs=pl.BlockSpec((None, 2 * n_ch, KV_W), lambda b: (b, 0, 0)),
        out_shape=jax.ShapeDtypeStruct((nb, 2 * n_ch, KV_W), BF16),
        compiler_params=_cparams(("arbitrary",)),
        name="cmp_prompt",
    )(rows_k, rows_v, wbig, pe8, w2bd)


def _page_copies(cache_ref, pt_ref, b, dst, sem, n_pages, page):
    return [pltpu.make_async_copy(cache_ref.at[pt_ref[b, p]], dst.at[:, pl.ds(p * page, page)], sem)
            for p in range(n_pages)]


def _gather_pages(cache_ref, pt_ref, buf, sem, n_pages, page):
    b = pl.program_id(0)
    nb = pl.num_programs(0)
    slot = b % 2
    copies = lambda bb, sl: _page_copies(cache_ref, pt_ref, bb, buf.at[sl], sem.at[sl], n_pages, page)

    @pl.when(b == 0)
    def _():
        for c in copies(0, 0):
            c.start()

    @pl.when(b + 1 < nb)
    def _():
        for c in copies(b + 1, 1 - slot):
            c.start()

    for c in copies(b, slot):
        c.wait()
    return slot


def _cmp_sample_kernel(pt_ref, cache_ref, wc_ref, pe_ref, w2_ref, o_ref,
                       buf_a, buf_b, xk_a, xv_a, xk_b, xv_b, sem, *, n_pages, page, nb):
    b = pl.program_id(0)
    bufs = (buf_a, buf_b)
    xs = ((xk_a, xv_a), (xk_b, xv_b))
    past = xk_a.shape[0]
    n_ch = past // CMP_STRIDE
    n_slabs = CMP_STRIDE // 2
    slab = past // n_slabs
    copies = lambda bb, par: _page_copies(cache_ref, pt_ref, bb, bufs[par], sem.at[par], n_pages, page)

    def transpose_slab(par, c, j):
        xs[par][c][j * slab:(j + 1) * slab, :] = bufs[par][c * KV_HALF:(c + 1) * KV_HALF, j * slab:(j + 1) * slab].T

    @pl.when(b == 0)
    def _():
        for cp in copies(0, 0):
            cp.start()
        if nb > 1:
            for cp in copies(1, 1):
                cp.start()
        for cp in copies(0, 0):
            cp.wait()
        for c in range(2):
            for j in range(n_slabs):
                transpose_slab(0, c, j)

    def step(par):
        nxt = 1 - par
        hook = None
        if nb > 1:
            @pl.when(b + 1 < nb)
            def _():
                for cp in copies(b + 1, nxt):
                    cp.wait()

            @pl.when(b + 2 < nb)
            def _():
                for cp in copies(b + 2, par):
                    cp.start()

            hook = lambda c, lp: transpose_slab(nxt, c, lp)
        p = _chunk_partials(lambda l, c: xs[par][c][pl.ds(l, n_ch, stride=CMP_STRIDE), :].astype(BF16), wc_ref, hook)
        o_ref[...] = _compress_tail(p, wc_ref, pe_ref, w2_ref).astype(BF16)

    for par in range(2):
        pl.when(b % 2 == par)(functools.partial(step, par))


def _cmp_sample(page_table, cache_t, wbig, pe8, w2bd):
    nb, n_pages = page_table.shape
    page = cache_t.shape[2]
    past = n_pages * page
    n_ch = past // CMP_STRIDE
    return pl.pallas_call(
        functools.partial(_cmp_sample_kernel, n_pages=n_pages, page=page, nb=nb),
        grid_spec=pltpu.PrefetchScalarGridSpec(
            num_scalar_prefetch=1,
            grid=(nb,),
            in_specs=[pl.BlockSpec(memory_space=pl.ANY),
                      _resident(wbig.shape), _resident(pe8.shape), _resident(w2bd.shape)],
            out_specs=pl.BlockSpec((None, n_ch, KV_W), lambda b, pt: (b, 0, 0)),
            scratch_shapes=[pltpu.VMEM((KV_W, past), F32)] * 2 + [pltpu.VMEM((past, KV_HALF), F32)] * 4
            + [pltpu.SemaphoreType.DMA((2,))]),
        out_shape=jax.ShapeDtypeStruct((nb, n_ch, KV_W), BF16),
        compiler_params=_cparams(("arbitrary",)),
        name="cmp_sample",
    )(page_table, cache_t, wbig, pe8, w2bd)


def _bias_values(d, table):
    acc = jnp.zeros(d.shape, F32) + table(0)
    for b in range(1, N_BUCKETS):
        acc = jnp.where(d >= BUCKET_THR[b], table(b), acc)
    return jnp.where(d >= 0, (acc - table(N_BUCKETS - 1)) * LOG2E, NEG)


CMP_BACK = 16


def _bias_prompt_kernel(tbl_ref, bc_ref, t0_ref, t1_ref):
    h = pl.program_id(0)
    table = lambda b: tbl_ref[b, h]
    ncp, tq = bc_ref.shape
    kj = lax.broadcasted_iota(jnp.int32, (tq, tq), 0)
    qi = lax.broadcasted_iota(jnp.int32, (tq, tq), 1)
    t0_ref[...] = _bias_values(qi - kj, table)
    t1_ref[...] = _bias_values(qi - kj + tq, table)
    c = lax.broadcasted_iota(jnp.int32, (ncp, tq), 0)
    qi = lax.broadcasted_iota(jnp.int32, (ncp, tq), 1)
    d = qi - CMP_STRIDE * c + (CMP_STRIDE * CMP_BACK - (CMP_BLOCK - 1))
    bc_ref[...] = jnp.where(c < CMP_NEAR, _bias_values(d, table), 0.0)


def _bias_prompt(rel_bias, ncp):
    tq = Q_TILE
    return pl.pallas_call(
        _bias_prompt_kernel,
        grid=(N_HEADS,),
        in_specs=[pl.BlockSpec(memory_space=pltpu.SMEM)],
        out_specs=[pl.BlockSpec((None, ncp, tq), lambda h: (h, 0, 0)),
                   pl.BlockSpec((None, tq, tq), lambda h: (h, 0, 0)),
                   pl.BlockSpec((None, tq, tq), lambda h: (h, 0, 0))],
        out_shape=[jax.ShapeDtypeStruct((N_HEADS, ncp, tq), F32),
                   jax.ShapeDtypeStruct((N_HEADS, tq, tq), F32),
                   jax.ShapeDtypeStruct((N_HEADS, tq, tq), F32)],
        compiler_params=_cparams(("arbitrary",)),
        name="bias_prompt",
    )(rel_bias)


def _bias_sample_kernel(tbl_ref, bcs_ref, bsl_ref, bnew_ref, bw_ref, *, past, s_len, n_c):
    table = lambda b: tbl_ref[:, b:b + 1]

    def qpos(shape):
        return past + lax.rem(lax.broadcasted_iota(jnp.int32, shape, 0), s_len)

    def lane(shape):
        return lax.broadcasted_iota(jnp.int32, shape, 1)

    sh = bcs_ref.shape
    n = lane(sh)
    bcs_ref[...] = jnp.where(n < n_c, _bias_values(qpos(sh) - (CMP_STRIDE * n + CMP_BLOCK - 1), table), NEG)
    sh = bsl_ref.shape
    bsl_ref[...] = _bias_values(qpos(sh) - (past - sh[1] + lane(sh)), table)
    sh = bnew_ref.shape
    bnew_ref[...] = jnp.where(lane(sh) < s_len, _bias_values(qpos(sh) - (past + lane(sh)), table), NEG)
    sh = bw_ref.shape
    d = qpos(sh) - (past - sh[1] + lane(sh))
    bw_ref[...] = jnp.where(d < WINDOW, _bias_values(d, table), NEG)


def _bias_sample(tbl_rows, past, s_len, ncs, ck, wb):
    rows = tbl_rows.shape[0]
    widths = (ncs, ck, LANES, wb)
    return pl.pallas_call(
        functools.partial(_bias_sample_kernel, past=past, s_len=s_len, n_c=(past + s_len - CMP_BLOCK) // CMP_STRIDE + 1),
        out_shape=[jax.ShapeDtypeStruct((rows, w), F32) for w in widths],
        name="bias_sample",
    )(tbl_rows)


MASK_FLOOR = -1e8
M_INIT = -1e30
TAKEN = -3e38
SCORE_AHEAD = 3
FAR_PER_ITER = 6
VALUE_BEHIND = 2


def _sel_weights(v):
    return jnp.where((v >= 0) & (v <= 2), 2.0, jnp.where((v == -1) | (v == 3), 1.0, 0.0))


def _attn_prompt_kernel(q_ref, gate_ref, ks0_ref, ks1_ref, vs0_ref, vs1_ref, kwk_ref, vw0_ref, vw1_ref,
                        kc2_ref, bc_ref, t0_ref, t1_ref, o_ref,
                        qm_ref, selr_ref, oc_ref, ms_ref, as_ref, mw_ref, aw_ref, *, ncp, nsel):
    tq = Q_TILE
    i = pl.program_id(1)
    q0 = i * tq
    lane = lax.broadcasted_iota(jnp.int32, (tq, LANES), 1)
    heads = [(g, r) for g in range(H_KV) for r in range(GROUP)]

    for g, r in heads:
        gmask = (lane < HEAD_DIM) if g == 0 else (lane >= HEAD_DIM)
        qm_ref[g * GROUP + r] = jnp.where(gmask, q_ref[:, r * LANES:(r + 1) * LANES], jnp.zeros((tq, LANES), BF16))

    n0 = (tq // CMP_STRIDE) * i - CMP_BACK
    ws = pl.multiple_of(lax.rem(n0 + ncp, ncp), 8)
    cl = lax.broadcasted_iota(jnp.int32, (ncp, tq), 0)
    dm = jnp.where((cl >= CMP_NEAR) & (cl < ncp - n0), NEG, 0.0) + jnp.where(cl < -n0, NEG, 0.0)
    jrow = lax.broadcasted_iota(jnp.int32, (nsel, ncp), 0)
    nn = lax.broadcasted_iota(jnp.int32, (nsel, ncp), 1) + n0
    nn = jnp.where(nn < 0, nn + ncp, nn)
    nn = jnp.where(nn >= ncp, nn - ncp, nn)
    mit = _sel_weights(nn - (SEL_BLOCK // CMP_STRIDE) * jrow).astype(BF16)

    jr = lax.broadcasted_iota(jnp.int32, (nsel, tq), 0)
    qpos = q0 + lax.broadcasted_iota(jnp.int32, (nsel, tq), 1)
    cur = lax.shift_right_logical(qpos, int(math.log2(SEL_BLOCK)))
    invalid_pen = jnp.where(jr * SEL_BLOCK <= qpos, 0.0, NEG)
    bonus = jnp.where((jr == 0) | (jr == cur) | (jr == cur - 1), FORCE_BONUS, 0.0)
    pj = lax.broadcasted_iota(jnp.int32, (LANES, LANES), 0)
    pl_ = lax.broadcasted_iota(jnp.int32, (LANES, LANES), 1)
    kj = lax.broadcasted_iota(jnp.int32, (tq, tq), 0)
    qi = lax.broadcasted_iota(jnp.int32, (tq, tq), 1)
    wmask = jnp.where(kj > qi, 0.0, NEG)
    blocks_per_chunk = tq // SEL_BLOCK

    kc = kc2_ref[pl.ds(ws, ncp), 0:KV_HALF]
    vct = kc2_ref[pl.ds(ws, ncp), KV_HALF:KV_W].astype(F32).T.astype(BF16)
    def cmp_scores(h):
        return _dot_nt(kc, qm_ref[h]) + bc_ref[h] + dm

    ahead = {h: cmp_scores(h) for h in range(2)}
    psum = [None] * H_KV
    for h in range(N_HEADS):
        s = ahead.pop(h)
        if h + 2 < N_HEADS:
            ahead[h + 2] = cmp_scores(h + 2)
        m = jnp.maximum(jnp.max(s, axis=0, keepdims=True), MASK_FLOOR)
        e = jnp.exp2(s - m)
        p = e * (1.0 / jnp.maximum(jnp.sum(e, axis=0, keepdims=True), 1e-30))
        oc_ref[h] = _dot(vct, p.astype(BF16))
        psum[h // GROUP] = p if psum[h // GROUP] is None else psum[h // GROUP] + p

    scores = []
    for g in range(H_KV):
        h1, h2, h3 = _split3(psum[g])
        pslc = (_dot(mit, h1) + _dot(mit, h2)) + _dot(mit, h3)
        scores.append(jnp.where(invalid_pen < 0.0, NEG, pslc + bonus))

    def take_max(_, carry):
        out = []
        for work, sel_acc in carry:
            mx = jnp.max(work, axis=0, keepdims=True)
            first = jnp.min(jnp.where(work == mx, jr, nsel), axis=0, keepdims=True)
            hit = jr == first
            out.append((jnp.where(hit, TAKEN, work), jnp.where(hit, 1.0, sel_acc)))
        return tuple(out)

    picked = lax.fori_loop(0, min(TOP_N, nsel), take_max,
                           tuple((sc, jnp.zeros((nsel, tq), F32)) for sc in scores))
    for g in range(H_KV):
        sel_t = picked[g][1]
        if nsel < LANES:
            sel_t = jnp.concatenate([sel_t, jnp.zeros((LANES - nsel, tq), F32)], axis=0)
        selr_ref[g] = sel_t.T.astype(BF16)

    def masked_queries(t, g):
        off = pl_ - _aug_lane(g)
        place = jnp.where((pj - t * blocks_per_chunk == off) & (off >= 0) & (off < blocks_per_chunk), 1.0, 0.0)
        picked = _dot(selr_ref[g], place.astype(BF16))
        auglane = (lane >= _aug_lane(g)) & (lane < _aug_lane(g) + blocks_per_chunk)
        pen = jnp.where(auglane, (picked - 1.0) * (-NEG), 0.0).astype(BF16)
        return [qm_ref[g * GROUP + r] + pen for r in range(GROUP)]

    def run(chunks):
        items = []
        for k_of, vt_of, queries, extra, m_ref, a_ref in chunks:
            cache = {}
            items += [(h, k_of, vt_of, queries, extra, m_ref, a_ref, cache) for h in range(N_HEADS)]

        def kv(item):
            g, cache = item[0] // GROUP, item[7]
            if g not in cache:
                cache[g] = (item[1](g), item[2](g))
            return cache[g]

        def scores(item):
            h, queries, extra = item[0], item[3], item[4]
            s = _dot_nt(kv(item)[0], queries(h))
            return s if extra is None else s + extra(h)

        def softmax(item, s):
            h, m_ref = item[0], item[5]
            m_prev = m_ref[h]
            m_new = jnp.maximum(m_prev, jnp.max(s, axis=0, keepdims=True))
            m_ref[h] = m_new
            return jnp.exp2(s - m_new).astype(BF16), jnp.exp2(m_prev - m_new)

        def accumulate(item, p, alpha):
            h, a_ref = item[0], item[6]
            a_ref[h] = alpha * a_ref[h] + _dot(kv(item)[1], p)

        n = len(items)
        s = {j: scores(items[j]) for j in range(SCORE_AHEAD)}
        pa = {}
        for j in range(n):
            pa[j] = softmax(items[j], s.pop(j))
            if j + SCORE_AHEAD < n:
                s[j + SCORE_AHEAD] = scores(items[j + SCORE_AHEAD])
            if j >= VALUE_BEHIND:
                accumulate(items[j - VALUE_BEHIND], *pa.pop(j - VALUE_BEHIND))
        for j in range(n - VALUE_BEHIND, n):
            accumulate(items[j], *pa.pop(j))

    for m_ref, a_ref in ((ms_ref, as_ref), (mw_ref, aw_ref)):
        m_ref[...] = jnp.full(m_ref.shape, M_INIT, F32)
        a_ref[...] = jnp.zeros(a_ref.shape, F32)

    def sel_chunk(t, extra):
        sl = pl.ds(pl.multiple_of(t * tq, tq), tq)
        qs = [masked_queries(t, g) for g in range(H_KV)]
        return (lambda g: (ks0_ref, ks1_ref)[g][sl, :], lambda g: (vs0_ref, vs1_ref)[g][t],
                lambda h: qs[h // GROUP][h % GROUP], extra, ms_ref, as_ref)

    def win_chunk(t, extra):
        sl = pl.ds(pl.multiple_of(t * tq, tq), tq)
        return (lambda g: kwk_ref[sl, :], lambda g: (vw0_ref, vw1_ref)[g][t],
                lambda h: qm_ref[h], extra, mw_ref, aw_ref)

    n_far = jnp.maximum(i - 1, 0)
    n_iter = n_far // FAR_PER_ITER

    def far_group(u, carry):
        run([sel_chunk(FAR_PER_ITER * u + j, None) for j in range(FAR_PER_ITER)])
        return carry

    lax.fori_loop(0, n_iter, far_group, 0)
    for rem in range(1, FAR_PER_ITER):
        @pl.when(n_far - FAR_PER_ITER * n_iter == rem)
        def _(rem=rem):
            run([sel_chunk(FAR_PER_ITER * n_iter + j, None) for j in range(rem)])

    gone1 = jnp.where(i >= 1, 0.0, NEG)
    gone2 = jnp.where(i >= 2, 0.0, NEG)
    wmask2 = wmask + gone2
    run([sel_chunk(jnp.maximum(i - 1, 0), lambda h: t1_ref[h] + gone1),
         sel_chunk(i, lambda h: t0_ref[h]),
         win_chunk(jnp.maximum(i - 2, 0), lambda h: wmask2),
         win_chunk(jnp.maximum(i - 1, 0), lambda h: t1_ref[h] + gone1),
         win_chunk(i, lambda h: t0_ref[h])])

    gates_t = gate_ref[...].T
    for r in range(GROUP):
        per_g = []
        for g in range(H_KV):
            h = g * GROUP + r
            a_s, a_w = as_ref[h], aw_ref[h]
            o_sel = a_s[0:HEAD_DIM] * (1.0 / jnp.maximum(a_s[HEAD_DIM:HEAD_DIM + 1], 1e-30))
            o_win = a_w[0:HEAD_DIM] * (1.0 / jnp.maximum(a_w[HEAD_DIM:HEAD_DIM + 1], 1e-30))
            per_g.append(gates_t[h:h + 1, :] * oc_ref[h, g * HEAD_DIM:(g + 1) * HEAD_DIM, :]
                         + gates_t[N_HEADS + h:N_HEADS + h + 1, :] * o_sel
                         + gates_t[2 * N_HEADS + h:2 * N_HEADS + h + 1, :] * o_win)
        o_ref[:, r * LANES:(r + 1) * LANES] = jnp.concatenate(per_g, axis=0).T.astype(BF16)


def _attn_prompt(q, gates, ks_g, vs_g, kw_k, vw_g, kc2, bias_tiles, nb, t):
    tq = Q_TILE
    assert t % tq == 0 and WINDOW == 2 * tq
    nt = t // tq
    ncp = t // CMP_STRIDE
    nsel = t // SEL_BLOCK
    assert nsel <= LANES and ncp % LANES == 0
    bc, t0, t1 = bias_tiles
    tok = lambda w: pl.BlockSpec((tq, w), lambda b, i: (b * nt + i, 0))
    k_rows = pl.BlockSpec((None, t, KV_HALF), lambda b, i: (b, 0, 0))
    v_cols = pl.BlockSpec((None, nt, V_ROWS, tq), lambda b, i: (b, 0, 0, 0))
    state = [pltpu.VMEM((N_HEADS, 1, tq), F32), pltpu.VMEM((N_HEADS, V_ROWS, tq), F32)]
    kr = lambda a: a.reshape(nb, t, KV_HALF)
    vc = lambda a: a.reshape(nb, nt, V_ROWS, tq)
    return pl.pallas_call(
        functools.partial(_attn_prompt_kernel, ncp=ncp, nsel=nsel),
        grid=(nb, nt),
        in_specs=[tok(q.shape[1]), tok(LANES), k_rows, k_rows, v_cols, v_cols, k_rows, v_cols, v_cols,
                  pl.BlockSpec((None, 2 * ncp, KV_W), lambda b, i: (b, 0, 0)),
                  _resident(bc.shape), _resident(t0.shape), _resident(t1.shape)],
        out_specs=tok(q.shape[1]),
        out_shape=jax.ShapeDtypeStruct(q.shape, BF16),
        scratch_shapes=[pltpu.VMEM((N_HEADS, tq, LANES), BF16), pltpu.VMEM((H_KV, tq, LANES), BF16),
                        pltpu.VMEM((N_HEADS, LANES, tq), F32)] + state + state,
        compiler_params=_cparams(("arbitrary", "arbitrary")),
        name="attn_prompt",
    )(q, gates, kr(ks_g[0]), kr(ks_g[1]), vc(vs_g[0]), vc(vs_g[1]), kr(kw_k), vc(vw_g[0]), vc(vw_g[1]),
      kc2, bc, t0, t1)


SAMPLE_CK = 1024


def _softmax_part(s, v, transposed):
    m = jnp.max(s, axis=-1, keepdims=True)
    p = jnp.exp2(s - m)
    pv = _dot_nt(p.astype(BF16), v) if transposed else _dot(p.astype(BF16), v)
    return m, jnp.sum(p, axis=-1, keepdims=True), pv


def _merge_parts(parts):
    m = parts[0][0]
    for mt, _, _ in parts[1:]:
        m = jnp.maximum(m, mt)
    l, acc = None, None
    for mt, lt, at in parts:
        w = jnp.exp2(mt - m)
        l = w * lt if l is None else l + w * lt
        acc = w * at if acc is None else acc + w * at
    return acc / jnp.maximum(l, 1e-30)


def _attn_sample_kernel(pt_ref, q_ref, g_ref, knew_ref, wnew_ref, wnewt_ref, win_ref, kc_ref,
                        bcs_ref, bsl_ref, bnew_ref, bw_ref, msel_ref, cache_ref, o_ref, wout_ref, buf, sem,
                        *, n_pages, page, past, s_len, nsel):
    slot = _gather_pages(cache_ref, pt_ref, buf, sem, n_pages, page)

    wb = win_ref.shape[1]
    shifted = pltpu.roll(win_ref[...], wb - s_len, 1)
    fresh = pltpu.roll(wnewt_ref[...], LANES - s_len, 1)
    tail_lane = lax.broadcasted_iota(jnp.int32, fresh.shape, 1) >= LANES - s_len
    wout_ref[:, 0:wb - LANES] = shifted[:, 0:wb - LANES]
    wout_ref[:, wb - LANES:wb] = jnp.where(tail_lane, fresh, shifted[:, wb - LANES:wb])
    q = q_ref[...]
    rows = q.shape[0]
    gs = rows // GROUP
    nselp = msel_ref.shape[1]

    s = _dot_nt(q, kc_ref[:, 0:KV_HALF]) + bcs_ref[...]
    m = jnp.maximum(jnp.max(s, axis=-1, keepdims=True), MASK_FLOOR)
    e = jnp.exp2(s - m)
    p = e / jnp.maximum(jnp.sum(e, axis=-1, keepdims=True), 1e-30)
    o_cmp = _dot(p.astype(BF16), kc_ref[:, KV_HALF:KV_W])
    psum = p[0:gs]
    for r in range(1, GROUP):
        psum = psum + p[r * gs:(r + 1) * gs]

    h1, h2, h3 = _split3(psum)
    msel = msel_ref[...]
    pslc = (_dot(h1, msel) + _dot(h2, msel)) + _dot(h3, msel)
    j = lax.broadcasted_iota(jnp.int32, (gs, nselp), 1)
    qpos = past + lax.rem(lax.broadcasted_iota(jnp.int32, (gs, nselp), 0), s_len)
    cur = lax.shift_right_logical(qpos, int(math.log2(SEL_BLOCK)))
    valid = j * SEL_BLOCK <= qpos
    forced = (j == 0) | (j == cur) | (j == cur - 1)
    score = jnp.where(valid, pslc + jnp.where(forced, FORCE_BONUS, 0.0), NEG)
    score = jnp.where(j < nsel, score, -3e38)
    score_t = jnp.concatenate([score, jnp.zeros((LANES - gs, nselp), F32)], axis=0).T
    jp = lax.broadcasted_iota(jnp.int32, (nselp, nselp), 0)
    jj = lax.broadcasted_iota(jnp.int32, (nselp, nselp), 1)
    sel_rows = []
    for r in range(gs):
        col = score_t[:, r:r + 1]
        row = score[r:r + 1, :]
        beats = jnp.where(jj > jp, jnp.where(col >= row, 1.0, 0.0), jnp.where(col > row, 1.0, 0.0))
        rank = jnp.sum(beats, axis=0, keepdims=True)
        sel_rows.append(jnp.where(rank < min(TOP_N, nsel), 1.0, 0.0))
    sel8 = jnp.concatenate(sel_rows, axis=0)
    sel = jnp.concatenate([sel8] * GROUP, axis=0).astype(BF16)

    ck = bsl_ref.shape[1]
    n_chunks = past // ck
    erow = lax.broadcasted_iota(jnp.int32, (nselp, ck), 0)
    eblk = lax.shift_right_logical(lax.broadcasted_iota(jnp.int32, (nselp, ck), 1), int(math.log2(SEL_BLOCK)))
    scs = []
    for t in range(n_chunks):
        kt = buf[slot, 0:KV_HALF, t * ck:(t + 1) * ck].astype(BF16)
        e_t = jnp.where(erow == t * (ck // SEL_BLOCK) + eblk, 1.0, 0.0).astype(BF16)
        sc = _dot(q, kt) + (_dot(sel, e_t) - 1.0) * (-NEG)
        scs.append(sc + bsl_ref[...] if t == n_chunks - 1 else sc)
    knew = knew_ref[...]
    wnew = wnew_ref[...]
    s_new = _dot_nt(q, knew[:, 0:KV_HALF]) + bnew_ref[...]
    s_win = _dot(q, win_ref[0:KV_HALF, :].astype(BF16)) + bw_ref[...]
    s_wnew = _dot_nt(q, wnew[:, 0:KV_HALF]) + bnew_ref[...]
    parts = [_softmax_part(scs[t], buf[slot, KV_HALF:KV_W, t * ck:(t + 1) * ck].astype(BF16), True)
             for t in range(n_chunks)]
    parts.append(_softmax_part(s_new, knew[:, KV_HALF:KV_W], False))
    o_sel = _merge_parts(parts)

    o_win = _merge_parts([_softmax_part(s_win, win_ref[KV_HALF:KV_W, :].astype(BF16), True),
                          _softmax_part(s_wnew, wnew[:, KV_HALF:KV_W], False)])

    g = g_ref[...]
    o_ref[...] = g[:, 0:1] * o_cmp + g[:, 1:2] * o_sel + g[:, 2:3] * o_win


def _attn_sample(page_table, q32, g32, knew, wnew, wnew_t, win_t, kvc, tiles, msel, cache_t, past, s_len):
    nb, n_pages = page_table.shape
    page = cache_t.shape[2]
    rows = q32.shape[1]
    wb = win_t.shape[2]
    nsel = -(-(past + s_len) // SEL_BLOCK)
    bcs, bsl, bnew, bw = tiles
    per_b = lambda r, w: pl.BlockSpec((None, r, w), lambda b, pt: (b, 0, 0))
    return pl.pallas_call(
        functools.partial(_attn_sample_kernel, n_pages=n_pages, page=page, past=past, s_len=s_len, nsel=nsel),
        grid_spec=pltpu.PrefetchScalarGridSpec(
            num_scalar_prefetch=1,
            grid=(nb,),
            in_specs=[per_b(rows, LANES), per_b(rows, LANES), per_b(LANES, KV_W), per_b(LANES, KV_W),
                      per_b(KV_W, LANES), per_b(KV_W, wb), per_b(kvc.shape[1], KV_W),
                      _resident(bcs.shape), _resident(bsl.shape), _resident(bnew.shape),
                      _resident(bw.shape), _resident(msel.shape),
                      pl.BlockSpec(memory_space=pl.ANY)],
            out_specs=[per_b(rows, LANES), per_b(KV_W, wb)],
            scratch_shapes=[pltpu.VMEM((2, KV_W, past), F32), pltpu.SemaphoreType.DMA((2,))]),
        out_shape=[jax.ShapeDtypeStruct((nb, rows, LANES), F32), jax.ShapeDtypeStruct((nb, KV_W, wb), F32)],
        compiler_params=_cparams(("arbitrary",)),
        name="attn_sample",
    )(page_table, q32, g32, knew, wnew, wnew_t, win_t, kvc, bcs, bsl, bnew, bw, msel, cache_t)


def _prep_weights(w_in, w_cmp1, w_cmp2, pe_cmp, w_conv, w_nsa_out):
    d = w_in.shape[0]
    sizes = (N_HEADS * HEAD_DIM, KV_W, KV_W, KV_W, 3 * N_HEADS, _C_W, _C_W, _C_W, 2 * d)
    offs = [0]
    for sz in sizes:
        offs.append(offs[-1] + sz)
    part = lambda k: w_in[:, offs[k]:offs[k + 1]]
    qp = part(0).reshape(d, H_KV, GROUP, HEAD_DIM).transpose(0, 2, 1, 3).reshape(d, N_HEADS * HEAD_DIM)
    gn = jnp.pad(part(4), ((0, 0), (0, _G_W - 3 * N_HEADS)))
    w_in_p = jnp.concatenate([qp, part(1), part(2), part(3), gn, part(5), part(6), part(7), part(8)],
                             axis=1).astype(BF16)
    w_nsa_p = w_nsa_out.reshape(H_KV, GROUP, HEAD_DIM, -1).transpose(1, 0, 2, 3).reshape(
        N_HEADS * HEAD_DIM, -1).astype(BF16)
    r = CMP_BLOCK // CMP_STRIDE
    w1r = w_cmp1.reshape(2, r, CMP_STRIDE // 2, 2, HEAD_DIM, CMP_HIDDEN)
    w1t = jnp.transpose(w1r, (0, 2, 3, 4, 1, 5))
    w1e = w1t[:, :, :, :, :, None, :].astype(BF16)
    z1 = jnp.zeros_like(w1e)
    wb = jnp.stack([jnp.concatenate([w1e if gp == g else z1 for gp in range(H_KV)], axis=5)
                    for g in range(H_KV)], axis=3)
    wbig = wb.reshape(2, CMP_STRIDE // 2, 2 * KV_HALF, CMP_OUT_W)
    z2 = jnp.zeros((CMP_HIDDEN, HEAD_DIM), F32)
    w2bd = jnp.concatenate(
        [jnp.concatenate([w_cmp2[c] if (cp, gp) == (c, g) else z2 for cp in range(2) for gp in range(H_KV)], axis=1)
         for c in range(2) for g in range(H_KV)], axis=0).astype(BF16)
    pe_r = pe_cmp.reshape(2, r, CMP_STRIDE, HEAD_DIM).transpose(1, 2, 0, 3)
    pe_rows = jnp.broadcast_to(pe_r[:, :, :, None, :], (r, CMP_STRIDE, 2, H_KV, HEAD_DIM)).reshape(r, CHUNK_W)
    pe8 = jnp.pad(pe_rows, ((0, 8 - r), (0, 0)))
    w_conv8 = jnp.pad(w_conv, ((0, 8 - CONV_WIDTH), (0, 0)))
    return w_in_p, w_nsa_p, wbig, w2bd, pe8, w_conv8


def _sel_matrix(n_c, nselp):
    n = jnp.arange(n_c + 1)[:, None]
    j = jnp.arange(nselp)[None, :]
    return _sel_weights(n - (SEL_BLOCK // CMP_STRIDE) * j).astype(BF16)


def kernel(x_prompt, x_sample, cache_kv_cmp, cache_kv_sel, state_kv_win, state_conv, page_table,
           c_prompt, c_sample, w_ada, b_ada, g_norm, w_ffn1_gu, w_ffn1_down, w_ffn2_gu, w_ffn2_down,
           w_in, w_cmp1, w_cmp2, pe_cmp, w_conv, w_nsa_out, w_conv_out, w_out, rel_bias, g_final):
    assert w_ada.shape[0] == 1, "single-layer trunk"
    nbp, t, d = x_prompt.shape
    nbs, s_len, _ = x_sample.shape
    n_pages = page_table.shape[1]
    page = cache_kv_cmp.shape[2]
    past = n_pages * page
    n_phys = cache_kv_cmp.shape[1]
    wb = state_kv_win.shape[2]
    assert wb == WINDOW and past % SAMPLE_CK == 0

    w_in_p, w_nsa_p, wbig, w2bd, pe8, w_conv8 = _prep_weights(
        w_in[0], w_cmp1[0], w_cmp2[0], pe_cmp[0], w_conv[0], w_nsa_out[0])
    wgu1, wd1 = w_ffn1_gu[0].astype(BF16), w_ffn1_down[0].astype(BF16)
    wgu2, wd2 = w_ffn2_gu[0].astype(BF16), w_ffn2_down[0].astype(BF16)
    w_cv, w_o = w_conv_out[0].astype(BF16), w_out[0].astype(BF16)
    gn = [g_norm[0][k:k + 1] for k in range(N_SUB)]

    n_c_rows = nbp + nbs
    c_all = jnp.pad(jnp.concatenate([c_prompt, c_sample], axis=0), ((0, (-n_c_rows) % 8), (0, 0)))
    mod_all = _ada(c_all, w_ada[0], b_ada[0])
    mod_p = mod_all[:nbp].reshape(nbp * 3 * N_SUB, 1, d)
    mod_s = jnp.transpose(jnp.repeat(mod_all[nbp:n_c_rows].reshape(nbs, 3 * N_SUB, d), s_len, axis=0), (1, 0, 2))

    tm = TOKEN_TILE if t % TOKEN_TILE == 0 else t
    tpb = t // tm
    xp = x_prompt.reshape(nbp * t, d)
    h1 = _ffn(xp, mod_p, 0, gn[0], wgu1, wd1, tm, tpb)
    (q, kc_k, kc_v, kvc_t, kvs_t, kvw_t, ks0, ks1, vs0, vs1, kw_k, vw0, vw1, gates, z, cb, gm) = _inproj(
        h1, mod_p, gn[1], w_in_p, tm, tpb, True)
    kc2 = _cmp_prompt(kc_k, kc_v, t, wbig, pe8, w2bd)
    tiles_p = _bias_prompt(rel_bias, t // CMP_STRIDE)
    o_nsa = _attn_prompt(q, gates, (ks0, ks1), (vs0, vs1), kw_k, (vw0, vw1), kc2, tiles_p, nbp, t)
    tm_mix = 2 * tm if t % (2 * tm) == 0 else tm
    h2 = _mixout(h1, o_nsa, z, None, cb, gm, mod_p, w_conv8, w_nsa_p, w_cv, w_o, tm_mix, t // tm_mix)
    y_prompt = _ffn(h2, mod_p, 2, gn[2], wgu2, wd2, tm, tpb, g_final).reshape(nbp, t, d)

    kv_shape = (2, H_KV, HEAD_DIM)
    kv_out = lambda a: jnp.transpose(a.reshape((1, nbp) + kv_shape + (a.shape[-1],)), (0, 1, 5, 2, 3, 4))
    kv_cmp_p = kv_out(kvc_t)
    kv_sel_p = kv_out(kvs_t)
    keep = min(WINDOW, t)
    kv_win_p = kv_out(kvw_t[:, :, t - keep:])
    conv_p = z.reshape(1, nbp, t, -1)[:, :, t - (CONV_WIDTH - 1):]

    ns = nbs * s_len
    xs = x_sample.reshape(ns, d)
    h1s = _ffn(xs, mod_s, 0, gn[0], wgu1, wd1, ns, 1)
    qs, kvc_s, kvs_s, kvw_s, kvs_sb, kvw_sb, gates_s, z_s, cb_s, gm_s = _inproj(
        h1s, mod_s, gn[1], w_in_p, ns, 1, False)

    pos_minor = lambda a: jnp.transpose(a, (0, 2, 3, 4, 1)).reshape(a.shape[0], KV_W, a.shape[1])
    kvc_past = _cmp_sample(page_table, pos_minor(cache_kv_cmp[0]), wbig, pe8, w2bd)
    n_c = (past + s_len - CMP_BLOCK) // CMP_STRIDE + 1
    nsel = -(-(past + s_len) // SEL_BLOCK)
    nselp = -(-nsel // LANES) * LANES
    rows = N_HEADS * s_len
    head_of_row = [g * GROUP + r for r in range(GROUP) for g in range(H_KV) for _ in range(s_len)]
    tbl_rows = jnp.pad(rel_bias.T[jnp.array(head_of_row)], ((0, 0), (0, LANES - N_BUCKETS)))
    tiles_s = _bias_sample(tbl_rows, past, s_len, past // CMP_STRIDE, SAMPLE_CK, wb)
    msel = _sel_matrix(past // CMP_STRIDE - 1, nselp)

    q5 = qs.reshape(nbs, s_len, GROUP, H_KV, HEAD_DIM).transpose(0, 2, 3, 1, 4)
    q32 = jnp.zeros((nbs, GROUP, H_KV, s_len, H_KV, HEAD_DIM), BF16)
    for g in range(H_KV):
        q32 = q32.at[:, :, g, :, g, :].set(q5[:, :, g])
    q32 = q32.reshape(nbs, rows, LANES)
    g5 = gates_s[:, :3 * N_HEADS].reshape(nbs, s_len, 3, H_KV, GROUP).transpose(0, 4, 3, 1, 2)
    g32 = jnp.pad(g5.reshape(nbs, rows, 3), ((0, 0), (0, 0), (0, LANES - 3)))
    pad_new = lambda a: jnp.pad(a.reshape(nbs, s_len, KV_W), ((0, 0), (0, LANES - s_len), (0, 0)))
    wnew_t = jnp.pad(jnp.transpose(kvw_s.reshape(nbs, s_len, KV_W), (0, 2, 1)), ((0, 0), (0, 0), (0, LANES - s_len)))
    o32, win_next = _attn_sample(page_table, q32, g32, pad_new(kvs_sb), pad_new(kvw_sb), wnew_t,
                                 pos_minor(state_kv_win[0]), kvc_past, tiles_s, msel,
                                 pos_minor(cache_kv_sel[0]), past, s_len)
    o6 = o32.reshape(nbs, GROUP, H_KV, s_len, H_KV, HEAD_DIM)
    o_s = jnp.stack([o6[:, :, g, :, g, :] for g in range(H_KV)], axis=3)
    o_nsa_s = o_s.transpose(0, 2, 1, 3, 4).reshape(ns, N_HEADS * HEAD_DIM).astype(BF16)

    full = jnp.concatenate([state_conv[0], z_s.reshape(nbs, s_len, -1)], axis=1)
    z_shift = (full[:, 1:1 + s_len].reshape(ns, -1), full[:, 0:s_len].reshape(ns, -1))
    h2s = _mixout(h1s, o_nsa_s, z_s, z_shift, cb_s, gm_s, mod_s, w_conv8, w_nsa_p, w_cv, w_o, ns, 1)
    y_sample = _ffn(h2s, mod_s, 2, gn[2], wgu2, wd2, ns, 1, g_final).reshape(nbs, s_len, d)

    kv_cmp_s = kvc_s.reshape((1, nbs, s_len) + kv_shape)
    kv_sel_s = kvs_s.reshape((1, nbs, s_len) + kv_shape)
    kv_win_s = jnp.transpose(win_next.reshape((1, nbs) + kv_shape + (wb,)), (0, 1, 5, 2, 3, 4))
    conv_s = full[None, :, s_len:]
    return (y_prompt, y_sample, kv_cmp_p, kv_sel_p, kv_win_p, conv_p, kv_cmp_s, kv_sel_s, kv_win_s, conv_s)
```

```python
import functools
import math

import jax
import jax.numpy as jnp
from jax import lax
from jax.experimental import pallas as pl
from jax.experimental.pallas import tpu as pltpu

F32 = jnp.float32
BF16 = jnp.bfloat16

HEAD_DIM = 64
N_HEADS = 8
H_KV = 2
GROUP = N_HEADS // H_KV
CMP_BLOCK = 32
CMP_STRIDE = 16
CMP_HIDDEN = 2 * HEAD_DIM
SEL_BLOCK = 64
TOP_N = 16
WINDOW = 512
CONV_WIDTH = 3
N_BUCKETS = 32
MAX_DISTANCE = 128
N_SUB = 3
EPS = 1e-6
NEG = -1e9
LOG2E = math.log2(math.e)
FORCE_BONUS = 1e3
KV_W = 2 * H_KV * HEAD_DIM
KV_HALF = H_KV * HEAD_DIM
LANES = 128
Q_TILE = 256
CMP_NEAR = 32
VMEM_LIMIT = 56 * 1024 * 1024
TOKEN_TILE = 512
ADA_COL_TILES = 8


def _bucket_thresholds():
    max_exact = N_BUCKETS // 2

    def bucket(d):
        if d < max_exact:
            return d
        large = max_exact + int(math.log(d / max_exact) / math.log(MAX_DISTANCE / max_exact)
                                * (N_BUCKETS - max_exact))
        return min(large, N_BUCKETS - 1)

    thr, d = [], 0
    for b in range(N_BUCKETS):
        while bucket(d) < b:
            d += 1
        thr.append(d)
    return tuple(thr)


BUCKET_THR = _bucket_thresholds()


def _cparams(sem):
    return pltpu.CompilerParams(dimension_semantics=sem, vmem_limit_bytes=VMEM_LIMIT)


def _resident(shape):
    nd = len(shape)
    return pl.BlockSpec(shape, lambda *_: (0,) * nd, pipeline_mode=pl.Buffered(1))


def _dot(a, b):
    return jnp.dot(a, b, preferred_element_type=F32)


def _dot_nt(a, b):
    return lax.dot_general(a, b, (((1,), (1,)), ((), ())), preferred_element_type=F32)


def _split3(x):
    h1 = x.astype(BF16)
    r1 = x - h1.astype(F32)
    h2 = r1.astype(BF16)
    h3 = (r1 - h2.astype(F32)).astype(BF16)
    return h1, h2, h3


def _modulated_norm(x, g, shift, scale):
    y = x * lax.rsqrt(jnp.mean(x * x, axis=-1, keepdims=True) + EPS)
    return (y * g) * (1.0 + scale) + shift


def _ada_kernel(c_ref, w_ref, b_ref, o_ref):
    c = c_ref[...]
    a = (c * jax.nn.sigmoid(c)).astype(BF16)
    o_ref[...] = _dot(a, w_ref[...].astype(BF16)) + b_ref[...]


def _ada(c_all, w_ada, b_ada):
    rows, d = c_all.shape
    n = w_ada.shape[1]
    tn = n // ADA_COL_TILES
    return pl.pallas_call(
        _ada_kernel,
        grid=(n // tn,),
        in_specs=[pl.BlockSpec((rows, d), lambda j: (0, 0)),
                  pl.BlockSpec((d, tn), lambda j: (0, j)),
                  pl.BlockSpec((1, tn), lambda j: (0, j))],
        out_specs=pl.BlockSpec((rows, tn), lambda j: (0, j)),
        out_shape=jax.ShapeDtypeStruct((rows, n), F32),
        compiler_params=_cparams(("arbitrary",)),
        name="ada",
    )(c_all, w_ada, b_ada.reshape(1, n))


def _ffn_kernel(*refs, d_ff, fc, final_norm):
    if final_norm:
        x_ref, sh_ref, sc_ref, gt_ref, gn_ref, wgu_ref, wd_ref, gf_ref, o_ref = refs
    else:
        x_ref, sh_ref, sc_ref, gt_ref, gn_ref, wgu_ref, wd_ref, o_ref = refs
    x = x_ref[...]
    ub = _modulated_norm(x, gn_ref[...], sh_ref[...], sc_ref[...]).astype(BF16)
    acc = jnp.zeros(x.shape, F32)
    for c in range(d_ff // fc):
        g = _dot(ub, wgu_ref[:, c * fc:(c + 1) * fc])
        v = _dot(ub, wgu_ref[:, d_ff + c * fc:d_ff + (c + 1) * fc])
        a = ((g * jax.nn.sigmoid(g)) * v).astype(BF16)
        acc = acc + _dot(a, wd_ref[c * fc:(c + 1) * fc, :])
    h = x + (0.5 * gt_ref[...]) * acc
    if final_norm:
        h = (h * lax.rsqrt(jnp.mean(h * h, axis=-1, keepdims=True) + EPS)) * gf_ref[...]
    o_ref[...] = h


def _mod_specs(mod, ks, tm, tiles_per_batch):
    if mod.ndim == 3 and mod.shape[1] == 1:
        d = mod.shape[-1]
        specs = [pl.BlockSpec((None, 1, d), lambda i, k=k: ((i // tiles_per_batch) * (3 * N_SUB) + k, 0, 0))
                 for k in ks]
        return specs, [mod] * len(ks)
    d = mod.shape[-1]
    specs = [pl.BlockSpec((None, tm, d), lambda i, k=k: (k, i, 0)) for k in ks]
    return specs, [mod] * len(ks)


def _ffn(x, mod, sub, gn_row, w_gu, w_down, tm, tiles_per_batch, g_final=None):
    n, d = x.shape
    d_ff = w_down.shape[0]
    fc = d_ff // 2 if (d_ff // 2) % LANES == 0 else d_ff
    final_norm = g_final is not None
    mspecs, mops = _mod_specs(mod, (3 * sub, 3 * sub + 1, 3 * sub + 2), tm, tiles_per_batch)
    in_specs = [pl.BlockSpec((tm, d), lambda i: (i, 0))] + mspecs + [
        pl.BlockSpec((1, d), lambda i: (0, 0)), _resident(w_gu.shape), _resident(w_down.shape)]
    ops = [x] + mops + [gn_row, w_gu, w_down]
    if final_norm:
        in_specs.append(pl.BlockSpec((1, d), lambda i: (0, 0)))
        ops.append(g_final.reshape(1, d))
    return pl.pallas_call(
        functools.partial(_ffn_kernel, d_ff=d_ff, fc=fc, final_norm=final_norm),
        grid=(n // tm,),
        in_specs=in_specs,
        out_specs=pl.BlockSpec((tm, d), lambda i: (i, 0)),
        out_shape=jax.ShapeDtypeStruct((n, d), F32),
        compiler_params=_cparams(("arbitrary",)),
        name="ffn_final" if final_norm else "ffn",
    )(*ops)


_Q_W = N_HEADS * HEAD_DIM
_G_W = LANES
_C_W = 512
_SEG = {}
_off = 0
for _name, _w in (("q", _Q_W), ("kc", KV_W), ("ks", KV_W), ("kw", KV_W), ("gn", _G_W),
                  ("ch", _C_W), ("cb", _C_W), ("cc", _C_W)):
    _SEG[_name] = (_off, _off + _w)
    _off += _w
_MG_OFF = _off


V_ROWS = HEAD_DIM + 16


def _aug_lane(g):
    return HEAD_DIM if g == 0 else 0


def _inproj_kernel(*refs, prompt):
    x_ref, sh_ref, sc_ref, gn_ref, w_ref = refs[:5]
    ub = _modulated_norm(x_ref[...], gn_ref[...], sh_ref[...], sc_ref[...]).astype(BF16)

    def seg(name):
        lo, hi = _SEG[name]
        return _dot(ub, w_ref[:, lo:hi])

    if prompt:
        (q_ref, kck_ref, kcv_ref, kct_ref, kst_ref, kwt_ref, ks0_ref, ks1_ref, vs0_ref, vs1_ref, kwk_ref,
         vw0_ref, vw1_ref, g_ref, z_ref, cb_ref, gm_ref) = refs[5:]
        kc = seg("kc")
        kck_ref[...] = kc[:, 0:KV_HALF]
        kcv_ref[...] = kc[:, KV_HALF:KV_W]
        kct_ref[...] = kc.T
        tm = kc.shape[0]
        lane = lax.broadcasted_iota(jnp.int32, (tm, KV_HALF), 1)
        blk = lax.shift_right_logical(lax.broadcasted_iota(jnp.int32, (tm, KV_HALF), 0),
                                      int(math.log2(SEL_BLOCK))) & (Q_TILE // SEL_BLOCK - 1)
        ones_rows = jnp.where(lax.broadcasted_iota(jnp.int32, (V_ROWS - HEAD_DIM, tm), 0) == 0, 1.0, 0.0)
        for name, t_ref, k_refs, vt_refs in (("ks", kst_ref, (ks0_ref, ks1_ref), (vs0_ref, vs1_ref)),
                                             ("kw", kwt_ref, (kwk_ref,), (vw0_ref, vw1_ref))):
            kv = seg(name)
            kv_t = kv.T
            t_ref[...] = kv_t
            kk = kv[:, 0:KV_HALF]
            if len(k_refs) == 1:
                k_refs[0][...] = kk.astype(BF16)
            else:
                for g, k_ref in enumerate(k_refs):
                    own = (lane < HEAD_DIM) if g == 0 else (lane >= HEAD_DIM)
                    onehot = jnp.where(lane - _aug_lane(g) == blk, 1.0, 0.0)
                    k_ref[...] = jnp.where(own, kk, onehot).astype(BF16)
            for g, vt_ref in enumerate(vt_refs):
                vg = jnp.concatenate(
                    [kv_t[KV_HALF + g * HEAD_DIM:KV_HALF + (g + 1) * HEAD_DIM, :], ones_rows], axis=0)
                for c in range(vt_ref.shape[0]):
                    vt_ref[c] = vg[:, c * Q_TILE:(c + 1) * Q_TILE].astype(BF16)
    else:
        q_ref, kc_ref, ks_ref, kw_ref, ksb_ref, kwb_ref, g_ref, z_ref, cb_ref, gm_ref = refs[5:]
        kc_ref[...] = seg("kc")
        ks = seg("ks")
        ks_ref[...] = ks
        ksb_ref[...] = ks.astype(BF16)
        kw = seg("kw")
        kw_ref[...] = kw
        kwb_ref[...] = kw.astype(BF16)
    q_ref[...] = (seg("q") * (HEAD_DIM ** -0.5 * LOG2E)).astype(BF16)
    g_ref[...] = jax.nn.sigmoid(seg("gn"))
    z_ref[...] = seg("cc") * seg("ch")
    cb_ref[...] = seg("cb").astype(cb_ref.dtype)
    d2 = gm_ref.shape[1]
    half = d2 // 2
    for c in range(2):
        gm_ref[:, c * half:(c + 1) * half] = jax.nn.sigmoid(
            _dot(ub, w_ref[:, _MG_OFF + c * half:_MG_OFF + (c + 1) * half])).astype(gm_ref.dtype)


def _inproj(h, mod, gn_row, w_in_p, tm, tiles_per_batch, prompt):
    n, d = h.shape
    d_conv = _C_W
    mspecs, mops = _mod_specs(mod, (3, 4), tm, tiles_per_batch)
    rows = lambda w, dt: (pl.BlockSpec((tm, w), lambda i: (i, 0)), jax.ShapeDtypeStruct((n, w), dt))
    tail = [rows(_G_W, F32), rows(d_conv, F32), rows(d_conv, BF16), rows(2 * d, BF16)]
    if prompt:
        nb = n // (tm * tiles_per_batch)
        t = tm * tiles_per_batch
        cpt = tm // Q_TILE
        tr = (pl.BlockSpec((None, KV_W, tm), lambda i: (i // tiles_per_batch, 0, i % tiles_per_batch)),
              jax.ShapeDtypeStruct((nb, KV_W, t), F32))
        vt = (pl.BlockSpec((cpt, V_ROWS, Q_TILE), lambda i: (i, 0, 0)),
              jax.ShapeDtypeStruct((n // Q_TILE, V_ROWS, Q_TILE), BF16))
        kr = rows(KV_HALF, BF16)
        outs = [rows(_Q_W, BF16), rows(KV_HALF, F32), rows(KV_HALF, F32), tr, tr, tr,
                kr, kr, vt, vt, kr, vt, vt] + tail
    else:
        outs = [rows(_Q_W, BF16), rows(KV_W, F32), rows(KV_W, F32), rows(KV_W, F32),
                rows(KV_W, BF16), rows(KV_W, BF16)] + tail
    return pl.pallas_call(
        functools.partial(_inproj_kernel, prompt=prompt),
        grid=(n // tm,),
        in_specs=[pl.BlockSpec((tm, d), lambda i: (i, 0))] + mspecs + [
            pl.BlockSpec((1, d), lambda i: (0, 0)), _resident(w_in_p.shape)],
        out_specs=[o[0] for o in outs],
        out_shape=[o[1] for o in outs],
        compiler_params=_cparams(("arbitrary",)),
        name="inproj_prompt" if prompt else "inproj",
    )(h, *mops, gn_row, w_in_p)


def _mixout_kernel(*refs, halo, tiles_per_batch):
    if halo:
        (h_ref, o_ref, z_ref, zp_ref, cb_ref, gm_ref, g2_ref, wc_ref,
         wn_ref, wcv_ref, wo_ref, out_ref) = refs
        z = z_ref[...]
        tm = z.shape[0]
        first = (pl.program_id(0) % tiles_per_batch) == 0
        prev = jnp.where(first, 0.0, zp_ref[...])
        row = lax.broadcasted_iota(jnp.int32, z.shape, 0)
        zm1 = jnp.where(row == 0, prev[7:8, :], pltpu.roll(z, 1, 0))
        zm2 = jnp.where(row == 0, prev[6:7, :], jnp.where(row == 1, prev[7:8, :], pltpu.roll(z, 2, 0)))
    else:
        (h_ref, o_ref, z_ref, zm1_ref, zm2_ref, cb_ref, gm_ref, g2_ref, wc_ref,
         wn_ref, wcv_ref, wo_ref, out_ref) = refs
        z, zm1, zm2 = z_ref[...], zm1_ref[...], zm2_ref[...]
    conv = wc_ref[0:1, :] * zm2 + wc_ref[1:2, :] * zm1 + wc_ref[2:3, :] * z
    y = (cb_ref[...].astype(F32) * conv).astype(BF16)
    d = h_ref.shape[1]
    merged = gm_ref[:, 0:d].astype(F32) * _dot(o_ref[...], wn_ref[...]) \
        + gm_ref[:, d:2 * d].astype(F32) * _dot(y, wcv_ref[...])
    out_ref[...] = h_ref[...] + g2_ref[...] * _dot(merged.astype(BF16), wo_ref[...])


def _mixout(h, o_nsa, z, z_shift, cb, gm, mod, w_conv8, w_nsa_p, w_cv, w_o, tm, tiles_per_batch):
    n, d = h.shape
    dc = z.shape[1]
    halo = z_shift is None
    mspecs, mops = _mod_specs(mod, (5,), tm, tiles_per_batch)
    tok = lambda w: pl.BlockSpec((tm, w), lambda i: (i, 0))
    if halo:
        zspecs = [tok(dc), pl.BlockSpec((8, dc), lambda i: (jnp.maximum(i * (tm // 8) - 1, 0), 0))]
        zops = [z, z]
    else:
        zspecs = [tok(dc), tok(dc), tok(dc)]
        zops = [z, z_shift[0], z_shift[1]]
    return pl.pallas_call(
        functools.partial(_mixout_kernel, halo=halo, tiles_per_batch=tiles_per_batch),
        grid=(n // tm,),
        in_specs=[tok(d), tok(o_nsa.shape[1])] + zspecs + [tok(dc), tok(2 * d)] + mspecs + [
            pl.BlockSpec((8, dc), lambda i: (0, 0)),
            _resident(w_nsa_p.shape), _resident(w_cv.shape), _resident(w_o.shape)],
        out_specs=tok(d),
        out_shape=jax.ShapeDtypeStruct((n, d), F32),
        compiler_params=_cparams(("arbitrary",)),
        name="mixout",
    )(h, o_nsa, *zops, cb, gm, *mops, w_conv8, w_nsa_p, w_cv, w_o)


CHUNK_W = CMP_STRIDE * KV_W
CMP_OUT_W = 2 * H_KV * CMP_HIDDEN


def _gelu_tanh(x):
    return x * (0.5 * (1.0 + jnp.tanh(math.sqrt(2.0 / math.pi) * (x + 0.044715 * (x * x * x)))))


def _chunk_partials(tile, wc_ref, after_step=None):
    accs = []
    for c in range(2):
        acc = None
        for lp in range(CMP_STRIDE // 2):
            x2 = jnp.concatenate([tile(2 * lp, c), tile(2 * lp + 1, c)], axis=1)
            part = _dot(x2, wc_ref[c, lp])
            acc = part if acc is None else acc + part
            if after_step is not None:
                after_step(c, lp)
        accs.append(acc)
    hw = CMP_OUT_W // 2
    return jnp.concatenate([accs[0][:, 0:hw], accs[1][:, 0:hw], accs[0][:, hw:], accs[1][:, hw:]], axis=1)


def _compress_tail(p, wc_ref, pe_ref, w2_ref):
    n_ch = p.shape[0]
    pb = _chunk_partials(
        lambda l, c: pe_ref[:, l * KV_W + c * KV_HALF:l * KV_W + (c + 1) * KV_HALF].astype(BF16), wc_ref)
    pre = (pb[0:1, 0:CMP_OUT_W] + pb[1:2, CMP_OUT_W:]) + p[:, 0:CMP_OUT_W] \
        + pltpu.roll(p[:, CMP_OUT_W:], n_ch - 1, 0)
    out = _dot(_gelu_tanh(pre).astype(BF16), w2_ref[...])
    row = lax.broadcasted_iota(jnp.int32, out.shape, 0)
    return jnp.where(row < n_ch - 1, out, 0.0)


def _cmp_prompt_kernel(xk_ref, xv_ref, wc_ref, pe_ref, w2_ref, o_ref):
    xs = (xk_ref, xv_ref)
    n_ch = xk_ref.shape[0] // CMP_STRIDE
    p = _chunk_partials(lambda l, c: xs[c][pl.ds(l, n_ch, stride=CMP_STRIDE), :].astype(BF16), wc_ref)
    out = _compress_tail(p, wc_ref, pe_ref, w2_ref).astype(BF16)
    o_ref[0:n_ch, :] = out
    o_ref[n_ch:2 * n_ch, :] = out


def _cmp_prompt(rows_k, rows_v, t, wbig, pe8, w2bd):
    nb = rows_k.shape[0] // t
    n_ch = t // CMP_STRIDE
    half = pl.BlockSpec((t, KV_HALF), lambda b: (b, 0))
    return pl.pallas_call(
        _cmp_prompt_kernel,
        grid=(nb,),
        in_specs=[half, half, _resident(wbig.shape), _resident(pe8.shape), _resident(w2bd.shape)],
        out_specs=pl.BlockSpec((None, 2 * n_ch, KV_W), lambda b: (b, 0, 0)),
        out_shape=jax.ShapeDtypeStruct((nb, 2 * n_ch, KV_W), BF16),
        compiler_params=_cparams(("arbitrary",)),
        name="cmp_prompt",
    )(rows_k, rows_v, wbig, pe8, w2bd)


def _page_copies(cache_ref, pt_ref, b, dst, sem, n_pages, page):
    return [pltpu.make_async_copy(cache_ref.at[pt_ref[b, p]], dst.at[:, pl.ds(p * page, page)], sem)
            for p in range(n_pages)]


def _gather_pages(cache_ref, pt_ref, buf, sem, n_pages, page):
    b = pl.program_id(0)
    nb = pl.num_programs(0)
    slot = b % 2
    copies = lambda bb, sl: _page_copies(cache_ref, pt_ref, bb, buf.at[sl], sem.at[sl], n_pages, page)

    @pl.when(b == 0)
    def _():
        for c in copies(0, 0):
            c.start()

    @pl.when(b + 1 < nb)
    def _():
        for c in copies(b + 1, 1 - slot):
            c.start()

    for c in copies(b, slot):
        c.wait()
    return slot


def _cmp_sample_kernel(pt_ref, cache_ref, wc_ref, pe_ref, w2_ref, o_ref,
                       buf_a, buf_b, xk_a, xv_a, xk_b, xv_b, sem, *, n_pages, page, nb):
    b = pl.program_id(0)
    bufs = (buf_a, buf_b)
    xs = ((xk_a, xv_a), (xk_b, xv_b))
    past = xk_a.shape[0]
    n_ch = past // CMP_STRIDE
    n_slabs = CMP_STRIDE // 2
    slab = past // n_slabs
    copies = lambda bb, par: _page_copies(cache_ref, pt_ref, bb, bufs[par], sem.at[par], n_pages, page)

    def transpose_slab(par, c, j):
        xs[par][c][j * slab:(j + 1) * slab, :] = bufs[par][c * KV_HALF:(c + 1) * KV_HALF, j * slab:(j + 1) * slab].T

    @pl.when(b == 0)
    def _():
        for cp in copies(0, 0):
            cp.start()
        if nb > 1:
            for cp in copies(1, 1):
                cp.start()
        for cp in copies(0, 0):
            cp.wait()
        for c in range(2):
            for j in range(n_slabs):
                transpose_slab(0, c, j)

    def step(par):
        nxt = 1 - par
        hook = None
        if nb > 1:
            @pl.when(b + 1 < nb)
            def _():
                for cp in copies(b + 1, nxt):
                    cp.wait()

            @pl.when(b + 2 < nb)
            def _():
                for cp in copies(b + 2, par):
                    cp.start()

            hook = lambda c, lp: transpose_slab(nxt, c, lp)
        p = _chunk_partials(lambda l, c: xs[par][c][pl.ds(l, n_ch, stride=CMP_STRIDE), :].astype(BF16), wc_ref, hook)
        o_ref[...] = _compress_tail(p, wc_ref, pe_ref, w2_ref).astype(BF16)

    for par in range(2):
        pl.when(b % 2 == par)(functools.partial(step, par))


def _cmp_sample(page_table, cache_t, wbig, pe8, w2bd):
    nb, n_pages = page_table.shape
    page = cache_t.shape[2]
    past = n_pages * page
    n_ch = past // CMP_STRIDE
    return pl.pallas_call(
        functools.partial(_cmp_sample_kernel, n_pages=n_pages, page=page, nb=nb),
        grid_spec=pltpu.PrefetchScalarGridSpec(
            num_scalar_prefetch=1,
            grid=(nb,),
            in_specs=[pl.BlockSpec(memory_space=pl.ANY),
                      _resident(wbig.shape), _resident(pe8.shape), _resident(w2bd.shape)],
            out_specs=pl.BlockSpec((None, n_ch, KV_W), lambda b, pt: (b, 0, 0)),
            scratch_shapes=[pltpu.VMEM((KV_W, past), F32)] * 2 + [pltpu.VMEM((past, KV_HALF), F32)] * 4
            + [pltpu.SemaphoreType.DMA((2,))]),
        out_shape=jax.ShapeDtypeStruct((nb, n_ch, KV_W), BF16),
        compiler_params=_cparams(("arbitrary",)),
        name="cmp_sample",
    )(page_table, cache_t, wbig, pe8, w2bd)


def _bias_values(d, table):
    acc = jnp.zeros(d.shape, F32) + table(0)
    for b in range(1, N_BUCKETS):
        acc = jnp.where(d >= BUCKET_THR[b], table(b), acc)
    return jnp.where(d >= 0, (acc - table(N_BUCKETS - 1)) * LOG2E, NEG)


CMP_BACK = 16


def _bias_prompt_kernel(tbl_ref, bc_ref, t0_ref, t1_ref):
    h = pl.program_id(0)
    table = lambda b: tbl_ref[b, h]
    ncp, tq = bc_ref.shape
    kj = lax.broadcasted_iota(jnp.int32, (tq, tq), 0)
    qi = lax.broadcasted_iota(jnp.int32, (tq, tq), 1)
    t0_ref[...] = _bias_values(qi - kj, table)
    t1_ref[...] = _bias_values(qi - kj + tq, table)
    c = lax.broadcasted_iota(jnp.int32, (ncp, tq), 0)
    qi = lax.broadcasted_iota(jnp.int32, (ncp, tq), 1)
    d = qi - CMP_STRIDE * c + (CMP_STRIDE * CMP_BACK - (CMP_BLOCK - 1))
    bc_ref[...] = jnp.where(c < CMP_NEAR, _bias_values(d, table), 0.0)


def _bias_prompt(rel_bias, ncp):
    tq = Q_TILE
    return pl.pallas_call(
        _bias_prompt_kernel,
        grid=(N_HEADS,),
        in_specs=[pl.BlockSpec(memory_space=pltpu.SMEM)],
        out_specs=[pl.BlockSpec((None, ncp, tq), lambda h: (h, 0, 0)),
                   pl.BlockSpec((None, tq, tq), lambda h: (h, 0, 0)),
                   pl.BlockSpec((None, tq, tq), lambda h: (h, 0, 0))],
        out_shape=[jax.ShapeDtypeStruct((N_HEADS, ncp, tq), F32),
                   jax.ShapeDtypeStruct((N_HEADS, tq, tq), F32),
                   jax.ShapeDtypeStruct((N_HEADS, tq, tq), F32)],
        compiler_params=_cparams(("arbitrary",)),
        name="bias_prompt",
    )(rel_bias)


def _bias_sample_kernel(tbl_ref, bcs_ref, bsl_ref, bnew_ref, bw_ref, *, past, s_len, n_c):
    table = lambda b: tbl_ref[:, b:b + 1]

    def qpos(shape):
        return past + lax.rem(lax.broadcasted_iota(jnp.int32, shape, 0), s_len)

    def lane(shape):
        return lax.broadcasted_iota(jnp.int32, shape, 1)

    sh = bcs_ref.shape
    n = lane(sh)
    bcs_ref[...] = jnp.where(n < n_c, _bias_values(qpos(sh) - (CMP_STRIDE * n + CMP_BLOCK - 1), table), NEG)
    sh = bsl_ref.shape
    bsl_ref[...] = _bias_values(qpos(sh) - (past - sh[1] + lane(sh)), table)
    sh = bnew_ref.shape
    bnew_ref[...] = jnp.where(lane(sh) < s_len, _bias_values(qpos(sh) - (past + lane(sh)), table), NEG)
    sh = bw_ref.shape
    d = qpos(sh) - (past - sh[1] + lane(sh))
    bw_ref[...] = jnp.where(d < WINDOW, _bias_values(d, table), NEG)


def _bias_sample(tbl_rows, past, s_len, ncs, ck, wb):
    rows = tbl_rows.shape[0]
    widths = (ncs, ck, LANES, wb)
    return pl.pallas_call(
        functools.partial(_bias_sample_kernel, past=past, s_len=s_len, n_c=(past + s_len - CMP_BLOCK) // CMP_STRIDE + 1),
        out_shape=[jax.ShapeDtypeStruct((rows, w), F32) for w in widths],
        name="bias_sample",
    )(tbl_rows)


MASK_FLOOR = -1e8
M_INIT = -1e30
TAKEN = -3e38
N_FORCED = 3
SCORE_AHEAD = 3
FAR_PER_ITER = 6
VALUE_BEHIND = 2


def _sel_weights(v):
    return jnp.where((v >= 0) & (v <= 2), 2.0, jnp.where((v == -1) | (v == 3), 1.0, 0.0))


def _attn_prompt_kernel(q_ref, gate_ref, ks0_ref, ks1_ref, vs0_ref, vs1_ref, kwk_ref, vw0_ref, vw1_ref,
                        kc2_ref, bc_ref, t0_ref, t1_ref, o_ref,
                        qm_ref, selr_ref, oc_ref, ms_ref, as_ref, mw_ref, aw_ref, *, ncp, nsel):
    tq = Q_TILE
    i = pl.program_id(1)
    q0 = i * tq
    lane = lax.broadcasted_iota(jnp.int32, (tq, LANES), 1)
    heads = [(g, r) for g in range(H_KV) for r in range(GROUP)]

    for g, r in heads:
        gmask = (lane < HEAD_DIM) if g == 0 else (lane >= HEAD_DIM)
        qm_ref[g * GROUP + r] = jnp.where(gmask, q_ref[:, r * LANES:(r + 1) * LANES], jnp.zeros((tq, LANES), BF16))

    n0 = (tq // CMP_STRIDE) * i - CMP_BACK
    ws = pl.multiple_of(lax.rem(n0 + ncp, ncp), 8)
    cl = lax.broadcasted_iota(jnp.int32, (ncp, tq), 0)
    dm = jnp.where((cl >= CMP_NEAR) & (cl < ncp - n0), NEG, 0.0) + jnp.where(cl < -n0, NEG, 0.0)
    jrow = lax.broadcasted_iota(jnp.int32, (nsel, ncp), 0)
    nn = lax.broadcasted_iota(jnp.int32, (nsel, ncp), 1) + n0
    nn = jnp.where(nn < 0, nn + ncp, nn)
    nn = jnp.where(nn >= ncp, nn - ncp, nn)
    mit = _sel_weights(nn - (SEL_BLOCK // CMP_STRIDE) * jrow).astype(BF16)

    jr = lax.broadcasted_iota(jnp.int32, (nsel, tq), 0)
    qpos = q0 + lax.broadcasted_iota(jnp.int32, (nsel, tq), 1)
    cur = lax.shift_right_logical(qpos, int(math.log2(SEL_BLOCK)))
    invalid_pen = jnp.where(jr * SEL_BLOCK <= qpos, 0.0, NEG)
    bonus = jnp.where((jr == 0) | (jr == cur) | (jr == cur - 1), FORCE_BONUS, 0.0)
    pj = lax.broadcasted_iota(jnp.int32, (LANES, LANES), 0)
    pl_ = lax.broadcasted_iota(jnp.int32, (LANES, LANES), 1)
    kj = lax.broadcasted_iota(jnp.int32, (tq, tq), 0)
    qi = lax.broadcasted_iota(jnp.int32, (tq, tq), 1)
    wmask = jnp.where(kj > qi, 0.0, NEG)
    blocks_per_chunk = tq // SEL_BLOCK

    kc = kc2_ref[pl.ds(ws, ncp), 0:KV_HALF]
    vct = kc2_ref[pl.ds(ws, ncp), KV_HALF:KV_W].astype(F32).T.astype(BF16)
    def cmp_scores(h):
        return _dot_nt(kc, qm_ref[h]) + bc_ref[h] + dm

    ahead = {h: cmp_scores(h) for h in range(2)}
    psum = [None] * H_KV
    for h in range(N_HEADS):
        s = ahead.pop(h)
        if h + 2 < N_HEADS:
            ahead[h + 2] = cmp_scores(h + 2)
        m = jnp.maximum(jnp.max(s, axis=0, keepdims=True), MASK_FLOOR)
        e = jnp.exp2(s - m)
        p = e * (1.0 / jnp.maximum(jnp.sum(e, axis=0, keepdims=True), 1e-30))
        oc_ref[h] = _dot(vct, p.astype(BF16))
        psum[h // GROUP] = p if psum[h // GROUP] is None else psum[h // GROUP] + p

    scores = []
    for g in range(H_KV):
        h1, h2, h3 = _split3(psum[g])
        pslc = (_dot(mit, h1) + _dot(mit, h2)) + _dot(mit, h3)
        scores.append(jnp.where(invalid_pen < 0.0, NEG, pslc + bonus))

    def take_max(_, carry):
        out = []
        for work, sel_acc in carry:
            mx = jnp.max(work, axis=0, keepdims=True)
            first = jnp.min(jnp.where(work == mx, jr, nsel), axis=0, keepdims=True)
            hit = jr == first
            out.append((jnp.where(hit, TAKEN, work), jnp.where(hit, 1.0, sel_acc)))
        return tuple(out)

    is_forced = bonus > 0.0
    picked = lax.fori_loop(0, max(min(TOP_N, nsel) - N_FORCED, 0), take_max,
                           tuple((jnp.where(is_forced, TAKEN, sc), jnp.where(is_forced, 1.0, 0.0)) for sc in scores))
    for g in range(H_KV):
        sel_t = picked[g][1]
        if nsel < LANES:
            sel_t = jnp.concatenate([sel_t, jnp.zeros((LANES - nsel, tq), F32)], axis=0)
        selr_ref[g] = sel_t.T.astype(BF16)

    def masked_queries(t, g):
        off = pl_ - _aug_lane(g)
        place = jnp.where((pj - t * blocks_per_chunk == off) & (off >= 0) & (off < blocks_per_chunk), 1.0, 0.0)
        picked = _dot(selr_ref[g], place.astype(BF16))
        auglane = (lane >= _aug_lane(g)) & (lane < _aug_lane(g) + blocks_per_chunk)
        pen = jnp.where(auglane, (picked - 1.0) * (-NEG), 0.0).astype(BF16)
        return [qm_ref[g * GROUP + r] + pen for r in range(GROUP)]

    def run(chunks):
        items = []
        for k_of, vt_of, queries, extra, m_ref, a_ref in chunks:
            cache = {}
            items += [(h, k_of, vt_of, queries, extra, m_ref, a_ref, cache) for h in range(N_HEADS)]

        def kv(item):
            g, cache = item[0] // GROUP, item[7]
            if g not in cache:
                cache[g] = (item[1](g), item[2](g))
            return cache[g]

        def scores(item):
            h, queries, extra = item[0], item[3], item[4]
            s = _dot_nt(kv(item)[0], queries(h))
            return s if extra is None else s + extra(h)

        def softmax(item, s):
            h, m_ref = item[0], item[5]
            m_prev = m_ref[h]
            m_new = jnp.maximum(m_prev, jnp.max(s, axis=0, keepdims=True))
            m_ref[h] = m_new
            return jnp.exp2(s - m_new).astype(BF16), jnp.exp2(m_prev - m_new)

        def accumulate(item, p, alpha):
            h, a_ref = item[0], item[6]
            a_ref[h] = alpha * a_ref[h] + _dot(kv(item)[1], p)

        n = len(items)
        s = {j: scores(items[j]) for j in range(SCORE_AHEAD)}
        pa = {}
        for j in range(n):
            pa[j] = softmax(items[j], s.pop(j))
            if j + SCORE_AHEAD < n:
                s[j + SCORE_AHEAD] = scores(items[j + SCORE_AHEAD])
            if j >= VALUE_BEHIND:
                accumulate(items[j - VALUE_BEHIND], *pa.pop(j - VALUE_BEHIND))
        for j in range(n - VALUE_BEHIND, n):
            accumulate(items[j], *pa.pop(j))

    for m_ref, a_ref in ((ms_ref, as_ref), (mw_ref, aw_ref)):
        m_ref[...] = jnp.full(m_ref.shape, M_INIT, F32)
        a_ref[...] = jnp.zeros(a_ref.shape, F32)

    def sel_chunk(t, extra):
        sl = pl.ds(pl.multiple_of(t * tq, tq), tq)
        qs = [masked_queries(t, g) for g in range(H_KV)]
        return (lambda g: (ks0_ref, ks1_ref)[g][sl, :], lambda g: (vs0_ref, vs1_ref)[g][t],
                lambda h: qs[h // GROUP][h % GROUP], extra, ms_ref, as_ref)

    def win_chunk(t, extra):
        sl = pl.ds(pl.multiple_of(t * tq, tq), tq)
        return (lambda g: kwk_ref[sl, :], lambda g: (vw0_ref, vw1_ref)[g][t],
                lambda h: qm_ref[h], extra, mw_ref, aw_ref)

    n_far = jnp.maximum(i - 1, 0)
    n_iter = n_far // FAR_PER_ITER

    def far_group(u, carry):
        run([sel_chunk(FAR_PER_ITER * u + j, None) for j in range(FAR_PER_ITER)])
        return carry

    lax.fori_loop(0, n_iter, far_group, 0)
    for rem in range(1, FAR_PER_ITER):
        @pl.when(n_far - FAR_PER_ITER * n_iter == rem)
        def _(rem=rem):
            run([sel_chunk(FAR_PER_ITER * n_iter + j, None) for j in range(rem)])

    gone1 = jnp.where(i >= 1, 0.0, NEG)
    gone2 = jnp.where(i >= 2, 0.0, NEG)
    wmask2 = wmask + gone2
    run([sel_chunk(jnp.maximum(i - 1, 0), lambda h: t1_ref[h] + gone1),
         sel_chunk(i, lambda h: t0_ref[h]),
         win_chunk(jnp.maximum(i - 2, 0), lambda h: wmask2),
         win_chunk(jnp.maximum(i - 1, 0), lambda h: t1_ref[h] + gone1),
         win_chunk(i, lambda h: t0_ref[h])])

    gates_t = gate_ref[...].T
    for r in range(GROUP):
        per_g = []
        for g in range(H_KV):
            h = g * GROUP + r
            a_s, a_w = as_ref[h], aw_ref[h]
            o_sel = a_s[0:HEAD_DIM] * (1.0 / jnp.maximum(a_s[HEAD_DIM:HEAD_DIM + 1], 1e-30))
            o_win = a_w[0:HEAD_DIM] * (1.0 / jnp.maximum(a_w[HEAD_DIM:HEAD_DIM + 1], 1e-30))
            per_g.append(gates_t[h:h + 1, :] * oc_ref[h, g * HEAD_DIM:(g + 1) * HEAD_DIM, :]
                         + gates_t[N_HEADS + h:N_HEADS + h + 1, :] * o_sel
                         + gates_t[2 * N_HEADS + h:2 * N_HEADS + h + 1, :] * o_win)
        o_ref[:, r * LANES:(r + 1) * LANES] = jnp.concatenate(per_g, axis=0).T.astype(BF16)


def _attn_prompt(q, gates, ks_g, vs_g, kw_k, vw_g, kc2, bias_tiles, nb, t):
    tq = Q_TILE
    assert t % tq == 0 and WINDOW == 2 * tq
    nt = t // tq
    ncp = t // CMP_STRIDE
    nsel = t // SEL_BLOCK
    assert nsel <= LANES and ncp % LANES == 0
    bc, t0, t1 = bias_tiles
    tok = lambda w: pl.BlockSpec((tq, w), lambda b, i: (b * nt + i, 0))
    k_rows = pl.BlockSpec((None, t, KV_HALF), lambda b, i: (b, 0, 0))
    v_cols = pl.BlockSpec((None, nt, V_ROWS, tq), lambda b, i: (b, 0, 0, 0))
    state = [pltpu.VMEM((N_HEADS, 1, tq), F32), pltpu.VMEM((N_HEADS, V_ROWS, tq), F32)]
    kr = lambda a: a.reshape(nb, t, KV_HALF)
    vc = lambda a: a.reshape(nb, nt, V_ROWS, tq)
    return pl.pallas_call(
        functools.partial(_attn_prompt_kernel, ncp=ncp, nsel=nsel),
        grid=(nb, nt),
        in_specs=[tok(q.shape[1]), tok(LANES), k_rows, k_rows, v_cols, v_cols, k_rows, v_cols, v_cols,
                  pl.BlockSpec((None, 2 * ncp, KV_W), lambda b, i: (b, 0, 0)),
                  _resident(bc.shape), _resident(t0.shape), _resident(t1.shape)],
        out_specs=tok(q.shape[1]),
        out_shape=jax.ShapeDtypeStruct(q.shape, BF16),
        scratch_shapes=[pltpu.VMEM((N_HEADS, tq, LANES), BF16), pltpu.VMEM((H_KV, tq, LANES), BF16),
                        pltpu.VMEM((N_HEADS, LANES, tq), F32)] + state + state,
        compiler_params=_cparams(("arbitrary", "arbitrary")),
        name="attn_prompt",
    )(q, gates, kr(ks_g[0]), kr(ks_g[1]), vc(vs_g[0]), vc(vs_g[1]), kr(kw_k), vc(vw_g[0]), vc(vw_g[1]),
      kc2, bc, t0, t1)


SAMPLE_CK = 1024


def _softmax_part(s, v, transposed):
    m = jnp.max(s, axis=-1, keepdims=True)
    p = jnp.exp2(s - m)
    pv = _dot_nt(p.astype(BF16), v) if transposed else _dot(p.astype(BF16), v)
    return m, jnp.sum(p, axis=-1, keepdims=True), pv


def _merge_parts(parts):
    m = parts[0][0]
    for mt, _, _ in parts[1:]:
        m = jnp.maximum(m, mt)
    l, acc = None, None
    for mt, lt, at in parts:
        w = jnp.exp2(mt - m)
        l = w * lt if l is None else l + w * lt
        acc = w * at if acc is None else acc + w * at
    return acc / jnp.maximum(l, 1e-30)


def _attn_sample_kernel(pt_ref, q_ref, g_ref, knew_ref, wnew_ref, wnewt_ref, win_ref, kc_ref,
                        bcs_ref, bsl_ref, bnew_ref, bw_ref, msel_ref, cache_ref, o_ref, wout_ref, buf, sem,
                        *, n_pages, page, past, s_len, nsel):
    slot = _gather_pages(cache_ref, pt_ref, buf, sem, n_pages, page)

    wb = win_ref.shape[1]
    shifted = pltpu.roll(win_ref[...], wb - s_len, 1)
    fresh = pltpu.roll(wnewt_ref[...], LANES - s_len, 1)
    tail_lane = lax.broadcasted_iota(jnp.int32, fresh.shape, 1) >= LANES - s_len
    wout_ref[:, 0:wb - LANES] = shifted[:, 0:wb - LANES]
    wout_ref[:, wb - LANES:wb] = jnp.where(tail_lane, fresh, shifted[:, wb - LANES:wb])
    q = q_ref[...]
    rows = q.shape[0]
    gs = rows // GROUP
    nselp = msel_ref.shape[1]

    s = _dot_nt(q, kc_ref[:, 0:KV_HALF]) + bcs_ref[...]
    m = jnp.maximum(jnp.max(s, axis=-1, keepdims=True), MASK_FLOOR)
    e = jnp.exp2(s - m)
    p = e / jnp.maximum(jnp.sum(e, axis=-1, keepdims=True), 1e-30)
    o_cmp = _dot(p.astype(BF16), kc_ref[:, KV_HALF:KV_W])
    psum = p[0:gs]
    for r in range(1, GROUP):
        psum = psum + p[r * gs:(r + 1) * gs]

    h1, h2, h3 = _split3(psum)
    msel = msel_ref[...]
    pslc = (_dot(h1, msel) + _dot(h2, msel)) + _dot(h3, msel)
    j = lax.broadcasted_iota(jnp.int32, (gs, nselp), 1)
    qpos = past + lax.rem(lax.broadcasted_iota(jnp.int32, (gs, nselp), 0), s_len)
    cur = lax.shift_right_logical(qpos, int(math.log2(SEL_BLOCK)))
    valid = j * SEL_BLOCK <= qpos
    forced = (j == 0) | (j == cur) | (j == cur - 1)
    score = jnp.where(valid, pslc + jnp.where(forced, FORCE_BONUS, 0.0), NEG)
    score = jnp.where(j < nsel, score, -3e38)
    score_t = jnp.concatenate([score, jnp.zeros((LANES - gs, nselp), F32)], axis=0).T
    jp = lax.broadcasted_iota(jnp.int32, (nselp, nselp), 0)
    jj = lax.broadcasted_iota(jnp.int32, (nselp, nselp), 1)
    sel_rows = []
    for r in range(gs):
        col = score_t[:, r:r + 1]
        row = score[r:r + 1, :]
        beats = jnp.where(jj > jp, jnp.where(col >= row, 1.0, 0.0), jnp.where(col > row, 1.0, 0.0))
        rank = jnp.sum(beats, axis=0, keepdims=True)
        sel_rows.append(jnp.where(rank < min(TOP_N, nsel), 1.0, 0.0))
    sel8 = jnp.concatenate(sel_rows, axis=0)
    sel = jnp.concatenate([sel8] * GROUP, axis=0).astype(BF16)

    ck = bsl_ref.shape[1]
    n_chunks = past // ck
    erow = lax.broadcasted_iota(jnp.int32, (nselp, ck), 0)
    eblk = lax.shift_right_logical(lax.broadcasted_iota(jnp.int32, (nselp, ck), 1), int(math.log2(SEL_BLOCK)))
    scs = []
    for t in range(n_chunks):
        kt = buf[slot, 0:KV_HALF, t * ck:(t + 1) * ck].astype(BF16)
        e_t = jnp.where(erow == t * (ck // SEL_BLOCK) + eblk, 1.0, 0.0).astype(BF16)
        sc = _dot(q, kt) + (_dot(sel, e_t) - 1.0) * (-NEG)
        scs.append(sc + bsl_ref[...] if t == n_chunks - 1 else sc)
    knew = knew_ref[...]
    wnew = wnew_ref[...]
    s_new = _dot_nt(q, knew[:, 0:KV_HALF]) + bnew_ref[...]
    s_win = _dot(q, win_ref[0:KV_HALF, :].astype(BF16)) + bw_ref[...]
    s_wnew = _dot_nt(q, wnew[:, 0:KV_HALF]) + bnew_ref[...]
    parts = [_softmax_part(scs[t], buf[slot, KV_HALF:KV_W, t * ck:(t + 1) * ck].astype(BF16), True)
             for t in range(n_chunks)]
    parts.append(_softmax_part(s_new, knew[:, KV_HALF:KV_W], False))
    o_sel = _merge_parts(parts)

    o_win = _merge_parts([_softmax_part(s_win, win_ref[KV_HALF:KV_W, :].astype(BF16), True),
                          _softmax_part(s_wnew, wnew[:, KV_HALF:KV_W], False)])

    g = g_ref[...]
    o_ref[...] = g[:, 0:1] * o_cmp + g[:, 1:2] * o_sel + g[:, 2:3] * o_win


def _attn_sample(page_table, q32, g32, knew, wnew, wnew_t, win_t, kvc, tiles, msel, cache_t, past, s_len):
    nb, n_pages = page_table.shape
    page = cache_t.shape[2]
    rows = q32.shape[1]
    wb = win_t.shape[2]
    nsel = -(-(past + s_len) // SEL_BLOCK)
    bcs, bsl, bnew, bw = tiles
    per_b = lambda r, w: pl.BlockSpec((None, r, w), lambda b, pt: (b, 0, 0))
    return pl.pallas_call(
        functools.partial(_attn_sample_kernel, n_pages=n_pages, page=page, past=past, s_len=s_len, nsel=nsel),
        grid_spec=pltpu.PrefetchScalarGridSpec(
            num_scalar_prefetch=1,
            grid=(nb,),
            in_specs=[per_b(rows, LANES), per_b(rows, LANES), per_b(LANES, KV_W), per_b(LANES, KV_W),
                      per_b(KV_W, LANES), per_b(KV_W, wb), per_b(kvc.shape[1], KV_W),
                      _resident(bcs.shape), _resident(bsl.shape), _resident(bnew.shape),
                      _resident(bw.shape), _resident(msel.shape),
                      pl.BlockSpec(memory_space=pl.ANY)],
            out_specs=[per_b(rows, LANES), per_b(KV_W, wb)],
            scratch_shapes=[pltpu.VMEM((2, KV_W, past), F32), pltpu.SemaphoreType.DMA((2,))]),
        out_shape=[jax.ShapeDtypeStruct((nb, rows, LANES), F32), jax.ShapeDtypeStruct((nb, KV_W, wb), F32)],
        compiler_params=_cparams(("arbitrary",)),
        name="attn_sample",
    )(page_table, q32, g32, knew, wnew, wnew_t, win_t, kvc, bcs, bsl, bnew, bw, msel, cache_t)


def _prep_weights(w_in, w_cmp1, w_cmp2, pe_cmp, w_conv, w_nsa_out):
    d = w_in.shape[0]
    sizes = (N_HEADS * HEAD_DIM, KV_W, KV_W, KV_W, 3 * N_HEADS, _C_W, _C_W, _C_W, 2 * d)
    offs = [0]
    for sz in sizes:
        offs.append(offs[-1] + sz)
    part = lambda k: w_in[:, offs[k]:offs[k + 1]]
    qp = part(0).reshape(d, H_KV, GROUP, HEAD_DIM).transpose(0, 2, 1, 3).reshape(d, N_HEADS * HEAD_DIM)
    gn = jnp.pad(part(4), ((0, 0), (0, _G_W - 3 * N_HEADS)))
    w_in_p = jnp.concatenate([qp, part(1), part(2), part(3), gn, part(5), part(6), part(7), part(8)],
                             axis=1).astype(BF16)
    w_nsa_p = w_nsa_out.reshape(H_KV, GROUP, HEAD_DIM, -1).transpose(1, 0, 2, 3).reshape(
        N_HEADS * HEAD_DIM, -1).astype(BF16)
    r = CMP_BLOCK // CMP_STRIDE
    w1r = w_cmp1.reshape(2, r, CMP_STRIDE // 2, 2, HEAD_DIM, CMP_HIDDEN)
    w1t = jnp.transpose(w1r, (0, 2, 3, 4, 1, 5))
    w1e = w1t[:, :, :, :, :, None, :].astype(BF16)
    z1 = jnp.zeros_like(w1e)
    wb = jnp.stack([jnp.concatenate([w1e if gp == g else z1 for gp in range(H_KV)], axis=5)
                    for g in range(H_KV)], axis=3)
    wbig = wb.reshape(2, CMP_STRIDE // 2, 2 * KV_HALF, CMP_OUT_W)
    z2 = jnp.zeros((CMP_HIDDEN, HEAD_DIM), F32)
    w2bd = jnp.concatenate(
        [jnp.concatenate([w_cmp2[c] if (cp, gp) == (c, g) else z2 for cp in range(2) for gp in range(H_KV)], axis=1)
         for c in range(2) for g in range(H_KV)], axis=0).astype(BF16)
    pe_r = pe_cmp.reshape(2, r, CMP_STRIDE, HEAD_DIM).transpose(1, 2, 0, 3)
    pe_rows = jnp.broadcast_to(pe_r[:, :, :, None, :], (r, CMP_STRIDE, 2, H_KV, HEAD_DIM)).reshape(r, CHUNK_W)
    pe8 = jnp.pad(pe_rows, ((0, 8 - r), (0, 0)))
    w_conv8 = jnp.pad(w_conv, ((0, 8 - CONV_WIDTH), (0, 0)))
    return w_in_p, w_nsa_p, wbig, w2bd, pe8, w_conv8


def _sel_matrix(n_c, nselp):
    n = jnp.arange(n_c + 1)[:, None]
    j = jnp.arange(nselp)[None, :]
    return _sel_weights(n - (SEL_BLOCK // CMP_STRIDE) * j).astype(BF16)


def kernel(x_prompt, x_sample, cache_kv_cmp, cache_kv_sel, state_kv_win, state_conv, page_table,
           c_prompt, c_sample, w_ada, b_ada, g_norm, w_ffn1_gu, w_ffn1_down, w_ffn2_gu, w_ffn2_down,
           w_in, w_cmp1, w_cmp2, pe_cmp, w_conv, w_nsa_out, w_conv_out, w_out, rel_bias, g_final):
    assert w_ada.shape[0] == 1, "single-layer trunk"
    nbp, t, d = x_prompt.shape
    nbs, s_len, _ = x_sample.shape
    n_pages = page_table.shape[1]
    page = cache_kv_cmp.shape[2]
    past = n_pages * page
    n_phys = cache_kv_cmp.shape[1]
    wb = state_kv_win.shape[2]
    assert wb == WINDOW and past % SAMPLE_CK == 0

    w_in_p, w_nsa_p, wbig, w2bd, pe8, w_conv8 = _prep_weights(
        w_in[0], w_cmp1[0], w_cmp2[0], pe_cmp[0], w_conv[0], w_nsa_out[0])
    wgu1, wd1 = w_ffn1_gu[0].astype(BF16), w_ffn1_down[0].astype(BF16)
    wgu2, wd2 = w_ffn2_gu[0].astype(BF16), w_ffn2_down[0].astype(BF16)
    w_cv, w_o = w_conv_out[0].astype(BF16), w_out[0].astype(BF16)
    gn = [g_norm[0][k:k + 1] for k in range(N_SUB)]

    n_c_rows = nbp + nbs
    c_all = jnp.pad(jnp.concatenate([c_prompt, c_sample], axis=0), ((0, (-n_c_rows) % 8), (0, 0)))
    mod_all = _ada(c_all, w_ada[0], b_ada[0])
    mod_p = mod_all[:nbp].reshape(nbp * 3 * N_SUB, 1, d)
    mod_s = jnp.transpose(jnp.repeat(mod_all[nbp:n_c_rows].reshape(nbs, 3 * N_SUB, d), s_len, axis=0), (1, 0, 2))

    tm = TOKEN_TILE if t % TOKEN_TILE == 0 else t
    tpb = t // tm
    xp = x_prompt.reshape(nbp * t, d)
    h1 = _ffn(xp, mod_p, 0, gn[0], wgu1, wd1, tm, tpb)
    (q, kc_k, kc_v, kvc_t, kvs_t, kvw_t, ks0, ks1, vs0, vs1, kw_k, vw0, vw1, gates, z, cb, gm) = _inproj(
        h1, mod_p, gn[1], w_in_p, tm, tpb, True)
    kc2 = _cmp_prompt(kc_k, kc_v, t, wbig, pe8, w2bd)
    tiles_p = _bias_prompt(rel_bias, t // CMP_STRIDE)
    o_nsa = _attn_prompt(q, gates, (ks0, ks1), (vs0, vs1), kw_k, (vw0, vw1), kc2, tiles_p, nbp, t)
    tm_mix = 2 * tm if t % (2 * tm) == 0 else tm
    h2 = _mixout(h1, o_nsa, z, None, cb, gm, mod_p, w_conv8, w_nsa_p, w_cv, w_o, tm_mix, t // tm_mix)
    y_prompt = _ffn(h2, mod_p, 2, gn[2], wgu2, wd2, tm, tpb, g_final).reshape(nbp, t, d)

    kv_shape = (2, H_KV, HEAD_DIM)
    kv_out = lambda a: jnp.transpose(a.reshape((1, nbp) + kv_shape + (a.shape[-1],)), (0, 1, 5, 2, 3, 4))
    kv_cmp_p = kv_out(kvc_t)
    kv_sel_p = kv_out(kvs_t)
    keep = min(WINDOW, t)
    kv_win_p = kv_out(kvw_t[:, :, t - keep:])
    conv_p = z.reshape(1, nbp, t, -1)[:, :, t - (CONV_WIDTH - 1):]

    ns = nbs * s_len
    xs = x_sample.reshape(ns, d)
    h1s = _ffn(xs, mod_s, 0, gn[0], wgu1, wd1, ns, 1)
    qs, kvc_s, kvs_s, kvw_s, kvs_sb, kvw_sb, gates_s, z_s, cb_s, gm_s = _inproj(
        h1s, mod_s, gn[1], w_in_p, ns, 1, False)

    pos_minor = lambda a: jnp.transpose(a, (0, 2, 3, 4, 1)).reshape(a.shape[0], KV_W, a.shape[1])
    kvc_past = _cmp_sample(page_table, pos_minor(cache_kv_cmp[0]), wbig, pe8, w2bd)
    n_c = (past + s_len - CMP_BLOCK) // CMP_STRIDE + 1
    nsel = -(-(past + s_len) // SEL_BLOCK)
    nselp = -(-nsel // LANES) * LANES
    rows = N_HEADS * s_len
    head_of_row = [g * GROUP + r for r in range(GROUP) for g in range(H_KV) for _ in range(s_len)]
    tbl_rows = jnp.pad(rel_bias.T[jnp.array(head_of_row)], ((0, 0), (0, LANES - N_BUCKETS)))
    tiles_s = _bias_sample(tbl_rows, past, s_len, past // CMP_STRIDE, SAMPLE_CK, wb)
    msel = _sel_matrix(past // CMP_STRIDE - 1, nselp)

    q5 = qs.reshape(nbs, s_len, GROUP, H_KV, HEAD_DIM).transpose(0, 2, 3, 1, 4)
    q32 = jnp.zeros((nbs, GROUP, H_KV, s_len, H_KV, HEAD_DIM), BF16)
    for g in range(H_KV):
        q32 = q32.at[:, :, g, :, g, :].set(q5[:, :, g])
    q32 = q32.reshape(nbs, rows, LANES)
    g5 = gates_s[:, :3 * N_HEADS].reshape(nbs, s_len, 3, H_KV, GROUP).transpose(0, 4, 3, 1, 2)
    g32 = jnp.pad(g5.reshape(nbs, rows, 3), ((0, 0), (0, 0), (0, LANES - 3)))
    pad_new = lambda a: jnp.pad(a.reshape(nbs, s_len, KV_W), ((0, 0), (0, LANES - s_len), (0, 0)))
    wnew_t = jnp.pad(jnp.transpose(kvw_s.reshape(nbs, s_len, KV_W), (0, 2, 1)), ((0, 0), (0, 0), (0, LANES - s_len)))
    o32, win_next = _attn_sample(page_table, q32, g32, pad_new(kvs_sb), pad_new(kvw_sb), wnew_t,
                                 pos_minor(state_kv_win[0]), kvc_past, tiles_s, msel,
                                 pos_minor(cache_kv_sel[0]), past, s_len)
    o6 = o32.reshape(nbs, GROUP, H_KV, s_len, H_KV, HEAD_DIM)
    o_s = jnp.stack([o6[:, :, g, :, g, :] for g in range(H_KV)], axis=3)
    o_nsa_s = o_s.transpose(0, 2, 1, 3, 4).reshape(ns, N_HEADS * HEAD_DIM).astype(BF16)

    full = jnp.concatenate([state_conv[0], z_s.reshape(nbs, s_len, -1)], axis=1)
    z_shift = (full[:, 1:1 + s_len].reshape(ns, -1), full[:, 0:s_len].reshape(ns, -1))
    h2s = _mixout(h1s, o_nsa_s, z_s, z_shift, cb_s, gm_s, mod_s, w_conv8, w_nsa_p, w_cv, w_o, ns, 1)
    y_sample = _ffn(h2s, mod_s, 2, gn[2], wgu2, wd2, ns, 1, g_final).reshape(nbs, s_len, d)

    kv_cmp_s = kvc_s.reshape((1, nbs, s_len) + kv_shape)
    kv_sel_s = kvs_s.reshape((1, nbs, s_len) + kv_shape)
    kv_win_s = jnp.transpose(win_next.reshape((1, nbs) + kv_shape + (wb,)), (0, 1, 5, 2, 3, 4))
    conv_s = full[None, :, s_len:]
    return (y_prompt, y_sample, kv_cmp_p, kv_sel_p, kv_win_p, conv_p, kv_cmp_s, kv_sel_s, kv_win_s, conv_s)
```

```python
import functools
import math

import jax
import jax.numpy as jnp
from jax import lax
from jax.experimental import pallas as pl
from jax.experimental.pallas import tpu as pltpu

F32 = jnp.float32
BF16 = jnp.bfloat16

HEAD_DIM = 64
N_HEADS = 8
H_KV = 2
GROUP = N_HEADS // H_KV
CMP_BLOCK = 32
CMP_STRIDE = 16
CMP_HIDDEN = 2 * HEAD_DIM
SEL_BLOCK = 64
TOP_N = 16
WINDOW = 512
CONV_WIDTH = 3
N_BUCKETS = 32
MAX_DISTANCE = 128
N_SUB = 3
EPS = 1e-6
NEG = -1e9
LOG2E = math.log2(math.e)
FORCE_BONUS = 1e3
KV_W = 2 * H_KV * HEAD_DIM
KV_HALF = H_KV * HEAD_DIM
LANES = 128
Q_TILE = 256
CMP_NEAR = 32
VMEM_LIMIT = 56 * 1024 * 1024
TOKEN_TILE = 512
ADA_COL_TILES = 8


def _bucket_thresholds():
    max_exact = N_BUCKETS // 2

    def bucket(d):
        if d < max_exact:
            return d
        large = max_exact + int(math.log(d / max_exact) / math.log(MAX_DISTANCE / max_exact)
                                * (N_BUCKETS - max_exact))
        return min(large, N_BUCKETS - 1)

    thr, d = [], 0
    for b in range(N_BUCKETS):
        while bucket(d) < b:
            d += 1
        thr.append(d)
    return tuple(thr)


BUCKET_THR = _bucket_thresholds()


def _cparams(sem):
    return pltpu.CompilerParams(dimension_semantics=sem, vmem_limit_bytes=VMEM_LIMIT)


def _resident(shape):
    nd = len(shape)
    return pl.BlockSpec(shape, lambda *_: (0,) * nd, pipeline_mode=pl.Buffered(1))


def _dot(a, b):
    return jnp.dot(a, b, preferred_element_type=F32)


def _dot_nt(a, b):
    return lax.dot_general(a, b, (((1,), (1,)), ((), ())), preferred_element_type=F32)


def _split3(x):
    h1 = x.astype(BF16)
    r1 = x - h1.astype(F32)
    h2 = r1.astype(BF16)
    h3 = (r1 - h2.astype(F32)).astype(BF16)
    return h1, h2, h3


def _modulated_norm(x, g, shift, scale):
    y = x * lax.rsqrt(jnp.mean(x * x, axis=-1, keepdims=True) + EPS)
    return (y * g) * (1.0 + scale) + shift


def _ada_kernel(c_ref, w_ref, b_ref, o_ref):
    c = c_ref[...]
    a = (c * jax.nn.sigmoid(c)).astype(BF16)
    o_ref[...] = _dot(a, w_ref[...].astype(BF16)) + b_ref[...]


def _ada(c_all, w_ada, b_ada):
    rows, d = c_all.shape
    n = w_ada.shape[1]
    tn = n // ADA_COL_TILES
    return pl.pallas_call(
        _ada_kernel,
        grid=(n // tn,),
        in_specs=[pl.BlockSpec((rows, d), lambda j: (0, 0)),
                  pl.BlockSpec((d, tn), lambda j: (0, j)),
                  pl.BlockSpec((1, tn), lambda j: (0, j))],
        out_specs=pl.BlockSpec((rows, tn), lambda j: (0, j)),
        out_shape=jax.ShapeDtypeStruct((rows, n), F32),
        compiler_params=_cparams(("arbitrary",)),
        name="ada",
    )(c_all, w_ada, b_ada.reshape(1, n))


def _ffn_kernel(*refs, d_ff, fc, final_norm):
    if final_norm:
        x_ref, sh_ref, sc_ref, gt_ref, gn_ref, wgu_ref, wd_ref, gf_ref, o_ref = refs
    else:
        x_ref, sh_ref, sc_ref, gt_ref, gn_ref, wgu_ref, wd_ref, o_ref = refs
    x = x_ref[...]
    ub = _modulated_norm(x, gn_ref[...], sh_ref[...], sc_ref[...]).astype(BF16)
    acc = jnp.zeros(x.shape, F32)
    for c in range(d_ff // fc):
        g = _dot(ub, wgu_ref[:, c * fc:(c + 1) * fc])
        v = _dot(ub, wgu_ref[:, d_ff + c * fc:d_ff + (c + 1) * fc])
        a = ((g * jax.nn.sigmoid(g)) * v).astype(BF16)
        acc = acc + _dot(a, wd_ref[c * fc:(c + 1) * fc, :])
    h = x + (0.5 * gt_ref[...]) * acc
    if final_norm:
        h = (h * lax.rsqrt(jnp.mean(h * h, axis=-1, keepdims=True) + EPS)) * gf_ref[...]
    o_ref[...] = h


def _mod_specs(mod, ks, tm, tiles_per_batch):
    if mod.ndim == 3 and mod.shape[1] == 1:
        d = mod.shape[-1]
        specs = [pl.BlockSpec((None, 1, d), lambda i, k=k: ((i // tiles_per_batch) * (3 * N_SUB) + k, 0, 0))
                 for k in ks]
        return specs, [mod] * len(ks)
    d = mod.shape[-1]
    specs = [pl.BlockSpec((None, tm, d), lambda i, k=k: (k, i, 0)) for k in ks]
    return specs, [mod] * len(ks)


def _ffn(x, mod, sub, gn_row, w_gu, w_down, tm, tiles_per_batch, g_final=None):
    n, d = x.shape
    d_ff = w_down.shape[0]
    fc = d_ff // 2 if (d_ff // 2) % LANES == 0 else d_ff
    final_norm = g_final is not None
    mspecs, mops = _mod_specs(mod, (3 * sub, 3 * sub + 1, 3 * sub + 2), tm, tiles_per_batch)
    in_specs = [pl.BlockSpec((tm, d), lambda i: (i, 0))] + mspecs + [
        pl.BlockSpec((1, d), lambda i: (0, 0)), _resident(w_gu.shape), _resident(w_down.shape)]
    ops = [x] + mops + [gn_row, w_gu, w_down]
    if final_norm:
        in_specs.append(pl.BlockSpec((1, d), lambda i: (0, 0)))
        ops.append(g_final.reshape(1, d))
    return pl.pallas_call(
        functools.partial(_ffn_kernel, d_ff=d_ff, fc=fc, final_norm=final_norm),
        grid=(n // tm,),
        in_specs=in_specs,
        out_specs=pl.BlockSpec((tm, d), lambda i: (i, 0)),
        out_shape=jax.ShapeDtypeStruct((n, d), F32),
        compiler_params=_cparams(("arbitrary",)),
        name="ffn_final" if final_norm else "ffn",
    )(*ops)


_Q_W = N_HEADS * HEAD_DIM
_G_W = LANES
_C_W = 512
_SEG = {}
_off = 0
for _name, _w in (("q", _Q_W), ("kc", KV_W), ("ks", KV_W), ("kw", KV_W), ("gn", _G_W),
                  ("ch", _C_W), ("cb", _C_W), ("cc", _C_W)):
    _SEG[_name] = (_off, _off + _w)
    _off += _w
_MG_OFF = _off


V_ROWS = HEAD_DIM + 16


def _aug_lane(g):
    return HEAD_DIM if g == 0 else 0


def _inproj_kernel(*refs, prompt):
    x_ref, sh_ref, sc_ref, gn_ref, w_ref = refs[:5]
    ub = _modulated_norm(x_ref[...], gn_ref[...], sh_ref[...], sc_ref[...]).astype(BF16)

    def seg(name):
        lo, hi = _SEG[name]
        return _dot(ub, w_ref[:, lo:hi])

    if prompt:
        (q_ref, kck_ref, kcv_ref, kct_ref, kst_ref, kwt_ref, ks0_ref, ks1_ref, vs0_ref, vs1_ref, kwk_ref,
         vw0_ref, vw1_ref, g_ref, z_ref, cb_ref, gm_ref) = refs[5:]
        kc = seg("kc")
        kck_ref[...] = kc[:, 0:KV_HALF]
        kcv_ref[...] = kc[:, KV_HALF:KV_W]
        kct_ref[...] = kc.T
        tm = kc.shape[0]
        lane = lax.broadcasted_iota(jnp.int32, (tm, KV_HALF), 1)
        blk = lax.shift_right_logical(lax.broadcasted_iota(jnp.int32, (tm, KV_HALF), 0),
                                      int(math.log2(SEL_BLOCK))) & (Q_TILE // SEL_BLOCK - 1)
        ones_rows = jnp.where(lax.broadcasted_iota(jnp.int32, (V_ROWS - HEAD_DIM, tm), 0) == 0, 1.0, 0.0)
        for name, t_ref, k_refs, vt_refs in (("ks", kst_ref, (ks0_ref, ks1_ref), (vs0_ref, vs1_ref)),
                                             ("kw", kwt_ref, (kwk_ref,), (vw0_ref, vw1_ref))):
            kv = seg(name)
            kv_t = kv.T
            t_ref[...] = kv_t
            kk = kv[:, 0:KV_HALF]
            if len(k_refs) == 1:
                k_refs[0][...] = kk.astype(BF16)
            else:
                for g, k_ref in enumerate(k_refs):
                    own = (lane < HEAD_DIM) if g == 0 else (lane >= HEAD_DIM)
                    onehot = jnp.where(lane - _aug_lane(g) == blk, 1.0, 0.0)
                    k_ref[...] = jnp.where(own, kk, onehot).astype(BF16)
            for g, vt_ref in enumerate(vt_refs):
                vg = jnp.concatenate(
                    [kv_t[KV_HALF + g * HEAD_DIM:KV_HALF + (g + 1) * HEAD_DIM, :], ones_rows], axis=0)
                for c in range(vt_ref.shape[0]):
                    vt_ref[c] = vg[:, c * Q_TILE:(c + 1) * Q_TILE].astype(BF16)
    else:
        q_ref, kc_ref, ks_ref, kw_ref, ksb_ref, kwb_ref, g_ref, z_ref, cb_ref, gm_ref = refs[5:]
        kc_ref[...] = seg("kc")
        ks = seg("ks")
        ks_ref[...] = ks
        ksb_ref[...] = ks.astype(BF16)
        kw = seg("kw")
        kw_ref[...] = kw
        kwb_ref[...] = kw.astype(BF16)
    q_ref[...] = (seg("q") * (HEAD_DIM ** -0.5 * LOG2E)).astype(BF16)
    g_ref[...] = jax.nn.sigmoid(seg("gn"))
    z_ref[...] = seg("cc") * seg("ch")
    cb_ref[...] = seg("cb").astype(cb_ref.dtype)
    d2 = gm_ref.shape[1]
    half = d2 // 2
    for c in range(2):
        gm_ref[:, c * half:(c + 1) * half] = jax.nn.sigmoid(
            _dot(ub, w_ref[:, _MG_OFF + c * half:_MG_OFF + (c + 1) * half])).astype(gm_ref.dtype)


def _inproj(h, mod, gn_row, w_in_p, tm, tiles_per_batch, prompt):
    n, d = h.shape
    d_conv = _C_W
    mspecs, mops = _mod_specs(mod, (3, 4), tm, tiles_per_batch)
    rows = lambda w, dt: (pl.BlockSpec((tm, w), lambda i: (i, 0)), jax.ShapeDtypeStruct((n, w), dt))
    tail = [rows(_G_W, F32), rows(d_conv, F32), rows(d_conv, BF16), rows(2 * d, BF16)]
    if prompt:
        nb = n // (tm * tiles_per_batch)
        t = tm * tiles_per_batch
        cpt = tm // Q_TILE
        tr = (pl.BlockSpec((None, KV_W, tm), lambda i: (i // tiles_per_batch, 0, i % tiles_per_batch)),
              jax.ShapeDtypeStruct((nb, KV_W, t), F32))
        vt = (pl.BlockSpec((cpt, V_ROWS, Q_TILE), lambda i: (i, 0, 0)),
              jax.ShapeDtypeStruct((n // Q_TILE, V_ROWS, Q_TILE), BF16))
        kr = rows(KV_HALF, BF16)
        outs = [rows(_Q_W, BF16), rows(KV_HALF, F32), rows(KV_HALF, F32), tr, tr, tr,
                kr, kr, vt, vt, kr, vt, vt] + tail
    else:
        outs = [rows(_Q_W, BF16), rows(KV_W, F32), rows(KV_W, F32), rows(KV_W, F32),
                rows(KV_W, BF16), rows(KV_W, BF16)] + tail
    return pl.pallas_call(
        functools.partial(_inproj_kernel, prompt=prompt),
        grid=(n // tm,),
        in_specs=[pl.BlockSpec((tm, d), lambda i: (i, 0))] + mspecs + [
            pl.BlockSpec((1, d), lambda i: (0, 0)), _resident(w_in_p.shape)],
        out_specs=[o[0] for o in outs],
        out_shape=[o[1] for o in outs],
        compiler_params=_cparams(("arbitrary",)),
        name="inproj_prompt" if prompt else "inproj",
    )(h, *mops, gn_row, w_in_p)


def _mixout_kernel(*refs, halo, tiles_per_batch):
    if halo:
        (h_ref, o_ref, z_ref, zp_ref, cb_ref, gm_ref, g2_ref, wc_ref,
         wn_ref, wcv_ref, wo_ref, out_ref) = refs
        z = z_ref[...]
        tm = z.shape[0]
        first = (pl.program_id(0) % tiles_per_batch) == 0
        prev = jnp.where(first, 0.0, zp_ref[...])
        row = lax.broadcasted_iota(jnp.int32, z.shape, 0)
        zm1 = jnp.where(row == 0, prev[7:8, :], pltpu.roll(z, 1, 0))
        zm2 = jnp.where(row == 0, prev[6:7, :], jnp.where(row == 1, prev[7:8, :], pltpu.roll(z, 2, 0)))
    else:
        (h_ref, o_ref, z_ref, zm1_ref, zm2_ref, cb_ref, gm_ref, g2_ref, wc_ref,
         wn_ref, wcv_ref, wo_ref, out_ref) = refs
        z, zm1, zm2 = z_ref[...], zm1_ref[...], zm2_ref[...]
    conv = wc_ref[0:1, :] * zm2 + wc_ref[1:2, :] * zm1 + wc_ref[2:3, :] * z
    y = (cb_ref[...].astype(F32) * conv).astype(BF16)
    d = h_ref.shape[1]
    merged = gm_ref[:, 0:d].astype(F32) * _dot(o_ref[...], wn_ref[...]) \
        + gm_ref[:, d:2 * d].astype(F32) * _dot(y, wcv_ref[...])
    out_ref[...] = h_ref[...] + g2_ref[...] * _dot(merged.astype(BF16), wo_ref[...])


def _mixout(h, o_nsa, z, z_shift, cb, gm, mod, w_conv8, w_nsa_p, w_cv, w_o, tm, tiles_per_batch):
    n, d = h.shape
    dc = z.shape[1]
    halo = z_shift is None
    mspecs, mops = _mod_specs(mod, (5,), tm, tiles_per_batch)
    tok = lambda w: pl.BlockSpec((tm, w), lambda i: (i, 0))
    if halo:
        zspecs = [tok(dc), pl.BlockSpec((8, dc), lambda i: (jnp.maximum(i * (tm // 8) - 1, 0), 0))]
        zops = [z, z]
    else:
        zspecs = [tok(dc), tok(dc), tok(dc)]
        zops = [z, z_shift[0], z_shift[1]]
    return pl.pallas_call(
        functools.partial(_mixout_kernel, halo=halo, tiles_per_batch=tiles_per_batch),
        grid=(n // tm,),
        in_specs=[tok(d), tok(o_nsa.shape[1])] + zspecs + [tok(dc), tok(2 * d)] + mspecs + [
            pl.BlockSpec((8, dc), lambda i: (0, 0)),
            _resident(w_nsa_p.shape), _resident(w_cv.shape), _resident(w_o.shape)],
        out_specs=tok(d),
        out_shape=jax.ShapeDtypeStruct((n, d), F32),
        compiler_params=_cparams(("arbitrary",)),
        name="mixout",
    )(h, o_nsa, *zops, cb, gm, *mops, w_conv8, w_nsa_p, w_cv, w_o)


CHUNK_W = CMP_STRIDE * KV_W
CMP_OUT_W = 2 * H_KV * CMP_HIDDEN


def _gelu_tanh(x):
    return x * (0.5 * (1.0 + jnp.tanh(math.sqrt(2.0 / math.pi) * (x + 0.044715 * (x * x * x)))))


def _chunk_partials(tile, wc_ref, after_step=None):
    accs = []
    for c in range(2):
        acc = None
        for lp in range(CMP_STRIDE // 2):
            x2 = jnp.concatenate([tile(2 * lp, c), tile(2 * lp + 1, c)], axis=1)
            part = _dot(x2, wc_ref[c, lp])
            acc = part if acc is None else acc + part
            if after_step is not None:
                after_step(c, lp)
        accs.append(acc)
    hw = CMP_OUT_W // 2
    return jnp.concatenate([accs[0][:, 0:hw], accs[1][:, 0:hw], accs[0][:, hw:], accs[1][:, hw:]], axis=1)


def _compress_tail(p, wc_ref, pe_ref, w2_ref):
    n_ch = p.shape[0]
    pb = _chunk_partials(
        lambda l, c: pe_ref[:, l * KV_W + c * KV_HALF:l * KV_W + (c + 1) * KV_HALF].astype(BF16), wc_ref)
    pre = (pb[0:1, 0:CMP_OUT_W] + pb[1:2, CMP_OUT_W:]) + p[:, 0:CMP_OUT_W] \
        + pltpu.roll(p[:, CMP_OUT_W:], n_ch - 1, 0)
    out = _dot(_gelu_tanh(pre).astype(BF16), w2_ref[...])
    row = lax.broadcasted_iota(jnp.int32, out.shape, 0)
    return jnp.where(row < n_ch - 1, out, 0.0)


def _cmp_prompt_kernel(xk_ref, xv_ref, wc_ref, pe_ref, w2_ref, o_ref):
    xs = (xk_ref, xv_ref)
    n_ch = xk_ref.shape[0] // CMP_STRIDE
    p = _chunk_partials(lambda l, c: xs[c][pl.ds(l, n_ch, stride=CMP_STRIDE), :].astype(BF16), wc_ref)
    out = _compress_tail(p, wc_ref, pe_ref, w2_ref).astype(BF16)
    o_ref[0:n_ch, :] = out
    o_ref[n_ch:2 * n_ch, :] = out


def _cmp_prompt(rows_k, rows_v, t, wbig, pe8, w2bd):
    nb = rows_k.shape[0] // t
    n_ch = t // CMP_STRIDE
    half = pl.BlockSpec((t, KV_HALF), lambda b: (b, 0))
    return pl.pallas_call(
        _cmp_prompt_kernel,
        grid=(nb,),
        in_specs=[half, half, _resident(wbig.shape), _resident(pe8.shape), _resident(w2bd.shape)],
        out_specs=pl.BlockSpec((None, 2 * n_ch, KV_W), lambda b: (b, 0, 0)),
        out_shape=jax.ShapeDtypeStruct((nb, 2 * n_ch, KV_W), BF16),
        compiler_params=_cparams(("arbitrary",)),
        name="cmp_prompt",
    )(rows_k, rows_v, wbig, pe8, w2bd)


def _page_copies(cache_ref, pt_ref, b, dst, sem, n_pages, page):
    return [pltpu.make_async_copy(cache_ref.at[pt_ref[b, p]], dst.at[:, pl.ds(p * page, page)], sem)
            for p in range(n_pages)]


def _gather_pages(cache_ref, pt_ref, buf, sem, n_pages, page):
    b = pl.program_id(0)
    nb = pl.num_programs(0)
    slot = b % 2
    copies = lambda bb, sl: _page_copies(cache_ref, pt_ref, bb, buf.at[sl], sem.at[sl], n_pages, page)

    @pl.when(b == 0)
    def _():
        for c in copies(0, 0):
            c.start()

    @pl.when(b + 1 < nb)
    def _():
        for c in copies(b + 1, 1 - slot):
            c.start()

    for c in copies(b, slot):
        c.wait()
    return slot


def _cmp_sample_kernel(pt_ref, cache_ref, wc_ref, pe_ref, w2_ref, o_ref,
                       buf_a, buf_b, xk_a, xv_a, xk_b, xv_b, sem, *, n_pages, page, nb):
    b = pl.program_id(0)
    bufs = (buf_a, buf_b)
    xs = ((xk_a, xv_a), (xk_b, xv_b))
    past = xk_a.shape[0]
    n_ch = past // CMP_STRIDE
    n_slabs = CMP_STRIDE // 2
    slab = past // n_slabs
    copies = lambda bb, par: _page_copies(cache_ref, pt_ref, bb, bufs[par], sem.at[par], n_pages, page)

    def transpose_slab(par, c, j):
        xs[par][c][j * slab:(j + 1) * slab, :] = bufs[par][c * KV_HALF:(c + 1) * KV_HALF, j * slab:(j + 1) * slab].T

    @pl.when(b == 0)
    def _():
        for cp in copies(0, 0):
            cp.start()
        if nb > 1:
            for cp in copies(1, 1):
                cp.start()
        for cp in copies(0, 0):
            cp.wait()
        for c in range(2):
            for j in range(n_slabs):
                transpose_slab(0, c, j)

    def step(par):
        nxt = 1 - par
        hook = None
        if nb > 1:
            @pl.when(b + 1 < nb)
            def _():
                for cp in copies(b + 1, nxt):
                    cp.wait()

            @pl.when(b + 2 < nb)
            def _():
                for cp in copies(b + 2, par):
                    cp.start()

            hook = lambda c, lp: transpose_slab(nxt, c, lp)
        p = _chunk_partials(lambda l, c: xs[par][c][pl.ds(l, n_ch, stride=CMP_STRIDE), :].astype(BF16), wc_ref, hook)
        o_ref[...] = _compress_tail(p, wc_ref, pe_ref, w2_ref).astype(BF16)

    for par in range(2):
        pl.when(b % 2 == par)(functools.partial(step, par))


def _cmp_sample(page_table, cache_t, wbig, pe8, w2bd):
    nb, n_pages = page_table.shape
    page = cache_t.shape[2]
    past = n_pages * page
    n_ch = past // CMP_STRIDE
    return pl.pallas_call(
        functools.partial(_cmp_sample_kernel, n_pages=n_pages, page=page, nb=nb),
        grid_spec=pltpu.PrefetchScalarGridSpec(
            num_scalar_prefetch=1,
            grid=(nb,),
            in_specs=[pl.BlockSpec(memory_space=pl.ANY),
                      _resident(wbig.shape), _resident(pe8.shape), _resident(w2bd.shape)],
            out_specs=pl.BlockSpec((None, n_ch, KV_W), lambda b, pt: (b, 0, 0)),
            scratch_shapes=[pltpu.VMEM((KV_W, past), F32)] * 2 + [pltpu.VMEM((past, KV_HALF), F32)] * 4
            + [pltpu.SemaphoreType.DMA((2,))]),
        out_shape=jax.ShapeDtypeStruct((nb, n_ch, KV_W), BF16),
        compiler_params=_cparams(("arbitrary",)),
        name="cmp_sample",
    )(page_table, cache_t, wbig, pe8, w2bd)


def _bias_values(d, table):
    acc = jnp.zeros(d.shape, F32) + table(0)
    for b in range(1, N_BUCKETS):
        acc = jnp.where(d >= BUCKET_THR[b], table(b), acc)
    return jnp.where(d >= 0, (acc - table(N_BUCKETS - 1)) * LOG2E, NEG)


CMP_BACK = 16


def _bias_prompt_kernel(tbl_ref, bc_ref, t0_ref, t1_ref):
    h = pl.program_id(0)
    table = lambda b: tbl_ref[b, h]
    ncp, tq = bc_ref.shape
    kj = lax.broadcasted_iota(jnp.int32, (tq, tq), 0)
    qi = lax.broadcasted_iota(jnp.int32, (tq, tq), 1)
    t0_ref[...] = _bias_values(qi - kj, table)
    t1_ref[...] = _bias_values(qi - kj + tq, table)
    c = lax.broadcasted_iota(jnp.int32, (ncp, tq), 0)
    qi = lax.broadcasted_iota(jnp.int32, (ncp, tq), 1)
    d = qi - CMP_STRIDE * c + (CMP_STRIDE * CMP_BACK - (CMP_BLOCK - 1))
    bc_ref[...] = jnp.where(c < CMP_NEAR, _bias_values(d, table), 0.0)


def _bias_prompt(rel_bias, ncp):
    tq = Q_TILE
    return pl.pallas_call(
        _bias_prompt_kernel,
        grid=(N_HEADS,),
        in_specs=[pl.BlockSpec(memory_space=pltpu.SMEM)],
        out_specs=[pl.BlockSpec((None, ncp, tq), lambda h: (h, 0, 0)),
                   pl.BlockSpec((None, tq, tq), lambda h: (h, 0, 0)),
                   pl.BlockSpec((None, tq, tq), lambda h: (h, 0, 0))],
        out_shape=[jax.ShapeDtypeStruct((N_HEADS, ncp, tq), F32),
                   jax.ShapeDtypeStruct((N_HEADS, tq, tq), F32),
                   jax.ShapeDtypeStruct((N_HEADS, tq, tq), F32)],
        compiler_params=_cparams(("arbitrary",)),
        name="bias_prompt",
    )(rel_bias)


def _bias_sample_kernel(tbl_ref, bcs_ref, bsl_ref, bnew_ref, bw_ref, *, past, s_len, n_c):
    table = lambda b: tbl_ref[:, b:b + 1]

    def qpos(shape):
        return past + lax.rem(lax.broadcasted_iota(jnp.int32, shape, 0), s_len)

    def lane(shape):
        return lax.broadcasted_iota(jnp.int32, shape, 1)

    sh = bcs_ref.shape
    n = lane(sh)
    bcs_ref[...] = jnp.where(n < n_c, _bias_values(qpos(sh) - (CMP_STRIDE * n + CMP_BLOCK - 1), table), NEG)
    sh = bsl_ref.shape
    bsl_ref[...] = _bias_values(qpos(sh) - (past - sh[1] + lane(sh)), table)
    sh = bnew_ref.shape
    bnew_ref[...] = jnp.where(lane(sh) < s_len, _bias_values(qpos(sh) - (past + lane(sh)), table), NEG)
    sh = bw_ref.shape
    d = qpos(sh) - (past - sh[1] + lane(sh))
    bw_ref[...] = jnp.where(d < WINDOW, _bias_values(d, table), NEG)


def _bias_sample(tbl_rows, past, s_len, ncs, ck, wb):
    rows = tbl_rows.shape[0]
    widths = (ncs, ck, LANES, wb)
    return pl.pallas_call(
        functools.partial(_bias_sample_kernel, past=past, s_len=s_len, n_c=(past + s_len - CMP_BLOCK) // CMP_STRIDE + 1),
        out_shape=[jax.ShapeDtypeStruct((rows, w), F32) for w in widths],
        name="bias_sample",
    )(tbl_rows)


MASK_FLOOR = -1e8
M_INIT = -1e30
TAKEN = -3e38
N_FORCED = 3
SCORE_AHEAD = 3
FAR_PER_ITER = 8
VALUE_BEHIND = 2


def _sel_weights(v):
    return jnp.where((v >= 0) & (v <= 2), 2.0, jnp.where((v == -1) | (v == 3), 1.0, 0.0))


def _attn_prompt_kernel(q_ref, gate_ref, ks0_ref, ks1_ref, vs0_ref, vs1_ref, kwk_ref, vw0_ref, vw1_ref,
                        kc2_ref, bc_ref, t0_ref, t1_ref, o_ref,
                        qm_ref, selr_ref, oc_ref, ms_ref, as_ref, mw_ref, aw_ref, *, ncp, nsel):
    tq = Q_TILE
    i = pl.program_id(1)
    q0 = i * tq
    lane = lax.broadcasted_iota(jnp.int32, (tq, LANES), 1)
    heads = [(g, r) for g in range(H_KV) for r in range(GROUP)]

    for g, r in heads:
        gmask = (lane < HEAD_DIM) if g == 0 else (lane >= HEAD_DIM)
        qm_ref[g * GROUP + r] = jnp.where(gmask, q_ref[:, r * LANES:(r + 1) * LANES], jnp.zeros((tq, LANES), BF16))

    n0 = (tq // CMP_STRIDE) * i - CMP_BACK
    ws = pl.multiple_of(lax.rem(n0 + ncp, ncp), 8)
    cl = lax.broadcasted_iota(jnp.int32, (ncp, tq), 0)
    dm = jnp.where((cl >= CMP_NEAR) & (cl < ncp - n0), NEG, 0.0) + jnp.where(cl < -n0, NEG, 0.0)
    jrow = lax.broadcasted_iota(jnp.int32, (nsel, ncp), 0)
    nn = lax.broadcasted_iota(jnp.int32, (nsel, ncp), 1) + n0
    nn = jnp.where(nn < 0, nn + ncp, nn)
    nn = jnp.where(nn >= ncp, nn - ncp, nn)
    mit = _sel_weights(nn - (SEL_BLOCK // CMP_STRIDE) * jrow).astype(BF16)

    jr = lax.broadcasted_iota(jnp.int32, (nsel, tq), 0)
    qpos = q0 + lax.broadcasted_iota(jnp.int32, (nsel, tq), 1)
    cur = lax.shift_right_logical(qpos, int(math.log2(SEL_BLOCK)))
    invalid_pen = jnp.where(jr * SEL_BLOCK <= qpos, 0.0, NEG)
    bonus = jnp.where((jr == 0) | (jr == cur) | (jr == cur - 1), FORCE_BONUS, 0.0)
    pj = lax.broadcasted_iota(jnp.int32, (LANES, LANES), 0)
    pl_ = lax.broadcasted_iota(jnp.int32, (LANES, LANES), 1)
    kj = lax.broadcasted_iota(jnp.int32, (tq, tq), 0)
    qi = lax.broadcasted_iota(jnp.int32, (tq, tq), 1)
    wmask = jnp.where(kj > qi, 0.0, NEG)
    blocks_per_chunk = tq // SEL_BLOCK

    kc = kc2_ref[pl.ds(ws, ncp), 0:KV_HALF]
    vct = kc2_ref[pl.ds(ws, ncp), KV_HALF:KV_W].astype(F32).T.astype(BF16)
    def cmp_scores(h):
        return _dot_nt(kc, qm_ref[h]) + bc_ref[h] + dm

    ahead = {h: cmp_scores(h) for h in range(2)}
    psum = [None] * H_KV
    for h in range(N_HEADS):
        s = ahead.pop(h)
        if h + 2 < N_HEADS:
            ahead[h + 2] = cmp_scores(h + 2)
        m = jnp.maximum(jnp.max(s, axis=0, keepdims=True), MASK_FLOOR)
        e = jnp.exp2(s - m)
        p = e * (1.0 / jnp.maximum(jnp.sum(e, axis=0, keepdims=True), 1e-30))
        oc_ref[h] = _dot(vct, p.astype(BF16))
        psum[h // GROUP] = p if psum[h // GROUP] is None else psum[h // GROUP] + p

    scores = []
    for g in range(H_KV):
        h1, h2, h3 = _split3(psum[g])
        pslc = (_dot(mit, h1) + _dot(mit, h2)) + _dot(mit, h3)
        scores.append(jnp.where(invalid_pen < 0.0, NEG, pslc + bonus))

    def take_max(_, carry):
        out = []
        for work, sel_acc in carry:
            mx = jnp.max(work, axis=0, keepdims=True)
            first = jnp.min(jnp.where(work == mx, jr, nsel), axis=0, keepdims=True)
            hit = jr == first
            out.append((jnp.where(hit, TAKEN, work), jnp.where(hit, 1.0, sel_acc)))
        return tuple(out)

    is_forced = bonus > 0.0
    picked = lax.fori_loop(0, max(min(TOP_N, nsel) - N_FORCED, 0), take_max,
                           tuple((jnp.where(is_forced, TAKEN, sc), jnp.where(is_forced, 1.0, 0.0)) for sc in scores))
    for g in range(H_KV):
        sel_t = picked[g][1]
        if nsel < LANES:
            sel_t = jnp.concatenate([sel_t, jnp.zeros((LANES - nsel, tq), F32)], axis=0)
        selr_ref[g] = sel_t.T.astype(BF16)

    def masked_queries(t, g):
        off = pl_ - _aug_lane(g)
        place = jnp.where((pj - t * blocks_per_chunk == off) & (off >= 0) & (off < blocks_per_chunk), 1.0, 0.0)
        picked = _dot(selr_ref[g], place.astype(BF16))
        auglane = (lane >= _aug_lane(g)) & (lane < _aug_lane(g) + blocks_per_chunk)
        pen = jnp.where(auglane, (picked - 1.0) * (-NEG), 0.0).astype(BF16)
        return [qm_ref[g * GROUP + r] + pen for r in range(GROUP)]

    def run(chunks):
        items = []
        for k_of, vt_of, queries, extra, m_ref, a_ref in chunks:
            cache = {}
            items += [(h, k_of, vt_of, queries, extra, m_ref, a_ref, cache) for h in range(N_HEADS)]

        def kv(item):
            g, cache = item[0] // GROUP, item[7]
            if g not in cache:
                cache[g] = (item[1](g), item[2](g))
            return cache[g]

        def scores(item):
            h, queries, extra = item[0], item[3], item[4]
            s = _dot_nt(kv(item)[0], queries(h))
            return s if extra is None else s + extra(h)

        def softmax(item, s):
            h, m_ref = item[0], item[5]
            m_prev = m_ref[h]
            m_new = jnp.maximum(m_prev, jnp.max(s, axis=0, keepdims=True))
            m_ref[h] = m_new
            return jnp.exp2(s - m_new).astype(BF16), jnp.exp2(m_prev - m_new)

        def accumulate(item, p, alpha):
            h, a_ref = item[0], item[6]
            a_ref[h] = alpha * a_ref[h] + _dot(kv(item)[1], p)

        n = len(items)
        s = {j: scores(items[j]) for j in range(SCORE_AHEAD)}
        pa = {}
        for j in range(n):
            pa[j] = softmax(items[j], s.pop(j))
            if j + SCORE_AHEAD < n:
                s[j + SCORE_AHEAD] = scores(items[j + SCORE_AHEAD])
            if j >= VALUE_BEHIND:
                accumulate(items[j - VALUE_BEHIND], *pa.pop(j - VALUE_BEHIND))
        for j in range(n - VALUE_BEHIND, n):
            accumulate(items[j], *pa.pop(j))

    for m_ref, a_ref in ((ms_ref, as_ref), (mw_ref, aw_ref)):
        m_ref[...] = jnp.full(m_ref.shape, M_INIT, F32)
        a_ref[...] = jnp.zeros(a_ref.shape, F32)

    def sel_chunk(t, extra):
        sl = pl.ds(pl.multiple_of(t * tq, tq), tq)
        qs = [masked_queries(t, g) for g in range(H_KV)]
        return (lambda g: (ks0_ref, ks1_ref)[g][sl, :], lambda g: (vs0_ref, vs1_ref)[g][t],
                lambda h: qs[h // GROUP][h % GROUP], extra, ms_ref, as_ref)

    def win_chunk(t, extra):
        sl = pl.ds(pl.multiple_of(t * tq, tq), tq)
        return (lambda g: kwk_ref[sl, :], lambda g: (vw0_ref, vw1_ref)[g][t],
                lambda h: qm_ref[h], extra, mw_ref, aw_ref)

    n_far = jnp.maximum(i - 1, 0)
    n_iter = n_far // FAR_PER_ITER

    def far_group(u, carry):
        run([sel_chunk(FAR_PER_ITER * u + j, None) for j in range(FAR_PER_ITER)])
        return carry

    lax.fori_loop(0, n_iter, far_group, 0)
    for rem in range(1, FAR_PER_ITER):
        @pl.when(n_far - FAR_PER_ITER * n_iter == rem)
        def _(rem=rem):
            run([sel_chunk(FAR_PER_ITER * n_iter + j, None) for j in range(rem)])

    gone1 = jnp.where(i >= 1, 0.0, NEG)
    gone2 = jnp.where(i >= 2, 0.0, NEG)
    wmask2 = wmask + gone2
    run([sel_chunk(jnp.maximum(i - 1, 0), lambda h: t1_ref[h] + gone1),
         sel_chunk(i, lambda h: t0_ref[h]),
         win_chunk(jnp.maximum(i - 2, 0), lambda h: wmask2),
         win_chunk(jnp.maximum(i - 1, 0), lambda h: t1_ref[h] + gone1),
         win_chunk(i, lambda h: t0_ref[h])])

    gates_t = gate_ref[...].T
    for r in range(GROUP):
        per_g = []
        for g in range(H_KV):
            h = g * GROUP + r
            a_s, a_w = as_ref[h], aw_ref[h]
            o_sel = a_s[0:HEAD_DIM] * (1.0 / jnp.maximum(a_s[HEAD_DIM:HEAD_DIM + 1], 1e-30))
            o_win = a_w[0:HEAD_DIM] * (1.0 / jnp.maximum(a_w[HEAD_DIM:HEAD_DIM + 1], 1e-30))
            per_g.append(gates_t[h:h + 1, :] * oc_ref[h, g * HEAD_DIM:(g + 1) * HEAD_DIM, :]
                         + gates_t[N_HEADS + h:N_HEADS + h + 1, :] * o_sel
                         + gates_t[2 * N_HEADS + h:2 * N_HEADS + h + 1, :] * o_win)
        o_ref[:, r * LANES:(r + 1) * LANES] = jnp.concatenate(per_g, axis=0).T.astype(BF16)


def _attn_prompt(q, gates, ks_g, vs_g, kw_k, vw_g, kc2, bias_tiles, nb, t):
    tq = Q_TILE
    assert t % tq == 0 and WINDOW == 2 * tq
    nt = t // tq
    ncp = t // CMP_STRIDE
    nsel = t // SEL_BLOCK
    assert nsel <= LANES and ncp % LANES == 0
    bc, t0, t1 = bias_tiles
    tok = lambda w: pl.BlockSpec((tq, w), lambda b, i: (b * nt + i, 0))
    k_rows = pl.BlockSpec((None, t, KV_HALF), lambda b, i: (b, 0, 0))
    v_cols = pl.BlockSpec((None, nt, V_ROWS, tq), lambda b, i: (b, 0, 0, 0))
    state = [pltpu.VMEM((N_HEADS, 1, tq), F32), pltpu.VMEM((N_HEADS, V_ROWS, tq), F32)]
    kr = lambda a: a.reshape(nb, t, KV_HALF)
    vc = lambda a: a.reshape(nb, nt, V_ROWS, tq)
    return pl.pallas_call(
        functools.partial(_attn_prompt_kernel, ncp=ncp, nsel=nsel),
        grid=(nb, nt),
        in_specs=[tok(q.shape[1]), tok(LANES), k_rows, k_rows, v_cols, v_cols, k_rows, v_cols, v_cols,
                  pl.BlockSpec((None, 2 * ncp, KV_W), lambda b, i: (b, 0, 0)),
                  _resident(bc.shape), _resident(t0.shape), _resident(t1.shape)],
        out_specs=tok(q.shape[1]),
        out_shape=jax.ShapeDtypeStruct(q.shape, BF16),
        scratch_shapes=[pltpu.VMEM((N_HEADS, tq, LANES), BF16), pltpu.VMEM((H_KV, tq, LANES), BF16),
                        pltpu.VMEM((N_HEADS, LANES, tq), F32)] + state + state,
        compiler_params=_cparams(("arbitrary", "arbitrary")),
        name="attn_prompt",
    )(q, gates, kr(ks_g[0]), kr(ks_g[1]), vc(vs_g[0]), vc(vs_g[1]), kr(kw_k), vc(vw_g[0]), vc(vw_g[1]),
      kc2, bc, t0, t1)


SAMPLE_CK = 1024


def _softmax_part(s, v, transposed):
    m = jnp.max(s, axis=-1, keepdims=True)
    p = jnp.exp2(s - m)
    pv = _dot_nt(p.astype(BF16), v) if transposed else _dot(p.astype(BF16), v)
    return m, jnp.sum(p, axis=-1, keepdims=True), pv


def _merge_parts(parts):
    m = parts[0][0]
    for mt, _, _ in parts[1:]:
        m = jnp.maximum(m, mt)
    l, acc = None, None
    for mt, lt, at in parts:
        w = jnp.exp2(mt - m)
        l = w * lt if l is None else l + w * lt
        acc = w * at if acc is None else acc + w * at
    return acc / jnp.maximum(l, 1e-30)


def _attn_sample_kernel(pt_ref, q_ref, g_ref, knew_ref, wnew_ref, wnewt_ref, win_ref, kc_ref,
                        bcs_ref, bsl_ref, bnew_ref, bw_ref, msel_ref, cache_ref, o_ref, wout_ref, buf, sem,
                        *, n_pages, page, past, s_len, nsel):
    slot = _gather_pages(cache_ref, pt_ref, buf, sem, n_pages, page)

    wb = win_ref.shape[1]
    shifted = pltpu.roll(win_ref[...], wb - s_len, 1)
    fresh = pltpu.roll(wnewt_ref[...], LANES - s_len, 1)
    tail_lane = lax.broadcasted_iota(jnp.int32, fresh.shape, 1) >= LANES - s_len
    wout_ref[:, 0:wb - LANES] = shifted[:, 0:wb - LANES]
    wout_ref[:, wb - LANES:wb] = jnp.where(tail_lane, fresh, shifted[:, wb - LANES:wb])
    q = q_ref[...]
    rows = q.shape[0]
    gs = rows // GROUP
    nselp = msel_ref.shape[1]

    s = _dot_nt(q, kc_ref[:, 0:KV_HALF]) + bcs_ref[...]
    m = jnp.maximum(jnp.max(s, axis=-1, keepdims=True), MASK_FLOOR)
    e = jnp.exp2(s - m)
    p = e / jnp.maximum(jnp.sum(e, axis=-1, keepdims=True), 1e-30)
    o_cmp = _dot(p.astype(BF16), kc_ref[:, KV_HALF:KV_W])
    psum = p[0:gs]
    for r in range(1, GROUP):
        psum = psum + p[r * gs:(r + 1) * gs]

    h1, h2, h3 = _split3(psum)
    msel = msel_ref[...]
    pslc = (_dot(h1, msel) + _dot(h2, msel)) + _dot(h3, msel)
    j = lax.broadcasted_iota(jnp.int32, (gs, nselp), 1)
    qpos = past + lax.rem(lax.broadcasted_iota(jnp.int32, (gs, nselp), 0), s_len)
    cur = lax.shift_right_logical(qpos, int(math.log2(SEL_BLOCK)))
    valid = j * SEL_BLOCK <= qpos
    forced = (j == 0) | (j == cur) | (j == cur - 1)
    score = jnp.where(valid, pslc + jnp.where(forced, FORCE_BONUS, 0.0), NEG)
    score = jnp.where(j < nsel, score, -3e38)
    score_t = jnp.concatenate([score, jnp.zeros((LANES - gs, nselp), F32)], axis=0).T
    jp = lax.broadcasted_iota(jnp.int32, (nselp, nselp), 0)
    jj = lax.broadcasted_iota(jnp.int32, (nselp, nselp), 1)
    sel_rows = []
    for r in range(gs):
        col = score_t[:, r:r + 1]
        row = score[r:r + 1, :]
        beats = jnp.where(jj > jp, jnp.where(col >= row, 1.0, 0.0), jnp.where(col > row, 1.0, 0.0))
        rank = jnp.sum(beats, axis=0, keepdims=True)
        sel_rows.append(jnp.where(rank < min(TOP_N, nsel), 1.0, 0.0))
    sel8 = jnp.concatenate(sel_rows, axis=0)
    sel = jnp.concatenate([sel8] * GROUP, axis=0).astype(BF16)

    ck = bsl_ref.shape[1]
    n_chunks = past // ck
    erow = lax.broadcasted_iota(jnp.int32, (nselp, ck), 0)
    eblk = lax.shift_right_logical(lax.broadcasted_iota(jnp.int32, (nselp, ck), 1), int(math.log2(SEL_BLOCK)))
    scs = []
    for t in range(n_chunks):
        kt = buf[slot, 0:KV_HALF, t * ck:(t + 1) * ck].astype(BF16)
        e_t = jnp.where(erow == t * (ck // SEL_BLOCK) + eblk, 1.0, 0.0).astype(BF16)
        sc = _dot(q, kt) + (_dot(sel, e_t) - 1.0) * (-NEG)
        scs.append(sc + bsl_ref[...] if t == n_chunks - 1 else sc)
    knew = knew_ref[...]
    wnew = wnew_ref[...]
    s_new = _dot_nt(q, knew[:, 0:KV_HALF]) + bnew_ref[...]
    s_win = _dot(q, win_ref[0:KV_HALF, :].astype(BF16)) + bw_ref[...]
    s_wnew = _dot_nt(q, wnew[:, 0:KV_HALF]) + bnew_ref[...]
    parts = [_softmax_part(scs[t], buf[slot, KV_HALF:KV_W, t * ck:(t + 1) * ck].astype(BF16), True)
             for t in range(n_chunks)]
    parts.append(_softmax_part(s_new, knew[:, KV_HALF:KV_W], False))
    o_sel = _merge_parts(parts)

    o_win = _merge_parts([_softmax_part(s_win, win_ref[KV_HALF:KV_W, :].astype(BF16), True),
                          _softmax_part(s_wnew, wnew[:, KV_HALF:KV_W], False)])

    g = g_ref[...]
    o_ref[...] = g[:, 0:1] * o_cmp + g[:, 1:2] * o_sel + g[:, 2:3] * o_win


def _attn_sample(page_table, q32, g32, knew, wnew, wnew_t, win_t, kvc, tiles, msel, cache_t, past, s_len):
    nb, n_pages = page_table.shape
    page = cache_t.shape[2]
    rows = q32.shape[1]
    wb = win_t.shape[2]
    nsel = -(-(past + s_len) // SEL_BLOCK)
    bcs, bsl, bnew, bw = tiles
    per_b = lambda r, w: pl.BlockSpec((None, r, w), lambda b, pt: (b, 0, 0))
    return pl.pallas_call(
        functools.partial(_attn_sample_kernel, n_pages=n_pages, page=page, past=past, s_len=s_len, nsel=nsel),
        grid_spec=pltpu.PrefetchScalarGridSpec(
            num_scalar_prefetch=1,
            grid=(nb,),
            in_specs=[per_b(rows, LANES), per_b(rows, LANES), per_b(LANES, KV_W), per_b(LANES, KV_W),
                      per_b(KV_W, LANES), per_b(KV_W, wb), per_b(kvc.shape[1], KV_W),
                      _resident(bcs.shape), _resident(bsl.shape), _resident(bnew.shape),
                      _resident(bw.shape), _resident(msel.shape),
                      pl.BlockSpec(memory_space=pl.ANY)],
            out_specs=[per_b(rows, LANES), per_b(KV_W, wb)],
            scratch_shapes=[pltpu.VMEM((2, KV_W, past), F32), pltpu.SemaphoreType.DMA((2,))]),
        out_shape=[jax.ShapeDtypeStruct((nb, rows, LANES), F32), jax.ShapeDtypeStruct((nb, KV_W, wb), F32)],
        compiler_params=_cparams(("arbitrary",)),
        name="attn_sample",
    )(page_table, q32, g32, knew, wnew, wnew_t, win_t, kvc, bcs, bsl, bnew, bw, msel, cache_t)


def _prep_weights(w_in, w_cmp1, w_cmp2, pe_cmp, w_conv, w_nsa_out):
    d = w_in.shape[0]
    sizes = (N_HEADS * HEAD_DIM, KV_W, KV_W, KV_W, 3 * N_HEADS, _C_W, _C_W, _C_W, 2 * d)
    offs = [0]
    for sz in sizes:
        offs.append(offs[-1] + sz)
    part = lambda k: w_in[:, offs[k]:offs[k + 1]]
    qp = part(0).reshape(d, H_KV, GROUP, HEAD_DIM).transpose(0, 2, 1, 3).reshape(d, N_HEADS * HEAD_DIM)
    gn = jnp.pad(part(4), ((0, 0), (0, _G_W - 3 * N_HEADS)))
    w_in_p = jnp.concatenate([qp, part(1), part(2), part(3), gn, part(5), part(6), part(7), part(8)],
                             axis=1).astype(BF16)
    w_nsa_p = w_nsa_out.reshape(H_KV, GROUP, HEAD_DIM, -1).transpose(1, 0, 2, 3).reshape(
        N_HEADS * HEAD_DIM, -1).astype(BF16)
    r = CMP_BLOCK // CMP_STRIDE
    w1r = w_cmp1.reshape(2, r, CMP_STRIDE // 2, 2, HEAD_DIM, CMP_HIDDEN)
    w1t = jnp.transpose(w1r, (0, 2, 3, 4, 1, 5))
    w1e = w1t[:, :, :, :, :, None, :].astype(BF16)
    z1 = jnp.zeros_like(w1e)
    wb = jnp.stack([jnp.concatenate([w1e if gp == g else z1 for gp in range(H_KV)], axis=5)
                    for g in range(H_KV)], axis=3)
    wbig = wb.reshape(2, CMP_STRIDE // 2, 2 * KV_HALF, CMP_OUT_W)
    z2 = jnp.zeros((CMP_HIDDEN, HEAD_DIM), F32)
    w2bd = jnp.concatenate(
        [jnp.concatenate([w_cmp2[c] if (cp, gp) == (c, g) else z2 for cp in range(2) for gp in range(H_KV)], axis=1)
         for c in range(2) for g in range(H_KV)], axis=0).astype(BF16)
    pe_r = pe_cmp.reshape(2, r, CMP_STRIDE, HEAD_DIM).transpose(1, 2, 0, 3)
    pe_rows = jnp.broadcast_to(pe_r[:, :, :, None, :], (r, CMP_STRIDE, 2, H_KV, HEAD_DIM)).reshape(r, CHUNK_W)
    pe8 = jnp.pad(pe_rows, ((0, 8 - r), (0, 0)))
    w_conv8 = jnp.pad(w_conv, ((0, 8 - CONV_WIDTH), (0, 0)))
    return w_in_p, w_nsa_p, wbig, w2bd, pe8, w_conv8


def _sel_matrix(n_c, nselp):
    n = jnp.arange(n_c + 1)[:, None]
    j = jnp.arange(nselp)[None, :]
    return _sel_weights(n - (SEL_BLOCK // CMP_STRIDE) * j).astype(BF16)


def kernel(x_prompt, x_sample, cache_kv_cmp, cache_kv_sel, state_kv_win, state_conv, page_table,
           c_prompt, c_sample, w_ada, b_ada, g_norm, w_ffn1_gu, w_ffn1_down, w_ffn2_gu, w_ffn2_down,
           w_in, w_cmp1, w_cmp2, pe_cmp, w_conv, w_nsa_out, w_conv_out, w_out, rel_bias, g_final):
    assert w_ada.shape[0] == 1, "single-layer trunk"
    nbp, t, d = x_prompt.shape
    nbs, s_len, _ = x_sample.shape
    n_pages = page_table.shape[1]
    page = cache_kv_cmp.shape[2]
    past = n_pages * page
    n_phys = cache_kv_cmp.shape[1]
    wb = state_kv_win.shape[2]
    assert wb == WINDOW and past % SAMPLE_CK == 0

    w_in_p, w_nsa_p, wbig, w2bd, pe8, w_conv8 = _prep_weights(
        w_in[0], w_cmp1[0], w_cmp2[0], pe_cmp[0], w_conv[0], w_nsa_out[0])
    wgu1, wd1 = w_ffn1_gu[0].astype(BF16), w_ffn1_down[0].astype(BF16)
    wgu2, wd2 = w_ffn2_gu[0].astype(BF16), w_ffn2_down[0].astype(BF16)
    w_cv, w_o = w_conv_out[0].astype(BF16), w_out[0].astype(BF16)
    gn = [g_norm[0][k:k + 1] for k in range(N_SUB)]

    n_c_rows = nbp + nbs
    c_all = jnp.pad(jnp.concatenate([c_prompt, c_sample], axis=0), ((0, (-n_c_rows) % 8), (0, 0)))
    mod_all = _ada(c_all, w_ada[0], b_ada[0])
    mod_p = mod_all[:nbp].reshape(nbp * 3 * N_SUB, 1, d)
    mod_s = jnp.transpose(jnp.repeat(mod_all[nbp:n_c_rows].reshape(nbs, 3 * N_SUB, d), s_len, axis=0), (1, 0, 2))

    tm = TOKEN_TILE if t % TOKEN_TILE == 0 else t
    tpb = t // tm
    xp = x_prompt.reshape(nbp * t, d)
    h1 = _ffn(xp, mod_p, 0, gn[0], wgu1, wd1, tm, tpb)
    (q, kc_k, kc_v, kvc_t, kvs_t, kvw_t, ks0, ks1, vs0, vs1, kw_k, vw0, vw1, gates, z, cb, gm) = _inproj(
        h1, mod_p, gn[1], w_in_p, tm, tpb, True)
    kc2 = _cmp_prompt(kc_k, kc_v, t, wbig, pe8, w2bd)
    tiles_p = _bias_prompt(rel_bias, t // CMP_STRIDE)
    o_nsa = _attn_prompt(q, gates, (ks0, ks1), (vs0, vs1), kw_k, (vw0, vw1), kc2, tiles_p, nbp, t)
    tm_mix = 2 * tm if t % (2 * tm) == 0 else tm
    h2 = _mixout(h1, o_nsa, z, None, cb, gm, mod_p, w_conv8, w_nsa_p, w_cv, w_o, tm_mix, t // tm_mix)
    y_prompt = _ffn(h2, mod_p, 2, gn[2], wgu2, wd2, tm, tpb, g_final).reshape(nbp, t, d)

    kv_shape = (2, H_KV, HEAD_DIM)
    kv_out = lambda a: jnp.transpose(a.reshape((1, nbp) + kv_shape + (a.shape[-1],)), (0, 1, 5, 2, 3, 4))
    kv_cmp_p = kv_out(kvc_t)
    kv_sel_p = kv_out(kvs_t)
    keep = min(WINDOW, t)
    kv_win_p = kv_out(kvw_t[:, :, t - keep:])
    conv_p = z.reshape(1, nbp, t, -1)[:, :, t - (CONV_WIDTH - 1):]

    ns = nbs * s_len
    xs = x_sample.reshape(ns, d)
    h1s = _ffn(xs, mod_s, 0, gn[0], wgu1, wd1, ns, 1)
    qs, kvc_s, kvs_s, kvw_s, kvs_sb, kvw_sb, gates_s, z_s, cb_s, gm_s = _inproj(
        h1s, mod_s, gn[1], w_in_p, ns, 1, False)

    pos_minor = lambda a: jnp.transpose(a, (0, 2, 3, 4, 1)).reshape(a.shape[0], KV_W, a.shape[1])
    kvc_past = _cmp_sample(page_table, pos_minor(cache_kv_cmp[0]), wbig, pe8, w2bd)
    n_c = (past + s_len - CMP_BLOCK) // CMP_STRIDE + 1
    nsel = -(-(past + s_len) // SEL_BLOCK)
    nselp = -(-nsel // LANES) * LANES
    rows = N_HEADS * s_len
    head_of_row = [g * GROUP + r for r in range(GROUP) for g in range(H_KV) for _ in range(s_len)]
    tbl_rows = jnp.pad(rel_bias.T[jnp.array(head_of_row)], ((0, 0), (0, LANES - N_BUCKETS)))
    tiles_s = _bias_sample(tbl_rows, past, s_len, past // CMP_STRIDE, SAMPLE_CK, wb)
    msel = _sel_matrix(past // CMP_STRIDE - 1, nselp)

    q5 = qs.reshape(nbs, s_len, GROUP, H_KV, HEAD_DIM).transpose(0, 2, 3, 1, 4)
    q32 = jnp.zeros((nbs, GROUP, H_KV, s_len, H_KV, HEAD_DIM), BF16)
    for g in range(H_KV):
        q32 = q32.at[:, :, g, :, g, :].set(q5[:, :, g])
    q32 = q32.reshape(nbs, rows, LANES)
    g5 = gates_s[:, :3 * N_HEADS].reshape(nbs, s_len, 3, H_KV, GROUP).transpose(0, 4, 3, 1, 2)
    g32 = jnp.pad(g5.reshape(nbs, rows, 3), ((0, 0), (0, 0), (0, LANES - 3)))
    pad_new = lambda a: jnp.pad(a.reshape(nbs, s_len, KV_W), ((0, 0), (0, LANES - s_len), (0, 0)))
    wnew_t = jnp.pad(jnp.transpose(kvw_s.reshape(nbs, s_len, KV_W), (0, 2, 1)), ((0, 0), (0, 0), (0, LANES - s_len)))
    o32, win_next = _attn_sample(page_table, q32, g32, pad_new(kvs_sb), pad_new(kvw_sb), wnew_t,
                                 pos_minor(state_kv_win[0]), kvc_past, tiles_s, msel,
                                 pos_minor(cache_kv_sel[0]), past, s_len)
    o6 = o32.reshape(nbs, GROUP, H_KV, s_len, H_KV, HEAD_DIM)
    o_s = jnp.stack([o6[:, :, g, :, g, :] for g in range(H_KV)], axis=3)
    o_nsa_s = o_s.transpose(0, 2, 1, 3, 4).reshape(ns, N_HEADS * HEAD_DIM).astype(BF16)

    full = jnp.concatenate([state_conv[0], z_s.reshape(nbs, s_len, -1)], axis=1)
    z_shift = (full[:, 1:1 + s_len].reshape(ns, -1), full[:, 0:s_len].reshape(ns, -1))
    h2s = _mixout(h1s, o_nsa_s, z_s, z_shift, cb_s, gm_s, mod_s, w_conv8, w_nsa_p, w_cv, w_o, ns, 1)
    y_sample = _ffn(h2s, mod_s, 2, gn[2], wgu2, wd2, ns, 1, g_final).reshape(nbs, s_len, d)

    kv_cmp_s = kvc_s.reshape((1, nbs, s_len) + kv_shape)
    kv_sel_s = kvs_s.reshape((1, nbs, s_len) + kv_shape)
    kv_win_s = jnp.transpose(win_next.reshape((1, nbs) + kv_shape + (wb,)), (0, 1, 5, 2, 3, 4))
    conv_s = full[None, :, s_len:]
    return (y_prompt, y_sample, kv_cmp_p, kv_sel_p, kv_win_p, conv_p, kv_cmp_s, kv_sel_s, kv_win_s, conv_s)
```
